```python
import jax
import jax.numpy as jnp
from jax import lax
import numpy as np

D_MODEL = 1024
BATCH = 16
SEQ = 2048
DEPTH = 2

EPS = 1e-6
GLA_HEADS = 4
GLA_DK = 64
GLA_DV = 128
GLA_GATE_RANK = 16
GLA_TAU = 16.0
GLA_CHUNK = 64
DSA_HEADS = 8
DSA_Q_RANK = 256
DSA_KV_RANK = 128
DSA_HEAD_DIM = 64
DSA_V_DIM = 64
IDX_HEADS = 8
IDX_DIM = 32
DSA_TOPK_MAX = 256
Q_BLOCK = 128
MLSTM_HEADS = 4
MLSTM_DQK = 64
MLSTM_DV = 128
MLSTM_CONV = 4
MLSTM_CHUNK = 64
N_EXPERTS = 64
TOP_K = 6
N_GROUPS = 8
TOPK_GROUPS = 4
EXPERT_FF = 256
SHARED_FF = 256
ROUTED_SCALE = 2.5
TOKEN_BLOCK = 128

GLA_QK = GLA_HEADS * GLA_DK
GLA_V = GLA_HEADS * GLA_DV
DSA_V = DSA_HEADS * DSA_V_DIM
MLSTM_QK = MLSTM_HEADS * MLSTM_DQK
MLSTM_V = MLSTM_HEADS * MLSTM_DV
IN_SIZES = (GLA_QK, GLA_QK, GLA_V, GLA_V, GLA_GATE_RANK,
            DSA_Q_RANK, DSA_KV_RANK, IDX_DIM, IDX_HEADS,
            MLSTM_QK, MLSTM_QK, MLSTM_V, MLSTM_HEADS, MLSTM_HEADS, MLSTM_V,
            3 * D_MODEL)
N_IN = sum(IN_SIZES)

kernel_name = 'hybrid_gla_dsa_mlstm_moe'


def _rms(x, g):
    xf = x.astype(jnp.float32)
    y = xf * lax.rsqrt(jnp.mean(xf * xf, axis=-1, keepdims=True) + EPS)
    return (y * g.astype(jnp.float32)).astype(x.dtype)


def _head_rms(x, g):
    xf = x.astype(jnp.float32)
    y = xf * lax.rsqrt(jnp.mean(xf * xf, axis=-1, keepdims=True) + EPS)
    return y.reshape(x.shape[0], x.shape[1], -1) * g.astype(jnp.float32)


def _to_chunks(t, L):
    b, s = t.shape[0], t.shape[1]
    return jnp.moveaxis(t.reshape((b, s // L, L) + t.shape[2:]), 1, 0)


def _from_chunks(t):
    t = jnp.moveaxis(t, 0, 1)
    return t.reshape((t.shape[0], t.shape[1] * t.shape[2]) + t.shape[3:])


def _causal_dwconv(x, w):
    K, C = w.shape
    return lax.conv_general_dilated(x, w[:, None, :].astype(x.dtype), window_strides=(1,),
                                    padding=[(K - 1, 0)], dimension_numbers=('NWC', 'WIO', 'NWC'),
                                    feature_group_count=C)


def _gla(q, k, v, log_a):
    f32 = jnp.float32
    b, s, nh, dk = q.shape
    dv = v.shape[-1]
    L = GLA_CHUNK
    causal = jnp.tril(jnp.ones((L, L), dtype=bool))[None, :, :, None, None]
    xs = tuple(_to_chunks(t.astype(f32), L) for t in (q * dk ** -0.5, k, v, log_a))

    def step(state, inp):
        qb, kb, vb, ab = inp
        cum = jnp.cumsum(ab, axis=1)
        inter = jnp.einsum('blhk,bhkv->blhv', qb * jnp.exp(cum), state)
        rel = jnp.exp(jnp.where(causal, cum[:, :, None] - cum[:, None, :], -jnp.inf))
        scores = jnp.einsum('bthk,btshk,bshk->btsh', qb, rel, kb)
        intra = jnp.einsum('btsh,bshv->bthv', scores, vb)
        tot = cum[:, -1]
        k_dec = kb * jnp.exp(tot[:, None] - cum)
        state = state * jnp.exp(tot)[..., None] + jnp.einsum('blhk,blhv->bhkv', k_dec, vb)
        return state, inter + intra

    _, out = lax.scan(step, jnp.zeros((b, nh, dk, dv), f32), xs)
    return _from_chunks(out)


def _mlstm(q, k, v, i_pre, log_f):
    f32 = jnp.float32
    b, s, nh, dk = q.shape
    dv = v.shape[-1]
    L = MLSTM_CHUNK
    causal = jnp.tril(jnp.ones((L, L), dtype=bool))[None, :, :, None]
    xs = tuple(_to_chunks(t.astype(f32), L) for t in (q * dk ** -0.5, k, v, i_pre, log_f))

    def step(carry, inp):
        C, n, m = carry
        qb, kb, vb, ib, fb = inp
        bcum = jnp.cumsum(fb, axis=1)
        dlog = jnp.where(causal, bcum[:, :, None] - bcum[:, None, :] + ib[:, None], -jnp.inf)
        inter_log = bcum + m[:, None]
        m_t = jnp.maximum(inter_log, jnp.max(dlog, axis=2))
        w_intra = jnp.exp(dlog - m_t[:, :, None])
        w_inter = jnp.exp(inter_log - m_t)
        qk = jnp.einsum('bthk,bshk->btsh', qb, kb) * w_intra
        num = jnp.einsum('btsh,bshv->bthv', qk, vb) + w_inter[..., None] * jnp.einsum('bthk,bhkv->bthv', qb, C)
        den = jnp.sum(qk, axis=2) + w_inter * jnp.einsum('bthk,bhk->bth', qb, n)
        h = num / jnp.maximum(jnp.abs(den), jnp.exp(-m_t))[..., None]
        tot = bcum[:, -1]
        g_log = tot[:, None] - bcum + ib
        m_new = jnp.maximum(tot + m, jnp.max(g_log, axis=1))
        w_s = jnp.exp(g_log - m_new[:, None])
        w_c = jnp.exp(tot + m - m_new)
        C = C * w_c[..., None, None] + jnp.einsum('blh,blhk,blhv->bhkv', w_s, kb, vb)
        n = n * w_c[..., None] + jnp.einsum('blh,blhk->bhk', w_s, kb)
        return (C, n, m_new), h

    init = (jnp.zeros((b, nh, dk, dv), f32), jnp.zeros((b, nh, dk), f32), jnp.zeros((b, nh), f32))
    _, out = lax.scan(step, init, xs)
    return _from_chunks(out)


def _dsa(cq, ckv, k_idx, w_idx, w_uq, w_uk, w_uv, w_qi):
    f32 = jnp.float32
    b, s, _ = cq.shape
    topk = min(DSA_TOPK_MAX, s // 4)
    q = jnp.einsum('bsr,rhd->bshd', cq, w_uq)
    q_lat = jnp.einsum('bshd,chd->bshc', q, w_uk) * DSA_HEAD_DIM ** -0.5
    q_idx = jnp.einsum('bsr,rhd->bshd', cq, w_qi) * IDX_DIM ** -0.5
    w_head = w_idx * IDX_HEADS ** -0.5
    slopes = 2.0 ** (-8.0 * jnp.arange(1, DSA_HEADS + 1, dtype=f32) / DSA_HEADS)
    key_pos = jnp.arange(s)

    def block(inp):
        ql, qi, wh, pos = inp
        idx_scores = jnp.einsum('bthd,bsd->bths', qi, k_idx)
        I = jnp.einsum('bth,bths->bts', wh, jax.nn.relu(idx_scores)).astype(f32)
        I = jnp.where((key_pos[None, :] <= pos[:, None])[None], I, -jnp.inf)
        _, sel = lax.top_k(I, topk)
        kv = jax.vmap(lambda cb, ib: cb[ib])(ckv, sel)
        dist = (pos[None, :, None] - sel).astype(f32)
        logits = jnp.einsum('bthr,btkr->bthk', ql, kv).astype(f32) - slopes[:, None] * dist[:, :, None, :]
        logits = jnp.where((dist >= 0)[:, :, None, :], logits, -jnp.inf)
        p = jax.nn.softmax(logits, axis=-1).astype(kv.dtype)
        return jnp.einsum('bthk,btkr->bthr', p, kv)

    xs = (_to_chunks(q_lat, Q_BLOCK), _to_chunks(q_idx, Q_BLOCK), _to_chunks(w_head, Q_BLOCK),
          key_pos.reshape(-1, Q_BLOCK))
    o_lat = _from_chunks(lax.map(block, xs))
    o = jnp.einsum('bshr,rhv->bshv', o_lat, w_uv)
    return o.reshape(b, s, -1)


def _hybrid_mixer(h, w_in, gla_w_a2, gla_b_a, gla_norm, dsa_norm_q, dsa_norm_kv, dsa_w_uq, dsa_w_uk,
                  dsa_w_uv, dsa_w_qi, mlstm_conv, mlstm_b_i, mlstm_b_f, mlstm_norm,
                  w_up_a, w_up_b, w_up_c, w_o):
    b, s, _ = h.shape
    f32 = jnp.float32
    split_points = [int(v) for v in np.cumsum(IN_SIZES)[:-1]]
    (qa, ka, va, ga, a_lr, cq, ckv, k_idx, w_idx,
     qc, kc, vc, ic, fc, oc, gates) = jnp.split(h @ w_in, split_points, axis=-1)

    log_a = jax.nn.log_sigmoid((a_lr @ gla_w_a2 + gla_b_a).astype(f32)) / GLA_TAU
    oa = _gla(qa.reshape(b, s, GLA_HEADS, GLA_DK), ka.reshape(b, s, GLA_HEADS, GLA_DK),
              va.reshape(b, s, GLA_HEADS, GLA_DV), log_a.reshape(b, s, GLA_HEADS, GLA_DK))
    ya = (_head_rms(oa, gla_norm) * jax.nn.silu(ga.astype(f32))).astype(h.dtype) @ w_up_a

    ob = _dsa(_rms(cq, dsa_norm_q), _rms(ckv, dsa_norm_kv), k_idx, w_idx,
              dsa_w_uq, dsa_w_uk, dsa_w_uv, dsa_w_qi)
    yb = ob @ w_up_b

    qk = jax.nn.silu(_causal_dwconv(jnp.concatenate([qc, kc], axis=-1), mlstm_conv))
    qc, kc = jnp.split(qk, 2, axis=-1)
    i_pre = (ic + mlstm_b_i).astype(f32)
    log_f = jax.nn.log_sigmoid((fc + mlstm_b_f).astype(f32))
    o_c = _mlstm(qc.reshape(b, s, MLSTM_HEADS, MLSTM_DQK), kc.reshape(b, s, MLSTM_HEADS, MLSTM_DQK),
                 vc.reshape(b, s, MLSTM_HEADS, MLSTM_DV), i_pre, log_f)
    yc = (_head_rms(o_c, mlstm_norm) * jax.nn.sigmoid(oc.astype(f32))).astype(h.dtype) @ w_up_c

    g_a, g_b, g_c = jnp.split(jax.nn.sigmoid(gates.astype(f32)).astype(h.dtype), 3, axis=-1)
    return (g_a * ya + g_b * yb + g_c * yc) @ w_o


def _moe(h, router_w, router_bias, w_gate, w_up, w_down, s_gate, s_up, s_down):
    b, s, d = h.shape
    t = h.reshape(-1, d)
    scores = jax.nn.sigmoid(jnp.einsum('td,de->te', t, router_w).astype(jnp.float32))
    sel = scores + router_bias.astype(jnp.float32)
    grp = sel.reshape(-1, N_GROUPS, N_EXPERTS // N_GROUPS)
    grp_score = jnp.sum(lax.top_k(grp, 2)[0], axis=-1)
    _, grp_idx = lax.top_k(grp_score, TOPK_GROUPS)
    grp_mask = jnp.sum(jax.nn.one_hot(grp_idx, N_GROUPS, dtype=jnp.float32), axis=1) > 0
    sel = jnp.where(jnp.repeat(grp_mask, N_EXPERTS // N_GROUPS, axis=1), sel, -jnp.inf)
    _, idx = lax.top_k(sel, TOP_K)
    w = jnp.take_along_axis(scores, idx, axis=1)
    w = w / jnp.sum(w, axis=-1, keepdims=True) * ROUTED_SCALE
    gates = jnp.sum(jax.nn.one_hot(idx, N_EXPERTS, dtype=jnp.float32) * w[..., None], axis=1)

    def block(inp):
        xb, gb = inp
        a = jnp.einsum('td,edf->tef', xb, w_gate)
        u = jnp.einsum('td,edf->tef', xb, w_up)
        return jnp.einsum('tef,efd->td', jax.nn.silu(a) * u * gb[..., None], w_down)

    xs = (t.reshape(-1, TOKEN_BLOCK, d), gates.astype(t.dtype).reshape(-1, TOKEN_BLOCK, N_EXPERTS))
    routed = lax.map(block, xs).reshape(-1, d)
    shared = (jax.nn.silu(t @ s_gate) * (t @ s_up)) @ s_down
    return (routed + shared).reshape(b, s, d)


def setup_inputs(seed: int = 0) -> dict:
    key = jax.random.key(seed)
    keys = jax.random.split(key, 40)
    cnt = [0]

    def nxt():
        cnt[0] += 1
        return keys[cnt[0] - 1]

    def nrm(shape, scale):
        return jax.random.normal(nxt(), shape, jnp.float32) * scale

    def gain(shape):
        return 1.0 + 0.05 * jax.random.normal(nxt(), shape, jnp.float32)

    D, L, E, F = D_MODEL, DEPTH, N_EXPERTS, EXPERT_FF
    return {
        'x': nrm((BATCH, SEQ, D), 1.0),
        'c': nrm((BATCH, D), 1.0),
        'ada_w': nrm((L, D, 6 * D), 0.3 * D ** -0.5),
        'ada_b': nrm((L, 6 * D), 0.02),
        'norm_mix': gain((L, D)),
        'norm_ffn': gain((L, D)),
        'w_in': nrm((L, D, N_IN), D ** -0.5),
        'gla_w_a2': nrm((L, GLA_GATE_RANK, GLA_QK), GLA_GATE_RANK ** -0.5),
        'gla_b_a': nrm((L, GLA_QK), 0.1),
        'gla_norm': gain((L, GLA_V)),
        'dsa_norm_q': gain((L, DSA_Q_RANK)),
        'dsa_norm_kv': gain((L, DSA_KV_RANK)),
        'dsa_w_uq': nrm((L, DSA_Q_RANK, DSA_HEADS, DSA_HEAD_DIM), DSA_Q_RANK ** -0.5),
        'dsa_w_uk': nrm((L, DSA_KV_RANK, DSA_HEADS, DSA_HEAD_DIM), DSA_KV_RANK ** -0.5),
        'dsa_w_uv': nrm((L, DSA_KV_RANK, DSA_HEADS, DSA_V_DIM), DSA_KV_RANK ** -0.5),
        'dsa_w_qi': nrm((L, DSA_Q_RANK, IDX_HEADS, IDX_DIM), DSA_Q_RANK ** -0.5),
        'mlstm_conv': nrm((L, MLSTM_CONV, 2 * MLSTM_QK), MLSTM_CONV ** -0.5),
        'mlstm_b_i': nrm((L, MLSTM_HEADS), 0.1),
        'mlstm_b_f': 3.0 + nrm((L, MLSTM_HEADS), 0.5),
        'mlstm_norm': gain((L, MLSTM_V)),
        'w_up_a': nrm((L, GLA_V, D), GLA_V ** -0.5),
        'w_up_b': nrm((L, DSA_V, D), DSA_V ** -0.5),
        'w_up_c': nrm((L, MLSTM_V, D), MLSTM_V ** -0.5),
        'w_o': nrm((L, D, D), D ** -0.5),
        'router_w': nrm((L, D, E), D ** -0.5),
        'router_bias': nrm((L, E), 0.01),
        'exp_w_gate': nrm((L, E, D, F), D ** -0.5),
        'exp_w_up': nrm((L, E, D, F), D ** -0.5),
        'exp_w_down': nrm((L, E, F, D), F ** -0.5),
        'sh_w_gate': nrm((L, D, SHARED_FF), D ** -0.5),
        'sh_w_up': nrm((L, D, SHARED_FF), D ** -0.5),
        'sh_w_down': nrm((L, SHARED_FF, D), SHARED_FF ** -0.5),
        'norm_final': gain((D,)),
    }


def reference(x, c, ada_w, ada_b, norm_mix, norm_ffn, w_in, gla_w_a2, gla_b_a, gla_norm,
              dsa_norm_q, dsa_norm_kv, dsa_w_uq, dsa_w_uk, dsa_w_uv, dsa_w_qi,
              mlstm_conv, mlstm_b_i, mlstm_b_f, mlstm_norm,
              w_up_a, w_up_b, w_up_c, w_o,
              router_w, router_bias, exp_w_gate, exp_w_up, exp_w_down,
              sh_w_gate, sh_w_up, sh_w_down, norm_final):
    cs = jax.nn.silu(c)
    for l in range(DEPTH):
        mod = cs @ ada_w[l] + ada_b[l]
        sh1, sc1, g1, sh2, sc2, g2 = jnp.split(mod[:, None, :], 6, axis=-1)
        h = _rms(x, norm_mix[l]) * (1 + sc1) + sh1
        x = x + g1 * _hybrid_mixer(h, w_in[l], gla_w_a2[l], gla_b_a[l], gla_norm[l],
                                   dsa_norm_q[l], dsa_norm_kv[l], dsa_w_uq[l], dsa_w_uk[l],
                                   dsa_w_uv[l], dsa_w_qi[l], mlstm_conv[l], mlstm_b_i[l],
                                   mlstm_b_f[l], mlstm_norm[l], w_up_a[l], w_up_b[l],
                                   w_up_c[l], w_o[l])
        h = _rms(x, norm_ffn[l]) * (1 + sc2) + sh2
        x = x + g2 * _moe(h, router_w[l], router_bias[l], exp_w_gate[l], exp_w_up[l],
                          exp_w_down[l], sh_w_gate[l], sh_w_up[l], sh_w_down[l])
    return _rms(x, norm_final)
```

```python
import functools

import jax
import jax.numpy as jnp
from jax import lax
from jax.experimental import pallas as pl
from jax.experimental.pallas import tpu as pltpu

F32 = jnp.float32
BF16 = jnp.bfloat16
MXU_DTYPE = jnp.bfloat16
ACT_DTYPE = jnp.bfloat16
HIGHEST = lax.Precision.HIGHEST

EPS = 1e-6
D_MODEL = 1024
GLA_HEADS, GLA_DK, GLA_DV, GLA_GATE_RANK, GLA_TAU, GLA_CHUNK = 4, 64, 128, 16, 16.0, 64
GLA_SUB = 16
DSA_HEADS, DSA_Q_RANK, DSA_KV_RANK, DSA_HEAD_DIM, DSA_V_DIM = 8, 256, 128, 64, 64
IDX_HEADS, IDX_DIM, DSA_TOPK_MAX = 8, 32, 256
MLSTM_HEADS, MLSTM_DQK, MLSTM_DV, MLSTM_CONV, MLSTM_CHUNK = 4, 64, 128, 4, 64
N_EXPERTS, TOP_K, N_GROUPS, TOPK_GROUPS, EXPERT_FF, SHARED_FF = 64, 6, 8, 4, 256, 256
ROUTED_SCALE = 2.5

GLA_QK = GLA_HEADS * GLA_DK
GLA_V = GLA_HEADS * GLA_DV
DSA_V = DSA_HEADS * DSA_V_DIM
MLSTM_QK = MLSTM_HEADS * MLSTM_DQK
MLSTM_V = MLSTM_HEADS * MLSTM_DV
IN_SIZES = (GLA_QK, GLA_QK, GLA_V, GLA_V, GLA_GATE_RANK,
            DSA_Q_RANK, DSA_KV_RANK, IDX_DIM, IDX_HEADS,
            MLSTM_QK, MLSTM_QK, MLSTM_V, MLSTM_HEADS, MLSTM_HEADS, MLSTM_V,
            3 * D_MODEL)

LANE = 128
KEY_TILE = 128
Q_TILE = 128
NEG_BIG = -1e30

C_GATES = 0
C_QA = 3072
C_KA = 3328
C_VA = 3584
C_GA = 4096
C_VC = 4608
C_OC = 5120
C_QKC = 5632
C_CQ = 6144
C_CKV = 6400
C_KW = 6528
C_ALR = 6656
C_ICFC = 6784
N_PACK = 6912
W_IDX_LANE = IDX_DIM


def _mx(x):
    return x.astype(MXU_DTYPE)


def _dot(a, b):
    return jnp.dot(_mx(a), _mx(b), preferred_element_type=F32)


def _dot_nt(a, b):
    return lax.dot_general(_mx(a), _mx(b), (((1,), (1,)), ((), ())), preferred_element_type=F32)


def _dot_tn(a, b):
    return lax.dot_general(_mx(a), _mx(b), (((0,), (0,)), ((), ())), preferred_element_type=F32)


def _rms(x, g):
    return x * lax.rsqrt(jnp.mean(x * x, axis=-1, keepdims=True) + EPS) * g


def _silu(x):
    return x * jax.nn.sigmoid(x)


def _log_sigmoid(z):
    return jnp.minimum(z, 0.0) - jnp.log1p(jnp.exp(-jnp.abs(z)))


def _cumsum_rows(x):
    n = x.shape[0]
    row = lax.broadcasted_iota(jnp.int32, x.shape, 0)
    k = 1
    while k < n:
        x = x + jnp.where(row >= k, pltpu.roll(x, k, 0), 0.0)
        k *= 2
    return x


def _params(sem, vmem_mb=40):
    return pltpu.CompilerParams(dimension_semantics=sem, vmem_limit_bytes=vmem_mb * 1024 * 1024)


def _ada_kernel(c_ref, w_ref, b_ref, o_ref):
    cs = _silu(c_ref[...])
    o_ref[0] = jnp.dot(cs, w_ref[0], preferred_element_type=F32, precision=HIGHEST) + b_ref[0]


def _ada_mod(c, ada_w, ada_b):
    depth, d, n = ada_w.shape
    b = c.shape[0]
    return pl.pallas_call(
        _ada_kernel, name="ada_mod",
        grid=(depth, n // d),
        in_specs=[pl.BlockSpec((b, d), lambda l, j: (0, 0)),
                  pl.BlockSpec((1, d, d), lambda l, j: (l, 0, j)),
                  pl.BlockSpec((1, 1, d), lambda l, j: (l, 0, j))],
        out_specs=pl.BlockSpec((1, b, d), lambda l, j: (l, 0, j)),
        out_shape=jax.ShapeDtypeStruct((depth, b, n), F32),
        compiler_params=_params(("parallel", "parallel")),
    )(c, ada_w, ada_b.reshape(depth, 1, n))


def _inproj_kernel(x_ref, sc_ref, sh_ref, g_ref, w_ref, o_ref):
    h = _rms(x_ref[...], g_ref[...]) * (1.0 + sc_ref[0]) + sh_ref[0]
    o_ref[...] = _dot(h, w_ref[...])


def _in_proj(x2, mod3, norm_g, w_pack, seq):
    t, d = x2.shape
    tm = min(512, seq)
    n_col = 3
    cw = N_PACK // n_col
    return pl.pallas_call(
        _inproj_kernel, name="in_proj",
        grid=(n_col, t // tm),
        in_specs=[pl.BlockSpec((tm, d), lambda j, i: (i, 0)),
                  pl.BlockSpec((1, 1, d), lambda j, i: ((i * tm) // seq, 0, 1)),
                  pl.BlockSpec((1, 1, d), lambda j, i: ((i * tm) // seq, 0, 0)),
                  pl.BlockSpec((1, d), lambda j, i: (0, 0)),
                  pl.BlockSpec((d, cw), lambda j, i: (0, j))],
        out_specs=pl.BlockSpec((tm, cw), lambda j, i: (i, j)),
        out_shape=jax.ShapeDtypeStruct((t, N_PACK), F32),
        compiler_params=_params(("parallel", "parallel")),
    )(x2, mod3, mod3, norm_g, w_pack)


def _gla_kernel(q_ref, k_ref, v_ref, g_ref, alr_ref, wa2_ref, ba_ref, gn_ref, o_ref, s_ref, acc_ref):
    L, sub, nh, dk, dv = GLA_CHUNK, GLA_SUB, GLA_HEADS, GLA_DK, GLA_DV

    @pl.when(pl.program_id(1) == 0)
    def _():
        s_ref[...] = jnp.zeros_like(s_ref)

    z = jnp.dot(alr_ref[0], wa2_ref[...], preferred_element_type=F32, precision=HIGHEST) + ba_ref[...]
    log_a = _log_sigmoid(z) * (1.0 / GLA_TAU)
    cum = _cumsum_rows(log_a)
    q = q_ref[0] * (dk ** -0.5)
    k = k_ref[0]
    vb = _mx(v_ref[0])
    tot = cum[L - 1:L, :]
    q_in = _mx(q * jnp.exp(cum))
    k_dec = _mx(k * jnp.exp(tot - cum))

    for i in range(L // sub):
        r0, r1 = i * sub, (i + 1) * sub
        base = cum[r0 - 1:r0, :] if i > 0 else jnp.zeros_like(tot)
        qi = _mx(q[r0:r1] * jnp.exp(cum[r0:r1] - base))
        ka = _mx(k[:r1] * jnp.exp(base - cum[:r1]))
        row = lax.broadcasted_iota(jnp.int32, (sub, r1), 0)
        col = lax.broadcasted_iota(jnp.int32, (sub, r1), 1)
        causal = col <= row + r0
        for h in range(nh):
            s = _dot_nt(qi[:, h * dk:(h + 1) * dk], ka[:, h * dk:(h + 1) * dk])
            s = jnp.where(causal, s, 0.0)
            acc_ref[r0:r1, h * dv:(h + 1) * dv] = _dot(s, vb[:r1, h * dv:(h + 1) * dv])

    g = g_ref[0]
    gn = gn_ref[...]
    for h in range(nh):
        ks, vs = slice(h * dk, (h + 1) * dk), slice(h * dv, (h + 1) * dv)
        st = s_ref[h]
        o = acc_ref[:, vs] + _dot(q_in[:, ks], st)
        y = _rms(o, gn[:, vs]) * _silu(g[:, vs])
        o_ref[0, :, vs] = y.astype(o_ref.dtype)
        decay = jnp.transpose(jnp.exp(tot[:, ks]))
        s_ref[h] = st * decay + _dot_tn(k_dec[:, ks], vb[:, vs])


def _gla(proj3, wa2_pad, ba, gn):
    b, s, _ = proj3.shape
    L = GLA_CHUNK
    blk = lambda w, c0: pl.BlockSpec((1, L, w), lambda bi, ci: (bi, ci, c0 // w))
    full = lambda a: pl.BlockSpec(a.shape, lambda bi, ci: (0,) * a.ndim)
    return pl.pallas_call(
        _gla_kernel, name="gla",
        grid=(b, s // L),
        in_specs=[blk(GLA_QK, C_QA), blk(GLA_QK, C_KA), blk(GLA_V, C_VA), blk(GLA_V, C_GA),
                  blk(LANE, C_ALR), full(wa2_pad), full(ba), full(gn)],
        out_specs=pl.BlockSpec((1, L, GLA_V), lambda bi, ci: (bi, ci, 0)),
        out_shape=jax.ShapeDtypeStruct((b, s, GLA_V), ACT_DTYPE),
        scratch_shapes=[pltpu.VMEM((GLA_HEADS, GLA_DK, GLA_DV), F32),
                        pltpu.VMEM((L, GLA_V), F32)],
        compiler_params=_params(("parallel", "arbitrary")),
    )(proj3, proj3, proj3, proj3, proj3, wa2_pad, ba, gn)


def _mlstm_kernel(qk_ref, v_ref, oc_ref, if_ref, conv_ref, bias_ref, gn_ref, o_ref,
                  xbuf_ref, c_ref, n_ref, m_ref):
    L, nh, dk, dv, kc = MLSTM_CHUNK, MLSTM_HEADS, MLSTM_DQK, MLSTM_DV, MLSTM_CONV
    pad = 8

    @pl.when(pl.program_id(1) == 0)
    def _():
        xbuf_ref[0:pad, :] = jnp.zeros((pad, 2 * MLSTM_QK), F32)
        c_ref[...] = jnp.zeros_like(c_ref)
        n_ref[...] = jnp.zeros_like(n_ref)
        m_ref[...] = jnp.zeros_like(m_ref)

    xbuf_ref[pad:pad + L, :] = qk_ref[0]
    cw = conv_ref[...]
    conv = jnp.zeros((L, 2 * MLSTM_QK), F32)
    for j in range(kc):
        conv = conv + cw[j:j + 1, :] * xbuf_ref[pl.ds(pad - (kc - 1) + j, L), :]
    xbuf_ref[0:pad, :] = xbuf_ref[L:L + pad, :]
    qk = _silu(conv)
    q = qk[:, :MLSTM_QK] * (dk ** -0.5)
    k = qk[:, MLSTM_QK:]
    v = v_ref[0]

    pre = if_ref[0] + bias_ref[...]
    bcum = _cumsum_rows(_log_sigmoid(pre))
    pre_t = jnp.transpose(pre)
    bcum_t = jnp.transpose(bcum)
    row = lax.broadcasted_iota(jnp.int32, (L, L), 0)
    col = lax.broadcasted_iota(jnp.int32, (L, L), 1)
    causal = col <= row
    oc = oc_ref[0]
    gn = gn_ref[...]

    for h in range(nh):
        ks, vs = slice(h * dk, (h + 1) * dk), slice(h * dv, (h + 1) * dv)
        b_col = bcum[:, nh + h:nh + h + 1]
        b_row = bcum_t[nh + h:nh + h + 1, :]
        i_col = pre[:, h:h + 1]
        i_row = pre_t[h:h + 1, :]
        m_prev = m_ref[h]
        c_st = c_ref[h]
        n_st = n_ref[h]
        qh, kh, vh = q[:, ks], k[:, ks], v[:, vs]

        dlog = jnp.where(causal, b_col - b_row + i_row, -jnp.inf)
        inter_log = b_col + m_prev
        m_t = jnp.maximum(inter_log, jnp.max(dlog, axis=1, keepdims=True))
        w_intra = jnp.exp(dlog - m_t)
        w_inter = jnp.exp(inter_log - m_t)
        s = _dot_nt(qh, kh) * w_intra
        num = _dot(s, vh) + w_inter * _dot(qh, c_st)
        den = jnp.sum(s, axis=1, keepdims=True) + w_inter * jnp.sum(qh * n_st, axis=1, keepdims=True)
        hout = num / jnp.maximum(jnp.abs(den), jnp.exp(-m_t))
        y = _rms(hout, gn[:, vs]) * jax.nn.sigmoid(oc[:, vs])
        o_ref[0, :, vs] = y.astype(o_ref.dtype)

        tot = b_col[L - 1:L, :]
        g_log = tot - b_col + i_col
        m_new = jnp.maximum(tot + m_prev, jnp.max(g_log, axis=0, keepdims=True))
        w_s = jnp.exp(g_log - m_new)
        w_c = jnp.exp(tot + m_prev - m_new)
        k_s = kh * w_s
        c_ref[h] = c_st * w_c + _dot_tn(k_s, vh)
        n_ref[h] = n_st * w_c + jnp.sum(k_s, axis=0, keepdims=True)
        m_ref[h] = m_new


def _mlstm(proj3, conv_w, bias_row, gn):
    b, s, _ = proj3.shape
    L = MLSTM_CHUNK
    blk = lambda w, c0: pl.BlockSpec((1, L, w), lambda bi, ci: (bi, ci, c0 // w))
    full = lambda a: pl.BlockSpec(a.shape, lambda bi, ci: (0,) * a.ndim)
    return pl.pallas_call(
        _mlstm_kernel, name="mlstm",
        grid=(b, s // L),
        in_specs=[blk(2 * MLSTM_QK, C_QKC), blk(MLSTM_V, C_VC), blk(MLSTM_V, C_OC), blk(LANE, C_ICFC),
                  full(conv_w), full(bias_row), full(gn)],
        out_specs=pl.BlockSpec((1, L, MLSTM_V), lambda bi, ci: (bi, ci, 0)),
        out_shape=jax.ShapeDtypeStruct((b, s, MLSTM_V), ACT_DTYPE),
        scratch_shapes=[pltpu.VMEM((L + 8, 2 * MLSTM_QK), F32),
                        pltpu.VMEM((MLSTM_HEADS, MLSTM_DQK, MLSTM_DV), F32),
                        pltpu.VMEM((MLSTM_HEADS, 1, MLSTM_DQK), F32),
                        pltpu.VMEM((MLSTM_HEADS, 1, 1), F32)],
        compiler_params=_params(("parallel", "arbitrary")),
    )(proj3, proj3, proj3, proj3, conv_w, bias_row, gn)


def _dsa_prep_kernel(cq_ref, ckv_ref, kw_ref, nq_ref, nkv_ref, wuq_ref, wuk_ref, wqi_ref,
                     ckvn_ref, kwb_ref, qlat_ref, qidx_ref):
    cqn = _mx(_rms(cq_ref[...], nq_ref[...]))
    ckvn_ref[...] = _rms(ckv_ref[...], nkv_ref[...]).astype(ckvn_ref.dtype)
    kwb_ref[...] = kw_ref[...].astype(kwb_ref.dtype)
    q = jnp.dot(cqn, wuq_ref[...], preferred_element_type=F32)
    for h in range(DSA_HEADS):
        ql = _dot(q[:, h * DSA_HEAD_DIM:(h + 1) * DSA_HEAD_DIM], wuk_ref[h]) * (DSA_HEAD_DIM ** -0.5)
        qlat_ref[0, h] = ql.astype(qlat_ref.dtype)
    qi = jnp.dot(cqn, wqi_ref[...], preferred_element_type=F32) * (IDX_DIM ** -0.5)
    for h in range(IDX_HEADS):
        qidx_ref[0, h] = qi[:, h * LANE:(h + 1) * LANE].astype(qidx_ref.dtype)


def _dsa_prep(proj2, seq, nq, nkv, wuq, wuk_t, wqi_pad):
    t = proj2.shape[0]
    b = t // seq
    tm = min(512, seq)
    per = seq // tm
    blk = lambda w, c0: pl.BlockSpec((tm, w), lambda i: (i, c0 // w))
    full = lambda a: pl.BlockSpec(a.shape, lambda i: (0,) * a.ndim)
    hmap = lambda i: (i // per, 0, i % per, 0)
    return pl.pallas_call(
        _dsa_prep_kernel, name="dsa_prep",
        grid=(t // tm,),
        in_specs=[blk(DSA_Q_RANK, C_CQ), blk(DSA_KV_RANK, C_CKV), blk(LANE, C_KW),
                  full(nq), full(nkv), full(wuq), full(wuk_t), full(wqi_pad)],
        out_specs=[pl.BlockSpec((tm, DSA_KV_RANK), lambda i: (i, 0)),
                   pl.BlockSpec((tm, LANE), lambda i: (i, 0)),
                   pl.BlockSpec((1, DSA_HEADS, tm, DSA_KV_RANK), hmap),
                   pl.BlockSpec((1, IDX_HEADS, tm, LANE), hmap)],
        out_shape=[jax.ShapeDtypeStruct((t, DSA_KV_RANK), ACT_DTYPE),
                   jax.ShapeDtypeStruct((t, LANE), ACT_DTYPE),
                   jax.ShapeDtypeStruct((b, DSA_HEADS, seq, DSA_KV_RANK), ACT_DTYPE),
                   jax.ShapeDtypeStruct((b, IDX_HEADS, seq, LANE), ACT_DTYPE)],
        compiler_params=_params(("parallel",)),
    )(proj2, proj2, proj2, nq, nkv, wuq, wuk_t, wqi_pad)


def _dsa_kernel(qlat_ref, qidx_ref, kwq_ref, kwk_ref, ckv_ref, wuv_ref, o_ref,
                ibuf_ref, tau_ref, need_ref, m_ref, l_ref, acc_ref, *, topk):
    nh, tq, tk = DSA_HEADS, Q_TILE, KEY_TILE
    qb = pl.program_id(1)
    n_kt = qb + 1
    t_pos = qb * tq + lax.broadcasted_iota(jnp.int32, (tq, tk), 0)
    s_loc = lax.broadcasted_iota(jnp.int32, (tq, tk), 1)

    qi = qidx_ref[0].reshape(IDX_HEADS * tq, LANE)
    wq = kwq_ref[0]
    wb = [jnp.broadcast_to(wq[:, W_IDX_LANE + h:W_IDX_LANE + h + 1] * (IDX_HEADS ** -0.5), (tq, tk))
          for h in range(IDX_HEADS)]

    def idx_body(kt, carry):
        kk = kwk_ref[0, pl.ds(pl.multiple_of(kt * tk, tk), tk), :]
        sc = jnp.maximum(_dot_nt(qi, kk), 0.0)
        tot = wb[0] * sc[0:tq]
        for h in range(1, IDX_HEADS):
            tot = tot + wb[h] * sc[h * tq:(h + 1) * tq]
        ibuf_ref[kt] = jnp.where(kt * tk + s_loc <= t_pos, tot, -jnp.inf)
        return carry

    lax.fori_loop(0, n_kt, idx_body, 0)

    def count_ge(cand):
        def body(kt, c):
            return c + jnp.where(ibuf_ref[kt] >= cand, 1.0, 0.0)
        return jnp.sum(lax.fori_loop(0, n_kt, body, jnp.zeros((tq, tk), F32)), axis=1, keepdims=True)

    def count_gt(cand):
        def body(kt, c):
            return c + jnp.where(ibuf_ref[kt] > cand, 1.0, 0.0)
        return jnp.sum(lax.fori_loop(0, n_kt, body, jnp.zeros((tq, tk), F32)), axis=1, keepdims=True)

    def key_to_float(u):
        key = u ^ jnp.int32(-2 ** 31)
        bits = jnp.where(key >= 0, key, key ^ jnp.int32(0x7FFFFFFF))
        return lax.bitcast_convert_type(bits, F32)

    tau_ref[...] = jnp.full((tq, 1), -jnp.inf, F32)

    @pl.when(n_kt * tk > topk)
    def _():
        def bit_body(i, u):
            cand_u = u | lax.shift_left(jnp.int32(1), 31 - i)
            cnt = count_ge(key_to_float(cand_u))
            return jnp.where(cnt >= topk, cand_u, u)
        u = lax.fori_loop(0, 32, bit_body, jnp.zeros((tq, 1), jnp.int32))
        tau_ref[...] = key_to_float(u)

    tau = tau_ref[...]
    need_ref[...] = topk - count_gt(tau)
    need = need_ref[...]

    m_ref[...] = jnp.full(m_ref.shape, NEG_BIG, F32)
    l_ref[...] = jnp.zeros_like(l_ref)
    acc_ref[...] = jnp.zeros_like(acc_ref)
    ql = qlat_ref[0].reshape(nh * tq, DSA_KV_RANK)
    tri = jnp.where(lax.broadcasted_iota(jnp.int32, (tk, tk), 0) <= s_loc, 1.0, 0.0).astype(BF16)

    def att_body(kt, eq_seen):
        kv = ckv_ref[0, pl.ds(pl.multiple_of(kt * tk, tk), tk), :]
        lg = _dot_nt(ql, kv)
        it = ibuf_ref[kt]
        eq = it == tau
        eqf = jnp.where(eq, 1.0, 0.0)
        rank = jnp.dot(eqf.astype(BF16), tri, preferred_element_type=F32) + eq_seen
        dist = (t_pos - (kt * tk + s_loc)).astype(F32)
        valid = ((it > tau) | (eq & (rank <= need))) & (dist >= 0.0)
        for h in range(nh):
            rows = slice(h * tq, (h + 1) * tq)
            slope = 2.0 ** (-8.0 * (h + 1) / nh)
            lh = jnp.where(valid, lg[rows] - slope * dist, NEG_BIG)
            m_old = m_ref[rows]
            m_new = jnp.maximum(m_old, jnp.max(lh, axis=1, keepdims=True))
            p = jnp.where(valid, jnp.exp(lh - m_new), 0.0)
            alpha = jnp.exp(m_old - m_new)
            l_ref[rows] = alpha * l_ref[rows] + jnp.sum(p, axis=1, keepdims=True)
            acc_ref[rows] = alpha * acc_ref[rows] + _dot(p, kv)
            m_ref[rows] = m_new
        return eq_seen + jnp.sum(eqf, axis=1, keepdims=True)

    lax.fori_loop(0, n_kt, att_body, jnp.zeros((tq, 1), F32))

    outs = []
    for h in range(nh):
        rows = slice(h * tq, (h + 1) * tq)
        outs.append(_dot(acc_ref[rows] / l_ref[rows], wuv_ref[h]))
    o_ref[0] = jnp.concatenate(outs, axis=1).astype(o_ref.dtype)


def _dsa(proj3, ckvn3, kwb3, qlat, qidx, wuv_t):
    b, s, _ = proj3.shape
    topk = min(DSA_TOPK_MAX, s // 4)
    nh, tq = DSA_HEADS, Q_TILE
    return pl.pallas_call(
        functools.partial(_dsa_kernel, topk=topk), name="dsa_attn",
        grid=(b, s // tq),
        in_specs=[pl.BlockSpec((1, nh, tq, DSA_KV_RANK), lambda bi, qi: (bi, 0, qi, 0)),
                  pl.BlockSpec((1, IDX_HEADS, tq, LANE), lambda bi, qi: (bi, 0, qi, 0)),
                  pl.BlockSpec((1, tq, LANE), lambda bi, qi: (bi, qi, C_KW // LANE)),
                  pl.BlockSpec((1, s, LANE), lambda bi, qi: (bi, 0, 0)),
                  pl.BlockSpec((1, s, DSA_KV_RANK), lambda bi, qi: (bi, 0, 0)),
                  pl.BlockSpec(wuv_t.shape, lambda bi, qi: (0, 0, 0))],
        out_specs=pl.BlockSpec((1, tq, DSA_V), lambda bi, qi: (bi, qi, 0)),
        out_shape=jax.ShapeDtypeStruct((b, s, DSA_V), ACT_DTYPE),
        scratch_shapes=[pltpu.VMEM((s // KEY_TILE, tq, KEY_TILE), F32),
                        pltpu.VMEM((tq, 1), F32),
                        pltpu.VMEM((tq, 1), F32),
                        pltpu.VMEM((nh * tq, 1), F32),
                        pltpu.VMEM((nh * tq, 1), F32),
                        pltpu.VMEM((nh * tq, DSA_KV_RANK), F32)],
        compiler_params=_params(("parallel", "arbitrary")),
    )(qlat, qidx, proj3, kwb3, ckvn3, wuv_t)


def _merge_kernel(a_ref, b_ref, c_ref, gt_ref, x_ref, g1_ref, wa_ref, wb_ref, wc_ref, wo_ref, o_ref):
    d = D_MODEL
    g = jax.nn.sigmoid(gt_ref[...])
    ya = jnp.dot(a_ref[...], wa_ref[...], preferred_element_type=F32)
    yb = jnp.dot(b_ref[...], wb_ref[...], preferred_element_type=F32)
    yc = jnp.dot(c_ref[...], wc_ref[...], preferred_element_type=F32)
    m = g[:, :d] * ya + g[:, d:2 * d] * yb + g[:, 2 * d:] * yc
    o_ref[...] = x_ref[...] + g1_ref[0] * _dot(m, wo_ref[...])


def _merge(ya_in, yb_in, yc_in, proj2, x2, mod3, wa, wb, wc, wo, seq):
    t, d = x2.shape
    tm = min(512, seq)
    full = lambda a: pl.BlockSpec(a.shape, lambda i: (0,) * a.ndim)
    br = lambda w: pl.BlockSpec((tm, w), lambda i: (i, 0))
    return pl.pallas_call(
        _merge_kernel, name="merge",
        grid=(t // tm,),
        in_specs=[br(GLA_V), br(DSA_V), br(MLSTM_V), br(3 * d), br(d),
                  pl.BlockSpec((1, 1, d), lambda i: ((i * tm) // seq, 0, 2)),
                  full(wa), full(wb), full(wc), full(wo)],
        out_specs=br(d),
        out_shape=jax.ShapeDtypeStruct((t, d), F32),
        compiler_params=_params(("parallel",)),
    )(ya_in, yb_in, yc_in, proj2, x2, mod3, wa, wb, wc, wo)


def _first_argmax_mask(cur, iota, axis, n):
    mx = jnp.max(cur, axis=axis, keepdims=True)
    ix = jnp.min(jnp.where(cur == mx, iota, n), axis=axis, keepdims=True)
    return iota == ix


def _router_kernel(x_ref, sc_ref, sh_ref, g_ref, rwt_ref, rb_ref, o_ref):
    ne, ng = N_EXPERTS, N_GROUPS
    eg = ne // ng
    h = _rms(x_ref[...], g_ref[...]) * (1.0 + sc_ref[0]) + sh_ref[0]
    tm = h.shape[0]
    logits = lax.dot_general(rwt_ref[...], h, (((1,), (1,)), ((), ())),
                             preferred_element_type=F32, precision=HIGHEST)
    scores = jax.nn.sigmoid(logits)
    sel = scores + rb_ref[...]
    s3 = sel.reshape(ng, eg, tm)
    io3 = lax.broadcasted_iota(jnp.int32, (ng, eg, tm), 1)
    m1 = jnp.max(s3, axis=1, keepdims=True)
    first = _first_argmax_mask(s3, io3, 1, eg)
    m2 = jnp.max(jnp.where(first, -jnp.inf, s3), axis=1, keepdims=True)
    gs = (m1 + m2).reshape(ng, tm)
    iog = lax.broadcasted_iota(jnp.int32, (ng, tm), 0)
    gkeep = jnp.zeros((ng, tm), F32)
    cur = gs
    for _ in range(TOPK_GROUPS):
        hit = _first_argmax_mask(cur, iog, 0, ng)
        gkeep = jnp.where(hit, 1.0, gkeep)
        cur = jnp.where(hit, -jnp.inf, cur)
    selm = jnp.where(gkeep.reshape(ng, 1, tm) > 0.0, s3, -jnp.inf).reshape(ne, tm)
    ioe = lax.broadcasted_iota(jnp.int32, (ne, tm), 0)
    chosen = jnp.zeros((ne, tm), F32)
    cur = selm
    for _ in range(TOP_K):
        hit = _first_argmax_mask(cur, ioe, 0, ne)
        chosen = jnp.where(hit, 1.0, chosen)
        cur = jnp.where(hit, -jnp.inf, cur)
    w = chosen * scores
    o_ref[...] = w / jnp.sum(w, axis=0, keepdims=True) * ROUTED_SCALE


def _router(x2, mod3, norm_g, rw_t, rb_col, seq):
    t, d = x2.shape
    tm = min(512, seq)
    return pl.pallas_call(
        _router_kernel, name="router",
        grid=(t // tm,),
        in_specs=[pl.BlockSpec((tm, d), lambda i: (i, 0)),
                  pl.BlockSpec((1, 1, d), lambda i: ((i * tm) // seq, 0, 4)),
                  pl.BlockSpec((1, 1, d), lambda i: ((i * tm) // seq, 0, 3)),
                  pl.BlockSpec((1, d), lambda i: (0, 0)),
                  pl.BlockSpec(rw_t.shape, lambda i: (0, 0)),
                  pl.BlockSpec(rb_col.shape, lambda i: (0, 0))],
        out_specs=pl.BlockSpec((N_EXPERTS, tm), lambda i: (0, i)),
        out_shape=jax.ShapeDtypeStruct((N_EXPERTS, t), F32),
        compiler_params=_params(("parallel",)),
    )(x2, mod3, mod3, norm_g, rw_t, rb_col)


def _moe_kernel(x_ref, sc_ref, sh_ref, g2_ref, gn_ref, gates_ref, wg_ref, wu_ref, wd_ref,
                sg_ref, su_ref, sd_ref, nf_ref, o_ref, h_ref, acc_ref, *, final):
    e = pl.program_id(1)

    @pl.when(e == 0)
    def _():
        h = _mx(_rms(x_ref[...], gn_ref[...]) * (1.0 + sc_ref[0]) + sh_ref[0])
        h_ref[...] = h
        acc_ref[...] = _dot(_silu(_dot(h, sg_ref[...])) * _dot(h, su_ref[...]), sd_ref[...])

    h = h_ref[...]
    gate = jnp.transpose(gates_ref[pl.ds(e, 1), :])
    y = _silu(_dot(h, wg_ref[0, 0])) * _dot(h, wu_ref[0, 0]) * gate
    acc_ref[...] += _dot(y, wd_ref[0, 0])

    @pl.when(e == pl.num_programs(1) - 1)
    def _():
        xo = x_ref[...] + g2_ref[0] * acc_ref[...]
        if final:
            xo = _rms(xo, nf_ref[...])
        o_ref[...] = xo


def _moe(x2, mod3, norm_g, gates_t, layer, wg, wu, wd, sg, su, sd, nf, seq, final):
    t, d = x2.shape
    tm = min(1024, seq)
    ne, ff = wg.shape[1], wg.shape[3]
    full = lambda a: pl.BlockSpec(a.shape, lambda i, e: (0,) * a.ndim)
    mod = lambda j: pl.BlockSpec((1, 1, d), lambda i, e: ((i * tm) // seq, 0, j))
    return pl.pallas_call(
        functools.partial(_moe_kernel, final=final), name="moe",
        grid=(t // tm, ne),
        in_specs=[pl.BlockSpec((tm, d), lambda i, e: (i, 0)), mod(4), mod(3), mod(5),
                  pl.BlockSpec((1, d), lambda i, e: (0, 0)),
                  pl.BlockSpec((ne, tm), lambda i, e: (0, i)),
                  pl.BlockSpec((1, 1, d, ff), lambda i, e: (layer, e, 0, 0)),
                  pl.BlockSpec((1, 1, d, ff), lambda i, e: (layer, e, 0, 0)),
                  pl.BlockSpec((1, 1, ff, d), lambda i, e: (layer, e, 0, 0)),
                  full(sg), full(su), full(sd), full(nf)],
        out_specs=pl.BlockSpec((tm, d), lambda i, e: (i, 0)),
        out_shape=jax.ShapeDtypeStruct((t, d), F32),
        scratch_shapes=[pltpu.VMEM((tm, d), MXU_DTYPE), pltpu.VMEM((tm, d), F32)],
        compiler_params=_params(("parallel", "arbitrary"), vmem_mb=48),
    )(x2, mod3, mod3, mod3, norm_g, gates_t, wg, wu, wd, sg, su, sd, nf)


def _pack_w_in(w):
    d = w.shape[0]
    offs = [0]
    for n in IN_SIZES:
        offs.append(offs[-1] + n)
    (qa, ka, va, ga, alr, cq, ckv, kidx, widx, qc, kc, vc, ic, fc, oc, gates) = [
        w[:, offs[i]:offs[i + 1]] for i in range(len(IN_SIZES))]
    z = lambda n: jnp.zeros((d, n), w.dtype)
    packed = jnp.concatenate(
        [gates, qa, ka, va, ga, vc, oc, qc, kc, cq, ckv,
         kidx, widx, z(LANE - IDX_DIM - IDX_HEADS),
         alr, z(LANE - GLA_GATE_RANK),
         ic, fc, z(LANE - 2 * MLSTM_HEADS)], axis=1)
    assert packed.shape[1] == N_PACK
    return packed.astype(MXU_DTYPE)


def kernel(x, c, ada_w, ada_b, norm_mix, norm_ffn, w_in, gla_w_a2, gla_b_a, gla_norm, dsa_norm_q,
           dsa_norm_kv, dsa_w_uq, dsa_w_uk, dsa_w_uv, dsa_w_qi, mlstm_conv, mlstm_b_i, mlstm_b_f,
           mlstm_norm, w_up_a, w_up_b, w_up_c, w_o, router_w, router_bias, exp_w_gate, exp_w_up,
           exp_w_down, sh_w_gate, sh_w_up, sh_w_down, norm_final):
    b, s, d = x.shape
    depth = ada_w.shape[0]
    t = b * s
    mod = _ada_mod(c, ada_w, ada_b)
    x2 = x.reshape(t, d)
    row = lambda v: v.reshape(1, -1)
    for l in range(depth):
        mod3 = mod[l].reshape(b, 1, 6 * d)
        proj2 = _in_proj(x2, mod3, row(norm_mix[l]), _pack_w_in(w_in[l]), s)
        proj3 = proj2.reshape(b, s, N_PACK)

        wa2_pad = jnp.zeros((LANE, GLA_QK), F32).at[:GLA_GATE_RANK].set(gla_w_a2[l])
        ya_in = _gla(proj3, wa2_pad, row(gla_b_a[l]), row(gla_norm[l]))

        bias_row = jnp.zeros((1, LANE), F32).at[0, :MLSTM_HEADS].set(mlstm_b_i[l])
        bias_row = bias_row.at[0, MLSTM_HEADS:2 * MLSTM_HEADS].set(mlstm_b_f[l])
        yc_in = _mlstm(proj3, mlstm_conv[l], bias_row, row(mlstm_norm[l]))

        wuq = dsa_w_uq[l].reshape(DSA_Q_RANK, DSA_HEADS * DSA_HEAD_DIM).astype(MXU_DTYPE)
        wuk_t = jnp.transpose(dsa_w_uk[l], (1, 2, 0)).astype(MXU_DTYPE)
        wuv_t = jnp.transpose(dsa_w_uv[l], (1, 0, 2)).astype(MXU_DTYPE)
        wqi_pad = jnp.zeros((DSA_Q_RANK, IDX_HEADS, LANE), F32).at[:, :, :IDX_DIM].set(dsa_w_qi[l])
        wqi_pad = wqi_pad.reshape(DSA_Q_RANK, IDX_HEADS * LANE).astype(MXU_DTYPE)
        ckvn, kwb, qlat, qidx = _dsa_prep(proj2, s, row(dsa_norm_q[l]), row(dsa_norm_kv[l]),
                                          wuq, wuk_t, wqi_pad)
        yb_in = _dsa(proj3, ckvn.reshape(b, s, DSA_KV_RANK), kwb.reshape(b, s, LANE), qlat, qidx, wuv_t)

        x2 = _merge(ya_in.reshape(t, GLA_V), yb_in.reshape(t, DSA_V), yc_in.reshape(t, MLSTM_V),
                    proj2, x2, mod3, w_up_a[l].astype(MXU_DTYPE), w_up_b[l].astype(MXU_DTYPE),
                    w_up_c[l].astype(MXU_DTYPE), w_o[l].astype(MXU_DTYPE), s)

        gates_t = _router(x2, mod3, row(norm_ffn[l]), jnp.transpose(router_w[l]),
                          router_bias[l].reshape(-1, 1), s)
        x2 = _moe(x2, mod3, row(norm_ffn[l]), gates_t, l, exp_w_gate, exp_w_up, exp_w_down,
                  sh_w_gate[l].astype(MXU_DTYPE), sh_w_up[l].astype(MXU_DTYPE),
                  sh_w_down[l].astype(MXU_DTYPE), row(norm_final), s, final=(l == depth - 1))
    return x2.reshape(b, s, d)
```

```python
import functools
import struct

import jax
import jax.numpy as jnp
from jax import lax
from jax.experimental import pallas as pl
from jax.experimental.pallas import tpu as pltpu

F32 = jnp.float32
BF16 = jnp.bfloat16
MXU_DTYPE = jnp.bfloat16
ACT_DTYPE = jnp.bfloat16
HIGHEST = lax.Precision.HIGHEST

EPS = 1e-6
D_MODEL = 1024
GLA_HEADS, GLA_DK, GLA_DV, GLA_GATE_RANK, GLA_TAU, GLA_CHUNK = 4, 64, 128, 16, 16.0, 64
GLA_SUB = 16
DSA_HEADS, DSA_Q_RANK, DSA_KV_RANK, DSA_HEAD_DIM, DSA_V_DIM = 8, 256, 128, 64, 64
IDX_HEADS, IDX_DIM, DSA_TOPK_MAX = 8, 32, 256
MLSTM_HEADS, MLSTM_DQK, MLSTM_DV, MLSTM_CONV, MLSTM_CHUNK = 4, 64, 128, 4, 64
N_EXPERTS, TOP_K, N_GROUPS, TOPK_GROUPS, EXPERT_FF, SHARED_FF = 64, 6, 8, 4, 256, 256
ROUTED_SCALE = 2.5

GLA_QK = GLA_HEADS * GLA_DK
GLA_V = GLA_HEADS * GLA_DV
DSA_V = DSA_HEADS * DSA_V_DIM
MLSTM_QK = MLSTM_HEADS * MLSTM_DQK
MLSTM_V = MLSTM_HEADS * MLSTM_DV
IN_SIZES = (GLA_QK, GLA_QK, GLA_V, GLA_V, GLA_GATE_RANK,
            DSA_Q_RANK, DSA_KV_RANK, IDX_DIM, IDX_HEADS,
            MLSTM_QK, MLSTM_QK, MLSTM_V, MLSTM_HEADS, MLSTM_HEADS, MLSTM_V,
            3 * D_MODEL)

LANE = 128
KEY_TILE = 256
Q_TILE = 128
NEG_BIG = -1e30
MOE_BLOCK = 1024
MOE_SUB = 256
MOE_ROW_TILE = 256
MOE_FFN_TILE = 128
ROW_ALIGN = 16
SLOT_ROWS = 8

C_GATES = 0
C_QA = 3072
C_KA = 3328
C_VA = 3584
C_GA = 4096
C_VC = 4608
C_OC = 5120
C_QKC = 5632
C_CQ = 6144
C_CKV = 6400
C_KW = 6528
C_ALR = 6656
C_ICFC = 6784
N_PACK = 6912
W_IDX_LANE = IDX_DIM


LOG2E = 1.4426950408889634
DSA_AUG = DSA_KV_RANK + LANE


def _bf16_pieces(x, n):
    out = []
    for _ in range(n):
        bits = struct.unpack("<I", struct.pack("<f", x))[0]
        bits = (bits + 0x7FFF + ((bits >> 16) & 1)) & 0xFFFF0000
        piece = struct.unpack("<f", struct.pack("<I", bits))[0]
        out.append(piece)
        x -= piece
    return out


def _mx(x):
    return x.astype(MXU_DTYPE)


def _dot(a, b):
    return jnp.dot(_mx(a), _mx(b), preferred_element_type=F32)


def _dot_nt(a, b):
    return lax.dot_general(_mx(a), _mx(b), (((1,), (1,)), ((), ())), preferred_element_type=F32)


def _dot_tn(a, b):
    return lax.dot_general(_mx(a), _mx(b), (((0,), (0,)), ((), ())), preferred_element_type=F32)


def _rms(x, g):
    return x * lax.rsqrt(jnp.mean(x * x, axis=-1, keepdims=True) + EPS) * g


def _silu(x):
    return x * jax.nn.sigmoid(x)


def _log_sigmoid(z):
    return jnp.minimum(z, 0.0) - jnp.log1p(jnp.exp(-jnp.abs(z)))


def _cumsum_rows(x):
    n = x.shape[0]
    row = lax.broadcasted_iota(jnp.int32, x.shape, 0)
    k = 1
    while k < n:
        x = x + jnp.where(row >= k, pltpu.roll(x, k, 0), 0.0)
        k *= 2
    return x


def _colreduce(x, op, width=32):
    n, c = x.shape
    return op(op(x.reshape(n // width, width, c), axis=0), axis=0, keepdims=True)


def _params(sem, vmem_mb=40):
    return pltpu.CompilerParams(dimension_semantics=sem, vmem_limit_bytes=vmem_mb * 1024 * 1024)


def _ada_kernel(c_ref, w_ref, b_ref, o_ref):
    cs = _silu(c_ref[...])
    o_ref[0] = jnp.dot(cs, w_ref[0], preferred_element_type=F32, precision=HIGHEST) + b_ref[0]


def _ada_mod(c, ada_w, ada_b):
    depth, d, n = ada_w.shape
    b = c.shape[0]
    return pl.pallas_call(
        _ada_kernel, name="ada_mod",
        grid=(depth, n // d),
        in_specs=[pl.BlockSpec((b, d), lambda l, j: (0, 0)),
                  pl.BlockSpec((1, d, d), lambda l, j: (l, 0, j)),
                  pl.BlockSpec((1, 1, d), lambda l, j: (l, 0, j))],
        out_specs=pl.BlockSpec((1, b, d), lambda l, j: (l, 0, j)),
        out_shape=jax.ShapeDtypeStruct((depth, b, n), F32),
        compiler_params=_params(("parallel", "parallel")),
    )(c, ada_w, ada_b.reshape(depth, 1, n))


def _inproj_kernel(x_ref, sc_ref, sh_ref, g_ref, w_ref, o_ref):
    h = _rms(x_ref[...], g_ref[...]) * (1.0 + sc_ref[0]) + sh_ref[0]
    o_ref[...] = _dot(h, w_ref[...])


def _in_proj(x2, mod3, norm_g, w_pack, seq):
    t, d = x2.shape
    tm = min(512, seq)
    n_col = 3
    cw = N_PACK // n_col
    return pl.pallas_call(
        _inproj_kernel, name="in_proj",
        grid=(n_col, t // tm),
        in_specs=[pl.BlockSpec((tm, d), lambda j, i: (i, 0)),
                  pl.BlockSpec((1, 1, d), lambda j, i: ((i * tm) // seq, 0, 1)),
                  pl.BlockSpec((1, 1, d), lambda j, i: ((i * tm) // seq, 0, 0)),
                  pl.BlockSpec((1, d), lambda j, i: (0, 0)),
                  pl.BlockSpec((d, cw), lambda j, i: (0, j))],
        out_specs=pl.BlockSpec((tm, cw), lambda j, i: (i, j)),
        out_shape=jax.ShapeDtypeStruct((t, N_PACK), F32),
        compiler_params=_params(("parallel", "parallel")),
    )(x2, mod3, mod3, norm_g, w_pack)


def _gla_kernel(q_ref, k_ref, v_ref, g_ref, alr_ref, wa2_ref, ba_ref, gn_ref, o_ref, s_ref, acc_ref):
    L, sub, nh, dk, dv = GLA_CHUNK, GLA_SUB, GLA_HEADS, GLA_DK, GLA_DV

    @pl.when(pl.program_id(1) == 0)
    def _():
        s_ref[...] = jnp.zeros_like(s_ref)

    z = jnp.dot(alr_ref[0], wa2_ref[...], preferred_element_type=F32, precision=HIGHEST) + ba_ref[...]
    log_a = _log_sigmoid(z) * (1.0 / GLA_TAU)
    cum = _cumsum_rows(log_a)
    q = q_ref[0] * (dk ** -0.5)
    k = k_ref[0]
    vb = _mx(v_ref[0])
    tot = cum[L - 1:L, :]
    q_in = _mx(q * jnp.exp(cum))
    k_dec = _mx(k * jnp.exp(tot - cum))

    for i in range(L // sub):
        r0, r1 = i * sub, (i + 1) * sub
        base = cum[r0 - 1:r0, :] if i > 0 else jnp.zeros_like(tot)
        qi = _mx(q[r0:r1] * jnp.exp(cum[r0:r1] - base))
        ka = _mx(k[:r1] * jnp.exp(base - cum[:r1]))
        row = lax.broadcasted_iota(jnp.int32, (sub, r1), 0)
        col = lax.broadcasted_iota(jnp.int32, (sub, r1), 1)
        causal = col <= row + r0
        for h in range(nh):
            s = _dot_nt(qi[:, h * dk:(h + 1) * dk], ka[:, h * dk:(h + 1) * dk])
            s = jnp.where(causal, s, 0.0)
            acc_ref[r0:r1, h * dv:(h + 1) * dv] = _dot(s, vb[:r1, h * dv:(h + 1) * dv])

    g = g_ref[0]
    gn = gn_ref[...]
    for h in range(nh):
        ks, vs = slice(h * dk, (h + 1) * dk), slice(h * dv, (h + 1) * dv)
        st = s_ref[h]
        o = acc_ref[:, vs] + _dot(q_in[:, ks], st)
        y = _rms(o, gn[:, vs]) * _silu(g[:, vs])
        o_ref[0, :, vs] = y.astype(o_ref.dtype)
        decay = jnp.transpose(jnp.exp(tot[:, ks]))
        s_ref[h] = st * decay + _dot_tn(k_dec[:, ks], vb[:, vs])


def _gla(proj3, wa2_pad, ba, gn):
    b, s, _ = proj3.shape
    L = GLA_CHUNK
    blk = lambda w, c0: pl.BlockSpec((1, L, w), lambda bi, ci: (bi, ci, c0 // w))
    full = lambda a: pl.BlockSpec(a.shape, lambda bi, ci: (0,) * a.ndim)
    return pl.pallas_call(
        _gla_kernel, name="gla",
        grid=(b, s // L),
        in_specs=[blk(GLA_QK, C_QA), blk(GLA_QK, C_KA), blk(GLA_V, C_VA), blk(GLA_V, C_GA),
                  blk(LANE, C_ALR), full(wa2_pad), full(ba), full(gn)],
        out_specs=pl.BlockSpec((1, L, GLA_V), lambda bi, ci: (bi, ci, 0)),
        out_shape=jax.ShapeDtypeStruct((b, s, GLA_V), ACT_DTYPE),
        scratch_shapes=[pltpu.VMEM((GLA_HEADS, GLA_DK, GLA_DV), F32),
                        pltpu.VMEM((L, GLA_V), F32)],
        compiler_params=_params(("parallel", "arbitrary")),
    )(proj3, proj3, proj3, proj3, proj3, wa2_pad, ba, gn)


def _mlstm_kernel(qk_ref, v_ref, oc_ref, if_ref, conv_ref, bias_ref, gn_ref, o_ref,
                  xbuf_ref, c_ref, n_ref, m_ref):
    L, nh, dk, dv, kc = MLSTM_CHUNK, MLSTM_HEADS, MLSTM_DQK, MLSTM_DV, MLSTM_CONV
    pad = 8

    @pl.when(pl.program_id(1) == 0)
    def _():
        xbuf_ref[0:pad, :] = jnp.zeros((pad, 2 * MLSTM_QK), F32)
        c_ref[...] = jnp.zeros_like(c_ref)
        n_ref[...] = jnp.zeros_like(n_ref)
        m_ref[...] = jnp.zeros_like(m_ref)

    xbuf_ref[pad:pad + L, :] = qk_ref[0]
    cw = conv_ref[...]
    conv = jnp.zeros((L, 2 * MLSTM_QK), F32)
    for j in range(kc):
        conv = conv + cw[j:j + 1, :] * xbuf_ref[pl.ds(pad - (kc - 1) + j, L), :]
    xbuf_ref[0:pad, :] = xbuf_ref[L:L + pad, :]
    qk = _silu(conv)
    q = qk[:, :MLSTM_QK] * (dk ** -0.5)
    k = qk[:, MLSTM_QK:]
    v = v_ref[0]

    pre = if_ref[0] + bias_ref[...]
    bcum = _cumsum_rows(_log_sigmoid(pre))
    pre_t = jnp.transpose(pre)
    bcum_t = jnp.transpose(bcum)
    row = lax.broadcasted_iota(jnp.int32, (L, L), 0)
    col = lax.broadcasted_iota(jnp.int32, (L, L), 1)
    causal = col <= row
    oc = oc_ref[0]
    gn = gn_ref[...]

    for h in range(nh):
        ks, vs = slice(h * dk, (h + 1) * dk), slice(h * dv, (h + 1) * dv)
        b_col = bcum[:, nh + h:nh + h + 1]
        b_row = bcum_t[nh + h:nh + h + 1, :]
        i_col = pre[:, h:h + 1]
        i_row = pre_t[h:h + 1, :]
        m_prev = m_ref[h]
        c_st = c_ref[h]
        n_st = n_ref[h]
        qh, kh, vh = q[:, ks], k[:, ks], v[:, vs]

        dlog = jnp.where(causal, b_col - b_row + i_row, -jnp.inf)
        inter_log = b_col + m_prev
        m_t = jnp.maximum(inter_log, jnp.max(dlog, axis=1, keepdims=True))
        w_intra = jnp.exp(dlog - m_t)
        w_inter = jnp.exp(inter_log - m_t)
        s = _dot_nt(qh, kh) * w_intra
        num = _dot(s, vh) + w_inter * _dot(qh, c_st)
        den = jnp.sum(s, axis=1, keepdims=True) + w_inter * jnp.sum(qh * n_st, axis=1, keepdims=True)
        hout = num / jnp.maximum(jnp.abs(den), jnp.exp(-m_t))
        y = _rms(hout, gn[:, vs]) * jax.nn.sigmoid(oc[:, vs])
        o_ref[0, :, vs] = y.astype(o_ref.dtype)

        tot = b_col[L - 1:L, :]
        g_log = tot - b_col + i_col
        m_new = jnp.maximum(tot + m_prev, jnp.max(g_log, axis=0, keepdims=True))
        w_s = jnp.exp(g_log - m_new)
        w_c = jnp.exp(tot + m_prev - m_new)
        k_s = kh * w_s
        c_ref[h] = c_st * w_c + _dot_tn(k_s, vh)
        n_ref[h] = n_st * w_c + jnp.sum(k_s, axis=0, keepdims=True)
        m_ref[h] = m_new


def _mlstm(proj3, conv_w, bias_row, gn):
    b, s, _ = proj3.shape
    L = MLSTM_CHUNK
    blk = lambda w, c0: pl.BlockSpec((1, L, w), lambda bi, ci: (bi, ci, c0 // w))
    full = lambda a: pl.BlockSpec(a.shape, lambda bi, ci: (0,) * a.ndim)
    return pl.pallas_call(
        _mlstm_kernel, name="mlstm",
        grid=(b, s // L),
        in_specs=[blk(2 * MLSTM_QK, C_QKC), blk(MLSTM_V, C_VC), blk(MLSTM_V, C_OC), blk(LANE, C_ICFC),
                  full(conv_w), full(bias_row), full(gn)],
        out_specs=pl.BlockSpec((1, L, MLSTM_V), lambda bi, ci: (bi, ci, 0)),
        out_shape=jax.ShapeDtypeStruct((b, s, MLSTM_V), ACT_DTYPE),
        scratch_shapes=[pltpu.VMEM((L + 8, 2 * MLSTM_QK), F32),
                        pltpu.VMEM((MLSTM_HEADS, MLSTM_DQK, MLSTM_DV), F32),
                        pltpu.VMEM((MLSTM_HEADS, 1, MLSTM_DQK), F32),
                        pltpu.VMEM((MLSTM_HEADS, 1, 1), F32)],
        compiler_params=_params(("parallel", "arbitrary")),
    )(proj3, proj3, proj3, proj3, conv_w, bias_row, gn)


def _dsa_prep_kernel(cq_ref, ckv_ref, kw_ref, nq_ref, nkv_ref, wuq_ref, wuk_ref, wqi_ref,
                     ckvn_ref, kwb_ref, qlat_ref, qidx_ref, wht_ref, *, tiles_per_seq):
    tm = cq_ref.shape[0]
    r = DSA_KV_RANK
    cqn = _mx(_rms(cq_ref[...], nq_ref[...]))
    pos = (pl.program_id(0) % tiles_per_seq) * tm + lax.broadcasted_iota(jnp.int32, (tm, LANE), 0)
    lane = lax.broadcasted_iota(jnp.int32, (tm, LANE), 1)
    pos_cols = jnp.where(lane < 3, pos >> 6, jnp.where(lane < 6, pos & 63, 0)).astype(F32)
    ckvn_ref[:, :r] = _rms(ckv_ref[...], nkv_ref[...]).astype(ckvn_ref.dtype)
    ckvn_ref[:, r:] = pos_cols.astype(ckvn_ref.dtype)
    kw = kw_ref[...]
    kwb_ref[...] = kw.astype(kwb_ref.dtype)
    wht_ref[0] = jnp.transpose(kw)[W_IDX_LANE:W_IDX_LANE + IDX_HEADS, :] * (IDX_HEADS ** -0.5)
    q = jnp.dot(cqn, wuq_ref[...], preferred_element_type=F32)
    for h in range(DSA_HEADS):
        ql = _dot(q[:, h * DSA_HEAD_DIM:(h + 1) * DSA_HEAD_DIM], wuk_ref[h]) * (DSA_HEAD_DIM ** -0.5 * LOG2E)
        qlat_ref[0, h, :, :r] = ql.astype(qlat_ref.dtype)
        c = _bf16_pieces(2.0 ** (-8.0 * (h + 1) / DSA_HEADS) * LOG2E, 3)
        consts = [64.0 * c[0], 64.0 * c[1], 64.0 * c[2], c[0], c[1], c[2]]
        slope_cols = jnp.zeros((tm, LANE), F32)
        for j, v in enumerate(consts):
            slope_cols = jnp.where(lane == j, v, slope_cols)
        qlat_ref[0, h, :, r:] = slope_cols.astype(qlat_ref.dtype)
    qi = jnp.dot(cqn, wqi_ref[...], preferred_element_type=F32) * (IDX_DIM ** -0.5)
    for h in range(IDX_HEADS):
        qidx_ref[0, h] = qi[:, h * LANE:(h + 1) * LANE].astype(qidx_ref.dtype)


def _dsa_prep(proj2, seq, nq, nkv, wuq, wuk_t, wqi_pad):
    t = proj2.shape[0]
    b = t // seq
    tm = min(512, seq)
    per = seq // tm
    blk = lambda w, c0: pl.BlockSpec((tm, w), lambda i: (i, c0 // w))
    full = lambda a: pl.BlockSpec(a.shape, lambda i: (0,) * a.ndim)
    hmap = lambda i: (i // per, 0, i % per, 0)
    return pl.pallas_call(
        functools.partial(_dsa_prep_kernel, tiles_per_seq=per), name="dsa_prep",
        grid=(t // tm,),
        in_specs=[blk(DSA_Q_RANK, C_CQ), blk(DSA_KV_RANK, C_CKV), blk(LANE, C_KW),
                  full(nq), full(nkv), full(wuq), full(wuk_t), full(wqi_pad)],
        out_specs=[pl.BlockSpec((tm, DSA_AUG), lambda i: (i, 0)),
                   pl.BlockSpec((tm, LANE), lambda i: (i, 0)),
                   pl.BlockSpec((1, DSA_HEADS, tm, DSA_AUG), hmap),
                   pl.BlockSpec((1, IDX_HEADS, tm, LANE), hmap),
                   pl.BlockSpec((1, IDX_HEADS, tm), lambda i: (i // per, 0, i % per))],
        out_shape=[jax.ShapeDtypeStruct((t, DSA_AUG), ACT_DTYPE),
                   jax.ShapeDtypeStruct((t, LANE), ACT_DTYPE),
                   jax.ShapeDtypeStruct((b, DSA_HEADS, seq, DSA_AUG), ACT_DTYPE),
                   jax.ShapeDtypeStruct((b, IDX_HEADS, seq, LANE), ACT_DTYPE),
                   jax.ShapeDtypeStruct((b, IDX_HEADS, seq), F32)],
        compiler_params=_params(("parallel",)),
    )(proj2, proj2, proj2, nq, nkv, wuq, wuk_t, wqi_pad)


def _dsa_kernel(qlat_ref, qidx_ref, wht_ref, kwk_ref, ckv_ref, wuv_ref, o_ref,
                ibuf_ref, acc_ref, *, topk):
    nh, tq, tk = DSA_HEADS, Q_TILE, KEY_TILE
    qb = pl.program_id(1)
    n_kt = (qb * tq + tq + tk - 1) // tk
    t_row = qb * tq + lax.broadcasted_iota(jnp.int32, (1, tq), 1)
    s_col = lax.broadcasted_iota(jnp.int32, (tk, 1), 0)

    qi = qidx_ref[0].reshape(IDX_HEADS * tq, LANE)
    wht = wht_ref[0]

    def idx_body(kt, carry):
        kk = kwk_ref[0, pl.ds(pl.multiple_of(kt * tk, tk), tk), :]
        sc = jnp.maximum(_dot_nt(kk, qi), 0.0)
        tot = wht[0:1, :] * sc[:, 0:tq]
        for h in range(1, IDX_HEADS):
            tot = tot + wht[h:h + 1, :] * sc[:, h * tq:(h + 1) * tq]
        ibuf_ref[kt] = jnp.where(kt * tk + s_col <= t_row, tot, -jnp.inf)
        return carry

    lax.fori_loop(0, n_kt, idx_body, 0)

    def count(pred):
        def body(kt, c):
            hit = jnp.where(pred(ibuf_ref[kt]), 1.0, 0.0)
            return c + jnp.sum(hit.reshape(tk // 32, 32, tq), axis=0)
        return jnp.sum(lax.fori_loop(0, n_kt, body, jnp.zeros((32, tq), F32)), axis=0, keepdims=True)

    def count_ge(cand):
        return count(lambda x: x >= cand)

    def count_gt(cand):
        return count(lambda x: x > cand)

    def key_to_float(u):
        key = u ^ jnp.int32(-2 ** 31)
        bits = jnp.where(key >= 0, key, key ^ jnp.int32(0x7FFFFFFF))
        return lax.bitcast_convert_type(bits, F32)

    few = t_row < topk
    n_bits = 32

    def search_cond(st):
        i, _, _, pending = st
        return (i < n_bits) & (pending > 0)

    def search_body(st):
        i, u, cnt_u, _ = st
        for _ in range(4):
            cand_u = u | lax.shift_left(jnp.int32(1), n_bits - 1 - i)
            cnt = count_ge(key_to_float(cand_u))
            ok = cnt >= topk
            u = jnp.where(ok, cand_u, u)
            cnt_u = jnp.where(ok, cnt, cnt_u)
            i = i + 1
        pending = jnp.max(jnp.where(few | (cnt_u == topk), 0, 1))
        return i, u, cnt_u, pending

    start = (jnp.int32(0), jnp.zeros((1, tq), jnp.int32), jnp.full((1, tq), float(2 ** 30), F32),
             (n_kt * tk > topk).astype(jnp.int32))
    _, u, _, _ = lax.while_loop(search_cond, search_body, start)
    tau = jnp.where(few, -jnp.inf, key_to_float(u))
    need = topk - count_gt(tau)

    acc_ref[...] = jnp.zeros_like(acc_ref)
    ql = qlat_ref[0].reshape(nh * tq, DSA_AUG)
    tri = jnp.where(lax.broadcasted_iota(jnp.int32, (tk, tk), 0) >= lax.broadcasted_iota(jnp.int32, (tk, tk), 1),
                    1.0, 0.0).astype(BF16)

    def keys(kt):
        return ckv_ref[0, pl.ds(pl.multiple_of(kt * tk, tk), tk), :]

    def att_body(kt, carry):
        m, l, eq_seen = carry
        kv_aug = keys(kt)
        lg = _dot_nt(kv_aug, ql)
        it = ibuf_ref[kt]
        eq = it == tau
        eqf = jnp.where(eq, 1.0, 0.0)
        rank = jnp.dot(tri, eqf.astype(BF16), preferred_element_type=F32) + eq_seen
        valid = ((it > tau) | (eq & (rank <= need))) & (kt * tk + s_col <= t_row)
        bias = jnp.where(valid, 0.0, NEG_BIG)
        ms, ls, als, ps = [], [], [], []
        for h in range(nh):
            cols = slice(h * tq, (h + 1) * tq)
            lh = lg[:, cols] + bias
            m_old = m[:, cols]
            m_new = jnp.maximum(m_old, _colreduce(lh, jnp.max))
            p = jnp.exp2(lh - m_new)
            alpha = jnp.exp2(m_old - m_new)
            ms.append(m_new)
            ls.append(alpha * l[:, cols] + _colreduce(p, jnp.sum))
            als.append(alpha)
            ps.append(_mx(p))
        acc_ref[...] = (jnp.concatenate(als, axis=1) * acc_ref[...]
                        + _dot_tn(kv_aug[:, :DSA_KV_RANK], jnp.concatenate(ps, axis=1)))
        return (jnp.concatenate(ms, axis=1), jnp.concatenate(ls, axis=1),
                eq_seen + jnp.sum(eqf, axis=0, keepdims=True))

    init = (jnp.full((1, nh * tq), NEG_BIG, F32), jnp.zeros((1, nh * tq), F32), jnp.zeros((1, tq), F32))
    _, l, _ = lax.fori_loop(0, n_kt, att_body, init)

    outs = []
    for h in range(nh):
        cols = slice(h * tq, (h + 1) * tq)
        outs.append(_dot(wuv_ref[h], acc_ref[:, cols] / l[:, cols]))
    o_ref[0] = jnp.transpose(jnp.concatenate(outs, axis=0)).astype(o_ref.dtype)


def _dsa(ckvn3, kwb3, qlat, qidx, wht, wuv_t):
    b, s, _ = ckvn3.shape
    topk = min(DSA_TOPK_MAX, s // 4)
    nh, tq = DSA_HEADS, Q_TILE
    return pl.pallas_call(
        functools.partial(_dsa_kernel, topk=topk), name="dsa_attn",
        grid=(b, s // tq),
        in_specs=[pl.BlockSpec((1, nh, tq, DSA_AUG), lambda bi, qi: (bi, 0, qi, 0)),
                  pl.BlockSpec((1, IDX_HEADS, tq, LANE), lambda bi, qi: (bi, 0, qi, 0)),
                  pl.BlockSpec((1, IDX_HEADS, tq), lambda bi, qi: (bi, 0, qi)),
                  pl.BlockSpec((1, s, LANE), lambda bi, qi: (bi, 0, 0)),
                  pl.BlockSpec((1, s, DSA_AUG), lambda bi, qi: (bi, 0, 0)),
                  pl.BlockSpec(wuv_t.shape, lambda bi, qi: (0, 0, 0))],
        out_specs=pl.BlockSpec((1, tq, DSA_V), lambda bi, qi: (bi, qi, 0)),
        out_shape=jax.ShapeDtypeStruct((b, s, DSA_V), ACT_DTYPE),
        scratch_shapes=[pltpu.VMEM((s // KEY_TILE, KEY_TILE, tq), F32),
                        pltpu.VMEM((DSA_KV_RANK, nh * tq), F32)],
        compiler_params=_params(("parallel", "arbitrary")),
    )(qlat, qidx, wht, kwb3, ckvn3, wuv_t)


def _merge_kernel(a_ref, b_ref, c_ref, gt_ref, x_ref, g1_ref, wa_ref, wb_ref, wc_ref, wo_ref, o_ref):
    d = D_MODEL
    g = jax.nn.sigmoid(gt_ref[...])
    ya = jnp.dot(a_ref[...], wa_ref[...], preferred_element_type=F32)
    yb = jnp.dot(b_ref[...], wb_ref[...], preferred_element_type=F32)
    yc = jnp.dot(c_ref[...], wc_ref[...], preferred_element_type=F32)
    m = g[:, :d] * ya + g[:, d:2 * d] * yb + g[:, 2 * d:] * yc
    o_ref[...] = x_ref[...] + g1_ref[0] * _dot(m, wo_ref[...])


def _merge(ya_in, yb_in, yc_in, proj2, x2, mod3, wa, wb, wc, wo, seq):
    t, d = x2.shape
    tm = min(512, seq)
    full = lambda a: pl.BlockSpec(a.shape, lambda i: (0,) * a.ndim)
    br = lambda w: pl.BlockSpec((tm, w), lambda i: (i, 0))
    return pl.pallas_call(
        _merge_kernel, name="merge",
        grid=(t // tm,),
        in_specs=[br(GLA_V), br(DSA_V), br(MLSTM_V), br(3 * d), br(d),
                  pl.BlockSpec((1, 1, d), lambda i: ((i * tm) // seq, 0, 2)),
                  full(wa), full(wb), full(wc), full(wo)],
        out_specs=br(d),
        out_shape=jax.ShapeDtypeStruct((t, d), F32),
        compiler_params=_params(("parallel",)),
    )(ya_in, yb_in, yc_in, proj2, x2, mod3, wa, wb, wc, wo)


def _first_argmax_mask(cur, iota, axis, n):
    mx = jnp.max(cur, axis=axis, keepdims=True)
    ix = jnp.min(jnp.where(cur == mx, iota, n), axis=axis, keepdims=True)
    return iota == ix


def _router_kernel(x_ref, sc_ref, sh_ref, g_ref, rwt_ref, rb_ref, dest_ref, wgt_ref, starts_ref, plens_ref):
    ne, ng = N_EXPERTS, N_GROUPS
    eg = ne // ng
    h = _rms(x_ref[...], g_ref[...]) * (1.0 + sc_ref[0]) + sh_ref[0]
    tm = h.shape[0]
    logits = lax.dot_general(rwt_ref[...], h, (((1,), (1,)), ((), ())),
                             preferred_element_type=F32, precision=HIGHEST)
    scores = jax.nn.sigmoid(logits)
    sel = scores + rb_ref[...]
    s3 = sel.reshape(ng, eg, tm)
    io3 = lax.broadcasted_iota(jnp.int32, (ng, eg, tm), 1)
    m1 = jnp.max(s3, axis=1, keepdims=True)
    first = _first_argmax_mask(s3, io3, 1, eg)
    m2 = jnp.max(jnp.where(first, -jnp.inf, s3), axis=1, keepdims=True)
    gs = (m1 + m2).reshape(ng, tm)
    iog = lax.broadcasted_iota(jnp.int32, (ng, tm), 0)
    gkeep = jnp.zeros((ng, tm), F32)
    cur = gs
    for _ in range(TOPK_GROUPS):
        hit = _first_argmax_mask(cur, iog, 0, ng)
        gkeep = jnp.where(hit, 1.0, gkeep)
        cur = jnp.where(hit, -jnp.inf, cur)
    selm = jnp.where(gkeep.reshape(ng, 1, tm) > 0.0, s3, -jnp.inf).reshape(ne, tm)
    ioe = lax.broadcasted_iota(jnp.int32, (ne, tm), 0)
    hits = []
    chosen = jnp.zeros((ne, tm), F32)
    cur = selm
    for _ in range(TOP_K):
        hit = _first_argmax_mask(cur, ioe, 0, ne)
        hits.append(hit)
        chosen = jnp.where(hit, 1.0, chosen)
        cur = jnp.where(hit, -jnp.inf, cur)
    w = chosen * scores
    w = w / jnp.sum(w, axis=0, keepdims=True) * ROUTED_SCALE

    cnt = jnp.sum(chosen, axis=1, keepdims=True)
    plen = jnp.ceil(cnt * (1.0 / ROW_ALIGN)) * ROW_ALIGN
    start = _cumsum_rows(plen) - plen
    before = (lax.broadcasted_iota(jnp.int32, (tm, tm), 0)
              < lax.broadcasted_iota(jnp.int32, (tm, tm), 1)).astype(BF16)
    rank = jnp.dot(chosen.astype(BF16), before, preferred_element_type=F32)
    row_of = start + rank
    pad_rows = SLOT_ROWS - TOP_K
    dest = [jnp.sum(jnp.where(hit, row_of, 0.0), axis=0, keepdims=True) for hit in hits]
    wsel = [jnp.sum(jnp.where(hit, w, 0.0), axis=0, keepdims=True) for hit in hits]
    dest_ref[...] = jnp.concatenate(dest + [jnp.full((pad_rows, tm), -1.0, F32)], axis=0).astype(jnp.int32)
    wgt_ref[...] = jnp.concatenate(wsel + [jnp.zeros((pad_rows, tm), F32)], axis=0)
    starts_ref[0] = start.astype(jnp.int32)
    plens_ref[0] = plen.astype(jnp.int32)


def _router(x2, mod3, norm_g, rw_t, rb_col, seq):
    t, d = x2.shape
    tm = MOE_SUB
    nsb = t // tm
    return pl.pallas_call(
        _router_kernel, name="router",
        grid=(nsb,),
        in_specs=[pl.BlockSpec((tm, d), lambda i: (i, 0)),
                  pl.BlockSpec((1, 1, d), lambda i: ((i * tm) // seq, 0, 4)),
                  pl.BlockSpec((1, 1, d), lambda i: ((i * tm) // seq, 0, 3)),
                  pl.BlockSpec((1, d), lambda i: (0, 0)),
                  pl.BlockSpec(rw_t.shape, lambda i: (0, 0)),
                  pl.BlockSpec(rb_col.shape, lambda i: (0, 0))],
        out_specs=[pl.BlockSpec((SLOT_ROWS, tm), lambda i: (0, i)),
                   pl.BlockSpec((SLOT_ROWS, tm), lambda i: (0, i)),
                   pl.BlockSpec((1, N_EXPERTS, 1), lambda i: (i, 0, 0)),
                   pl.BlockSpec((1, N_EXPERTS, 1), lambda i: (i, 0, 0))],
        out_shape=[jax.ShapeDtypeStruct((SLOT_ROWS, t), jnp.int32),
                   jax.ShapeDtypeStruct((SLOT_ROWS, t), F32),
                   jax.ShapeDtypeStruct((nsb, N_EXPERTS, 1), jnp.int32),
                   jax.ShapeDtypeStruct((nsb, N_EXPERTS, 1), jnp.int32)],
        compiler_params=_params(("parallel",)),
    )(x2, mod3, mod3, norm_g, rw_t, rb_col)


def _moe_kernel(starts_ref, plens_ref, x_ref, sc_ref, sh_ref, g2_ref, gn_ref, dest_ref, wgt_ref,
                wg_ref, wu_ref, wd_ref, sg_ref, su_ref, sd_ref, nf_ref, o_ref,
                h_ref, acc_ref, xs_ref, stage_ref, *, final):
    blk, e = pl.program_id(0), pl.program_id(1)
    ne = pl.num_programs(1)
    sub, rt, mt = MOE_SUB, MOE_ROW_TILE, MOE_FFN_TILE
    nsub = x_ref.shape[0] // sub
    rmax = xs_ref.shape[1]

    @pl.when(e == 0)
    def _():
        h = _mx(_rms(x_ref[...], gn_ref[...]) * (1.0 + sc_ref[0]) + sh_ref[0])
        h_ref[...] = h
        acc_ref[...] = _dot(_silu(_dot(h, sg_ref[...])) * _dot(h, su_ref[...]), sd_ref[...])
        stage_ref[...] = jnp.zeros_like(stage_ref)
        for sb in range(nsub):
            dest = dest_ref[:, sb * sub:(sb + 1) * sub]
            hs = h_ref[sb * sub:(sb + 1) * sub, :]
            for r in range(rmax // rt):
                rows = r * rt + lax.broadcasted_iota(jnp.int32, (rt, sub), 0)
                pick = jnp.zeros((rt, sub), F32)
                for j in range(TOP_K):
                    pick = jnp.where(dest[j:j + 1, :] == rows, 1.0, pick)
                xs_ref[sb, r * rt:(r + 1) * rt, :] = _dot(pick, hs).astype(xs_ref.dtype)

    def run(sb):
        i = (blk * nsub + sb) * ne + e
        return starts_ref[i], plens_ref[i] // ROW_ALIGN

    def copy_rows(src, src0, dst, dst0, n_chunks):
        def body(k, carry):
            s = pl.multiple_of(src0 + k * ROW_ALIGN, ROW_ALIGN)
            d = pl.multiple_of(dst0 + k * ROW_ALIGN, ROW_ALIGN)
            dst[pl.ds(d, ROW_ALIGN), :] = src[pl.ds(s, ROW_ALIGN), :]
            return carry
        lax.fori_loop(0, n_chunks, body, 0)

    cursor = 0
    placed = []
    for sb in range(nsub):
        st, nch = run(sb)
        copy_rows(xs_ref.at[sb], st, stage_ref, cursor, nch)
        placed.append((st, nch, cursor))
        cursor = cursor + nch * ROW_ALIGN

    def ffn(i, carry):
        r0 = pl.multiple_of(i * mt, mt)
        xt = stage_ref[pl.ds(r0, mt), :]
        y = _dot(_silu(_dot(xt, wg_ref[0])) * _dot(xt, wu_ref[0]), wd_ref[0])
        stage_ref[pl.ds(r0, mt), :] = y.astype(stage_ref.dtype)
        return carry

    lax.fori_loop(0, (cursor + mt - 1) // mt, ffn, 0)

    for sb, (st, nch, at) in enumerate(placed):
        copy_rows(stage_ref, at, xs_ref.at[sb], st, nch)

    @pl.when(e == ne - 1)
    def _():
        for sb in range(nsub):
            tok = slice(sb * sub, (sb + 1) * sub)
            dest_t = jnp.transpose(dest_ref[:, tok].astype(F32))
            wgt_t = jnp.transpose(wgt_ref[:, tok])
            routed = jnp.zeros((sub, x_ref.shape[1]), F32)
            for r in range(rmax // rt):
                cols = (r * rt + lax.broadcasted_iota(jnp.int32, (sub, rt), 1)).astype(F32)
                mix = jnp.zeros((sub, rt), F32)
                for j in range(TOP_K):
                    mix = jnp.where(dest_t[:, j:j + 1] == cols, wgt_t[:, j:j + 1], mix)
                routed = routed + _dot(mix, xs_ref[sb, r * rt:(r + 1) * rt, :])
            xo = x_ref[tok, :] + g2_ref[0] * (acc_ref[tok, :] + routed)
            if final:
                xo = _rms(xo, nf_ref[...])
            o_ref[tok, :] = xo


def _round_up(n, m):
    return (n + m - 1) // m * m


def _moe(x2, mod3, norm_g, dest, wgt, starts, plens, wg, wu, wd, sg, su, sd, nf, seq, final):
    t, d = x2.shape
    tm = min(MOE_BLOCK, seq)
    nsub = tm // MOE_SUB
    ne, ff = wg.shape[0], wg.shape[2]
    rmax = _round_up(TOP_K * MOE_SUB + ne * (ROW_ALIGN - 1), MOE_ROW_TILE)
    stage_rows = _round_up(tm + nsub * (ROW_ALIGN - 1), MOE_FFN_TILE)
    full = lambda a: pl.BlockSpec(a.shape, lambda i, e, *_: (0,) * a.ndim)
    mod = lambda j: pl.BlockSpec((1, 1, d), lambda i, e, *_: ((i * tm) // seq, 0, j))
    slot = pl.BlockSpec((SLOT_ROWS, tm), lambda i, e, *_: (0, i))
    grid_spec = pltpu.PrefetchScalarGridSpec(
        num_scalar_prefetch=2,
        grid=(t // tm, ne),
        in_specs=[pl.BlockSpec((tm, d), lambda i, e, *_: (i, 0)), mod(4), mod(3), mod(5),
                  pl.BlockSpec((1, d), lambda i, e, *_: (0, 0)), slot, slot,
                  pl.BlockSpec((1, d, ff), lambda i, e, *_: (e, 0, 0)),
                  pl.BlockSpec((1, d, ff), lambda i, e, *_: (e, 0, 0)),
                  pl.BlockSpec((1, ff, d), lambda i, e, *_: (e, 0, 0)),
                  full(sg), full(su), full(sd), full(nf)],
        out_specs=pl.BlockSpec((tm, d), lambda i, e, *_: (i, 0)),
        scratch_shapes=[pltpu.VMEM((tm, d), MXU_DTYPE), pltpu.VMEM((tm, d), F32),
                        pltpu.VMEM((nsub, rmax, d), MXU_DTYPE), pltpu.VMEM((stage_rows, d), MXU_DTYPE)])
    return pl.pallas_call(
        functools.partial(_moe_kernel, final=final), name="moe",
        grid_spec=grid_spec,
        out_shape=jax.ShapeDtypeStruct((t, d), F32),
        compiler_params=_params(("parallel", "arbitrary"), vmem_mb=56),
    )(starts.reshape(-1), plens.reshape(-1), x2, mod3, mod3, mod3, norm_g, dest, wgt,
      wg, wu, wd, sg, su, sd, nf)


def _pack_w_in(w):
    d = w.shape[0]
    offs = [0]
    for n in IN_SIZES:
        offs.append(offs[-1] + n)
    (qa, ka, va, ga, alr, cq, ckv, kidx, widx, qc, kc, vc, ic, fc, oc, gates) = [
        w[:, offs[i]:offs[i + 1]] for i in range(len(IN_SIZES))]
    z = lambda n: jnp.zeros((d, n), w.dtype)
    packed = jnp.concatenate(
        [gates, qa, ka, va, ga, vc, oc, qc, kc, cq, ckv,
         kidx, widx, z(LANE - IDX_DIM - IDX_HEADS),
         alr, z(LANE - GLA_GATE_RANK),
         ic, fc, z(LANE - 2 * MLSTM_HEADS)], axis=1)
    assert packed.shape[1] == N_PACK
    return packed.astype(MXU_DTYPE)


def kernel(x, c, ada_w, ada_b, norm_mix, norm_ffn, w_in, gla_w_a2, gla_b_a, gla_norm, dsa_norm_q,
           dsa_norm_kv, dsa_w_uq, dsa_w_uk, dsa_w_uv, dsa_w_qi, mlstm_conv, mlstm_b_i, mlstm_b_f,
           mlstm_norm, w_up_a, w_up_b, w_up_c, w_o, router_w, router_bias, exp_w_gate, exp_w_up,
           exp_w_down, sh_w_gate, sh_w_up, sh_w_down, norm_final):
    b, s, d = x.shape
    depth = ada_w.shape[0]
    t = b * s
    mod = _ada_mod(c, ada_w, ada_b)
    x2 = x.reshape(t, d)
    row = lambda v: v.reshape(1, -1)
    for l in range(depth):
        mod3 = mod[l].reshape(b, 1, 6 * d)
        proj2 = _in_proj(x2, mod3, row(norm_mix[l]), _pack_w_in(w_in[l]), s)
        proj3 = proj2.reshape(b, s, N_PACK)

        wa2_pad = jnp.zeros((LANE, GLA_QK), F32).at[:GLA_GATE_RANK].set(gla_w_a2[l])
        ya_in = _gla(proj3, wa2_pad, row(gla_b_a[l]), row(gla_norm[l]))

        bias_row = jnp.zeros((1, LANE), F32).at[0, :MLSTM_HEADS].set(mlstm_b_i[l])
        bias_row = bias_row.at[0, MLSTM_HEADS:2 * MLSTM_HEADS].set(mlstm_b_f[l])
        yc_in = _mlstm(proj3, mlstm_conv[l], bias_row, row(mlstm_norm[l]))

        wuq = dsa_w_uq[l].reshape(DSA_Q_RANK, DSA_HEADS * DSA_HEAD_DIM).astype(MXU_DTYPE)
        wuk_t = jnp.transpose(dsa_w_uk[l], (1, 2, 0)).astype(MXU_DTYPE)
        wuv_t = jnp.transpose(dsa_w_uv[l], (1, 2, 0)).astype(MXU_DTYPE)
        wqi_pad = jnp.zeros((DSA_Q_RANK, IDX_HEADS, LANE), F32).at[:, :, :IDX_DIM].set(dsa_w_qi[l])
        wqi_pad = wqi_pad.reshape(DSA_Q_RANK, IDX_HEADS * LANE).astype(MXU_DTYPE)
        ckvn, kwb, qlat, qidx, wht = _dsa_prep(proj2, s, row(dsa_norm_q[l]), row(dsa_norm_kv[l]),
                                               wuq, wuk_t, wqi_pad)
        yb_in = _dsa(ckvn.reshape(b, s, DSA_AUG), kwb.reshape(b, s, LANE), qlat, qidx, wht, wuv_t)

        x2 = _merge(ya_in.reshape(t, GLA_V), yb_in.reshape(t, DSA_V), yc_in.reshape(t, MLSTM_V),
                    proj2, x2, mod3, w_up_a[l].astype(MXU_DTYPE), w_up_b[l].astype(MXU_DTYPE),
                    w_up_c[l].astype(MXU_DTYPE), w_o[l].astype(MXU_DTYPE), s)

        dest, wgt, starts, plens = _router(x2, mod3, row(norm_ffn[l]), jnp.transpose(router_w[l]),
                                           router_bias[l].reshape(-1, 1), s)
        x2 = _moe(x2, mod3, row(norm_ffn[l]), dest, wgt, starts, plens,
                  exp_w_gate[l].astype(MXU_DTYPE), exp_w_up[l].astype(MXU_DTYPE),
                  exp_w_down[l].astype(MXU_DTYPE), sh_w_gate[l].astype(MXU_DTYPE),
                  sh_w_up[l].astype(MXU_DTYPE), sh_w_down[l].astype(MXU_DTYPE), row(norm_final), s,
                  final=(l == depth - 1))
    return x2.reshape(b, s, d)
```

```python
import functools
import struct

import jax
import jax.numpy as jnp
from jax import lax
from jax.experimental import pallas as pl
from jax.experimental.pallas import tpu as pltpu

F32 = jnp.float32
BF16 = jnp.bfloat16
MXU_DTYPE = jnp.bfloat16
ACT_DTYPE = jnp.bfloat16
HIGHEST = lax.Precision.HIGHEST

EPS = 1e-6
D_MODEL = 1024
GLA_HEADS, GLA_DK, GLA_DV, GLA_GATE_RANK, GLA_TAU, GLA_CHUNK = 4, 64, 128, 16, 16.0, 64
GLA_SUB = 16
DSA_HEADS, DSA_Q_RANK, DSA_KV_RANK, DSA_HEAD_DIM, DSA_V_DIM = 8, 256, 128, 64, 64
IDX_HEADS, IDX_DIM, DSA_TOPK_MAX = 8, 32, 256
MLSTM_HEADS, MLSTM_DQK, MLSTM_DV, MLSTM_CONV, MLSTM_CHUNK = 4, 64, 128, 4, 64
N_EXPERTS, TOP_K, N_GROUPS, TOPK_GROUPS, EXPERT_FF, SHARED_FF = 64, 6, 8, 4, 256, 256
ROUTED_SCALE = 2.5

GLA_QK = GLA_HEADS * GLA_DK
GLA_V = GLA_HEADS * GLA_DV
DSA_V = DSA_HEADS * DSA_V_DIM
MLSTM_QK = MLSTM_HEADS * MLSTM_DQK
MLSTM_V = MLSTM_HEADS * MLSTM_DV
IN_SIZES = (GLA_QK, GLA_QK, GLA_V, GLA_V, GLA_GATE_RANK,
            DSA_Q_RANK, DSA_KV_RANK, IDX_DIM, IDX_HEADS,
            MLSTM_QK, MLSTM_QK, MLSTM_V, MLSTM_HEADS, MLSTM_HEADS, MLSTM_V,
            3 * D_MODEL)

LANE = 128
KEY_TILE = 256
Q_TILE = 128
NEG_BIG = -1e30
MOE_BLOCK = 1024
MOE_SUB = 256
MOE_ROW_TILE = 256
MOE_FFN_TILE = 128
CONV_PAD = 8
SEQ_GROUP = 4
ROW_ALIGN = 16
SLOT_ROWS = 8

C_GATES = 0
C_QA = 3072
C_KA = 3328
C_VA = 3584
C_GA = 4096
C_VC = 4608
C_OC = 5120
C_QKC = 5632
C_CQ = 6144
C_CKV = 6400
C_KW = 6528
C_ALR = 6656
C_ICFC = 6784
N_PACK = 6912
W_IDX_LANE = IDX_DIM


LOG2E = 1.4426950408889634
DSA_AUG = DSA_KV_RANK + LANE


def _bf16_pieces(x, n):
    out = []
    for _ in range(n):
        bits = struct.unpack("<I", struct.pack("<f", x))[0]
        bits = (bits + 0x7FFF + ((bits >> 16) & 1)) & 0xFFFF0000
        piece = struct.unpack("<f", struct.pack("<I", bits))[0]
        out.append(piece)
        x -= piece
    return out


def _mx(x):
    return x.astype(MXU_DTYPE)


def _dot(a, b):
    return jnp.dot(_mx(a), _mx(b), preferred_element_type=F32)


def _dot_nt(a, b):
    return lax.dot_general(_mx(a), _mx(b), (((1,), (1,)), ((), ())), preferred_element_type=F32)


def _dot_tn(a, b):
    return lax.dot_general(_mx(a), _mx(b), (((0,), (0,)), ((), ())), preferred_element_type=F32)


def _rms(x, g):
    return x * lax.rsqrt(jnp.mean(x * x, axis=-1, keepdims=True) + EPS) * g


def _silu(x):
    return x * jax.nn.sigmoid(x)


def _log_sigmoid(z):
    return jnp.minimum(z, 0.0) - jnp.log1p(jnp.exp(-jnp.abs(z)))


def _cumsum_rows(x):
    n = x.shape[0]
    tri = (lax.broadcasted_iota(jnp.int32, (n, n), 1) <= lax.broadcasted_iota(jnp.int32, (n, n), 0)).astype(F32)
    return jnp.dot(tri, x, preferred_element_type=F32, precision=HIGHEST)


def _colreduce(x, op, width=32):
    n, c = x.shape
    return op(op(x.reshape(n // width, width, c), axis=0), axis=0, keepdims=True)


def _params(sem, vmem_mb=40):
    return pltpu.CompilerParams(dimension_semantics=sem, vmem_limit_bytes=vmem_mb * 1024 * 1024)


def _ada_kernel(c_ref, w_ref, b_ref, o_ref):
    cs = _silu(c_ref[...])
    o_ref[0] = jnp.dot(cs, w_ref[0], preferred_element_type=F32, precision=HIGHEST) + b_ref[0]


def _ada_mod(c, ada_w, ada_b):
    depth, d, n = ada_w.shape
    b = c.shape[0]
    return pl.pallas_call(
        _ada_kernel, name="ada_mod",
        grid=(depth, n // d),
        in_specs=[pl.BlockSpec((b, d), lambda l, j: (0, 0)),
                  pl.BlockSpec((1, d, d), lambda l, j: (l, 0, j)),
                  pl.BlockSpec((1, 1, d), lambda l, j: (l, 0, j))],
        out_specs=pl.BlockSpec((1, b, d), lambda l, j: (l, 0, j)),
        out_shape=jax.ShapeDtypeStruct((depth, b, n), F32),
        compiler_params=_params(("parallel", "parallel")),
    )(c, ada_w, ada_b.reshape(depth, 1, n))


def _inproj_kernel(x_ref, sc_ref, sh_ref, g_ref, w_ref, o_ref):
    h = _rms(x_ref[...], g_ref[...]) * (1.0 + sc_ref[0]) + sh_ref[0]
    o_ref[...] = _dot(h, w_ref[...])


def _in_proj(x2, mod3, norm_g, w_pack, seq):
    t, d = x2.shape
    tm = min(512, seq)
    n_col = 3
    cw = N_PACK // n_col
    return pl.pallas_call(
        _inproj_kernel, name="in_proj",
        grid=(n_col, t // tm),
        in_specs=[pl.BlockSpec((tm, d), lambda j, i: (i, 0)),
                  pl.BlockSpec((1, 1, d), lambda j, i: ((i * tm) // seq, 0, 1)),
                  pl.BlockSpec((1, 1, d), lambda j, i: ((i * tm) // seq, 0, 0)),
                  pl.BlockSpec((1, d), lambda j, i: (0, 0)),
                  pl.BlockSpec((d, cw), lambda j, i: (0, j))],
        out_specs=pl.BlockSpec((tm, cw), lambda j, i: (i, j)),
        out_shape=jax.ShapeDtypeStruct((t, N_PACK), F32),
        compiler_params=_params(("parallel", "parallel")),
    )(x2, mod3, mod3, norm_g, w_pack)


def _gla_kernel(q_ref, k_ref, v_ref, g_ref, alr_ref, wa2_ref, ba_ref, gn_ref, o_ref, s_ref, acc_ref):
    @pl.when(pl.program_id(1) == 0)
    def _():
        s_ref[...] = jnp.zeros_like(s_ref)

    L, sub, nh, dk, dv = GLA_CHUNK, GLA_SUB, GLA_HEADS, GLA_DK, GLA_DV
    seqs = range(q_ref.shape[0])
    heads = range(nh)
    hk = lambda h: slice(h * dk, (h + 1) * dk)
    hv = lambda h: slice(h * dv, (h + 1) * dv)

    pre = []
    for g in seqs:
        z = jnp.dot(alr_ref[g], wa2_ref[...], preferred_element_type=F32, precision=HIGHEST) + ba_ref[...]
        cum = _cumsum_rows(_log_sigmoid(z) * (1.0 / GLA_TAU))
        q = q_ref[g] * (dk ** -0.5)
        k = k_ref[g]
        tot = cum[L - 1:L, :]
        pre.append(dict(cum=cum, q=q, k=k, tot=tot, vb=_mx(v_ref[g]),
                        q_in=_mx(q * jnp.exp(cum)), k_dec=_mx(k * jnp.exp(tot - cum))))

    scores = {}
    for i in range(L // sub):
        r0, r1 = i * sub, (i + 1) * sub
        for g in seqs:
            p = pre[g]
            base = p["cum"][r0 - 1:r0, :] if i > 0 else jnp.zeros_like(p["tot"])
            qi = _mx(p["q"][r0:r1] * jnp.exp(p["cum"][r0:r1] - base))
            ka = _mx(p["k"][:r1] * jnp.exp(base - p["cum"][:r1]))
            for h in heads:
                scores[g, i, h] = _dot_nt(qi[:, hk(h)], ka[:, hk(h)])

    for i in range(L // sub):
        r0, r1 = i * sub, (i + 1) * sub
        causal = (lax.broadcasted_iota(jnp.int32, (sub, r1), 1)
                  <= lax.broadcasted_iota(jnp.int32, (sub, r1), 0) + r0)
        for g in seqs:
            for h in heads:
                s = jnp.where(causal, scores[g, i, h], 0.0)
                acc_ref[g, r0:r1, hv(h)] = _dot(s, pre[g]["vb"][:r1, hv(h)])

    inter = {(g, h): _dot(pre[g]["q_in"][:, hk(h)], s_ref[g * nh + h]) for g in seqs for h in heads}
    update = {(g, h): _dot_tn(pre[g]["k_dec"][:, hk(h)], pre[g]["vb"][:, hv(h)]) for g in seqs for h in heads}

    gn = gn_ref[...]
    for g in seqs:
        gate = g_ref[g]
        for h in heads:
            o = acc_ref[g, :, hv(h)] + inter[g, h]
            y = _rms(o, gn[:, hv(h)]) * _silu(gate[:, hv(h)])
            o_ref[g, :, hv(h)] = y.astype(o_ref.dtype)
            decay = jnp.transpose(jnp.exp(pre[g]["tot"][:, hk(h)]))
            s_ref[g * nh + h] = s_ref[g * nh + h] * decay + update[g, h]


def _seq_group(batch):
    for grp in (SEQ_GROUP, 2, 1):
        if batch % grp == 0:
            return grp


def _gla(proj3, wa2_pad, ba, gn):
    b, s, _ = proj3.shape
    L = GLA_CHUNK
    grp = _seq_group(b)
    blk = lambda w, c0: pl.BlockSpec((grp, L, w), lambda bi, ci: (bi, ci, c0 // w))
    full = lambda a: pl.BlockSpec(a.shape, lambda bi, ci: (0,) * a.ndim)
    return pl.pallas_call(
        _gla_kernel, name="gla",
        grid=(b // grp, s // L),
        in_specs=[blk(GLA_QK, C_QA), blk(GLA_QK, C_KA), blk(GLA_V, C_VA), blk(GLA_V, C_GA),
                  blk(LANE, C_ALR), full(wa2_pad), full(ba), full(gn)],
        out_specs=pl.BlockSpec((grp, L, GLA_V), lambda bi, ci: (bi, ci, 0)),
        out_shape=jax.ShapeDtypeStruct((b, s, GLA_V), ACT_DTYPE),
        scratch_shapes=[pltpu.VMEM((grp * GLA_HEADS, GLA_DK, GLA_DV), F32),
                        pltpu.VMEM((grp, L, GLA_V), F32)],
        compiler_params=_params(("parallel", "arbitrary")),
    )(proj3, proj3, proj3, proj3, proj3, wa2_pad, ba, gn)


def _mlstm_kernel(qk_ref, v_ref, oc_ref, if_ref, conv_ref, bias_ref, gn_ref, o_ref,
                  xbuf_ref, c_ref, n_ref, m_ref):
    @pl.when(pl.program_id(1) == 0)
    def _():
        xbuf_ref[:, 0:CONV_PAD, :] = jnp.zeros((xbuf_ref.shape[0], CONV_PAD, 2 * MLSTM_QK), F32)
        c_ref[...] = jnp.zeros_like(c_ref)
        n_ref[...] = jnp.zeros_like(n_ref)
        m_ref[...] = jnp.zeros_like(m_ref)

    L, nh, dk, dv, kc = MLSTM_CHUNK, MLSTM_HEADS, MLSTM_DQK, MLSTM_DV, MLSTM_CONV
    pad = CONV_PAD
    seqs = range(qk_ref.shape[0])
    hk = lambda h: slice(h * dk, (h + 1) * dk)
    hv = lambda h: slice(h * dv, (h + 1) * dv)
    causal = lax.broadcasted_iota(jnp.int32, (L, L), 1) <= lax.broadcasted_iota(jnp.int32, (L, L), 0)
    cw = conv_ref[...]
    gn = gn_ref[...]

    seq = {}
    for g in seqs:
        xbuf_ref[g, pad:pad + L, :] = qk_ref[g]
        conv = jnp.zeros((L, 2 * MLSTM_QK), F32)
        for j in range(kc):
            conv = conv + cw[j:j + 1, :] * xbuf_ref[g, pl.ds(pad - (kc - 1) + j, L), :]
        xbuf_ref[g, 0:pad, :] = xbuf_ref[g, L:L + pad, :]
        qk = _silu(conv)
        pre = if_ref[g] + bias_ref[...]
        bcum = _cumsum_rows(_log_sigmoid(pre))
        seq[g] = dict(q=qk[:, :MLSTM_QK] * (dk ** -0.5), k=qk[:, MLSTM_QK:], v=v_ref[g],
                      pre=pre, bcum=bcum, pre_t=jnp.transpose(pre), bcum_t=jnp.transpose(bcum))

        for h in range(nh):
            _mlstm_heads([(g, h)], seq, causal, gn, oc_ref, o_ref, c_ref, n_ref, m_ref)


def _mlstm_heads(combos, seq, causal, gn, oc_ref, o_ref, c_ref, n_ref, m_ref):
    L, nh, dk, dv = MLSTM_CHUNK, MLSTM_HEADS, MLSTM_DQK, MLSTM_DV
    hk = lambda h: slice(h * dk, (h + 1) * dk)
    hv = lambda h: slice(h * dv, (h + 1) * dv)
    for g, h in combos:
        p = seq[g]
        st = g * nh + h
        b_col = p["bcum"][:, nh + h:nh + h + 1]
        b_row = p["bcum_t"][nh + h:nh + h + 1, :]
        i_col = p["pre"][:, h:h + 1]
        i_row = p["pre_t"][h:h + 1, :]
        m_prev = m_ref[st]
        c_st = c_ref[st]
        n_st = n_ref[st]
        qh, kh, vh = p["q"][:, hk(h)], p["k"][:, hk(h)], p["v"][:, hv(h)]

        dlog = jnp.where(causal, b_col - b_row + i_row, -jnp.inf)
        inter_log = b_col + m_prev
        m_t = jnp.maximum(inter_log, jnp.max(dlog, axis=1, keepdims=True))
        w_intra = jnp.exp(dlog - m_t)
        w_inter = jnp.exp(inter_log - m_t)
        s = _dot_nt(qh, kh) * w_intra
        num = _dot(s, vh) + w_inter * _dot(qh, c_st)
        den = jnp.sum(s, axis=1, keepdims=True) + w_inter * jnp.sum(qh * n_st, axis=1, keepdims=True)
        hout = num / jnp.maximum(jnp.abs(den), jnp.exp(-m_t))
        y = _rms(hout, gn[:, hv(h)]) * jax.nn.sigmoid(oc_ref[g, :, hv(h)])
        o_ref[g, :, hv(h)] = y.astype(o_ref.dtype)

        tot = b_col[L - 1:L, :]
        g_log = tot - b_col + i_col
        m_new = jnp.maximum(tot + m_prev, jnp.max(g_log, axis=0, keepdims=True))
        w_s = jnp.exp(g_log - m_new)
        w_c = jnp.exp(tot + m_prev - m_new)
        k_s = kh * w_s
        c_ref[st] = c_st * w_c + _dot_tn(k_s, vh)
        n_ref[st] = n_st * w_c + jnp.sum(k_s, axis=0, keepdims=True)
        m_ref[st] = m_new


def _mlstm(proj3, conv_w, bias_row, gn):
    b, s, _ = proj3.shape
    L = MLSTM_CHUNK
    grp = _seq_group(b)
    blk = lambda w, c0: pl.BlockSpec((grp, L, w), lambda bi, ci: (bi, ci, c0 // w))
    full = lambda a: pl.BlockSpec(a.shape, lambda bi, ci: (0,) * a.ndim)
    return pl.pallas_call(
        _mlstm_kernel, name="mlstm",
        grid=(b // grp, s // L),
        in_specs=[blk(2 * MLSTM_QK, C_QKC), blk(MLSTM_V, C_VC), blk(MLSTM_V, C_OC), blk(LANE, C_ICFC),
                  full(conv_w), full(bias_row), full(gn)],
        out_specs=pl.BlockSpec((grp, L, MLSTM_V), lambda bi, ci: (bi, ci, 0)),
        out_shape=jax.ShapeDtypeStruct((b, s, MLSTM_V), ACT_DTYPE),
        scratch_shapes=[pltpu.VMEM((grp, L + CONV_PAD, 2 * MLSTM_QK), F32),
                        pltpu.VMEM((grp * MLSTM_HEADS, MLSTM_DQK, MLSTM_DV), F32),
                        pltpu.VMEM((grp * MLSTM_HEADS, 1, MLSTM_DQK), F32),
                        pltpu.VMEM((grp * MLSTM_HEADS, 1, 1), F32)],
        compiler_params=_params(("parallel", "arbitrary")),
    )(proj3, proj3, proj3, proj3, conv_w, bias_row, gn)


def _dsa_prep_kernel(cq_ref, ckv_ref, kw_ref, nq_ref, nkv_ref, wuq_ref, wuk_ref, wqi_ref,
                     ckvn_ref, kwb_ref, qlat_ref, qidx_ref, wht_ref, *, tiles_per_seq):
    tm = cq_ref.shape[0]
    r = DSA_KV_RANK
    cqn = _mx(_rms(cq_ref[...], nq_ref[...]))
    pos = (pl.program_id(0) % tiles_per_seq) * tm + lax.broadcasted_iota(jnp.int32, (tm, LANE), 0)
    lane = lax.broadcasted_iota(jnp.int32, (tm, LANE), 1)
    pos_cols = jnp.where(lane < 3, pos >> 6, jnp.where(lane < 6, pos & 63, 0)).astype(F32)
    ckvn_ref[:, :r] = _rms(ckv_ref[...], nkv_ref[...]).astype(ckvn_ref.dtype)
    ckvn_ref[:, r:] = pos_cols.astype(ckvn_ref.dtype)
    kw = kw_ref[...]
    kwb_ref[...] = kw.astype(kwb_ref.dtype)
    wht_ref[0] = jnp.transpose(kw)[W_IDX_LANE:W_IDX_LANE + IDX_HEADS, :] * (IDX_HEADS ** -0.5)
    q = jnp.dot(cqn, wuq_ref[...], preferred_element_type=F32)
    for h in range(DSA_HEADS):
        ql = _dot(q[:, h * DSA_HEAD_DIM:(h + 1) * DSA_HEAD_DIM], wuk_ref[h]) * (DSA_HEAD_DIM ** -0.5 * LOG2E)
        qlat_ref[0, h, :, :r] = ql.astype(qlat_ref.dtype)
        c = _bf16_pieces(2.0 ** (-8.0 * (h + 1) / DSA_HEADS) * LOG2E, 3)
        consts = [64.0 * c[0], 64.0 * c[1], 64.0 * c[2], c[0], c[1], c[2]]
        slope_cols = jnp.zeros((tm, LANE), F32)
        for j, v in enumerate(consts):
            slope_cols = jnp.where(lane == j, v, slope_cols)
        qlat_ref[0, h, :, r:] = slope_cols.astype(qlat_ref.dtype)
    qi = jnp.dot(cqn, wqi_ref[...], preferred_element_type=F32) * (IDX_DIM ** -0.5)
    for h in range(IDX_HEADS):
        qidx_ref[0, h] = qi[:, h * LANE:(h + 1) * LANE].astype(qidx_ref.dtype)


def _dsa_prep(proj2, seq, nq, nkv, wuq, wuk_t, wqi_pad):
    t = proj2.shape[0]
    b = t // seq
    tm = min(512, seq)
    per = seq // tm
    blk = lambda w, c0: pl.BlockSpec((tm, w), lambda i: (i, c0 // w))
    full = lambda a: pl.BlockSpec(a.shape, lambda i: (0,) * a.ndim)
    hmap = lambda i: (i // per, 0, i % per, 0)
    return pl.pallas_call(
        functools.partial(_dsa_prep_kernel, tiles_per_seq=per), name="dsa_prep",
        grid=(t // tm,),
        in_specs=[blk(DSA_Q_RANK, C_CQ), blk(DSA_KV_RANK, C_CKV), blk(LANE, C_KW),
                  full(nq), full(nkv), full(wuq), full(wuk_t), full(wqi_pad)],
        out_specs=[pl.BlockSpec((tm, DSA_AUG), lambda i: (i, 0)),
                   pl.BlockSpec((tm, LANE), lambda i: (i, 0)),
                   pl.BlockSpec((1, DSA_HEADS, tm, DSA_AUG), hmap),
                   pl.BlockSpec((1, IDX_HEADS, tm, LANE), hmap),
                   pl.BlockSpec((1, IDX_HEADS, tm), lambda i: (i // per, 0, i % per))],
        out_shape=[jax.ShapeDtypeStruct((t, DSA_AUG), ACT_DTYPE),
                   jax.ShapeDtypeStruct((t, LANE), ACT_DTYPE),
                   jax.ShapeDtypeStruct((b, DSA_HEADS, seq, DSA_AUG), ACT_DTYPE),
                   jax.ShapeDtypeStruct((b, IDX_HEADS, seq, LANE), ACT_DTYPE),
                   jax.ShapeDtypeStruct((b, IDX_HEADS, seq), F32)],
        compiler_params=_params(("parallel",)),
    )(proj2, proj2, proj2, nq, nkv, wuq, wuk_t, wqi_pad)


def _dsa_kernel(qlat_ref, qidx_ref, wht_ref, kwk_ref, ckv_ref, wuv_ref, o_ref,
                ibuf_ref, acc_ref, *, topk):
    nh, tq, tk = DSA_HEADS, Q_TILE, KEY_TILE
    qb = pl.program_id(1)
    n_kt = (qb * tq + tq + tk - 1) // tk
    t_row = qb * tq + lax.broadcasted_iota(jnp.int32, (1, tq), 1)
    s_col = lax.broadcasted_iota(jnp.int32, (tk, 1), 0)

    qi = qidx_ref[0].reshape(IDX_HEADS * tq, LANE)
    wht = wht_ref[0]

    def idx_body(kt, carry):
        kk = kwk_ref[0, pl.ds(pl.multiple_of(kt * tk, tk), tk), :]
        sc = jnp.maximum(_dot_nt(kk, qi), 0.0)
        tot = wht[0:1, :] * sc[:, 0:tq]
        for h in range(1, IDX_HEADS):
            tot = tot + wht[h:h + 1, :] * sc[:, h * tq:(h + 1) * tq]
        ibuf_ref[kt] = jnp.where(kt * tk + s_col <= t_row, tot, -jnp.inf)
        return carry

    lax.fori_loop(0, n_kt, idx_body, 0)

    def count(pred):
        def body(kt, c):
            hit = jnp.where(pred(ibuf_ref[kt]), 1.0, 0.0)
            return c + jnp.sum(hit.reshape(tk // 32, 32, tq), axis=0)
        return jnp.sum(lax.fori_loop(0, n_kt, body, jnp.zeros((32, tq), F32)), axis=0, keepdims=True)

    def count_ge(cand):
        return count(lambda x: x >= cand)

    def count_gt(cand):
        return count(lambda x: x > cand)

    def key_to_float(u):
        key = u ^ jnp.int32(-2 ** 31)
        bits = jnp.where(key >= 0, key, key ^ jnp.int32(0x7FFFFFFF))
        return lax.bitcast_convert_type(bits, F32)

    few = t_row < topk
    n_bits = 32

    def search_cond(st):
        i, _, _, pending = st
        return (i < n_bits) & (pending > 0)

    def search_body(st):
        i, u, cnt_u, _ = st
        for _ in range(4):
            cand_u = u | lax.shift_left(jnp.int32(1), n_bits - 1 - i)
            cnt = count_ge(key_to_float(cand_u))
            ok = cnt >= topk
            u = jnp.where(ok, cand_u, u)
            cnt_u = jnp.where(ok, cnt, cnt_u)
            i = i + 1
        pending = jnp.max(jnp.where(few | (cnt_u == topk), 0, 1))
        return i, u, cnt_u, pending

    start = (jnp.int32(0), jnp.zeros((1, tq), jnp.int32), jnp.full((1, tq), float(2 ** 30), F32),
             (n_kt * tk > topk).astype(jnp.int32))
    _, u, _, _ = lax.while_loop(search_cond, search_body, start)
    tau = jnp.where(few, -jnp.inf, key_to_float(u))
    need = topk - count_gt(tau)

    acc_ref[...] = jnp.zeros_like(acc_ref)
    ql = qlat_ref[0].reshape(nh * tq, DSA_AUG)
    tri = jnp.where(lax.broadcasted_iota(jnp.int32, (tk, tk), 0) >= lax.broadcasted_iota(jnp.int32, (tk, tk), 1),
                    1.0, 0.0).astype(BF16)

    def keys(kt):
        return ckv_ref[0, pl.ds(pl.multiple_of(kt * tk, tk), tk), :]

    def att_body(kt, carry):
        m, l, eq_seen = carry
        kv_aug = keys(kt)
        lg = _dot_nt(kv_aug, ql)
        it = ibuf_ref[kt]
        eq = it == tau
        eqf = jnp.where(eq, 1.0, 0.0)
        rank = jnp.dot(tri, eqf.astype(BF16), preferred_element_type=F32) + eq_seen
        valid = ((it > tau) | (eq & (rank <= need))) & (kt * tk + s_col <= t_row)
        bias = jnp.where(valid, 0.0, NEG_BIG)
        ms, ls, als, ps = [], [], [], []
        for h in range(nh):
            cols = slice(h * tq, (h + 1) * tq)
            lh = lg[:, cols] + bias
            m_old = m[:, cols]
            m_new = jnp.maximum(m_old, _colreduce(lh, jnp.max))
            p = jnp.exp2(lh - m_new)
            alpha = jnp.exp2(m_old - m_new)
            ms.append(m_new)
            ls.append(alpha * l[:, cols] + _colreduce(p, jnp.sum))
            als.append(alpha)
            ps.append(_mx(p))
        acc_ref[...] = (jnp.concatenate(als, axis=1) * acc_ref[...]
                        + _dot_tn(kv_aug[:, :DSA_KV_RANK], jnp.concatenate(ps, axis=1)))
        return (jnp.concatenate(ms, axis=1), jnp.concatenate(ls, axis=1),
                eq_seen + jnp.sum(eqf, axis=0, keepdims=True))

    init = (jnp.full((1, nh * tq), NEG_BIG, F32), jnp.zeros((1, nh * tq), F32), jnp.zeros((1, tq), F32))
    _, l, _ = lax.fori_loop(0, n_kt, att_body, init)

    outs = []
    for h in range(nh):
        cols = slice(h * tq, (h + 1) * tq)
        outs.append(_dot(wuv_ref[h], acc_ref[:, cols] / l[:, cols]))
    o_ref[0] = jnp.transpose(jnp.concatenate(outs, axis=0)).astype(o_ref.dtype)


def _dsa(ckvn3, kwb3, qlat, qidx, wht, wuv_t):
    b, s, _ = ckvn3.shape
    topk = min(DSA_TOPK_MAX, s // 4)
    nh, tq = DSA_HEADS, Q_TILE
    return pl.pallas_call(
        functools.partial(_dsa_kernel, topk=topk), name="dsa_attn",
        grid=(b, s // tq),
        in_specs=[pl.BlockSpec((1, nh, tq, DSA_AUG), lambda bi, qi: (bi, 0, qi, 0)),
                  pl.BlockSpec((1, IDX_HEADS, tq, LANE), lambda bi, qi: (bi, 0, qi, 0)),
                  pl.BlockSpec((1, IDX_HEADS, tq), lambda bi, qi: (bi, 0, qi)),
                  pl.BlockSpec((1, s, LANE), lambda bi, qi: (bi, 0, 0)),
                  pl.BlockSpec((1, s, DSA_AUG), lambda bi, qi: (bi, 0, 0)),
                  pl.BlockSpec(wuv_t.shape, lambda bi, qi: (0, 0, 0))],
        out_specs=pl.BlockSpec((1, tq, DSA_V), lambda bi, qi: (bi, qi, 0)),
        out_shape=jax.ShapeDtypeStruct((b, s, DSA_V), ACT_DTYPE),
        scratch_shapes=[pltpu.VMEM((s // KEY_TILE, KEY_TILE, tq), F32),
                        pltpu.VMEM((DSA_KV_RANK, nh * tq), F32)],
        compiler_params=_params(("parallel", "arbitrary")),
    )(qlat, qidx, wht, kwb3, ckvn3, wuv_t)


def _merge_kernel(a_ref, b_ref, c_ref, gt_ref, x_ref, g1_ref, wa_ref, wb_ref, wc_ref, wo_ref, o_ref):
    d = D_MODEL
    g = jax.nn.sigmoid(gt_ref[...])
    ya = jnp.dot(a_ref[...], wa_ref[...], preferred_element_type=F32)
    yb = jnp.dot(b_ref[...], wb_ref[...], preferred_element_type=F32)
    yc = jnp.dot(c_ref[...], wc_ref[...], preferred_element_type=F32)
    m = g[:, :d] * ya + g[:, d:2 * d] * yb + g[:, 2 * d:] * yc
    o_ref[...] = x_ref[...] + g1_ref[0] * _dot(m, wo_ref[...])


def _merge(ya_in, yb_in, yc_in, proj2, x2, mod3, wa, wb, wc, wo, seq):
    t, d = x2.shape
    tm = min(512, seq)
    full = lambda a: pl.BlockSpec(a.shape, lambda i: (0,) * a.ndim)
    br = lambda w: pl.BlockSpec((tm, w), lambda i: (i, 0))
    return pl.pallas_call(
        _merge_kernel, name="merge",
        grid=(t // tm,),
        in_specs=[br(GLA_V), br(DSA_V), br(MLSTM_V), br(3 * d), br(d),
                  pl.BlockSpec((1, 1, d), lambda i: ((i * tm) // seq, 0, 2)),
                  full(wa), full(wb), full(wc), full(wo)],
        out_specs=br(d),
        out_shape=jax.ShapeDtypeStruct((t, d), F32),
        compiler_params=_params(("parallel",)),
    )(ya_in, yb_in, yc_in, proj2, x2, mod3, wa, wb, wc, wo)


def _first_argmax_mask(cur, iota, axis, n):
    mx = jnp.max(cur, axis=axis, keepdims=True)
    ix = jnp.min(jnp.where(cur == mx, iota, n), axis=axis, keepdims=True)
    return iota == ix


def _router_kernel(x_ref, sc_ref, sh_ref, g_ref, rwt_ref, rb_ref, dest_ref, wgt_ref, starts_ref, plens_ref):
    ne, ng = N_EXPERTS, N_GROUPS
    eg = ne // ng
    h = _rms(x_ref[...], g_ref[...]) * (1.0 + sc_ref[0]) + sh_ref[0]
    tm = h.shape[0]
    logits = lax.dot_general(rwt_ref[...], h, (((1,), (1,)), ((), ())),
                             preferred_element_type=F32, precision=HIGHEST)
    scores = jax.nn.sigmoid(logits)
    sel = scores + rb_ref[...]
    s3 = sel.reshape(ng, eg, tm)
    io3 = lax.broadcasted_iota(jnp.int32, (ng, eg, tm), 1)
    m1 = jnp.max(s3, axis=1, keepdims=True)
    first = _first_argmax_mask(s3, io3, 1, eg)
    m2 = jnp.max(jnp.where(first, -jnp.inf, s3), axis=1, keepdims=True)
    gs = (m1 + m2).reshape(ng, tm)
    iog = lax.broadcasted_iota(jnp.int32, (ng, tm), 0)
    gkeep = jnp.zeros((ng, tm), F32)
    cur = gs
    for _ in range(TOPK_GROUPS):
        hit = _first_argmax_mask(cur, iog, 0, ng)
        gkeep = jnp.where(hit, 1.0, gkeep)
        cur = jnp.where(hit, -jnp.inf, cur)
    selm = jnp.where(gkeep.reshape(ng, 1, tm) > 0.0, s3, -jnp.inf).reshape(ne, tm)
    ioe = lax.broadcasted_iota(jnp.int32, (ne, tm), 0)
    hits = []
    chosen = jnp.zeros((ne, tm), F32)
    cur = selm
    for _ in range(TOP_K):
        hit = _first_argmax_mask(cur, ioe, 0, ne)
        hits.append(hit)
        chosen = jnp.where(hit, 1.0, chosen)
        cur = jnp.where(hit, -jnp.inf, cur)
    w = chosen * scores
    w = w / jnp.sum(w, axis=0, keepdims=True) * ROUTED_SCALE

    cnt = jnp.sum(chosen, axis=1, keepdims=True)
    plen = jnp.ceil(cnt * (1.0 / ROW_ALIGN)) * ROW_ALIGN
    start = _cumsum_rows(jnp.broadcast_to(plen, (ne, LANE)))[:, :1] - plen
    before = (lax.broadcasted_iota(jnp.int32, (tm, tm), 0)
              < lax.broadcasted_iota(jnp.int32, (tm, tm), 1)).astype(BF16)
    rank = jnp.dot(chosen.astype(BF16), before, preferred_element_type=F32)
    row_of = start + rank
    pad_rows = SLOT_ROWS - TOP_K
    dest = [jnp.sum(jnp.where(hit, row_of, 0.0), axis=0, keepdims=True) for hit in hits]
    wsel = [jnp.sum(jnp.where(hit, w, 0.0), axis=0, keepdims=True) for hit in hits]
    dest_ref[...] = jnp.concatenate(dest + [jnp.full((pad_rows, tm), -1.0, F32)], axis=0).astype(jnp.int32)
    wgt_ref[...] = jnp.concatenate(wsel + [jnp.zeros((pad_rows, tm), F32)], axis=0)
    starts_ref[0] = start.astype(jnp.int32)
    plens_ref[0] = plen.astype(jnp.int32)


def _router(x2, mod3, norm_g, rw_t, rb_col, seq):
    t, d = x2.shape
    tm = MOE_SUB
    nsb = t // tm
    return pl.pallas_call(
        _router_kernel, name="router",
        grid=(nsb,),
        in_specs=[pl.BlockSpec((tm, d), lambda i: (i, 0)),
                  pl.BlockSpec((1, 1, d), lambda i: ((i * tm) // seq, 0, 4)),
                  pl.BlockSpec((1, 1, d), lambda i: ((i * tm) // seq, 0, 3)),
                  pl.BlockSpec((1, d), lambda i: (0, 0)),
                  pl.BlockSpec(rw_t.shape, lambda i: (0, 0)),
                  pl.BlockSpec(rb_col.shape, lambda i: (0, 0))],
        out_specs=[pl.BlockSpec((SLOT_ROWS, tm), lambda i: (0, i)),
                   pl.BlockSpec((SLOT_ROWS, tm), lambda i: (0, i)),
                   pl.BlockSpec((1, N_EXPERTS, 1), lambda i: (i, 0, 0)),
                   pl.BlockSpec((1, N_EXPERTS, 1), lambda i: (i, 0, 0))],
        out_shape=[jax.ShapeDtypeStruct((SLOT_ROWS, t), jnp.int32),
                   jax.ShapeDtypeStruct((SLOT_ROWS, t), F32),
                   jax.ShapeDtypeStruct((nsb, N_EXPERTS, 1), jnp.int32),
                   jax.ShapeDtypeStruct((nsb, N_EXPERTS, 1), jnp.int32)],
        compiler_params=_params(("parallel",)),
    )(x2, mod3, mod3, norm_g, rw_t, rb_col)


def _moe_kernel(starts_ref, plens_ref, x_ref, sc_ref, sh_ref, g2_ref, gn_ref, dest_ref, wgt_ref,
                wg_ref, wu_ref, wd_ref, sg_ref, su_ref, sd_ref, nf_ref, o_ref,
                h_ref, acc_ref, xs_ref, stage_ref, *, final):
    blk, e = pl.program_id(0), pl.program_id(1)
    ne = pl.num_programs(1)
    sub, rt, mt = MOE_SUB, MOE_ROW_TILE, MOE_FFN_TILE
    nsub = x_ref.shape[0] // sub
    rmax = xs_ref.shape[1]

    @pl.when(e == 0)
    def _():
        h = _mx(_rms(x_ref[...], gn_ref[...]) * (1.0 + sc_ref[0]) + sh_ref[0])
        h_ref[...] = h
        acc_ref[...] = _dot(_silu(_dot(h, sg_ref[...])) * _dot(h, su_ref[...]), sd_ref[...])
        stage_ref[...] = jnp.zeros_like(stage_ref)
        for sb in range(nsub):
            dest = dest_ref[:, sb * sub:(sb + 1) * sub]
            hs = h_ref[sb * sub:(sb + 1) * sub, :]
            for r in range(rmax // rt):
                rows = r * rt + lax.broadcasted_iota(jnp.int32, (rt, sub), 0)
                pick = jnp.zeros((rt, sub), F32)
                for j in range(TOP_K):
                    pick = jnp.where(dest[j:j + 1, :] == rows, 1.0, pick)
                xs_ref[sb, r * rt:(r + 1) * rt, :] = _dot(pick, hs).astype(xs_ref.dtype)

    def run(sb):
        i = (blk * nsub + sb) * ne + e
        return starts_ref[i], plens_ref[i] // ROW_ALIGN

    def copy_rows(src, src0, dst, dst0, n_chunks):
        def body(k, carry):
            s = pl.multiple_of(src0 + k * ROW_ALIGN, ROW_ALIGN)
            d = pl.multiple_of(dst0 + k * ROW_ALIGN, ROW_ALIGN)
            dst[pl.ds(d, ROW_ALIGN), :] = src[pl.ds(s, ROW_ALIGN), :]
            return carry
        lax.fori_loop(0, n_chunks, body, 0)

    cursor = 0
    placed = []
    for sb in range(nsub):
        st, nch = run(sb)
        copy_rows(xs_ref.at[sb], st, stage_ref, cursor, nch)
        placed.append((st, nch, cursor))
        cursor = cursor + nch * ROW_ALIGN

    def ffn(i, carry):
        r0 = pl.multiple_of(i * mt, mt)
        xt = stage_ref[pl.ds(r0, mt), :]
        y = _dot(_silu(_dot(xt, wg_ref[0])) * _dot(xt, wu_ref[0]), wd_ref[0])
        stage_ref[pl.ds(r0, mt), :] = y.astype(stage_ref.dtype)
        return carry

    lax.fori_loop(0, (cursor + mt - 1) // mt, ffn, 0)

    for sb, (st, nch, at) in enumerate(placed):
        copy_rows(stage_ref, at, xs_ref.at[sb], st, nch)

    @pl.when(e == ne - 1)
    def _():
        for sb in range(nsub):
            tok = slice(sb * sub, (sb + 1) * sub)
            dest_t = jnp.transpose(dest_ref[:, tok].astype(F32))
            wgt_t = jnp.transpose(wgt_ref[:, tok])
            routed = jnp.zeros((sub, x_ref.shape[1]), F32)
            for r in range(rmax // rt):
                cols = (r * rt + lax.broadcasted_iota(jnp.int32, (sub, rt), 1)).astype(F32)
                mix = jnp.zeros((sub, rt), F32)
                for j in range(TOP_K):
                    mix = jnp.where(dest_t[:, j:j + 1] == cols, wgt_t[:, j:j + 1], mix)
                routed = routed + _dot(mix, xs_ref[sb, r * rt:(r + 1) * rt, :])
            xo = x_ref[tok, :] + g2_ref[0] * (acc_ref[tok, :] + routed)
            if final:
                xo = _rms(xo, nf_ref[...])
            o_ref[tok, :] = xo


def _round_up(n, m):
    return (n + m - 1) // m * m


def _moe(x2, mod3, norm_g, dest, wgt, starts, plens, wg, wu, wd, sg, su, sd, nf, seq, final):
    t, d = x2.shape
    tm = min(MOE_BLOCK, seq)
    nsub = tm // MOE_SUB
    ne, ff = wg.shape[0], wg.shape[2]
    rmax = _round_up(TOP_K * MOE_SUB + ne * (ROW_ALIGN - 1), MOE_ROW_TILE)
    stage_rows = _round_up(tm + nsub * (ROW_ALIGN - 1), MOE_FFN_TILE)
    full = lambda a: pl.BlockSpec(a.shape, lambda i, e, *_: (0,) * a.ndim)
    mod = lambda j: pl.BlockSpec((1, 1, d), lambda i, e, *_: ((i * tm) // seq, 0, j))
    slot = pl.BlockSpec((SLOT_ROWS, tm), lambda i, e, *_: (0, i))
    grid_spec = pltpu.PrefetchScalarGridSpec(
        num_scalar_prefetch=2,
        grid=(t // tm, ne),
        in_specs=[pl.BlockSpec((tm, d), lambda i, e, *_: (i, 0)), mod(4), mod(3), mod(5),
                  pl.BlockSpec((1, d), lambda i, e, *_: (0, 0)), slot, slot,
                  pl.BlockSpec((1, d, ff), lambda i, e, *_: (e, 0, 0)),
                  pl.BlockSpec((1, d, ff), lambda i, e, *_: (e, 0, 0)),
                  pl.BlockSpec((1, ff, d), lambda i, e, *_: (e, 0, 0)),
                  full(sg), full(su), full(sd), full(nf)],
        out_specs=pl.BlockSpec((tm, d), lambda i, e, *_: (i, 0)),
        scratch_shapes=[pltpu.VMEM((tm, d), MXU_DTYPE), pltpu.VMEM((tm, d), F32),
                        pltpu.VMEM((nsub, rmax, d), MXU_DTYPE), pltpu.VMEM((stage_rows, d), MXU_DTYPE)])
    return pl.pallas_call(
        functools.partial(_moe_kernel, final=final), name="moe",
        grid_spec=grid_spec,
        out_shape=jax.ShapeDtypeStruct((t, d), F32),
        compiler_params=_params(("parallel", "arbitrary"), vmem_mb=56),
    )(starts.reshape(-1), plens.reshape(-1), x2, mod3, mod3, mod3, norm_g, dest, wgt,
      wg, wu, wd, sg, su, sd, nf)


def _pack_w_in(w):
    d = w.shape[0]
    offs = [0]
    for n in IN_SIZES:
        offs.append(offs[-1] + n)
    (qa, ka, va, ga, alr, cq, ckv, kidx, widx, qc, kc, vc, ic, fc, oc, gates) = [
        w[:, offs[i]:offs[i + 1]] for i in range(len(IN_SIZES))]
    z = lambda n: jnp.zeros((d, n), w.dtype)
    packed = jnp.concatenate(
        [gates, qa, ka, va, ga, vc, oc, qc, kc, cq, ckv,
         kidx, widx, z(LANE - IDX_DIM - IDX_HEADS),
         alr, z(LANE - GLA_GATE_RANK),
         ic, fc, z(LANE - 2 * MLSTM_HEADS)], axis=1)
    assert packed.shape[1] == N_PACK
    return packed.astype(MXU_DTYPE)


def kernel(x, c, ada_w, ada_b, norm_mix, norm_ffn, w_in, gla_w_a2, gla_b_a, gla_norm, dsa_norm_q,
           dsa_norm_kv, dsa_w_uq, dsa_w_uk, dsa_w_uv, dsa_w_qi, mlstm_conv, mlstm_b_i, mlstm_b_f,
           mlstm_norm, w_up_a, w_up_b, w_up_c, w_o, router_w, router_bias, exp_w_gate, exp_w_up,
           exp_w_down, sh_w_gate, sh_w_up, sh_w_down, norm_final):
    b, s, d = x.shape
    depth = ada_w.shape[0]
    t = b * s
    mod = _ada_mod(c, ada_w, ada_b)
    x2 = x.reshape(t, d)
    row = lambda v: v.reshape(1, -1)
    for l in range(depth):
        mod3 = mod[l].reshape(b, 1, 6 * d)
        proj2 = _in_proj(x2, mod3, row(norm_mix[l]), _pack_w_in(w_in[l]), s)
        proj3 = proj2.reshape(b, s, N_PACK)

        wa2_pad = jnp.zeros((LANE, GLA_QK), F32).at[:GLA_GATE_RANK].set(gla_w_a2[l])
        ya_in = _gla(proj3, wa2_pad, row(gla_b_a[l]), row(gla_norm[l]))

        bias_row = jnp.zeros((1, LANE), F32).at[0, :MLSTM_HEADS].set(mlstm_b_i[l])
        bias_row = bias_row.at[0, MLSTM_HEADS:2 * MLSTM_HEADS].set(mlstm_b_f[l])
        yc_in = _mlstm(proj3, mlstm_conv[l], bias_row, row(mlstm_norm[l]))

        wuq = dsa_w_uq[l].reshape(DSA_Q_RANK, DSA_HEADS * DSA_HEAD_DIM).astype(MXU_DTYPE)
        wuk_t = jnp.transpose(dsa_w_uk[l], (1, 2, 0)).astype(MXU_DTYPE)
        wuv_t = jnp.transpose(dsa_w_uv[l], (1, 2, 0)).astype(MXU_DTYPE)
        wqi_pad = jnp.zeros((DSA_Q_RANK, IDX_HEADS, LANE), F32).at[:, :, :IDX_DIM].set(dsa_w_qi[l])
        wqi_pad = wqi_pad.reshape(DSA_Q_RANK, IDX_HEADS * LANE).astype(MXU_DTYPE)
        ckvn, kwb, qlat, qidx, wht = _dsa_prep(proj2, s, row(dsa_norm_q[l]), row(dsa_norm_kv[l]),
                                               wuq, wuk_t, wqi_pad)
        yb_in = _dsa(ckvn.reshape(b, s, DSA_AUG), kwb.reshape(b, s, LANE), qlat, qidx, wht, wuv_t)

        x2 = _merge(ya_in.reshape(t, GLA_V), yb_in.reshape(t, DSA_V), yc_in.reshape(t, MLSTM_V),
                    proj2, x2, mod3, w_up_a[l].astype(MXU_DTYPE), w_up_b[l].astype(MXU_DTYPE),
                    w_up_c[l].astype(MXU_DTYPE), w_o[l].astype(MXU_DTYPE), s)

        dest, wgt, starts, plens = _router(x2, mod3, row(norm_ffn[l]), jnp.transpose(router_w[l]),
                                           router_bias[l].reshape(-1, 1), s)
        x2 = _moe(x2, mod3, row(norm_ffn[l]), dest, wgt, starts, plens,
                  exp_w_gate[l].astype(MXU_DTYPE), exp_w_up[l].astype(MXU_DTYPE),
                  exp_w_down[l].astype(MXU_DTYPE), sh_w_gate[l].astype(MXU_DTYPE),
                  sh_w_up[l].astype(MXU_DTYPE), sh_w_down[l].astype(MXU_DTYPE), row(norm_final), s,
                  final=(l == depth - 1))
    return x2.reshape(b, s, d)
```

```python
import functools
import struct

import jax
import jax.numpy as jnp
from jax import lax
from jax.experimental import pallas as pl
from jax.experimental.pallas import tpu as pltpu

F32 = jnp.float32
BF16 = jnp.bfloat16
MXU_DTYPE = jnp.bfloat16
ACT_DTYPE = jnp.bfloat16
HIGHEST = lax.Precision.HIGHEST

EPS = 1e-6
D_MODEL = 1024
GLA_HEADS, GLA_DK, GLA_DV, GLA_GATE_RANK, GLA_TAU, GLA_CHUNK = 4, 64, 128, 16, 16.0, 64
GLA_SUB = 16
DSA_HEADS, DSA_Q_RANK, DSA_KV_RANK, DSA_HEAD_DIM, DSA_V_DIM = 8, 256, 128, 64, 64
IDX_HEADS, IDX_DIM, DSA_TOPK_MAX = 8, 32, 256
MLSTM_HEADS, MLSTM_DQK, MLSTM_DV, MLSTM_CONV, MLSTM_CHUNK = 4, 64, 128, 4, 64
N_EXPERTS, TOP_K, N_GROUPS, TOPK_GROUPS, EXPERT_FF, SHARED_FF = 64, 6, 8, 4, 256, 256
ROUTED_SCALE = 2.5

GLA_QK = GLA_HEADS * GLA_DK
GLA_V = GLA_HEADS * GLA_DV
DSA_V = DSA_HEADS * DSA_V_DIM
MLSTM_QK = MLSTM_HEADS * MLSTM_DQK
MLSTM_V = MLSTM_HEADS * MLSTM_DV
IN_SIZES = (GLA_QK, GLA_QK, GLA_V, GLA_V, GLA_GATE_RANK,
            DSA_Q_RANK, DSA_KV_RANK, IDX_DIM, IDX_HEADS,
            MLSTM_QK, MLSTM_QK, MLSTM_V, MLSTM_HEADS, MLSTM_HEADS, MLSTM_V,
            3 * D_MODEL)

LANE = 128
KEY_TILE = 256
Q_TILE = 256
NEG_BIG = -1e30
MOE_BLOCK = 1024
MOE_SUB = 256
MOE_ROW_TILE = 256
MOE_FIRST_TILE = 192
MOE_FFN_TILE = 128
MOE_EXPERTS_PER_STEP = 2
CONV_PAD = 8
SEQ_GROUP = 4
ROW_ALIGN = 16
SLOT_ROWS = 8

C_GATES = 0
C_QA = 3072
C_KA = 3328
C_VA = 3584
C_GA = 4096
C_VC = 4608
C_OC = 5120
C_QKC = 5632
C_CQ = 6144
C_CKV = 6400
C_KW = 6528
C_ALR = 6656
C_ICFC = 6784
N_PACK = 6912
W_IDX_LANE = IDX_DIM


LOG2E = 1.4426950408889634
DSA_AUG = DSA_KV_RANK + LANE


def _bf16_pieces(x, n):
    out = []
    for _ in range(n):
        bits = struct.unpack("<I", struct.pack("<f", x))[0]
        bits = (bits + 0x7FFF + ((bits >> 16) & 1)) & 0xFFFF0000
        piece = struct.unpack("<f", struct.pack("<I", bits))[0]
        out.append(piece)
        x -= piece
    return out


def _mx(x):
    return x.astype(MXU_DTYPE)


def _dot(a, b):
    return jnp.dot(_mx(a), _mx(b), preferred_element_type=F32)


def _dot_nt(a, b):
    return lax.dot_general(_mx(a), _mx(b), (((1,), (1,)), ((), ())), preferred_element_type=F32)


def _dot_tn(a, b):
    return lax.dot_general(_mx(a), _mx(b), (((0,), (0,)), ((), ())), preferred_element_type=F32)


def _rms(x, g):
    return x * lax.rsqrt(jnp.mean(x * x, axis=-1, keepdims=True) + EPS) * g


def _silu(x):
    return x * jax.nn.sigmoid(x)


def _log_sigmoid(z):
    return jnp.minimum(z, 0.0) - jnp.log1p(jnp.exp(-jnp.abs(z)))


def _cumsum_rows(x):
    n = x.shape[0]
    tri = (lax.broadcasted_iota(jnp.int32, (n, n), 1) <= lax.broadcasted_iota(jnp.int32, (n, n), 0)).astype(F32)
    return jnp.dot(tri, x, preferred_element_type=F32, precision=HIGHEST)


def _colreduce(x, op, width=32):
    n, c = x.shape
    return op(op(x.reshape(n // width, width, c), axis=0), axis=0, keepdims=True)


def _params(sem, vmem_mb=40):
    return pltpu.CompilerParams(dimension_semantics=sem, vmem_limit_bytes=vmem_mb * 1024 * 1024)


def _ada_kernel(c_ref, w_ref, b_ref, o_ref):
    cs = _silu(c_ref[...])
    o_ref[0] = jnp.dot(cs, w_ref[0], preferred_element_type=F32, precision=HIGHEST) + b_ref[0]


def _ada_mod(c, ada_w, ada_b):
    depth, d, n = ada_w.shape
    b = c.shape[0]
    return pl.pallas_call(
        _ada_kernel, name="ada_mod",
        grid=(depth, n // d),
        in_specs=[pl.BlockSpec((b, d), lambda l, j: (0, 0)),
                  pl.BlockSpec((1, d, d), lambda l, j: (l, 0, j)),
                  pl.BlockSpec((1, 1, d), lambda l, j: (l, 0, j))],
        out_specs=pl.BlockSpec((1, b, d), lambda l, j: (l, 0, j)),
        out_shape=jax.ShapeDtypeStruct((depth, b, n), F32),
        compiler_params=_params(("parallel", "parallel")),
    )(c, ada_w, ada_b.reshape(depth, 1, n))


def _inproj_kernel(x_ref, sc_ref, sh_ref, g_ref, w_ref, o_ref):
    h = _rms(x_ref[...], g_ref[...]) * (1.0 + sc_ref[0]) + sh_ref[0]
    o_ref[...] = _dot(h, w_ref[...])


def _in_proj(x2, mod3, norm_g, w_pack, seq):
    t, d = x2.shape
    tm = min(512, seq)
    n_col = 3
    cw = N_PACK // n_col
    return pl.pallas_call(
        _inproj_kernel, name="in_proj",
        grid=(n_col, t // tm),
        in_specs=[pl.BlockSpec((tm, d), lambda j, i: (i, 0)),
                  pl.BlockSpec((1, 1, d), lambda j, i: ((i * tm) // seq, 0, 1)),
                  pl.BlockSpec((1, 1, d), lambda j, i: ((i * tm) // seq, 0, 0)),
                  pl.BlockSpec((1, d), lambda j, i: (0, 0)),
                  pl.BlockSpec((d, cw), lambda j, i: (0, j))],
        out_specs=pl.BlockSpec((tm, cw), lambda j, i: (i, j)),
        out_shape=jax.ShapeDtypeStruct((t, N_PACK), F32),
        compiler_params=_params(("parallel", "parallel")),
    )(x2, mod3, mod3, norm_g, w_pack)


def _gla_kernel(q_ref, k_ref, v_ref, g_ref, alr_ref, wa2_ref, ba_ref, gn_ref, o_ref, s_ref, acc_ref):
    @pl.when(pl.program_id(1) == 0)
    def _():
        s_ref[...] = jnp.zeros_like(s_ref)

    L, sub, nh, dk, dv = GLA_CHUNK, GLA_SUB, GLA_HEADS, GLA_DK, GLA_DV
    seqs = range(q_ref.shape[0])
    heads = range(nh)
    hk = lambda h: slice(h * dk, (h + 1) * dk)
    hv = lambda h: slice(h * dv, (h + 1) * dv)

    pre = []
    for g in seqs:
        z = jnp.dot(alr_ref[g], wa2_ref[...], preferred_element_type=F32, precision=HIGHEST) + ba_ref[...]
        cum = _cumsum_rows(_log_sigmoid(z) * (1.0 / GLA_TAU))
        q = q_ref[g] * (dk ** -0.5)
        k = k_ref[g]
        tot = cum[L - 1:L, :]
        pre.append(dict(cum=cum, q=q, k=k, tot=tot, vb=_mx(v_ref[g]),
                        q_in=_mx(q * jnp.exp(cum)), k_dec=_mx(k * jnp.exp(tot - cum))))

    scores = {}
    for i in range(L // sub):
        r0, r1 = i * sub, (i + 1) * sub
        for g in seqs:
            p = pre[g]
            base = p["cum"][r0 - 1:r0, :] if i > 0 else jnp.zeros_like(p["tot"])
            qi = _mx(p["q"][r0:r1] * jnp.exp(p["cum"][r0:r1] - base))
            ka = _mx(p["k"][:r1] * jnp.exp(base - p["cum"][:r1]))
            for h in heads:
                scores[g, i, h] = _dot_nt(qi[:, hk(h)], ka[:, hk(h)])

    for i in range(L // sub):
        r0, r1 = i * sub, (i + 1) * sub
        causal = (lax.broadcasted_iota(jnp.int32, (sub, r1), 1)
                  <= lax.broadcasted_iota(jnp.int32, (sub, r1), 0) + r0)
        for g in seqs:
            for h in heads:
                s = jnp.where(causal, scores[g, i, h], 0.0)
                acc_ref[g, r0:r1, hv(h)] = _dot(s, pre[g]["vb"][:r1, hv(h)])

    inter = {(g, h): _dot(pre[g]["q_in"][:, hk(h)], s_ref[g * nh + h]) for g in seqs for h in heads}
    update = {(g, h): _dot_tn(pre[g]["k_dec"][:, hk(h)], pre[g]["vb"][:, hv(h)]) for g in seqs for h in heads}

    gn = gn_ref[...]
    for g in seqs:
        gate = g_ref[g]
        for h in heads:
            o = acc_ref[g, :, hv(h)] + inter[g, h]
            y = _rms(o, gn[:, hv(h)]) * _silu(gate[:, hv(h)])
            o_ref[g, :, hv(h)] = y.astype(o_ref.dtype)
            decay = jnp.transpose(jnp.exp(pre[g]["tot"][:, hk(h)]))
            s_ref[g * nh + h] = s_ref[g * nh + h] * decay + update[g, h]


def _seq_group(batch):
    for grp in (SEQ_GROUP, 2, 1):
        if batch % grp == 0:
            return grp


def _gla(proj3, wa2_pad, ba, gn):
    b, s, _ = proj3.shape
    L = GLA_CHUNK
    grp = _seq_group(b)
    blk = lambda w, c0: pl.BlockSpec((grp, L, w), lambda bi, ci: (bi, ci, c0 // w))
    full = lambda a: pl.BlockSpec(a.shape, lambda bi, ci: (0,) * a.ndim)
    return pl.pallas_call(
        _gla_kernel, name="gla",
        grid=(b // grp, s // L),
        in_specs=[blk(GLA_QK, C_QA), blk(GLA_QK, C_KA), blk(GLA_V, C_VA), blk(GLA_V, C_GA),
                  blk(LANE, C_ALR), full(wa2_pad), full(ba), full(gn)],
        out_specs=pl.BlockSpec((grp, L, GLA_V), lambda bi, ci: (bi, ci, 0)),
        out_shape=jax.ShapeDtypeStruct((b, s, GLA_V), ACT_DTYPE),
        scratch_shapes=[pltpu.VMEM((grp * GLA_HEADS, GLA_DK, GLA_DV), F32),
                        pltpu.VMEM((grp, L, GLA_V), F32)],
        compiler_params=_params(("parallel", "arbitrary")),
    )(proj3, proj3, proj3, proj3, proj3, wa2_pad, ba, gn)


def _mlstm_kernel(qk_ref, v_ref, oc_ref, if_ref, conv_ref, bias_ref, gn_ref, o_ref,
                  xbuf_ref, c_ref, n_ref, m_ref):
    @pl.when(pl.program_id(1) == 0)
    def _():
        xbuf_ref[:, 0:CONV_PAD, :] = jnp.zeros((xbuf_ref.shape[0], CONV_PAD, 2 * MLSTM_QK), F32)
        c_ref[...] = jnp.zeros_like(c_ref)
        n_ref[...] = jnp.zeros_like(n_ref)
        m_ref[...] = jnp.zeros_like(m_ref)

    L, nh, dk, dv, kc = MLSTM_CHUNK, MLSTM_HEADS, MLSTM_DQK, MLSTM_DV, MLSTM_CONV
    pad = CONV_PAD
    seqs = range(qk_ref.shape[0])
    hk = lambda h: slice(h * dk, (h + 1) * dk)
    hv = lambda h: slice(h * dv, (h + 1) * dv)
    causal = lax.broadcasted_iota(jnp.int32, (L, L), 1) <= lax.broadcasted_iota(jnp.int32, (L, L), 0)
    cw = conv_ref[...]
    gn = gn_ref[...]

    seq = {}
    for g in seqs:
        xbuf_ref[g, pad:pad + L, :] = qk_ref[g]
        conv = jnp.zeros((L, 2 * MLSTM_QK), F32)
        for j in range(kc):
            conv = conv + cw[j:j + 1, :] * xbuf_ref[g, pl.ds(pad - (kc - 1) + j, L), :]
        xbuf_ref[g, 0:pad, :] = xbuf_ref[g, L:L + pad, :]
        qk = _silu(conv)
        pre = if_ref[g] + bias_ref[...]
        bcum = _cumsum_rows(_log_sigmoid(pre))
        seq[g] = dict(q=qk[:, :MLSTM_QK] * (dk ** -0.5), k=qk[:, MLSTM_QK:], v=v_ref[g],
                      pre=pre, bcum=bcum, pre_t=jnp.transpose(pre), bcum_t=jnp.transpose(bcum))

        for h in range(nh):
            _mlstm_heads([(g, h)], seq, causal, gn, oc_ref, o_ref, c_ref, n_ref, m_ref)


def _mlstm_heads(combos, seq, causal, gn, oc_ref, o_ref, c_ref, n_ref, m_ref):
    L, nh, dk, dv = MLSTM_CHUNK, MLSTM_HEADS, MLSTM_DQK, MLSTM_DV
    hk = lambda h: slice(h * dk, (h + 1) * dk)
    hv = lambda h: slice(h * dv, (h + 1) * dv)
    for g, h in combos:
        p = seq[g]
        st = g * nh + h
        b_col = p["bcum"][:, nh + h:nh + h + 1]
        b_row = p["bcum_t"][nh + h:nh + h + 1, :]
        i_col = p["pre"][:, h:h + 1]
        i_row = p["pre_t"][h:h + 1, :]
        m_prev = m_ref[st]
        c_st = c_ref[st]
        n_st = n_ref[st]
        qh, kh, vh = p["q"][:, hk(h)], p["k"][:, hk(h)], p["v"][:, hv(h)]

        dlog = jnp.where(causal, b_col - b_row + i_row, -jnp.inf)
        inter_log = b_col + m_prev
        m_t = jnp.maximum(inter_log, jnp.max(dlog, axis=1, keepdims=True))
        w_intra = jnp.exp(dlog - m_t)
        w_inter = jnp.exp(inter_log - m_t)
        s = _dot_nt(qh, kh) * w_intra
        num = _dot(s, vh) + w_inter * _dot(qh, c_st)
        den = jnp.sum(s, axis=1, keepdims=True) + w_inter * jnp.sum(qh * n_st, axis=1, keepdims=True)
        hout = num / jnp.maximum(jnp.abs(den), jnp.exp(-m_t))
        y = _rms(hout, gn[:, hv(h)]) * jax.nn.sigmoid(oc_ref[g, :, hv(h)])
        o_ref[g, :, hv(h)] = y.astype(o_ref.dtype)

        tot = b_col[L - 1:L, :]
        g_log = tot - b_col + i_col
        m_new = jnp.maximum(tot + m_prev, jnp.max(g_log, axis=0, keepdims=True))
        w_s = jnp.exp(g_log - m_new)
        w_c = jnp.exp(tot + m_prev - m_new)
        k_s = kh * w_s
        c_ref[st] = c_st * w_c + _dot_tn(k_s, vh)
        n_ref[st] = n_st * w_c + jnp.sum(k_s, axis=0, keepdims=True)
        m_ref[st] = m_new


def _mlstm(proj3, conv_w, bias_row, gn):
    b, s, _ = proj3.shape
    L = MLSTM_CHUNK
    grp = _seq_group(b)
    blk = lambda w, c0: pl.BlockSpec((grp, L, w), lambda bi, ci: (bi, ci, c0 // w))
    full = lambda a: pl.BlockSpec(a.shape, lambda bi, ci: (0,) * a.ndim)
    return pl.pallas_call(
        _mlstm_kernel, name="mlstm",
        grid=(b // grp, s // L),
        in_specs=[blk(2 * MLSTM_QK, C_QKC), blk(MLSTM_V, C_VC), blk(MLSTM_V, C_OC), blk(LANE, C_ICFC),
                  full(conv_w), full(bias_row), full(gn)],
        out_specs=pl.BlockSpec((grp, L, MLSTM_V), lambda bi, ci: (bi, ci, 0)),
        out_shape=jax.ShapeDtypeStruct((b, s, MLSTM_V), ACT_DTYPE),
        scratch_shapes=[pltpu.VMEM((grp, L + CONV_PAD, 2 * MLSTM_QK), F32),
                        pltpu.VMEM((grp * MLSTM_HEADS, MLSTM_DQK, MLSTM_DV), F32),
                        pltpu.VMEM((grp * MLSTM_HEADS, 1, MLSTM_DQK), F32),
                        pltpu.VMEM((grp * MLSTM_HEADS, 1, 1), F32)],
        compiler_params=_params(("parallel", "arbitrary")),
    )(proj3, proj3, proj3, proj3, conv_w, bias_row, gn)


def _dsa_prep_kernel(cq_ref, ckv_ref, kw_ref, nq_ref, nkv_ref, wuq_ref, wuk_ref, wqi_ref,
                     ckvn_ref, kwb_ref, qlat_ref, qidx_ref, wht_ref, *, tiles_per_seq):
    tm = cq_ref.shape[0]
    r = DSA_KV_RANK
    cqn = _mx(_rms(cq_ref[...], nq_ref[...]))
    pos = (pl.program_id(0) % tiles_per_seq) * tm + lax.broadcasted_iota(jnp.int32, (tm, LANE), 0)
    lane = lax.broadcasted_iota(jnp.int32, (tm, LANE), 1)
    pos_cols = jnp.where(lane < 3, pos >> 6, jnp.where(lane < 6, pos & 63, 0)).astype(F32)
    ckvn_ref[:, :r] = _rms(ckv_ref[...], nkv_ref[...]).astype(ckvn_ref.dtype)
    ckvn_ref[:, r:] = pos_cols.astype(ckvn_ref.dtype)
    kw = kw_ref[...]
    kwb_ref[...] = kw.astype(kwb_ref.dtype)
    wht_ref[0] = jnp.transpose(kw)[W_IDX_LANE:W_IDX_LANE + IDX_HEADS, :] * (IDX_HEADS ** -0.5)
    q = jnp.dot(cqn, wuq_ref[...], preferred_element_type=F32)
    for h in range(DSA_HEADS):
        ql = _dot(q[:, h * DSA_HEAD_DIM:(h + 1) * DSA_HEAD_DIM], wuk_ref[h]) * (DSA_HEAD_DIM ** -0.5 * LOG2E)
        qlat_ref[0, h, :, :r] = ql.astype(qlat_ref.dtype)
        c = _bf16_pieces(2.0 ** (-8.0 * (h + 1) / DSA_HEADS) * LOG2E, 3)
        consts = [64.0 * c[0], 64.0 * c[1], 64.0 * c[2], c[0], c[1], c[2]]
        slope_cols = jnp.zeros((tm, LANE), F32)
        for j, v in enumerate(consts):
            slope_cols = jnp.where(lane == j, v, slope_cols)
        qlat_ref[0, h, :, r:] = slope_cols.astype(qlat_ref.dtype)
    qi = jnp.dot(cqn, wqi_ref[...], preferred_element_type=F32) * (IDX_DIM ** -0.5)
    for h in range(IDX_HEADS):
        qidx_ref[0, h] = qi[:, h * LANE:(h + 1) * LANE].astype(qidx_ref.dtype)


def _dsa_prep(proj2, seq, nq, nkv, wuq, wuk_t, wqi_pad):
    t = proj2.shape[0]
    b = t // seq
    tm = min(512, seq)
    per = seq // tm
    blk = lambda w, c0: pl.BlockSpec((tm, w), lambda i: (i, c0 // w))
    full = lambda a: pl.BlockSpec(a.shape, lambda i: (0,) * a.ndim)
    hmap = lambda i: (i // per, 0, i % per, 0)
    return pl.pallas_call(
        functools.partial(_dsa_prep_kernel, tiles_per_seq=per), name="dsa_prep",
        grid=(t // tm,),
        in_specs=[blk(DSA_Q_RANK, C_CQ), blk(DSA_KV_RANK, C_CKV), blk(LANE, C_KW),
                  full(nq), full(nkv), full(wuq), full(wuk_t), full(wqi_pad)],
        out_specs=[pl.BlockSpec((tm, DSA_AUG), lambda i: (i, 0)),
                   pl.BlockSpec((tm, LANE), lambda i: (i, 0)),
                   pl.BlockSpec((1, DSA_HEADS, tm, DSA_AUG), hmap),
                   pl.BlockSpec((1, IDX_HEADS, tm, LANE), hmap),
                   pl.BlockSpec((1, IDX_HEADS, tm), lambda i: (i // per, 0, i % per))],
        out_shape=[jax.ShapeDtypeStruct((t, DSA_AUG), ACT_DTYPE),
                   jax.ShapeDtypeStruct((t, LANE), ACT_DTYPE),
                   jax.ShapeDtypeStruct((b, DSA_HEADS, seq, DSA_AUG), ACT_DTYPE),
                   jax.ShapeDtypeStruct((b, IDX_HEADS, seq, LANE), ACT_DTYPE),
                   jax.ShapeDtypeStruct((b, IDX_HEADS, seq), F32)],
        compiler_params=_params(("parallel",)),
    )(proj2, proj2, proj2, nq, nkv, wuq, wuk_t, wqi_pad)


def _dsa_kernel(qlat_ref, qidx_ref, wht_ref, kwk_ref, ckv_ref, wuv_ref, o_ref,
                ibuf_ref, acc_ref, *, topk):
    nh, tq, tk = DSA_HEADS, Q_TILE, KEY_TILE
    qb = pl.program_id(1)
    n_kt = (qb * tq + tq + tk - 1) // tk
    t_row = qb * tq + lax.broadcasted_iota(jnp.int32, (1, tq), 1)
    s_col = lax.broadcasted_iota(jnp.int32, (tk, 1), 0)

    qi = qidx_ref[0].reshape(IDX_HEADS * tq, LANE)
    wht = wht_ref[0]

    def idx_body(kt, carry):
        kk = kwk_ref[0, pl.ds(pl.multiple_of(kt * tk, tk), tk), :]
        sc = jnp.maximum(_dot_nt(kk, qi), 0.0)
        tot = wht[0:1, :] * sc[:, 0:tq]
        for h in range(1, IDX_HEADS):
            tot = tot + wht[h:h + 1, :] * sc[:, h * tq:(h + 1) * tq]
        ibuf_ref[kt] = jnp.where(kt * tk + s_col <= t_row, tot, -jnp.inf)
        return carry

    lax.fori_loop(0, n_kt, idx_body, 0)

    def count(pred):
        def body(kt, c):
            hit = jnp.where(pred(ibuf_ref[kt]), 1.0, 0.0)
            return c + jnp.sum(hit.reshape(tk // 32, 32, tq), axis=0)
        return jnp.sum(lax.fori_loop(0, n_kt, body, jnp.zeros((32, tq), F32)), axis=0, keepdims=True)

    def count_ge(cand):
        return count(lambda x: x >= cand)

    def count_gt(cand):
        return count(lambda x: x > cand)

    def key_to_float(u):
        key = u ^ jnp.int32(-2 ** 31)
        bits = jnp.where(key >= 0, key, key ^ jnp.int32(0x7FFFFFFF))
        return lax.bitcast_convert_type(bits, F32)

    few = t_row < topk
    n_bits = 32

    def search_cond(st):
        i, _, _, pending = st
        return (i < n_bits) & (pending > 0)

    def search_body(st):
        i, u, cnt_u, _ = st
        for _ in range(4):
            cand_u = u | lax.shift_left(jnp.int32(1), n_bits - 1 - i)
            cnt = count_ge(key_to_float(cand_u))
            ok = cnt >= topk
            u = jnp.where(ok, cand_u, u)
            cnt_u = jnp.where(ok, cnt, cnt_u)
            i = i + 1
        pending = jnp.max(jnp.where(few | (cnt_u == topk), 0, 1))
        return i, u, cnt_u, pending

    start = (jnp.int32(0), jnp.zeros((1, tq), jnp.int32), jnp.full((1, tq), float(2 ** 30), F32),
             (n_kt * tk > topk).astype(jnp.int32))
    _, u, _, _ = lax.while_loop(search_cond, search_body, start)
    tau = jnp.where(few, -jnp.inf, key_to_float(u))
    need = topk - count_gt(tau)

    acc_ref[...] = jnp.zeros_like(acc_ref)
    ql = qlat_ref[0].reshape(nh * tq, DSA_AUG)
    tri = jnp.where(lax.broadcasted_iota(jnp.int32, (tk, tk), 0) >= lax.broadcasted_iota(jnp.int32, (tk, tk), 1),
                    1.0, 0.0).astype(BF16)

    def keys(kt):
        return ckv_ref[0, pl.ds(pl.multiple_of(kt * tk, tk), tk), :]

    def att_body(kt, carry):
        m, l, eq_seen = carry
        kv_aug = keys(kt)
        lg = _dot_nt(kv_aug, ql)
        it = ibuf_ref[kt]
        eq = it == tau
        eqf = jnp.where(eq, 1.0, 0.0)
        rank = jnp.dot(tri, eqf.astype(BF16), preferred_element_type=F32) + eq_seen
        valid = ((it > tau) | (eq & (rank <= need))) & (kt * tk + s_col <= t_row)
        bias = jnp.where(valid, 0.0, NEG_BIG)
        ms, ls, als, ps = [], [], [], []
        for h in range(nh):
            cols = slice(h * tq, (h + 1) * tq)
            lh = lg[:, cols] + bias
            m_old = m[:, cols]
            m_new = jnp.maximum(m_old, _colreduce(lh, jnp.max))
            p = jnp.exp2(lh - m_new)
            alpha = jnp.exp2(m_old - m_new)
            ms.append(m_new)
            ls.append(alpha * l[:, cols] + _colreduce(p, jnp.sum))
            als.append(alpha)
            ps.append(_mx(p))
        acc_ref[...] = (jnp.concatenate(als, axis=1) * acc_ref[...]
                        + _dot_tn(kv_aug[:, :DSA_KV_RANK], jnp.concatenate(ps, axis=1)))
        return (jnp.concatenate(ms, axis=1), jnp.concatenate(ls, axis=1),
                eq_seen + jnp.sum(eqf, axis=0, keepdims=True))

    init = (jnp.full((1, nh * tq), NEG_BIG, F32), jnp.zeros((1, nh * tq), F32), jnp.zeros((1, tq), F32))
    _, l, _ = lax.fori_loop(0, n_kt, att_body, init)

    outs = []
    for h in range(nh):
        cols = slice(h * tq, (h + 1) * tq)
        outs.append(_dot(wuv_ref[h], acc_ref[:, cols] / l[:, cols]))
    o_ref[0] = jnp.transpose(jnp.concatenate(outs, axis=0)).astype(o_ref.dtype)


def _dsa(ckvn3, kwb3, qlat, qidx, wht, wuv_t):
    b, s, _ = ckvn3.shape
    topk = min(DSA_TOPK_MAX, s // 4)
    nh, tq = DSA_HEADS, Q_TILE
    return pl.pallas_call(
        functools.partial(_dsa_kernel, topk=topk), name="dsa_attn",
        grid=(b, s // tq),
        in_specs=[pl.BlockSpec((1, nh, tq, DSA_AUG), lambda bi, qi: (bi, 0, qi, 0)),
                  pl.BlockSpec((1, IDX_HEADS, tq, LANE), lambda bi, qi: (bi, 0, qi, 0)),
                  pl.BlockSpec((1, IDX_HEADS, tq), lambda bi, qi: (bi, 0, qi)),
                  pl.BlockSpec((1, s, LANE), lambda bi, qi: (bi, 0, 0)),
                  pl.BlockSpec((1, s, DSA_AUG), lambda bi, qi: (bi, 0, 0)),
                  pl.BlockSpec(wuv_t.shape, lambda bi, qi: (0, 0, 0))],
        out_specs=pl.BlockSpec((1, tq, DSA_V), lambda bi, qi: (bi, qi, 0)),
        out_shape=jax.ShapeDtypeStruct((b, s, DSA_V), ACT_DTYPE),
        scratch_shapes=[pltpu.VMEM((s // KEY_TILE, KEY_TILE, tq), F32),
                        pltpu.VMEM((DSA_KV_RANK, nh * tq), F32)],
        compiler_params=_params(("parallel", "arbitrary")),
    )(qlat, qidx, wht, kwb3, ckvn3, wuv_t)


def _merge_kernel(a_ref, b_ref, c_ref, gt_ref, x_ref, g1_ref, wa_ref, wb_ref, wc_ref, wo_ref, o_ref):
    d = D_MODEL
    g = jax.nn.sigmoid(gt_ref[...])
    ya = jnp.dot(a_ref[...], wa_ref[...], preferred_element_type=F32)
    yb = jnp.dot(b_ref[...], wb_ref[...], preferred_element_type=F32)
    yc = jnp.dot(c_ref[...], wc_ref[...], preferred_element_type=F32)
    m = g[:, :d] * ya + g[:, d:2 * d] * yb + g[:, 2 * d:] * yc
    o_ref[...] = x_ref[...] + g1_ref[0] * _dot(m, wo_ref[...])


def _merge(ya_in, yb_in, yc_in, proj2, x2, mod3, wa, wb, wc, wo, seq):
    t, d = x2.shape
    tm = min(512, seq)
    full = lambda a: pl.BlockSpec(a.shape, lambda i: (0,) * a.ndim)
    br = lambda w: pl.BlockSpec((tm, w), lambda i: (i, 0))
    return pl.pallas_call(
        _merge_kernel, name="merge",
        grid=(t // tm,),
        in_specs=[br(GLA_V), br(DSA_V), br(MLSTM_V), br(3 * d), br(d),
                  pl.BlockSpec((1, 1, d), lambda i: ((i * tm) // seq, 0, 2)),
                  full(wa), full(wb), full(wc), full(wo)],
        out_specs=br(d),
        out_shape=jax.ShapeDtypeStruct((t, d), F32),
        compiler_params=_params(("parallel",)),
    )(ya_in, yb_in, yc_in, proj2, x2, mod3, wa, wb, wc, wo)


def _first_argmax_mask(cur, iota, axis, n):
    mx = jnp.max(cur, axis=axis, keepdims=True)
    ix = jnp.min(jnp.where(cur == mx, iota, n), axis=axis, keepdims=True)
    return iota == ix


def _router_kernel(x_ref, sc_ref, sh_ref, g_ref, rwt_ref, rb_ref, dest_ref, wgt_ref, starts_ref, plens_ref):
    ne, ng = N_EXPERTS, N_GROUPS
    eg = ne // ng
    h = _rms(x_ref[...], g_ref[...]) * (1.0 + sc_ref[0]) + sh_ref[0]
    tm = h.shape[0]
    logits = lax.dot_general(rwt_ref[...], h, (((1,), (1,)), ((), ())),
                             preferred_element_type=F32, precision=HIGHEST)
    scores = jax.nn.sigmoid(logits)
    sel = scores + rb_ref[...]
    s3 = sel.reshape(ng, eg, tm)
    io3 = lax.broadcasted_iota(jnp.int32, (ng, eg, tm), 1)
    m1 = jnp.max(s3, axis=1, keepdims=True)
    first = _first_argmax_mask(s3, io3, 1, eg)
    m2 = jnp.max(jnp.where(first, -jnp.inf, s3), axis=1, keepdims=True)
    gs = (m1 + m2).reshape(ng, tm)
    iog = lax.broadcasted_iota(jnp.int32, (ng, tm), 0)
    gkeep = jnp.zeros((ng, tm), F32)
    cur = gs
    for _ in range(TOPK_GROUPS):
        hit = _first_argmax_mask(cur, iog, 0, ng)
        gkeep = jnp.where(hit, 1.0, gkeep)
        cur = jnp.where(hit, -jnp.inf, cur)
    selm = jnp.where(gkeep.reshape(ng, 1, tm) > 0.0, s3, -jnp.inf).reshape(ne, tm)
    ioe = lax.broadcasted_iota(jnp.int32, (ne, tm), 0)
    hits = []
    chosen = jnp.zeros((ne, tm), F32)
    cur = selm
    for _ in range(TOP_K):
        hit = _first_argmax_mask(cur, ioe, 0, ne)
        hits.append(hit)
        chosen = jnp.where(hit, 1.0, chosen)
        cur = jnp.where(hit, -jnp.inf, cur)
    w = chosen * scores
    w = w / jnp.sum(w, axis=0, keepdims=True) * ROUTED_SCALE

    cnt = jnp.sum(chosen, axis=1, keepdims=True)
    plen = jnp.ceil(cnt * (1.0 / ROW_ALIGN)) * ROW_ALIGN
    start = _cumsum_rows(jnp.broadcast_to(plen, (ne, LANE)))[:, :1] - plen
    before = (lax.broadcasted_iota(jnp.int32, (tm, tm), 0)
              < lax.broadcasted_iota(jnp.int32, (tm, tm), 1)).astype(BF16)
    rank = jnp.dot(chosen.astype(BF16), before, preferred_element_type=F32)
    row_of = start + rank
    pad_rows = SLOT_ROWS - TOP_K
    dest = [jnp.sum(jnp.where(hit, row_of, 0.0), axis=0, keepdims=True) for hit in hits]
    wsel = [jnp.sum(jnp.where(hit, w, 0.0), axis=0, keepdims=True) for hit in hits]
    dest_ref[...] = jnp.concatenate(dest + [jnp.full((pad_rows, tm), -1.0, F32)], axis=0).astype(jnp.int32)
    wgt_ref[...] = jnp.concatenate(wsel + [jnp.zeros((pad_rows, tm), F32)], axis=0)
    starts_ref[0] = start.astype(jnp.int32)
    plens_ref[0] = plen.astype(jnp.int32)


def _router(x2, mod3, norm_g, rw_t, rb_col, seq):
    t, d = x2.shape
    tm = MOE_SUB
    nsb = t // tm
    return pl.pallas_call(
        _router_kernel, name="router",
        grid=(nsb,),
        in_specs=[pl.BlockSpec((tm, d), lambda i: (i, 0)),
                  pl.BlockSpec((1, 1, d), lambda i: ((i * tm) // seq, 0, 4)),
                  pl.BlockSpec((1, 1, d), lambda i: ((i * tm) // seq, 0, 3)),
                  pl.BlockSpec((1, d), lambda i: (0, 0)),
                  pl.BlockSpec(rw_t.shape, lambda i: (0, 0)),
                  pl.BlockSpec(rb_col.shape, lambda i: (0, 0))],
        out_specs=[pl.BlockSpec((SLOT_ROWS, tm), lambda i: (0, i)),
                   pl.BlockSpec((SLOT_ROWS, tm), lambda i: (0, i)),
                   pl.BlockSpec((1, N_EXPERTS, 1), lambda i: (i, 0, 0)),
                   pl.BlockSpec((1, N_EXPERTS, 1), lambda i: (i, 0, 0))],
        out_shape=[jax.ShapeDtypeStruct((SLOT_ROWS, t), jnp.int32),
                   jax.ShapeDtypeStruct((SLOT_ROWS, t), F32),
                   jax.ShapeDtypeStruct((nsb, N_EXPERTS, 1), jnp.int32),
                   jax.ShapeDtypeStruct((nsb, N_EXPERTS, 1), jnp.int32)],
        compiler_params=_params(("parallel",)),
    )(x2, mod3, mod3, norm_g, rw_t, rb_col)


def _moe_kernel(starts_ref, plens_ref, x_ref, sc_ref, sh_ref, g2_ref, gn_ref, dest_ref, wgt_ref,
                wg_ref, wu_ref, wd_ref, sg_ref, su_ref, sd_ref, nf_ref, o_ref,
                h_ref, xs_ref, stage_ref, *, final):
    blk, step = pl.program_id(0), pl.program_id(1)
    eps = wg_ref.shape[0]
    ne = pl.num_programs(1) * eps
    sub, rt, mt, mt0 = MOE_SUB, MOE_ROW_TILE, MOE_FFN_TILE, MOE_FIRST_TILE
    nsub = x_ref.shape[0] // sub
    rmax = xs_ref.shape[1]

    @pl.when(step == 0)
    def _():
        h = _mx(_rms(x_ref[...], gn_ref[...]) * (1.0 + sc_ref[0]) + sh_ref[0])
        h_ref[...] = h
        stage_ref[...] = jnp.zeros_like(stage_ref)
        for sb in range(nsub):
            dest = dest_ref[:, sb * sub:(sb + 1) * sub]
            hs = h_ref[sb * sub:(sb + 1) * sub, :]
            for r in range(rmax // rt):
                rows = r * rt + lax.broadcasted_iota(jnp.int32, (rt, sub), 0)
                pick = jnp.zeros((rt, sub), F32)
                for j in range(TOP_K):
                    pick = jnp.where(dest[j:j + 1, :] == rows, 1.0, pick)
                xs_ref[sb, r * rt:(r + 1) * rt, :] = _dot(pick, hs).astype(xs_ref.dtype)

    def run(sb, e):
        i = (blk * nsub + sb) * ne + e
        return starts_ref[i], plens_ref[i] // ROW_ALIGN

    def copy_rows(src, src0, dst, dst0, n_chunks):
        def body(k, carry):
            s = pl.multiple_of(src0 + k * ROW_ALIGN, ROW_ALIGN)
            d = pl.multiple_of(dst0 + k * ROW_ALIGN, ROW_ALIGN)
            dst[pl.ds(d, ROW_ALIGN), :] = src[pl.ds(s, ROW_ALIGN), :]
            return carry
        lax.fori_loop(0, n_chunks, body, 0)

    placed, n_rows = [], []
    for j in range(eps):
        cursor = 0
        for sb in range(nsub):
            st, nch = run(sb, step * eps + j)
            copy_rows(xs_ref.at[sb], st, stage_ref.at[j], cursor, nch)
            placed.append((j, sb, st, nch, cursor))
            cursor = cursor + nch * ROW_ALIGN
        n_rows.append(cursor)

    def expert(j, xt):
        return _dot(_silu(_dot(xt, wg_ref[j])) * _dot(xt, wu_ref[j]), wd_ref[j])

    first = [expert(j, stage_ref[j, 0:mt0, :]) for j in range(eps)]
    for j in range(eps):
        stage_ref[j, 0:mt0, :] = first[j].astype(stage_ref.dtype)

    for j in range(eps):
        def more(i, carry, j=j):
            r0 = pl.multiple_of(mt0 + i * mt, ROW_ALIGN)
            stage_ref[j, pl.ds(r0, mt), :] = expert(j, stage_ref[j, pl.ds(r0, mt), :]).astype(stage_ref.dtype)
            return carry
        lax.fori_loop(0, (jnp.maximum(n_rows[j] - mt0, 0) + mt - 1) // mt, more, 0)

    for j, sb, st, nch, at in placed:
        copy_rows(stage_ref.at[j], at, xs_ref.at[sb], st, nch)

    @pl.when(step == pl.num_programs(1) - 1)
    def _():
        for sb in range(nsub):
            tok = slice(sb * sub, (sb + 1) * sub)
            hs = h_ref[tok, :]
            shared = _dot(_silu(_dot(hs, sg_ref[...])) * _dot(hs, su_ref[...]), sd_ref[...])
            dest_t = jnp.transpose(dest_ref[:, tok].astype(F32))
            wgt_t = jnp.transpose(wgt_ref[:, tok])
            routed = jnp.zeros((sub, x_ref.shape[1]), F32)
            for r in range(rmax // rt):
                cols = (r * rt + lax.broadcasted_iota(jnp.int32, (sub, rt), 1)).astype(F32)
                mix = jnp.zeros((sub, rt), F32)
                for j in range(TOP_K):
                    mix = jnp.where(dest_t[:, j:j + 1] == cols, wgt_t[:, j:j + 1], mix)
                routed = routed + _dot(mix, xs_ref[sb, r * rt:(r + 1) * rt, :])
            xo = x_ref[tok, :] + g2_ref[0] * (shared + routed)
            if final:
                xo = _rms(xo, nf_ref[...])
            o_ref[tok, :] = xo


def _round_up(n, m):
    return (n + m - 1) // m * m


def _moe(x2, mod3, norm_g, dest, wgt, starts, plens, wg, wu, wd, sg, su, sd, nf, seq, final):
    t, d = x2.shape
    tm = min(MOE_BLOCK, seq)
    nsub = tm // MOE_SUB
    ne, ff = wg.shape[0], wg.shape[2]
    rmax = _round_up(TOP_K * MOE_SUB + ne * (ROW_ALIGN - 1), MOE_ROW_TILE)
    most_rows = tm + nsub * (ROW_ALIGN - 1)
    stage_rows = MOE_FIRST_TILE + _round_up(max(most_rows - MOE_FIRST_TILE, 0), MOE_FFN_TILE)
    eps = MOE_EXPERTS_PER_STEP
    full = lambda a: pl.BlockSpec(a.shape, lambda i, e, *_: (0,) * a.ndim)
    mod = lambda j: pl.BlockSpec((1, 1, d), lambda i, e, *_: ((i * tm) // seq, 0, j))
    slot = pl.BlockSpec((SLOT_ROWS, tm), lambda i, e, *_: (0, i))
    grid_spec = pltpu.PrefetchScalarGridSpec(
        num_scalar_prefetch=2,
        grid=(t // tm, ne // eps),
        in_specs=[pl.BlockSpec((tm, d), lambda i, e, *_: (i, 0)), mod(4), mod(3), mod(5),
                  pl.BlockSpec((1, d), lambda i, e, *_: (0, 0)), slot, slot,
                  pl.BlockSpec((eps, d, ff), lambda i, e, *_: (e, 0, 0)),
                  pl.BlockSpec((eps, d, ff), lambda i, e, *_: (e, 0, 0)),
                  pl.BlockSpec((eps, ff, d), lambda i, e, *_: (e, 0, 0)),
                  full(sg), full(su), full(sd), full(nf)],
        out_specs=pl.BlockSpec((tm, d), lambda i, e, *_: (i, 0)),
        scratch_shapes=[pltpu.VMEM((tm, d), MXU_DTYPE),
                        pltpu.VMEM((nsub, rmax, d), MXU_DTYPE),
                        pltpu.VMEM((eps, stage_rows, d), MXU_DTYPE)])
    return pl.pallas_call(
        functools.partial(_moe_kernel, final=final), name="moe",
        grid_spec=grid_spec,
        out_shape=jax.ShapeDtypeStruct((t, d), F32),
        compiler_params=_params(("parallel", "arbitrary"), vmem_mb=56),
    )(starts.reshape(-1), plens.reshape(-1), x2, mod3, mod3, mod3, norm_g, dest, wgt,
      wg, wu, wd, sg, su, sd, nf)


def _pack_w_in(w):
    d = w.shape[0]
    offs = [0]
    for n in IN_SIZES:
        offs.append(offs[-1] + n)
    (qa, ka, va, ga, alr, cq, ckv, kidx, widx, qc, kc, vc, ic, fc, oc, gates) = [
        w[:, offs[i]:offs[i + 1]] for i in range(len(IN_SIZES))]
    z = lambda n: jnp.zeros((d, n), w.dtype)
    packed = jnp.concatenate(
        [gates, qa, ka, va, ga, vc, oc, qc, kc, cq, ckv,
         kidx, widx, z(LANE - IDX_DIM - IDX_HEADS),
         alr, z(LANE - GLA_GATE_RANK),
         ic, fc, z(LANE - 2 * MLSTM_HEADS)], axis=1)
    assert packed.shape[1] == N_PACK
    return packed.astype(MXU_DTYPE)


def kernel(x, c, ada_w, ada_b, norm_mix, norm_ffn, w_in, gla_w_a2, gla_b_a, gla_norm, dsa_norm_q,
           dsa_norm_kv, dsa_w_uq, dsa_w_uk, dsa_w_uv, dsa_w_qi, mlstm_conv, mlstm_b_i, mlstm_b_f,
           mlstm_norm, w_up_a, w_up_b, w_up_c, w_o, router_w, router_bias, exp_w_gate, exp_w_up,
           exp_w_down, sh_w_gate, sh_w_up, sh_w_down, norm_final):
    b, s, d = x.shape
    depth = ada_w.shape[0]
    t = b * s
    mod = _ada_mod(c, ada_w, ada_b)
    x2 = x.reshape(t, d)
    row = lambda v: v.reshape(1, -1)
    for l in range(depth):
        mod3 = mod[l].reshape(b, 1, 6 * d)
        proj2 = _in_proj(x2, mod3, row(norm_mix[l]), _pack_w_in(w_in[l]), s)
        proj3 = proj2.reshape(b, s, N_PACK)

        wa2_pad = jnp.zeros((LANE, GLA_QK), F32).at[:GLA_GATE_RANK].set(gla_w_a2[l])
        ya_in = _gla(proj3, wa2_pad, row(gla_b_a[l]), row(gla_norm[l]))

        bias_row = jnp.zeros((1, LANE), F32).at[0, :MLSTM_HEADS].set(mlstm_b_i[l])
        bias_row = bias_row.at[0, MLSTM_HEADS:2 * MLSTM_HEADS].set(mlstm_b_f[l])
        yc_in = _mlstm(proj3, mlstm_conv[l], bias_row, row(mlstm_norm[l]))

        wuq = dsa_w_uq[l].reshape(DSA_Q_RANK, DSA_HEADS * DSA_HEAD_DIM).astype(MXU_DTYPE)
        wuk_t = jnp.transpose(dsa_w_uk[l], (1, 2, 0)).astype(MXU_DTYPE)
        wuv_t = jnp.transpose(dsa_w_uv[l], (1, 2, 0)).astype(MXU_DTYPE)
        wqi_pad = jnp.zeros((DSA_Q_RANK, IDX_HEADS, LANE), F32).at[:, :, :IDX_DIM].set(dsa_w_qi[l])
        wqi_pad = wqi_pad.reshape(DSA_Q_RANK, IDX_HEADS * LANE).astype(MXU_DTYPE)
        ckvn, kwb, qlat, qidx, wht = _dsa_prep(proj2, s, row(dsa_norm_q[l]), row(dsa_norm_kv[l]),
                                               wuq, wuk_t, wqi_pad)
        yb_in = _dsa(ckvn.reshape(b, s, DSA_AUG), kwb.reshape(b, s, LANE), qlat, qidx, wht, wuv_t)

        x2 = _merge(ya_in.reshape(t, GLA_V), yb_in.reshape(t, DSA_V), yc_in.reshape(t, MLSTM_V),
                    proj2, x2, mod3, w_up_a[l].astype(MXU_DTYPE), w_up_b[l].astype(MXU_DTYPE),
                    w_up_c[l].astype(MXU_DTYPE), w_o[l].astype(MXU_DTYPE), s)

        dest, wgt, starts, plens = _router(x2, mod3, row(norm_ffn[l]), jnp.transpose(router_w[l]),
                                           router_bias[l].reshape(-1, 1), s)
        x2 = _moe(x2, mod3, row(norm_ffn[l]), dest, wgt, starts, plens,
                  exp_w_gate[l].astype(MXU_DTYPE), exp_w_up[l].astype(MXU_DTYPE),
                  exp_w_down[l].astype(MXU_DTYPE), sh_w_gate[l].astype(MXU_DTYPE),
                  sh_w_up[l].astype(MXU_DTYPE), sh_w_down[l].astype(MXU_DTYPE), row(norm_final), s,
                  final=(l == depth - 1))
    return x2.reshape(b, s, d)
```

```python
import functools
import struct

import jax
import jax.numpy as jnp
from jax import lax
from jax.experimental import pallas as pl
from jax.experimental.pallas import tpu as pltpu

F32 = jnp.float32
BF16 = jnp.bfloat16
MXU_DTYPE = jnp.bfloat16
ACT_DTYPE = jnp.bfloat16
HIGHEST = lax.Precision.HIGHEST

EPS = 1e-6
D_MODEL = 1024
GLA_HEADS, GLA_DK, GLA_DV, GLA_GATE_RANK, GLA_TAU, GLA_CHUNK = 4, 64, 128, 16, 16.0, 64
GLA_SUB = 16
DSA_HEADS, DSA_Q_RANK, DSA_KV_RANK, DSA_HEAD_DIM, DSA_V_DIM = 8, 256, 128, 64, 64
IDX_HEADS, IDX_DIM, DSA_TOPK_MAX = 8, 32, 256
MLSTM_HEADS, MLSTM_DQK, MLSTM_DV, MLSTM_CONV, MLSTM_CHUNK = 4, 64, 128, 4, 64
N_EXPERTS, TOP_K, N_GROUPS, TOPK_GROUPS, EXPERT_FF, SHARED_FF = 64, 6, 8, 4, 256, 256
ROUTED_SCALE = 2.5

GLA_QK = GLA_HEADS * GLA_DK
GLA_V = GLA_HEADS * GLA_DV
DSA_V = DSA_HEADS * DSA_V_DIM
MLSTM_QK = MLSTM_HEADS * MLSTM_DQK
MLSTM_V = MLSTM_HEADS * MLSTM_DV
IN_SIZES = (GLA_QK, GLA_QK, GLA_V, GLA_V, GLA_GATE_RANK,
            DSA_Q_RANK, DSA_KV_RANK, IDX_DIM, IDX_HEADS,
            MLSTM_QK, MLSTM_QK, MLSTM_V, MLSTM_HEADS, MLSTM_HEADS, MLSTM_V,
            3 * D_MODEL)

LANE = 128
KEY_TILE = 256
Q_TILE = 256
NEG_BIG = -1e30
MOE_BLOCK = 1024
MOE_SUB = 256
MOE_ROW_TILE = 256
MOE_FIRST_TILE = 192
MOE_FFN_TILE = 128
MOE_EXPERTS_PER_STEP = 4
CONV_PAD = 8
SEQ_GROUP = 4
ROW_ALIGN = 16
SLOT_ROWS = 8

C_GATES = 0
C_QA = 3072
C_KA = 3328
C_VA = 3584
C_GA = 4096
C_VC = 4608
C_OC = 5120
C_QKC = 5632
C_CQ = 6144
C_CKV = 6400
C_KW = 6528
C_ALR = 6656
C_ICFC = 6784
N_PACK = 6912
W_IDX_LANE = IDX_DIM


LOG2E = 1.4426950408889634
DSA_AUG = DSA_KV_RANK + LANE


def _bf16_pieces(x, n):
    out = []
    for _ in range(n):
        bits = struct.unpack("<I", struct.pack("<f", x))[0]
        bits = (bits + 0x7FFF + ((bits >> 16) & 1)) & 0xFFFF0000
        piece = struct.unpack("<f", struct.pack("<I", bits))[0]
        out.append(piece)
        x -= piece
    return out


def _mx(x):
    return x.astype(MXU_DTYPE)


def _dot(a, b):
    return jnp.dot(_mx(a), _mx(b), preferred_element_type=F32)


def _dot_nt(a, b):
    return lax.dot_general(_mx(a), _mx(b), (((1,), (1,)), ((), ())), preferred_element_type=F32)


def _dot_tn(a, b):
    return lax.dot_general(_mx(a), _mx(b), (((0,), (0,)), ((), ())), preferred_element_type=F32)


def _rms(x, g):
    return x * lax.rsqrt(jnp.mean(x * x, axis=-1, keepdims=True) + EPS) * g


def _silu(x):
    return x * jax.nn.sigmoid(x)


def _log_sigmoid(z):
    return jnp.minimum(z, 0.0) - jnp.log1p(jnp.exp(-jnp.abs(z)))


def _cumsum_rows(x):
    n = x.shape[0]
    tri = (lax.broadcasted_iota(jnp.int32, (n, n), 1) <= lax.broadcasted_iota(jnp.int32, (n, n), 0)).astype(F32)
    return jnp.dot(tri, x, preferred_element_type=F32, precision=HIGHEST)


def _colreduce(x, op, width=32):
    n, c = x.shape
    return op(op(x.reshape(n // width, width, c), axis=0), axis=0, keepdims=True)


def _params(sem, vmem_mb=40):
    return pltpu.CompilerParams(dimension_semantics=sem, vmem_limit_bytes=vmem_mb * 1024 * 1024)


def _ada_kernel(c_ref, w_ref, b_ref, o_ref):
    cs = _silu(c_ref[...])
    o_ref[0] = jnp.dot(cs, w_ref[0], preferred_element_type=F32, precision=HIGHEST) + b_ref[0]


def _ada_mod(c, ada_w, ada_b):
    depth, d, n = ada_w.shape
    b = c.shape[0]
    return pl.pallas_call(
        _ada_kernel, name="ada_mod",
        grid=(depth, n // d),
        in_specs=[pl.BlockSpec((b, d), lambda l, j: (0, 0)),
                  pl.BlockSpec((1, d, d), lambda l, j: (l, 0, j)),
                  pl.BlockSpec((1, 1, d), lambda l, j: (l, 0, j))],
        out_specs=pl.BlockSpec((1, b, d), lambda l, j: (l, 0, j)),
        out_shape=jax.ShapeDtypeStruct((depth, b, n), F32),
        compiler_params=_params(("parallel", "parallel")),
    )(c, ada_w, ada_b.reshape(depth, 1, n))


def _inproj_kernel(x_ref, sc_ref, sh_ref, g_ref, w_ref, o_ref):
    h = _rms(x_ref[...], g_ref[...]) * (1.0 + sc_ref[0]) + sh_ref[0]
    o_ref[...] = _dot(h, w_ref[...])


def _in_proj(x2, mod3, norm_g, w_pack, seq):
    t, d = x2.shape
    tm = min(512, seq)
    n_col = 3
    cw = N_PACK // n_col
    return pl.pallas_call(
        _inproj_kernel, name="in_proj",
        grid=(n_col, t // tm),
        in_specs=[pl.BlockSpec((tm, d), lambda j, i: (i, 0)),
                  pl.BlockSpec((1, 1, d), lambda j, i: ((i * tm) // seq, 0, 1)),
                  pl.BlockSpec((1, 1, d), lambda j, i: ((i * tm) // seq, 0, 0)),
                  pl.BlockSpec((1, d), lambda j, i: (0, 0)),
                  pl.BlockSpec((d, cw), lambda j, i: (0, j))],
        out_specs=pl.BlockSpec((tm, cw), lambda j, i: (i, j)),
        out_shape=jax.ShapeDtypeStruct((t, N_PACK), F32),
        compiler_params=_params(("parallel", "parallel")),
    )(x2, mod3, mod3, norm_g, w_pack)


def _gla_kernel(q_ref, k_ref, v_ref, g_ref, alr_ref, wa2_ref, ba_ref, gn_ref, o_ref, s_ref, acc_ref):
    @pl.when(pl.program_id(1) == 0)
    def _():
        s_ref[...] = jnp.zeros_like(s_ref)

    L, sub, nh, dk, dv = GLA_CHUNK, GLA_SUB, GLA_HEADS, GLA_DK, GLA_DV
    seqs = range(q_ref.shape[0])
    heads = range(nh)
    hk = lambda h: slice(h * dk, (h + 1) * dk)
    hv = lambda h: slice(h * dv, (h + 1) * dv)

    pre = []
    for g in seqs:
        z = jnp.dot(alr_ref[g], wa2_ref[...], preferred_element_type=F32, precision=HIGHEST) + ba_ref[...]
        cum = _cumsum_rows(_log_sigmoid(z) * (1.0 / GLA_TAU))
        q = q_ref[g] * (dk ** -0.5)
        k = k_ref[g]
        tot = cum[L - 1:L, :]
        pre.append(dict(cum=cum, q=q, k=k, tot=tot, vb=_mx(v_ref[g]),
                        q_in=_mx(q * jnp.exp(cum)), k_dec=_mx(k * jnp.exp(tot - cum))))

    scores = {}
    for i in range(L // sub):
        r0, r1 = i * sub, (i + 1) * sub
        for g in seqs:
            p = pre[g]
            base = p["cum"][r0 - 1:r0, :] if i > 0 else jnp.zeros_like(p["tot"])
            qi = _mx(p["q"][r0:r1] * jnp.exp(p["cum"][r0:r1] - base))
            ka = _mx(p["k"][:r1] * jnp.exp(base - p["cum"][:r1]))
            for h in heads:
                scores[g, i, h] = _dot_nt(qi[:, hk(h)], ka[:, hk(h)])

    for i in range(L // sub):
        r0, r1 = i * sub, (i + 1) * sub
        causal = (lax.broadcasted_iota(jnp.int32, (sub, r1), 1)
                  <= lax.broadcasted_iota(jnp.int32, (sub, r1), 0) + r0)
        for g in seqs:
            for h in heads:
                s = jnp.where(causal, scores[g, i, h], 0.0)
                acc_ref[g, r0:r1, hv(h)] = _dot(s, pre[g]["vb"][:r1, hv(h)])

    inter = {(g, h): _dot(pre[g]["q_in"][:, hk(h)], s_ref[g * nh + h]) for g in seqs for h in heads}
    update = {(g, h): _dot_tn(pre[g]["k_dec"][:, hk(h)], pre[g]["vb"][:, hv(h)]) for g in seqs for h in heads}

    gn = gn_ref[...]
    for g in seqs:
        gate = g_ref[g]
        for h in heads:
            o = acc_ref[g, :, hv(h)] + inter[g, h]
            y = _rms(o, gn[:, hv(h)]) * _silu(gate[:, hv(h)])
            o_ref[g, :, hv(h)] = y.astype(o_ref.dtype)
            decay = jnp.transpose(jnp.exp(pre[g]["tot"][:, hk(h)]))
            s_ref[g * nh + h] = s_ref[g * nh + h] * decay + update[g, h]


def _seq_group(batch):
    for grp in (SEQ_GROUP, 2, 1):
        if batch % grp == 0:
            return grp


def _gla(proj3, wa2_pad, ba, gn):
    b, s, _ = proj3.shape
    L = GLA_CHUNK
    grp = _seq_group(b)
    blk = lambda w, c0: pl.BlockSpec((grp, L, w), lambda bi, ci: (bi, ci, c0 // w))
    full = lambda a: pl.BlockSpec(a.shape, lambda bi, ci: (0,) * a.ndim)
    return pl.pallas_call(
        _gla_kernel, name="gla",
        grid=(b // grp, s // L),
        in_specs=[blk(GLA_QK, C_QA), blk(GLA_QK, C_KA), blk(GLA_V, C_VA), blk(GLA_V, C_GA),
                  blk(LANE, C_ALR), full(wa2_pad), full(ba), full(gn)],
        out_specs=pl.BlockSpec((grp, L, GLA_V), lambda bi, ci: (bi, ci, 0)),
        out_shape=jax.ShapeDtypeStruct((b, s, GLA_V), ACT_DTYPE),
        scratch_shapes=[pltpu.VMEM((grp * GLA_HEADS, GLA_DK, GLA_DV), F32),
                        pltpu.VMEM((grp, L, GLA_V), F32)],
        compiler_params=_params(("parallel", "arbitrary")),
    )(proj3, proj3, proj3, proj3, proj3, wa2_pad, ba, gn)


def _mlstm_pair_kernel(qk_ref, v_ref, oc_ref, if_ref, conv_ref, bias_ref, gnt_ref, o_ref,
                       xbuf_ref, ct_ref, n_ref, m_ref):
    L, nh, dk, dv, kc = MLSTM_CHUNK, MLSTM_HEADS, MLSTM_DQK, MLSTM_DV, MLSTM_CONV
    pad = CONV_PAD
    npair = nh // 2
    assert 2 * dk == LANE and dv == LANE and L == dk

    @pl.when(pl.program_id(1) == 0)
    def _():
        xbuf_ref[:, 0:pad, :] = jnp.zeros((xbuf_ref.shape[0], pad, 2 * MLSTM_QK), F32)
        ct_ref[...] = jnp.zeros_like(ct_ref)
        n_ref[...] = jnp.zeros_like(n_ref)
        m_ref[...] = jnp.zeros_like(m_ref)

    lane = lax.broadcasted_iota(jnp.int32, (1, LANE), 1)
    half = [(lane < dk).astype(F32), (lane >= dk).astype(F32)]
    s_idx = lax.broadcasted_iota(jnp.int32, (L, LANE), 0)
    t_idx = lax.broadcasted_iota(jnp.int32, (L, LANE), 1) % L
    causal = s_idx <= t_idx
    lane_in = lax.broadcasted_iota(jnp.int32, (LANE, LANE), 0)
    head_of = lax.broadcasted_iota(jnp.int32, (LANE, LANE), 1) // dk
    cw = conv_ref[...]

    combos = [(g, p) for g in range(qk_ref.shape[0]) for p in range(npair)]
    seq = {}
    for g in range(qk_ref.shape[0]):
        xbuf_ref[g, pad:pad + L, :] = qk_ref[g]
        conv = jnp.zeros((L, 2 * MLSTM_QK), F32)
        for j in range(kc):
            conv = conv + cw[j:j + 1, :] * xbuf_ref[g, pl.ds(pad - (kc - 1) + j, L), :]
        xbuf_ref[g, 0:pad, :] = xbuf_ref[g, L:L + pad, :]
        qk = _silu(conv)
        pre = if_ref[g] + bias_ref[...]
        bcum = _cumsum_rows(_log_sigmoid(pre))
        v = v_ref[g]
        seq[g] = dict(q=qk[:, :MLSTM_QK] * (dk ** -0.5), k=qk[:, MLSTM_QK:],
                      vt=[_mx(jnp.transpose(v[:, h * dv:(h + 1) * dv])) for h in range(nh)],
                      gate_mix=jnp.where(lane < nh, pre, -bcum), bcum=bcum, bcum_t=jnp.transpose(bcum))

    st = {}
    for g, p in combos:
        sq = seq[g]
        sel = jnp.where((lane_in == 2 * p + head_of) | (lane_in == nh + 2 * p + head_of), 1.0, 0.0)
        selb = jnp.where(lane_in == nh + 2 * p + head_of, 1.0, 0.0)
        d_mat = jnp.dot(sq["gate_mix"], sel, preferred_element_type=F32, precision=HIGHEST)
        tot = jnp.dot(sq["bcum"][L - 8:L, :], selb, preferred_element_type=F32, precision=HIGHEST)[7:8, :]
        b_row = jnp.concatenate([sq["bcum_t"][nh + 2 * p:nh + 2 * p + 1, :],
                                 sq["bcum_t"][nh + 2 * p + 1:nh + 2 * p + 2, :]], axis=1)
        qt = sq["q"][:, p * LANE:(p + 1) * LANE]
        kt = sq["k"][:, p * LANE:(p + 1) * LANE]
        q_bd = _mx(jnp.concatenate([qt * half[0], qt * half[1]], axis=0))
        st[g, p] = dict(d=d_mat, tot=tot, b_row=b_row, kt=kt, q_bd=q_bd,
                        scores=_dot_nt(kt, q_bd),
                        qn=_dot_nt(jnp.broadcast_to(n_ref[g * npair + p], (8, LANE)), q_bd)[0:1, :],
                        inter=[_dot_nt(ct_ref[g * nh + 2 * p + hh], q_bd) for hh in range(2)])

    for g, p in combos:
        c = st[g, p]
        m_prev = m_ref[g * npair + p]
        dlog = jnp.where(causal, c["b_row"] + c["d"], -jnp.inf)
        inter_log = c["b_row"] + m_prev
        m_t = jnp.maximum(inter_log, jnp.max(dlog, axis=0, keepdims=True))
        g_log = c["tot"] + c["d"]
        m_new = jnp.maximum(c["tot"] + m_prev, jnp.max(g_log, axis=0, keepdims=True))
        c.update(m_t=m_t, w_inter=jnp.exp(inter_log - m_t), sw=c["scores"] * jnp.exp(dlog - m_t),
                 m_new=m_new, w_c=jnp.exp(c["tot"] + m_prev - m_new), ks=c["kt"] * jnp.exp(g_log - m_new))

    for g, p in combos:
        c = st[g, p]
        vt = seq[g]["vt"]
        c["sv"] = [_dot(vt[2 * p + hh], c["sw"] * half[hh]) for hh in range(2)]
        c["kv"] = [_dot(vt[2 * p + hh], c["ks"] * half[hh]) for hh in range(2)]

    for g, p in combos:
        c = st[g, p]
        num = c["sv"][0] + c["sv"][1] + c["w_inter"] * (c["inter"][0] + c["inter"][1])
        den = jnp.sum(c["sw"], axis=0, keepdims=True) + c["w_inter"] * c["qn"]
        hout = num / jnp.maximum(jnp.abs(den), jnp.exp(-c["m_t"]))
        y_t = hout * lax.rsqrt(jnp.mean(hout * hout, axis=0, keepdims=True) + EPS) * gnt_ref[p]
        y = jnp.transpose(y_t)
        for hh in range(2):
            cols = slice((2 * p + hh) * dv, (2 * p + hh + 1) * dv)
            o_ref[g, :, cols] = (y[hh * L:(hh + 1) * L, :] * jax.nn.sigmoid(oc_ref[g, :, cols])).astype(o_ref.dtype)
            ct_ref[g * nh + 2 * p + hh] = ct_ref[g * nh + 2 * p + hh] * c["w_c"] + c["kv"][hh]
        n_ref[g * npair + p] = n_ref[g * npair + p] * c["w_c"] + jnp.sum(c["ks"], axis=0, keepdims=True)
        m_ref[g * npair + p] = c["m_new"]


def _mlstm(proj3, conv_w, bias_row, gn):
    b, s, _ = proj3.shape
    L = MLSTM_CHUNK
    grp = _seq_group(b)
    blk = lambda w, c0: pl.BlockSpec((grp, L, w), lambda bi, ci: (bi, ci, c0 // w))
    full = lambda a: pl.BlockSpec(a.shape, lambda bi, ci: (0,) * a.ndim)
    npair = MLSTM_HEADS // 2
    gnt = jnp.repeat(jnp.transpose(gn.reshape(npair, 2, MLSTM_DV), (0, 2, 1)), L, axis=2)
    return pl.pallas_call(
        _mlstm_pair_kernel, name="mlstm",
        grid=(b // grp, s // L),
        in_specs=[blk(2 * MLSTM_QK, C_QKC), blk(MLSTM_V, C_VC), blk(MLSTM_V, C_OC), blk(LANE, C_ICFC),
                  full(conv_w), full(bias_row), full(gnt)],
        out_specs=pl.BlockSpec((grp, L, MLSTM_V), lambda bi, ci: (bi, ci, 0)),
        out_shape=jax.ShapeDtypeStruct((b, s, MLSTM_V), ACT_DTYPE),
        scratch_shapes=[pltpu.VMEM((grp, L + CONV_PAD, 2 * MLSTM_QK), F32),
                        pltpu.VMEM((grp * MLSTM_HEADS, MLSTM_DV, LANE), F32),
                        pltpu.VMEM((grp * npair, 1, LANE), F32),
                        pltpu.VMEM((grp * npair, 1, LANE), F32)],
        compiler_params=_params(("parallel", "arbitrary")),
    )(proj3, proj3, proj3, proj3, conv_w, bias_row, gnt)


def _dsa_prep_kernel(cq_ref, ckv_ref, kw_ref, nq_ref, nkv_ref, wuq_ref, wuk_ref, wqi_ref,
                     ckvn_ref, kwb_ref, qlat_ref, qidx_ref, wht_ref, *, tiles_per_seq):
    tm = cq_ref.shape[0]
    r = DSA_KV_RANK
    cqn = _mx(_rms(cq_ref[...], nq_ref[...]))
    pos = (pl.program_id(0) % tiles_per_seq) * tm + lax.broadcasted_iota(jnp.int32, (tm, LANE), 0)
    lane = lax.broadcasted_iota(jnp.int32, (tm, LANE), 1)
    pos_cols = jnp.where(lane < 3, pos >> 6, jnp.where(lane < 6, pos & 63, 0)).astype(F32)
    ckvn_ref[:, :r] = _rms(ckv_ref[...], nkv_ref[...]).astype(ckvn_ref.dtype)
    ckvn_ref[:, r:] = pos_cols.astype(ckvn_ref.dtype)
    kw = kw_ref[...]
    kwb_ref[...] = kw.astype(kwb_ref.dtype)
    wht_ref[0] = jnp.transpose(kw)[W_IDX_LANE:W_IDX_LANE + IDX_HEADS, :] * (IDX_HEADS ** -0.5)
    q = jnp.dot(cqn, wuq_ref[...], preferred_element_type=F32)
    for h in range(DSA_HEADS):
        ql = _dot(q[:, h * DSA_HEAD_DIM:(h + 1) * DSA_HEAD_DIM], wuk_ref[h]) * (DSA_HEAD_DIM ** -0.5 * LOG2E)
        qlat_ref[0, h, :, :r] = ql.astype(qlat_ref.dtype)
        c = _bf16_pieces(2.0 ** (-8.0 * (h + 1) / DSA_HEADS) * LOG2E, 3)
        consts = [64.0 * c[0], 64.0 * c[1], 64.0 * c[2], c[0], c[1], c[2]]
        slope_cols = jnp.zeros((tm, LANE), F32)
        for j, v in enumerate(consts):
            slope_cols = jnp.where(lane == j, v, slope_cols)
        qlat_ref[0, h, :, r:] = slope_cols.astype(qlat_ref.dtype)
    qi = jnp.dot(cqn, wqi_ref[...], preferred_element_type=F32) * (IDX_DIM ** -0.5)
    for h in range(IDX_HEADS):
        qidx_ref[0, h] = qi[:, h * LANE:(h + 1) * LANE].astype(qidx_ref.dtype)


def _dsa_prep(proj2, seq, nq, nkv, wuq, wuk_t, wqi_pad):
    t = proj2.shape[0]
    b = t // seq
    tm = min(512, seq)
    per = seq // tm
    blk = lambda w, c0: pl.BlockSpec((tm, w), lambda i: (i, c0 // w))
    full = lambda a: pl.BlockSpec(a.shape, lambda i: (0,) * a.ndim)
    hmap = lambda i: (i // per, 0, i % per, 0)
    return pl.pallas_call(
        functools.partial(_dsa_prep_kernel, tiles_per_seq=per), name="dsa_prep",
        grid=(t // tm,),
        in_specs=[blk(DSA_Q_RANK, C_CQ), blk(DSA_KV_RANK, C_CKV), blk(LANE, C_KW),
                  full(nq), full(nkv), full(wuq), full(wuk_t), full(wqi_pad)],
        out_specs=[pl.BlockSpec((tm, DSA_AUG), lambda i: (i, 0)),
                   pl.BlockSpec((tm, LANE), lambda i: (i, 0)),
                   pl.BlockSpec((1, DSA_HEADS, tm, DSA_AUG), hmap),
                   pl.BlockSpec((1, IDX_HEADS, tm, LANE), hmap),
                   pl.BlockSpec((1, IDX_HEADS, tm), lambda i: (i // per, 0, i % per))],
        out_shape=[jax.ShapeDtypeStruct((t, DSA_AUG), ACT_DTYPE),
                   jax.ShapeDtypeStruct((t, LANE), ACT_DTYPE),
                   jax.ShapeDtypeStruct((b, DSA_HEADS, seq, DSA_AUG), ACT_DTYPE),
                   jax.ShapeDtypeStruct((b, IDX_HEADS, seq, LANE), ACT_DTYPE),
                   jax.ShapeDtypeStruct((b, IDX_HEADS, seq), F32)],
        compiler_params=_params(("parallel",)),
    )(proj2, proj2, proj2, nq, nkv, wuq, wuk_t, wqi_pad)


def _dsa_kernel(qlat_ref, qidx_ref, wht_ref, kwk_ref, ckv_ref, wuv_ref, o_ref,
                ibuf_ref, acc_ref, *, topk):
    nh, tq, tk = DSA_HEADS, Q_TILE, KEY_TILE
    qb = pl.program_id(1)
    n_kt = (qb * tq + tq + tk - 1) // tk
    t_row = qb * tq + lax.broadcasted_iota(jnp.int32, (1, tq), 1)
    s_col = lax.broadcasted_iota(jnp.int32, (tk, 1), 0)

    qi = qidx_ref[0].reshape(IDX_HEADS * tq, LANE)
    wht = wht_ref[0]

    def idx_body(kt, carry):
        kk = kwk_ref[0, pl.ds(pl.multiple_of(kt * tk, tk), tk), :]
        sc = jnp.maximum(_dot_nt(kk, qi), 0.0)
        tot = wht[0:1, :] * sc[:, 0:tq]
        for h in range(1, IDX_HEADS):
            tot = tot + wht[h:h + 1, :] * sc[:, h * tq:(h + 1) * tq]
        ibuf_ref[kt] = jnp.where(kt * tk + s_col <= t_row, tot, -jnp.inf)
        return carry

    lax.fori_loop(0, n_kt, idx_body, 0)

    def count(pred):
        def body(kt, c):
            hit = jnp.where(pred(ibuf_ref[kt]), 1.0, 0.0)
            return c + jnp.sum(hit.reshape(tk // 32, 32, tq), axis=0)
        return jnp.sum(lax.fori_loop(0, n_kt, body, jnp.zeros((32, tq), F32)), axis=0, keepdims=True)

    def count_ge(cand):
        return count(lambda x: x >= cand)

    def count_gt(cand):
        return count(lambda x: x > cand)

    def key_to_float(u):
        key = u ^ jnp.int32(-2 ** 31)
        bits = jnp.where(key >= 0, key, key ^ jnp.int32(0x7FFFFFFF))
        return lax.bitcast_convert_type(bits, F32)

    few = t_row < topk
    n_bits = 32

    def search_cond(st):
        i, _, _, pending = st
        return (i < n_bits) & (pending > 0)

    def search_body(st):
        i, u, cnt_u, _ = st
        for _ in range(4):
            cand_u = u | lax.shift_left(jnp.int32(1), n_bits - 1 - i)
            cnt = count_ge(key_to_float(cand_u))
            ok = cnt >= topk
            u = jnp.where(ok, cand_u, u)
            cnt_u = jnp.where(ok, cnt, cnt_u)
            i = i + 1
        pending = jnp.max(jnp.where(few | (cnt_u == topk), 0, 1))
        return i, u, cnt_u, pending

    start = (jnp.int32(0), jnp.zeros((1, tq), jnp.int32), jnp.full((1, tq), float(2 ** 30), F32),
             (n_kt * tk > topk).astype(jnp.int32))
    _, u, _, _ = lax.while_loop(search_cond, search_body, start)
    tau = jnp.where(few, -jnp.inf, key_to_float(u))
    need = topk - count_gt(tau)

    acc_ref[...] = jnp.zeros_like(acc_ref)
    ql = qlat_ref[0].reshape(nh * tq, DSA_AUG)
    tri = jnp.where(lax.broadcasted_iota(jnp.int32, (tk, tk), 0) >= lax.broadcasted_iota(jnp.int32, (tk, tk), 1),
                    1.0, 0.0).astype(BF16)

    def keys(kt):
        return ckv_ref[0, pl.ds(pl.multiple_of(kt * tk, tk), tk), :]

    def att_body(kt, carry):
        m, l, eq_seen = carry
        kv_aug = keys(kt)
        lg = _dot_nt(kv_aug, ql)
        it = ibuf_ref[kt]
        eq = it == tau
        eqf = jnp.where(eq, 1.0, 0.0)
        rank = jnp.dot(tri, eqf.astype(BF16), preferred_element_type=F32) + eq_seen
        valid = ((it > tau) | (eq & (rank <= need))) & (kt * tk + s_col <= t_row)
        bias = jnp.where(valid, 0.0, NEG_BIG)
        ms, ls, als, ps = [], [], [], []
        for h in range(nh):
            cols = slice(h * tq, (h + 1) * tq)
            lh = lg[:, cols] + bias
            m_old = m[:, cols]
            m_new = jnp.maximum(m_old, _colreduce(lh, jnp.max))
            p = jnp.exp2(lh - m_new)
            alpha = jnp.exp2(m_old - m_new)
            ms.append(m_new)
            ls.append(alpha * l[:, cols] + _colreduce(p, jnp.sum))
            als.append(alpha)
            ps.append(_mx(p))
        acc_ref[...] = (jnp.concatenate(als, axis=1) * acc_ref[...]
                        + _dot_tn(kv_aug[:, :DSA_KV_RANK], jnp.concatenate(ps, axis=1)))
        return (jnp.concatenate(ms, axis=1), jnp.concatenate(ls, axis=1),
                eq_seen + jnp.sum(eqf, axis=0, keepdims=True))

    init = (jnp.full((1, nh * tq), NEG_BIG, F32), jnp.zeros((1, nh * tq), F32), jnp.zeros((1, tq), F32))
    _, l, _ = lax.fori_loop(0, n_kt, att_body, init)

    outs = []
    for h in range(nh):
        cols = slice(h * tq, (h + 1) * tq)
        outs.append(_dot(wuv_ref[h], acc_ref[:, cols] / l[:, cols]))
    o_ref[0] = jnp.transpose(jnp.concatenate(outs, axis=0)).astype(o_ref.dtype)


def _dsa(ckvn3, kwb3, qlat, qidx, wht, wuv_t):
    b, s, _ = ckvn3.shape
    topk = min(DSA_TOPK_MAX, s // 4)
    nh, tq = DSA_HEADS, Q_TILE
    return pl.pallas_call(
        functools.partial(_dsa_kernel, topk=topk), name="dsa_attn",
        grid=(b, s // tq),
        in_specs=[pl.BlockSpec((1, nh, tq, DSA_AUG), lambda bi, qi: (bi, 0, qi, 0)),
                  pl.BlockSpec((1, IDX_HEADS, tq, LANE), lambda bi, qi: (bi, 0, qi, 0)),
                  pl.BlockSpec((1, IDX_HEADS, tq), lambda bi, qi: (bi, 0, qi)),
                  pl.BlockSpec((1, s, LANE), lambda bi, qi: (bi, 0, 0)),
                  pl.BlockSpec((1, s, DSA_AUG), lambda bi, qi: (bi, 0, 0)),
                  pl.BlockSpec(wuv_t.shape, lambda bi, qi: (0, 0, 0))],
        out_specs=pl.BlockSpec((1, tq, DSA_V), lambda bi, qi: (bi, qi, 0)),
        out_shape=jax.ShapeDtypeStruct((b, s, DSA_V), ACT_DTYPE),
        scratch_shapes=[pltpu.VMEM((s // KEY_TILE, KEY_TILE, tq), F32),
                        pltpu.VMEM((DSA_KV_RANK, nh * tq), F32)],
        compiler_params=_params(("parallel", "arbitrary")),
    )(qlat, qidx, wht, kwb3, ckvn3, wuv_t)


def _merge_kernel(a_ref, b_ref, c_ref, gt_ref, x_ref, g1_ref, wa_ref, wb_ref, wc_ref, wo_ref, o_ref):
    d = D_MODEL
    g = jax.nn.sigmoid(gt_ref[...])
    ya = jnp.dot(a_ref[...], wa_ref[...], preferred_element_type=F32)
    yb = jnp.dot(b_ref[...], wb_ref[...], preferred_element_type=F32)
    yc = jnp.dot(c_ref[...], wc_ref[...], preferred_element_type=F32)
    m = g[:, :d] * ya + g[:, d:2 * d] * yb + g[:, 2 * d:] * yc
    o_ref[...] = x_ref[...] + g1_ref[0] * _dot(m, wo_ref[...])


def _merge(ya_in, yb_in, yc_in, proj2, x2, mod3, wa, wb, wc, wo, seq):
    t, d = x2.shape
    tm = min(512, seq)
    full = lambda a: pl.BlockSpec(a.shape, lambda i: (0,) * a.ndim)
    br = lambda w: pl.BlockSpec((tm, w), lambda i: (i, 0))
    return pl.pallas_call(
        _merge_kernel, name="merge",
        grid=(t // tm,),
        in_specs=[br(GLA_V), br(DSA_V), br(MLSTM_V), br(3 * d), br(d),
                  pl.BlockSpec((1, 1, d), lambda i: ((i * tm) // seq, 0, 2)),
                  full(wa), full(wb), full(wc), full(wo)],
        out_specs=br(d),
        out_shape=jax.ShapeDtypeStruct((t, d), F32),
        compiler_params=_params(("parallel",)),
    )(ya_in, yb_in, yc_in, proj2, x2, mod3, wa, wb, wc, wo)


def _first_argmax_mask(cur, iota, axis, n):
    mx = jnp.max(cur, axis=axis, keepdims=True)
    ix = jnp.min(jnp.where(cur == mx, iota, n), axis=axis, keepdims=True)
    return iota == ix


def _router_kernel(x_ref, sc_ref, sh_ref, g_ref, rwt_ref, rb_ref, dest_ref, wgt_ref, starts_ref, plens_ref):
    ne, ng = N_EXPERTS, N_GROUPS
    eg = ne // ng
    h = _rms(x_ref[...], g_ref[...]) * (1.0 + sc_ref[0]) + sh_ref[0]
    tm = h.shape[0]
    logits = lax.dot_general(rwt_ref[...], h, (((1,), (1,)), ((), ())),
                             preferred_element_type=F32, precision=HIGHEST)
    scores = jax.nn.sigmoid(logits)
    sel = scores + rb_ref[...]
    s3 = sel.reshape(ng, eg, tm)
    io3 = lax.broadcasted_iota(jnp.int32, (ng, eg, tm), 1)
    m1 = jnp.max(s3, axis=1, keepdims=True)
    first = _first_argmax_mask(s3, io3, 1, eg)
    m2 = jnp.max(jnp.where(first, -jnp.inf, s3), axis=1, keepdims=True)
    gs = (m1 + m2).reshape(ng, tm)
    iog = lax.broadcasted_iota(jnp.int32, (ng, tm), 0)
    gkeep = jnp.zeros((ng, tm), F32)
    cur = gs
    for _ in range(TOPK_GROUPS):
        hit = _first_argmax_mask(cur, iog, 0, ng)
        gkeep = jnp.where(hit, 1.0, gkeep)
        cur = jnp.where(hit, -jnp.inf, cur)
    selm = jnp.where(gkeep.reshape(ng, 1, tm) > 0.0, s3, -jnp.inf).reshape(ne, tm)
    ioe = lax.broadcasted_iota(jnp.int32, (ne, tm), 0)
    hits = []
    chosen = jnp.zeros((ne, tm), F32)
    cur = selm
    for _ in range(TOP_K):
        hit = _first_argmax_mask(cur, ioe, 0, ne)
        hits.append(hit)
        chosen = jnp.where(hit, 1.0, chosen)
        cur = jnp.where(hit, -jnp.inf, cur)
    w = chosen * scores
    w = w / jnp.sum(w, axis=0, keepdims=True) * ROUTED_SCALE

    cnt = jnp.sum(chosen, axis=1, keepdims=True)
    plen = jnp.ceil(cnt * (1.0 / ROW_ALIGN)) * ROW_ALIGN
    start = _cumsum_rows(jnp.broadcast_to(plen, (ne, LANE)))[:, :1] - plen
    before = (lax.broadcasted_iota(jnp.int32, (tm, tm), 0)
              < lax.broadcasted_iota(jnp.int32, (tm, tm), 1)).astype(BF16)
    rank = jnp.dot(chosen.astype(BF16), before, preferred_element_type=F32)
    row_of = start + rank
    pad_rows = SLOT_ROWS - TOP_K
    dest = [jnp.sum(jnp.where(hit, row_of, 0.0), axis=0, keepdims=True) for hit in hits]
    wsel = [jnp.sum(jnp.where(hit, w, 0.0), axis=0, keepdims=True) for hit in hits]
    dest_ref[...] = jnp.concatenate(dest + [jnp.full((pad_rows, tm), -1.0, F32)], axis=0).astype(jnp.int32)
    wgt_ref[...] = jnp.concatenate(wsel + [jnp.zeros((pad_rows, tm), F32)], axis=0)
    starts_ref[0] = start.astype(jnp.int32)
    plens_ref[0] = plen.astype(jnp.int32)


def _router(x2, mod3, norm_g, rw_t, rb_col, seq):
    t, d = x2.shape
    tm = MOE_SUB
    nsb = t // tm
    return pl.pallas_call(
        _router_kernel, name="router",
        grid=(nsb,),
        in_specs=[pl.BlockSpec((tm, d), lambda i: (i, 0)),
                  pl.BlockSpec((1, 1, d), lambda i: ((i * tm) // seq, 0, 4)),
                  pl.BlockSpec((1, 1, d), lambda i: ((i * tm) // seq, 0, 3)),
                  pl.BlockSpec((1, d), lambda i: (0, 0)),
                  pl.BlockSpec(rw_t.shape, lambda i: (0, 0)),
                  pl.BlockSpec(rb_col.shape, lambda i: (0, 0))],
        out_specs=[pl.BlockSpec((SLOT_ROWS, tm), lambda i: (0, i)),
                   pl.BlockSpec((SLOT_ROWS, tm), lambda i: (0, i)),
                   pl.BlockSpec((1, N_EXPERTS, 1), lambda i: (i, 0, 0)),
                   pl.BlockSpec((1, N_EXPERTS, 1), lambda i: (i, 0, 0))],
        out_shape=[jax.ShapeDtypeStruct((SLOT_ROWS, t), jnp.int32),
                   jax.ShapeDtypeStruct((SLOT_ROWS, t), F32),
                   jax.ShapeDtypeStruct((nsb, N_EXPERTS, 1), jnp.int32),
                   jax.ShapeDtypeStruct((nsb, N_EXPERTS, 1), jnp.int32)],
        compiler_params=_params(("parallel",)),
    )(x2, mod3, mod3, norm_g, rw_t, rb_col)


def _moe_kernel(starts_ref, plens_ref, x_ref, sc_ref, sh_ref, g2_ref, gn_ref, dest_ref, wgt_ref,
                wg_ref, wu_ref, wd_ref, sg_ref, su_ref, sd_ref, nf_ref, o_ref,
                h_ref, xs_ref, stage_ref, *, final):
    blk, step = pl.program_id(0), pl.program_id(1)
    eps = wg_ref.shape[0]
    ne = pl.num_programs(1) * eps
    sub, rt, mt, mt0 = MOE_SUB, MOE_ROW_TILE, MOE_FFN_TILE, MOE_FIRST_TILE
    nsub = x_ref.shape[0] // sub
    rmax = xs_ref.shape[1]

    @pl.when(step == 0)
    def _():
        h = _mx(_rms(x_ref[...], gn_ref[...]) * (1.0 + sc_ref[0]) + sh_ref[0])
        h_ref[...] = h
        stage_ref[...] = jnp.zeros_like(stage_ref)
        for sb in range(nsub):
            dest = dest_ref[:, sb * sub:(sb + 1) * sub]
            hs = h_ref[sb * sub:(sb + 1) * sub, :]
            for r in range(rmax // rt):
                rows = r * rt + lax.broadcasted_iota(jnp.int32, (rt, sub), 0)
                pick = jnp.zeros((rt, sub), F32)
                for j in range(TOP_K):
                    pick = jnp.where(dest[j:j + 1, :] == rows, 1.0, pick)
                xs_ref[sb, r * rt:(r + 1) * rt, :] = _dot(pick, hs).astype(xs_ref.dtype)

    def run(sb, e):
        i = (blk * nsub + sb) * ne + e
        return starts_ref[i], plens_ref[i] // ROW_ALIGN

    def copy_rows(src, src0, dst, dst0, n_chunks):
        def body(k, carry):
            s = pl.multiple_of(src0 + k * ROW_ALIGN, ROW_ALIGN)
            d = pl.multiple_of(dst0 + k * ROW_ALIGN, ROW_ALIGN)
            dst[pl.ds(d, ROW_ALIGN), :] = src[pl.ds(s, ROW_ALIGN), :]
            return carry
        lax.fori_loop(0, n_chunks, body, 0)

    placed, n_rows = [], []
    for j in range(eps):
        cursor = 0
        for sb in range(nsub):
            st, nch = run(sb, step * eps + j)
            copy_rows(xs_ref.at[sb], st, stage_ref.at[j], cursor, nch)
            placed.append((j, sb, st, nch, cursor))
            cursor = cursor + nch * ROW_ALIGN
        n_rows.append(cursor)

    def expert(j, xt):
        return _dot(_silu(_dot(xt, wg_ref[j])) * _dot(xt, wu_ref[j]), wd_ref[j])

    first = [expert(j, stage_ref[j, 0:mt0, :]) for j in range(eps)]
    for j in range(eps):
        stage_ref[j, 0:mt0, :] = first[j].astype(stage_ref.dtype)

    for j in range(eps):
        def more(i, carry, j=j):
            r0 = pl.multiple_of(mt0 + i * mt, ROW_ALIGN)
            stage_ref[j, pl.ds(r0, mt), :] = expert(j, stage_ref[j, pl.ds(r0, mt), :]).astype(stage_ref.dtype)
            return carry
        lax.fori_loop(0, (jnp.maximum(n_rows[j] - mt0, 0) + mt - 1) // mt, more, 0)

    for j, sb, st, nch, at in placed:
        copy_rows(stage_ref.at[j], at, xs_ref.at[sb], st, nch)

    @pl.when(step == pl.num_programs(1) - 1)
    def _():
        for sb in range(nsub):
            tok = slice(sb * sub, (sb + 1) * sub)
            hs = h_ref[tok, :]
            shared = _dot(_silu(_dot(hs, sg_ref[...])) * _dot(hs, su_ref[...]), sd_ref[...])
            dest_t = jnp.transpose(dest_ref[:, tok].astype(F32))
            wgt_t = jnp.transpose(wgt_ref[:, tok])
            routed = jnp.zeros((sub, x_ref.shape[1]), F32)
            for r in range(rmax // rt):
                cols = (r * rt + lax.broadcasted_iota(jnp.int32, (sub, rt), 1)).astype(F32)
                mix = jnp.zeros((sub, rt), F32)
                for j in range(TOP_K):
                    mix = jnp.where(dest_t[:, j:j + 1] == cols, wgt_t[:, j:j + 1], mix)
                routed = routed + _dot(mix, xs_ref[sb, r * rt:(r + 1) * rt, :])
            xo = x_ref[tok, :] + g2_ref[0] * (shared + routed)
            if final:
                xo = _rms(xo, nf_ref[...])
            o_ref[tok, :] = xo


def _round_up(n, m):
    return (n + m - 1) // m * m


def _moe(x2, mod3, norm_g, dest, wgt, starts, plens, wg, wu, wd, sg, su, sd, nf, seq, final):
    t, d = x2.shape
    tm = min(MOE_BLOCK, seq)
    nsub = tm // MOE_SUB
    ne, ff = wg.shape[0], wg.shape[2]
    rmax = _round_up(TOP_K * MOE_SUB + ne * (ROW_ALIGN - 1), MOE_ROW_TILE)
    most_rows = tm + nsub * (ROW_ALIGN - 1)
    stage_rows = MOE_FIRST_TILE + _round_up(max(most_rows - MOE_FIRST_TILE, 0), MOE_FFN_TILE)
    eps = MOE_EXPERTS_PER_STEP
    full = lambda a: pl.BlockSpec(a.shape, lambda i, e, *_: (0,) * a.ndim)
    once = lambda a: pl.BlockSpec(a.shape, lambda i, e, *_: (0,) * a.ndim, pipeline_mode=pl.Buffered(1))
    mod = lambda j: pl.BlockSpec((1, 1, d), lambda i, e, *_: ((i * tm) // seq, 0, j))
    slot = pl.BlockSpec((SLOT_ROWS, tm), lambda i, e, *_: (0, i))
    grid_spec = pltpu.PrefetchScalarGridSpec(
        num_scalar_prefetch=2,
        grid=(t // tm, ne // eps),
        in_specs=[pl.BlockSpec((tm, d), lambda i, e, *_: (i, 0), pipeline_mode=pl.Buffered(1)),
                  mod(4), mod(3), mod(5),
                  pl.BlockSpec((1, d), lambda i, e, *_: (0, 0)), slot, slot,
                  pl.BlockSpec((eps, d, ff), lambda i, e, *_: (e, 0, 0)),
                  pl.BlockSpec((eps, d, ff), lambda i, e, *_: (e, 0, 0)),
                  pl.BlockSpec((eps, ff, d), lambda i, e, *_: (e, 0, 0)),
                  once(sg), once(su), once(sd), full(nf)],
        out_specs=pl.BlockSpec((tm, d), lambda i, e, *_: (i, 0), pipeline_mode=pl.Buffered(1)),
        scratch_shapes=[pltpu.VMEM((tm, d), MXU_DTYPE),
                        pltpu.VMEM((nsub, rmax, d), MXU_DTYPE),
                        pltpu.VMEM((eps, stage_rows, d), MXU_DTYPE)])
    return pl.pallas_call(
        functools.partial(_moe_kernel, final=final), name="moe",
        grid_spec=grid_spec,
        out_shape=jax.ShapeDtypeStruct((t, d), F32),
        compiler_params=_params(("parallel", "arbitrary"), vmem_mb=58),
    )(starts.reshape(-1), plens.reshape(-1), x2, mod3, mod3, mod3, norm_g, dest, wgt,
      wg, wu, wd, sg, su, sd, nf)


def _pack_w_in(w):
    d = w.shape[0]
    offs = [0]
    for n in IN_SIZES:
        offs.append(offs[-1] + n)
    (qa, ka, va, ga, alr, cq, ckv, kidx, widx, qc, kc, vc, ic, fc, oc, gates) = [
        w[:, offs[i]:offs[i + 1]] for i in range(len(IN_SIZES))]
    z = lambda n: jnp.zeros((d, n), w.dtype)
    packed = jnp.concatenate(
        [gates, qa, ka, va, ga, vc, oc, qc, kc, cq, ckv,
         kidx, widx, z(LANE - IDX_DIM - IDX_HEADS),
         alr, z(LANE - GLA_GATE_RANK),
         ic, fc, z(LANE - 2 * MLSTM_HEADS)], axis=1)
    assert packed.shape[1] == N_PACK
    return packed.astype(MXU_DTYPE)


def kernel(x, c, ada_w, ada_b, norm_mix, norm_ffn, w_in, gla_w_a2, gla_b_a, gla_norm, dsa_norm_q,
           dsa_norm_kv, dsa_w_uq, dsa_w_uk, dsa_w_uv, dsa_w_qi, mlstm_conv, mlstm_b_i, mlstm_b_f,
           mlstm_norm, w_up_a, w_up_b, w_up_c, w_o, router_w, router_bias, exp_w_gate, exp_w_up,
           exp_w_down, sh_w_gate, sh_w_up, sh_w_down, norm_final):
    b, s, d = x.shape
    depth = ada_w.shape[0]
    t = b * s
    mod = _ada_mod(c, ada_w, ada_b)
    x2 = x.reshape(t, d)
    row = lambda v: v.reshape(1, -1)
    for l in range(depth):
        mod3 = mod[l].reshape(b, 1, 6 * d)
        proj2 = _in_proj(x2, mod3, row(norm_mix[l]), _pack_w_in(w_in[l]), s)
        proj3 = proj2.reshape(b, s, N_PACK)

        wa2_pad = jnp.zeros((LANE, GLA_QK), F32).at[:GLA_GATE_RANK].set(gla_w_a2[l])
        ya_in = _gla(proj3, wa2_pad, row(gla_b_a[l]), row(gla_norm[l]))

        bias_row = jnp.zeros((1, LANE), F32).at[0, :MLSTM_HEADS].set(mlstm_b_i[l])
        bias_row = bias_row.at[0, MLSTM_HEADS:2 * MLSTM_HEADS].set(mlstm_b_f[l])
        yc_in = _mlstm(proj3, mlstm_conv[l], bias_row, row(mlstm_norm[l]))

        wuq = dsa_w_uq[l].reshape(DSA_Q_RANK, DSA_HEADS * DSA_HEAD_DIM).astype(MXU_DTYPE)
        wuk_t = jnp.transpose(dsa_w_uk[l], (1, 2, 0)).astype(MXU_DTYPE)
        wuv_t = jnp.transpose(dsa_w_uv[l], (1, 2, 0)).astype(MXU_DTYPE)
        wqi_pad = jnp.zeros((DSA_Q_RANK, IDX_HEADS, LANE), F32).at[:, :, :IDX_DIM].set(dsa_w_qi[l])
        wqi_pad = wqi_pad.reshape(DSA_Q_RANK, IDX_HEADS * LANE).astype(MXU_DTYPE)
        ckvn, kwb, qlat, qidx, wht = _dsa_prep(proj2, s, row(dsa_norm_q[l]), row(dsa_norm_kv[l]),
                                               wuq, wuk_t, wqi_pad)
        yb_in = _dsa(ckvn.reshape(b, s, DSA_AUG), kwb.reshape(b, s, LANE), qlat, qidx, wht, wuv_t)

        x2 = _merge(ya_in.reshape(t, GLA_V), yb_in.reshape(t, DSA_V), yc_in.reshape(t, MLSTM_V),
                    proj2, x2, mod3, w_up_a[l].astype(MXU_DTYPE), w_up_b[l].astype(MXU_DTYPE),
                    w_up_c[l].astype(MXU_DTYPE), w_o[l].astype(MXU_DTYPE), s)

        dest, wgt, starts, plens = _router(x2, mod3, row(norm_ffn[l]), jnp.transpose(router_w[l]),
                                           router_bias[l].reshape(-1, 1), s)
        x2 = _moe(x2, mod3, row(norm_ffn[l]), dest, wgt, starts, plens,
                  exp_w_gate[l].astype(MXU_DTYPE), exp_w_up[l].astype(MXU_DTYPE),
                  exp_w_down[l].astype(MXU_DTYPE), sh_w_gate[l].astype(MXU_DTYPE),
                  sh_w_up[l].astype(MXU_DTYPE), sh_w_down[l].astype(MXU_DTYPE), row(norm_final), s,
                  final=(l == depth - 1))
    return x2.reshape(b, s, d)
```

```python
import functools
import struct

import jax
import jax.numpy as jnp
from jax import lax
from jax.experimental import pallas as pl
from jax.experimental.pallas import tpu as pltpu

F32 = jnp.float32
BF16 = jnp.bfloat16
MXU_DTYPE = jnp.bfloat16
ACT_DTYPE = jnp.bfloat16
HIGHEST = lax.Precision.HIGHEST

EPS = 1e-6
D_MODEL = 1024
GLA_HEADS, GLA_DK, GLA_DV, GLA_GATE_RANK, GLA_TAU, GLA_CHUNK = 4, 64, 128, 16, 16.0, 64
GLA_SUB = 16
DSA_HEADS, DSA_Q_RANK, DSA_KV_RANK, DSA_HEAD_DIM, DSA_V_DIM = 8, 256, 128, 64, 64
IDX_HEADS, IDX_DIM, DSA_TOPK_MAX = 8, 32, 256
MLSTM_HEADS, MLSTM_DQK, MLSTM_DV, MLSTM_CONV, MLSTM_CHUNK = 4, 64, 128, 4, 64
N_EXPERTS, TOP_K, N_GROUPS, TOPK_GROUPS, EXPERT_FF, SHARED_FF = 64, 6, 8, 4, 256, 256
ROUTED_SCALE = 2.5

GLA_QK = GLA_HEADS * GLA_DK
GLA_V = GLA_HEADS * GLA_DV
DSA_V = DSA_HEADS * DSA_V_DIM
MLSTM_QK = MLSTM_HEADS * MLSTM_DQK
MLSTM_V = MLSTM_HEADS * MLSTM_DV
IN_SIZES = (GLA_QK, GLA_QK, GLA_V, GLA_V, GLA_GATE_RANK,
            DSA_Q_RANK, DSA_KV_RANK, IDX_DIM, IDX_HEADS,
            MLSTM_QK, MLSTM_QK, MLSTM_V, MLSTM_HEADS, MLSTM_HEADS, MLSTM_V,
            3 * D_MODEL)

LANE = 128
KEY_TILE = 256
Q_TILE = 256
NEG_BIG = -1e30
MOE_BLOCK = 1024
MOE_SUB = 256
MOE_ROW_TILE = 256
MOE_WINDOW = 48
MOE_FFN_TILE = 128
MOE_EXPERTS_PER_STEP = 4
CONV_PAD = 8
SEQ_GROUP = 4
ROW_ALIGN = 16
SLOT_ROWS = 8

C_GATES = 0
C_QA = 3072
C_KA = 3328
C_VA = 3584
C_GA = 4096
C_VC = 4608
C_OC = 5120
C_QKC = 5632
C_CQ = 6144
N_MAIN = 6400
T_CKV = 0
T_KW = 128
T_ALR = 256
T_ICFC = 384
N_TAIL = 512
N_PACK = N_MAIN + N_TAIL
W_IDX_LANE = IDX_DIM


LOG2E = 1.4426950408889634
DSA_AUG = DSA_KV_RANK + LANE


def _bf16_pieces(x, n):
    out = []
    for _ in range(n):
        bits = struct.unpack("<I", struct.pack("<f", x))[0]
        bits = (bits + 0x7FFF + ((bits >> 16) & 1)) & 0xFFFF0000
        piece = struct.unpack("<f", struct.pack("<I", bits))[0]
        out.append(piece)
        x -= piece
    return out


def _mx(x):
    return x.astype(MXU_DTYPE)


def _dot(a, b):
    return jnp.dot(_mx(a), _mx(b), preferred_element_type=F32)


def _dot_nt(a, b):
    return lax.dot_general(_mx(a), _mx(b), (((1,), (1,)), ((), ())), preferred_element_type=F32)


def _dot_tn(a, b):
    return lax.dot_general(_mx(a), _mx(b), (((0,), (0,)), ((), ())), preferred_element_type=F32)


def _rms(x, g):
    return x * lax.rsqrt(jnp.mean(x * x, axis=-1, keepdims=True) + EPS) * g


def _silu(x):
    return x * jax.nn.sigmoid(x)


def _log_sigmoid(z):
    return jnp.minimum(z, 0.0) - jnp.log1p(jnp.exp(-jnp.abs(z)))


def _cumsum_rows(x):
    n = x.shape[0]
    tri = (lax.broadcasted_iota(jnp.int32, (n, n), 1) <= lax.broadcasted_iota(jnp.int32, (n, n), 0)).astype(F32)
    return jnp.dot(tri, x, preferred_element_type=F32, precision=HIGHEST)


def _colreduce(x, op, width=32):
    n, c = x.shape
    return op(op(x.reshape(n // width, width, c), axis=0), axis=0, keepdims=True)


def _params(sem, vmem_mb=40):
    return pltpu.CompilerParams(dimension_semantics=sem, vmem_limit_bytes=vmem_mb * 1024 * 1024)


def _ada_kernel(c_ref, w_ref, b_ref, o_ref):
    cs = _silu(c_ref[...])
    o_ref[0] = jnp.dot(cs, w_ref[0], preferred_element_type=F32, precision=HIGHEST) + b_ref[0]


def _ada_mod(c, ada_w, ada_b):
    depth, d, n = ada_w.shape
    b = c.shape[0]
    return pl.pallas_call(
        _ada_kernel, name="ada_mod",
        grid=(depth, n // d),
        in_specs=[pl.BlockSpec((b, d), lambda l, j: (0, 0)),
                  pl.BlockSpec((1, d, d), lambda l, j: (l, 0, j)),
                  pl.BlockSpec((1, 1, d), lambda l, j: (l, 0, j))],
        out_specs=pl.BlockSpec((1, b, d), lambda l, j: (l, 0, j)),
        out_shape=jax.ShapeDtypeStruct((depth, b, n), F32),
        compiler_params=_params(("parallel", "parallel")),
    )(c, ada_w, ada_b.reshape(depth, 1, n))


def _inproj_kernel(x_ref, sc_ref, sh_ref, g_ref, w_ref, o_ref):
    h = _rms(x_ref[...], g_ref[...]) * (1.0 + sc_ref[0]) + sh_ref[0]
    o_ref[...] = _dot(h, w_ref[...]).astype(o_ref.dtype)


def _in_proj(x2, mod3, norm_g, w_pack, seq, n_col, out_dtype):
    t, d = x2.shape
    tm = min(512, seq)
    width = w_pack.shape[1]
    cw = width // n_col
    return pl.pallas_call(
        _inproj_kernel, name="in_proj",
        grid=(n_col, t // tm),
        in_specs=[pl.BlockSpec((tm, d), lambda j, i: (i, 0)),
                  pl.BlockSpec((1, 1, d), lambda j, i: ((i * tm) // seq, 0, 1)),
                  pl.BlockSpec((1, 1, d), lambda j, i: ((i * tm) // seq, 0, 0)),
                  pl.BlockSpec((1, d), lambda j, i: (0, 0)),
                  pl.BlockSpec((d, cw), lambda j, i: (0, j))],
        out_specs=pl.BlockSpec((tm, cw), lambda j, i: (i, j)),
        out_shape=jax.ShapeDtypeStruct((t, width), out_dtype),
        compiler_params=_params(("parallel", "parallel")),
    )(x2, mod3, mod3, norm_g, w_pack)


def _gla_kernel(q_ref, k_ref, v_ref, g_ref, alr_ref, wa2_ref, ba_ref, gn_ref, o_ref, s_ref, acc_ref):
    @pl.when(pl.program_id(1) == 0)
    def _():
        s_ref[...] = jnp.zeros_like(s_ref)

    L, sub, nh, dk, dv = GLA_CHUNK, GLA_SUB, GLA_HEADS, GLA_DK, GLA_DV
    seqs = range(q_ref.shape[0])
    heads = range(nh)
    hk = lambda h: slice(h * dk, (h + 1) * dk)
    hv = lambda h: slice(h * dv, (h + 1) * dv)

    pre = []
    for g in seqs:
        z = jnp.dot(alr_ref[g], wa2_ref[...], preferred_element_type=F32, precision=HIGHEST) + ba_ref[...]
        cum = _cumsum_rows(_log_sigmoid(z) * (1.0 / GLA_TAU))
        q = q_ref[g].astype(F32) * (dk ** -0.5)
        k = k_ref[g].astype(F32)
        tot = cum[L - 1:L, :]
        pre.append(dict(cum=cum, q=q, k=k, tot=tot, vb=_mx(v_ref[g]),
                        q_in=_mx(q * jnp.exp(cum)), k_dec=_mx(k * jnp.exp(tot - cum))))

    scores = {}
    for i in range(L // sub):
        r0, r1 = i * sub, (i + 1) * sub
        for g in seqs:
            p = pre[g]
            base = p["cum"][r0 - 1:r0, :] if i > 0 else jnp.zeros_like(p["tot"])
            qi = _mx(p["q"][r0:r1] * jnp.exp(p["cum"][r0:r1] - base))
            ka = _mx(p["k"][:r1] * jnp.exp(base - p["cum"][:r1]))
            for h in heads:
                scores[g, i, h] = _dot_nt(qi[:, hk(h)], ka[:, hk(h)])

    for i in range(L // sub):
        r0, r1 = i * sub, (i + 1) * sub
        causal = (lax.broadcasted_iota(jnp.int32, (sub, r1), 1)
                  <= lax.broadcasted_iota(jnp.int32, (sub, r1), 0) + r0)
        for g in seqs:
            for h in heads:
                s = jnp.where(causal, scores[g, i, h], 0.0)
                acc_ref[g, r0:r1, hv(h)] = _dot(s, pre[g]["vb"][:r1, hv(h)])

    inter = {(g, h): _dot(pre[g]["q_in"][:, hk(h)], s_ref[g * nh + h]) for g in seqs for h in heads}
    update = {(g, h): _dot_tn(pre[g]["k_dec"][:, hk(h)], pre[g]["vb"][:, hv(h)]) for g in seqs for h in heads}

    gn = gn_ref[...]
    for g in seqs:
        gate = g_ref[g].astype(F32)
        for h in heads:
            o = acc_ref[g, :, hv(h)] + inter[g, h]
            y = _rms(o, gn[:, hv(h)]) * _silu(gate[:, hv(h)])
            o_ref[g, :, hv(h)] = y.astype(o_ref.dtype)
            decay = jnp.transpose(jnp.exp(pre[g]["tot"][:, hk(h)]))
            s_ref[g * nh + h] = s_ref[g * nh + h] * decay + update[g, h]


def _seq_group(batch):
    for grp in (SEQ_GROUP, 2, 1):
        if batch % grp == 0:
            return grp


def _gla(proj3, tail3, wa2_pad, ba, gn):
    b, s, _ = proj3.shape
    L = GLA_CHUNK
    grp = _seq_group(b)
    blk = lambda w, c0: pl.BlockSpec((grp, L, w), lambda bi, ci: (bi, ci, c0 // w))
    full = lambda a: pl.BlockSpec(a.shape, lambda bi, ci: (0,) * a.ndim)
    return pl.pallas_call(
        _gla_kernel, name="gla",
        grid=(b // grp, s // L),
        in_specs=[blk(GLA_QK, C_QA), blk(GLA_QK, C_KA), blk(GLA_V, C_VA), blk(GLA_V, C_GA),
                  blk(LANE, T_ALR), full(wa2_pad), full(ba), full(gn)],
        out_specs=pl.BlockSpec((grp, L, GLA_V), lambda bi, ci: (bi, ci, 0)),
        out_shape=jax.ShapeDtypeStruct((b, s, GLA_V), ACT_DTYPE),
        scratch_shapes=[pltpu.VMEM((grp * GLA_HEADS, GLA_DK, GLA_DV), F32),
                        pltpu.VMEM((grp, L, GLA_V), F32)],
        compiler_params=_params(("parallel", "arbitrary")),
    )(proj3, proj3, proj3, proj3, tail3, wa2_pad, ba, gn)


def _mlstm_pair_kernel(qk_ref, v_ref, oc_ref, if_ref, conv_ref, bias_ref, gnt_ref, o_ref,
                       xbuf_ref, ct_ref, n_ref, m_ref):
    L, nh, dk, dv, kc = MLSTM_CHUNK, MLSTM_HEADS, MLSTM_DQK, MLSTM_DV, MLSTM_CONV
    pad = CONV_PAD
    npair = nh // 2
    assert 2 * dk == LANE and dv == LANE and L == dk

    @pl.when(pl.program_id(1) == 0)
    def _():
        xbuf_ref[:, 0:pad, :] = jnp.zeros((xbuf_ref.shape[0], pad, 2 * MLSTM_QK), F32)
        ct_ref[...] = jnp.zeros_like(ct_ref)
        n_ref[...] = jnp.zeros_like(n_ref)
        m_ref[...] = jnp.zeros_like(m_ref)

    lane = lax.broadcasted_iota(jnp.int32, (1, LANE), 1)
    half = [(lane < dk).astype(F32), (lane >= dk).astype(F32)]
    s_idx = lax.broadcasted_iota(jnp.int32, (L, LANE), 0)
    t_idx = lax.broadcasted_iota(jnp.int32, (L, LANE), 1) % L
    causal = s_idx <= t_idx
    lane_in = lax.broadcasted_iota(jnp.int32, (LANE, LANE), 0)
    head_of = lax.broadcasted_iota(jnp.int32, (LANE, LANE), 1) // dk
    cw = conv_ref[...]

    combos = [(g, p) for g in range(qk_ref.shape[0]) for p in range(npair)]
    seq = {}
    for g in range(qk_ref.shape[0]):
        xbuf_ref[g, pad:pad + L, :] = qk_ref[g].astype(F32)
        conv = jnp.zeros((L, 2 * MLSTM_QK), F32)
        for j in range(kc):
            conv = conv + cw[j:j + 1, :] * xbuf_ref[g, pl.ds(pad - (kc - 1) + j, L), :]
        xbuf_ref[g, 0:pad, :] = xbuf_ref[g, L:L + pad, :]
        qk = _silu(conv)
        pre = if_ref[g] + bias_ref[...]
        bcum = _cumsum_rows(_log_sigmoid(pre))
        v = v_ref[g].astype(F32)
        seq[g] = dict(q=qk[:, :MLSTM_QK] * (dk ** -0.5), k=qk[:, MLSTM_QK:],
                      vt=[_mx(jnp.transpose(v[:, h * dv:(h + 1) * dv])) for h in range(nh)],
                      gate_mix=jnp.where(lane < nh, pre, -bcum), bcum=bcum, bcum_t=jnp.transpose(bcum))

    st = {}
    for g, p in combos:
        sq = seq[g]
        sel = jnp.where((lane_in == 2 * p + head_of) | (lane_in == nh + 2 * p + head_of), 1.0, 0.0)
        selb = jnp.where(lane_in == nh + 2 * p + head_of, 1.0, 0.0)
        d_mat = jnp.dot(sq["gate_mix"], sel, preferred_element_type=F32, precision=HIGHEST)
        tot = jnp.dot(sq["bcum"][L - 8:L, :], selb, preferred_element_type=F32, precision=HIGHEST)[7:8, :]
        b_row = jnp.concatenate([sq["bcum_t"][nh + 2 * p:nh + 2 * p + 1, :],
                                 sq["bcum_t"][nh + 2 * p + 1:nh + 2 * p + 2, :]], axis=1)
        qt = sq["q"][:, p * LANE:(p + 1) * LANE]
        kt = sq["k"][:, p * LANE:(p + 1) * LANE]
        q_bd = _mx(jnp.concatenate([qt * half[0], qt * half[1]], axis=0))
        st[g, p] = dict(d=d_mat, tot=tot, b_row=b_row, kt=kt, q_bd=q_bd,
                        scores=_dot_nt(kt, q_bd),
                        qn=_dot_nt(jnp.broadcast_to(n_ref[g * npair + p], (8, LANE)), q_bd)[0:1, :],
                        inter=[_dot_nt(ct_ref[g * nh + 2 * p + hh], q_bd) for hh in range(2)])

    for g, p in combos:
        c = st[g, p]
        m_prev = m_ref[g * npair + p]
        dlog = jnp.where(causal, c["b_row"] + c["d"], -jnp.inf)
        inter_log = c["b_row"] + m_prev
        m_t = jnp.maximum(inter_log, jnp.max(dlog, axis=0, keepdims=True))
        g_log = c["tot"] + c["d"]
        m_new = jnp.maximum(c["tot"] + m_prev, jnp.max(g_log, axis=0, keepdims=True))
        c.update(m_t=m_t, w_inter=jnp.exp(inter_log - m_t), sw=c["scores"] * jnp.exp(dlog - m_t),
                 m_new=m_new, w_c=jnp.exp(c["tot"] + m_prev - m_new), ks=c["kt"] * jnp.exp(g_log - m_new))

    for g, p in combos:
        c = st[g, p]
        vt = seq[g]["vt"]
        c["sv"] = [_dot(vt[2 * p + hh], c["sw"] * half[hh]) for hh in range(2)]
        c["kv"] = [_dot(vt[2 * p + hh], c["ks"] * half[hh]) for hh in range(2)]

    for g, p in combos:
        c = st[g, p]
        num = c["sv"][0] + c["sv"][1] + c["w_inter"] * (c["inter"][0] + c["inter"][1])
        den = jnp.sum(c["sw"], axis=0, keepdims=True) + c["w_inter"] * c["qn"]
        hout = num / jnp.maximum(jnp.abs(den), jnp.exp(-c["m_t"]))
        y_t = hout * lax.rsqrt(jnp.mean(hout * hout, axis=0, keepdims=True) + EPS) * gnt_ref[p]
        y = jnp.transpose(y_t)
        for hh in range(2):
            cols = slice((2 * p + hh) * dv, (2 * p + hh + 1) * dv)
            o_ref[g, :, cols] = (y[hh * L:(hh + 1) * L, :] * jax.nn.sigmoid(oc_ref[g, :, cols].astype(F32))).astype(o_ref.dtype)
            ct_ref[g * nh + 2 * p + hh] = ct_ref[g * nh + 2 * p + hh] * c["w_c"] + c["kv"][hh]
        n_ref[g * npair + p] = n_ref[g * npair + p] * c["w_c"] + jnp.sum(c["ks"], axis=0, keepdims=True)
        m_ref[g * npair + p] = c["m_new"]


def _mlstm(proj3, tail3, conv_w, bias_row, gn):
    b, s, _ = proj3.shape
    L = MLSTM_CHUNK
    grp = _seq_group(b)
    blk = lambda w, c0: pl.BlockSpec((grp, L, w), lambda bi, ci: (bi, ci, c0 // w))
    full = lambda a: pl.BlockSpec(a.shape, lambda bi, ci: (0,) * a.ndim)
    npair = MLSTM_HEADS // 2
    gnt = jnp.repeat(jnp.transpose(gn.reshape(npair, 2, MLSTM_DV), (0, 2, 1)), L, axis=2)
    return pl.pallas_call(
        _mlstm_pair_kernel, name="mlstm",
        grid=(b // grp, s // L),
        in_specs=[blk(2 * MLSTM_QK, C_QKC), blk(MLSTM_V, C_VC), blk(MLSTM_V, C_OC), blk(LANE, T_ICFC),
                  full(conv_w), full(bias_row), full(gnt)],
        out_specs=pl.BlockSpec((grp, L, MLSTM_V), lambda bi, ci: (bi, ci, 0)),
        out_shape=jax.ShapeDtypeStruct((b, s, MLSTM_V), ACT_DTYPE),
        scratch_shapes=[pltpu.VMEM((grp, L + CONV_PAD, 2 * MLSTM_QK), F32),
                        pltpu.VMEM((grp * MLSTM_HEADS, MLSTM_DV, LANE), F32),
                        pltpu.VMEM((grp * npair, 1, LANE), F32),
                        pltpu.VMEM((grp * npair, 1, LANE), F32)],
        compiler_params=_params(("parallel", "arbitrary")),
    )(proj3, proj3, proj3, tail3, conv_w, bias_row, gnt)


def _dsa_prep_kernel(cq_ref, ckv_ref, kw_ref, nq_ref, nkv_ref, wuq_ref, wuk_ref, wqi_ref,
                     ckvn_ref, kwb_ref, qlat_ref, qidx_ref, wht_ref, *, tiles_per_seq):
    tm = cq_ref.shape[0]
    r = DSA_KV_RANK
    cqn = _mx(_rms(cq_ref[...].astype(F32), nq_ref[...]))
    pos = (pl.program_id(0) % tiles_per_seq) * tm + lax.broadcasted_iota(jnp.int32, (tm, LANE), 0)
    lane = lax.broadcasted_iota(jnp.int32, (tm, LANE), 1)
    pos_cols = jnp.where(lane < 3, pos >> 6, jnp.where(lane < 6, pos & 63, 0)).astype(F32)
    ckvn_ref[:, :r] = _rms(ckv_ref[...], nkv_ref[...]).astype(ckvn_ref.dtype)
    ckvn_ref[:, r:] = pos_cols.astype(ckvn_ref.dtype)
    kw = kw_ref[...]
    kwb_ref[...] = kw.astype(kwb_ref.dtype)
    wht_ref[0] = jnp.transpose(kw)[W_IDX_LANE:W_IDX_LANE + IDX_HEADS, :] * (IDX_HEADS ** -0.5)
    q = jnp.dot(cqn, wuq_ref[...], preferred_element_type=F32)
    for h in range(DSA_HEADS):
        ql = _dot(q[:, h * DSA_HEAD_DIM:(h + 1) * DSA_HEAD_DIM], wuk_ref[h]) * (DSA_HEAD_DIM ** -0.5 * LOG2E)
        qlat_ref[0, h, :, :r] = ql.astype(qlat_ref.dtype)
        c = _bf16_pieces(2.0 ** (-8.0 * (h + 1) / DSA_HEADS) * LOG2E, 3)
        consts = [64.0 * c[0], 64.0 * c[1], 64.0 * c[2], c[0], c[1], c[2]]
        slope_cols = jnp.zeros((tm, LANE), F32)
        for j, v in enumerate(consts):
            slope_cols = jnp.where(lane == j, v, slope_cols)
        qlat_ref[0, h, :, r:] = slope_cols.astype(qlat_ref.dtype)
    qi = jnp.dot(cqn, wqi_ref[...], preferred_element_type=F32) * (IDX_DIM ** -0.5)
    for h in range(IDX_HEADS):
        qidx_ref[0, h] = qi[:, h * LANE:(h + 1) * LANE].astype(qidx_ref.dtype)


def _dsa_prep(proj2, tail2, seq, nq, nkv, wuq, wuk_t, wqi_pad):
    t = proj2.shape[0]
    b = t // seq
    tm = min(512, seq)
    per = seq // tm
    blk = lambda w, c0: pl.BlockSpec((tm, w), lambda i: (i, c0 // w))
    full = lambda a: pl.BlockSpec(a.shape, lambda i: (0,) * a.ndim)
    hmap = lambda i: (i // per, 0, i % per, 0)
    return pl.pallas_call(
        functools.partial(_dsa_prep_kernel, tiles_per_seq=per), name="dsa_prep",
        grid=(t // tm,),
        in_specs=[blk(DSA_Q_RANK, C_CQ), blk(DSA_KV_RANK, T_CKV), blk(LANE, T_KW),
                  full(nq), full(nkv), full(wuq), full(wuk_t), full(wqi_pad)],
        out_specs=[pl.BlockSpec((tm, DSA_AUG), lambda i: (i, 0)),
                   pl.BlockSpec((tm, LANE), lambda i: (i, 0)),
                   pl.BlockSpec((1, DSA_HEADS, tm, DSA_AUG), hmap),
                   pl.BlockSpec((1, IDX_HEADS, tm, LANE), hmap),
                   pl.BlockSpec((1, IDX_HEADS, tm), lambda i: (i // per, 0, i % per))],
        out_shape=[jax.ShapeDtypeStruct((t, DSA_AUG), ACT_DTYPE),
                   jax.ShapeDtypeStruct((t, LANE), ACT_DTYPE),
                   jax.ShapeDtypeStruct((b, DSA_HEADS, seq, DSA_AUG), ACT_DTYPE),
                   jax.ShapeDtypeStruct((b, IDX_HEADS, seq, LANE), ACT_DTYPE),
                   jax.ShapeDtypeStruct((b, IDX_HEADS, seq), F32)],
        compiler_params=_params(("parallel",)),
    )(proj2, tail2, tail2, nq, nkv, wuq, wuk_t, wqi_pad)


def _dsa_kernel(qlat_ref, qidx_ref, wht_ref, kwk_ref, ckv_ref, wuv_ref, o_ref,
                ibuf_ref, acc_ref, *, topk):
    nh, tq, tk = DSA_HEADS, Q_TILE, KEY_TILE
    qb = pl.program_id(1)
    n_kt = (qb * tq + tq + tk - 1) // tk
    t_row = qb * tq + lax.broadcasted_iota(jnp.int32, (1, tq), 1)
    s_col = lax.broadcasted_iota(jnp.int32, (tk, 1), 0)

    qi = qidx_ref[0].reshape(IDX_HEADS * tq, LANE)
    wht = wht_ref[0]

    def idx_body(kt, carry):
        kk = kwk_ref[0, pl.ds(pl.multiple_of(kt * tk, tk), tk), :]
        sc = jnp.maximum(_dot_nt(kk, qi), 0.0)
        tot = wht[0:1, :] * sc[:, 0:tq]
        for h in range(1, IDX_HEADS):
            tot = tot + wht[h:h + 1, :] * sc[:, h * tq:(h + 1) * tq]
        ibuf_ref[kt] = jnp.where(kt * tk + s_col <= t_row, tot, -jnp.inf)
        return carry

    lax.fori_loop(0, n_kt, idx_body, 0)

    def count(pred):
        def body(kt, c):
            hit = jnp.where(pred(ibuf_ref[kt]), 1.0, 0.0)
            return c + jnp.sum(hit.reshape(tk // 32, 32, tq), axis=0)
        return jnp.sum(lax.fori_loop(0, n_kt, body, jnp.zeros((32, tq), F32)), axis=0, keepdims=True)

    def count_ge(cand):
        return count(lambda x: x >= cand)

    def count_gt(cand):
        return count(lambda x: x > cand)

    def key_to_float(u):
        key = u ^ jnp.int32(-2 ** 31)
        bits = jnp.where(key >= 0, key, key ^ jnp.int32(0x7FFFFFFF))
        return lax.bitcast_convert_type(bits, F32)

    few = t_row < topk
    n_bits = 32

    def search_cond(st):
        i, _, _, pending = st
        return (i < n_bits) & (pending > 0)

    def search_body(st):
        i, u, cnt_u, _ = st
        for _ in range(4):
            cand_u = u | lax.shift_left(jnp.int32(1), n_bits - 1 - i)
            cnt = count_ge(key_to_float(cand_u))
            ok = cnt >= topk
            u = jnp.where(ok, cand_u, u)
            cnt_u = jnp.where(ok, cnt, cnt_u)
            i = i + 1
        pending = jnp.max(jnp.where(few | (cnt_u == topk), 0, 1))
        return i, u, cnt_u, pending

    start = (jnp.int32(0), jnp.zeros((1, tq), jnp.int32), jnp.full((1, tq), float(2 ** 30), F32),
             (n_kt * tk > topk).astype(jnp.int32))
    _, u, _, _ = lax.while_loop(search_cond, search_body, start)
    tau = jnp.where(few, -jnp.inf, key_to_float(u))
    need = topk - count_gt(tau)

    acc_ref[...] = jnp.zeros_like(acc_ref)
    ql = qlat_ref[0].reshape(nh * tq, DSA_AUG)
    tri = jnp.where(lax.broadcasted_iota(jnp.int32, (tk, tk), 0) >= lax.broadcasted_iota(jnp.int32, (tk, tk), 1),
                    1.0, 0.0).astype(BF16)

    def keys(kt):
        return ckv_ref[0, pl.ds(pl.multiple_of(kt * tk, tk), tk), :]

    def att_body(kt, carry):
        m, l, eq_seen = carry
        kv_aug = keys(kt)
        lg = _dot_nt(kv_aug, ql)
        it = ibuf_ref[kt]
        eq = it == tau
        eqf = jnp.where(eq, 1.0, 0.0)
        rank = jnp.dot(tri, eqf.astype(BF16), preferred_element_type=F32) + eq_seen
        valid = ((it > tau) | (eq & (rank <= need))) & (kt * tk + s_col <= t_row)
        bias = jnp.where(valid, 0.0, NEG_BIG)
        ms, ls, als, ps = [], [], [], []
        for h in range(nh):
            cols = slice(h * tq, (h + 1) * tq)
            lh = lg[:, cols] + bias
            m_old = m[:, cols]
            m_new = jnp.maximum(m_old, _colreduce(lh, jnp.max))
            p = jnp.exp2(lh - m_new)
            alpha = jnp.exp2(m_old - m_new)
            ms.append(m_new)
            ls.append(alpha * l[:, cols] + _colreduce(p, jnp.sum))
            als.append(alpha)
            ps.append(_mx(p))
        acc_ref[...] = (jnp.concatenate(als, axis=1) * acc_ref[...]
                        + _dot_tn(kv_aug[:, :DSA_KV_RANK], jnp.concatenate(ps, axis=1)))
        return (jnp.concatenate(ms, axis=1), jnp.concatenate(ls, axis=1),
                eq_seen + jnp.sum(eqf, axis=0, keepdims=True))

    init = (jnp.full((1, nh * tq), NEG_BIG, F32), jnp.zeros((1, nh * tq), F32), jnp.zeros((1, tq), F32))
    _, l, _ = lax.fori_loop(0, n_kt, att_body, init)

    outs = []
    for h in range(nh):
        cols = slice(h * tq, (h + 1) * tq)
        outs.append(_dot(wuv_ref[h], acc_ref[:, cols] / l[:, cols]))
    o_ref[0] = jnp.transpose(jnp.concatenate(outs, axis=0)).astype(o_ref.dtype)


def _dsa(ckvn3, kwb3, qlat, qidx, wht, wuv_t):
    b, s, _ = ckvn3.shape
    topk = min(DSA_TOPK_MAX, s // 4)
    nh, tq = DSA_HEADS, Q_TILE
    return pl.pallas_call(
        functools.partial(_dsa_kernel, topk=topk), name="dsa_attn",
        grid=(b, s // tq),
        in_specs=[pl.BlockSpec((1, nh, tq, DSA_AUG), lambda bi, qi: (bi, 0, qi, 0)),
                  pl.BlockSpec((1, IDX_HEADS, tq, LANE), lambda bi, qi: (bi, 0, qi, 0)),
                  pl.BlockSpec((1, IDX_HEADS, tq), lambda bi, qi: (bi, 0, qi)),
                  pl.BlockSpec((1, s, LANE), lambda bi, qi: (bi, 0, 0)),
                  pl.BlockSpec((1, s, DSA_AUG), lambda bi, qi: (bi, 0, 0)),
                  pl.BlockSpec(wuv_t.shape, lambda bi, qi: (0, 0, 0))],
        out_specs=pl.BlockSpec((1, tq, DSA_V), lambda bi, qi: (bi, qi, 0)),
        out_shape=jax.ShapeDtypeStruct((b, s, DSA_V), ACT_DTYPE),
        scratch_shapes=[pltpu.VMEM((s // KEY_TILE, KEY_TILE, tq), F32),
                        pltpu.VMEM((DSA_KV_RANK, nh * tq), F32)],
        compiler_params=_params(("parallel", "arbitrary")),
    )(qlat, qidx, wht, kwb3, ckvn3, wuv_t)


def _merge_kernel(a_ref, b_ref, c_ref, gt_ref, x_ref, g1_ref, wa_ref, wb_ref, wc_ref, wo_ref, o_ref):
    d = D_MODEL
    g = jax.nn.sigmoid(gt_ref[...].astype(F32))
    ya = jnp.dot(a_ref[...], wa_ref[...], preferred_element_type=F32)
    yb = jnp.dot(b_ref[...], wb_ref[...], preferred_element_type=F32)
    yc = jnp.dot(c_ref[...], wc_ref[...], preferred_element_type=F32)
    m = g[:, :d] * ya + g[:, d:2 * d] * yb + g[:, 2 * d:] * yc
    o_ref[...] = x_ref[...] + g1_ref[0] * _dot(m, wo_ref[...])


def _merge(ya_in, yb_in, yc_in, proj2, x2, mod3, wa, wb, wc, wo, seq):
    t, d = x2.shape
    tm = min(512, seq)
    full = lambda a: pl.BlockSpec(a.shape, lambda i: (0,) * a.ndim)
    br = lambda w: pl.BlockSpec((tm, w), lambda i: (i, 0))
    return pl.pallas_call(
        _merge_kernel, name="merge",
        grid=(t // tm,),
        in_specs=[br(GLA_V), br(DSA_V), br(MLSTM_V), br(3 * d), br(d),
                  pl.BlockSpec((1, 1, d), lambda i: ((i * tm) // seq, 0, 2)),
                  full(wa), full(wb), full(wc), full(wo)],
        out_specs=br(d),
        out_shape=jax.ShapeDtypeStruct((t, d), F32),
        compiler_params=_params(("parallel",)),
    )(ya_in, yb_in, yc_in, proj2, x2, mod3, wa, wb, wc, wo)


def _first_argmax_mask(cur, iota, axis, n):
    mx = jnp.max(cur, axis=axis, keepdims=True)
    ix = jnp.min(jnp.where(cur == mx, iota, n), axis=axis, keepdims=True)
    return iota == ix


def _router_kernel(x_ref, sc_ref, sh_ref, g_ref, rwt_ref, rb_ref, dest_ref, wgt_ref, starts_ref, plens_ref):
    ne, ng = N_EXPERTS, N_GROUPS
    eg = ne // ng
    h = _rms(x_ref[...], g_ref[...]) * (1.0 + sc_ref[0]) + sh_ref[0]
    tm = h.shape[0]
    logits = lax.dot_general(rwt_ref[...], h, (((1,), (1,)), ((), ())),
                             preferred_element_type=F32, precision=HIGHEST)
    scores = jax.nn.sigmoid(logits)
    sel = scores + rb_ref[...]
    s3 = sel.reshape(ng, eg, tm)
    io3 = lax.broadcasted_iota(jnp.int32, (ng, eg, tm), 1)
    m1 = jnp.max(s3, axis=1, keepdims=True)
    first = _first_argmax_mask(s3, io3, 1, eg)
    m2 = jnp.max(jnp.where(first, -jnp.inf, s3), axis=1, keepdims=True)
    gs = (m1 + m2).reshape(ng, tm)
    iog = lax.broadcasted_iota(jnp.int32, (ng, tm), 0)
    gkeep = jnp.zeros((ng, tm), F32)
    cur = gs
    for _ in range(TOPK_GROUPS):
        hit = _first_argmax_mask(cur, iog, 0, ng)
        gkeep = jnp.where(hit, 1.0, gkeep)
        cur = jnp.where(hit, -jnp.inf, cur)
    selm = jnp.where(gkeep.reshape(ng, 1, tm) > 0.0, s3, -jnp.inf).reshape(ne, tm)
    ioe = lax.broadcasted_iota(jnp.int32, (ne, tm), 0)
    hits = []
    chosen = jnp.zeros((ne, tm), F32)
    cur = selm
    for _ in range(TOP_K):
        hit = _first_argmax_mask(cur, ioe, 0, ne)
        hits.append(hit)
        chosen = jnp.where(hit, 1.0, chosen)
        cur = jnp.where(hit, -jnp.inf, cur)
    w = chosen * scores
    w = w / jnp.sum(w, axis=0, keepdims=True) * ROUTED_SCALE

    cnt = jnp.sum(chosen, axis=1, keepdims=True)
    plen = jnp.ceil(cnt * (1.0 / ROW_ALIGN)) * ROW_ALIGN
    start = _cumsum_rows(jnp.broadcast_to(plen, (ne, LANE)))[:, :1] - plen
    before = (lax.broadcasted_iota(jnp.int32, (tm, tm), 0)
              < lax.broadcasted_iota(jnp.int32, (tm, tm), 1)).astype(BF16)
    rank = jnp.dot(chosen.astype(BF16), before, preferred_element_type=F32)
    row_of = start + rank
    pad_rows = SLOT_ROWS - TOP_K
    dest = [jnp.sum(jnp.where(hit, row_of, 0.0), axis=0, keepdims=True) for hit in hits]
    wsel = [jnp.sum(jnp.where(hit, w, 0.0), axis=0, keepdims=True) for hit in hits]
    dest_ref[...] = jnp.concatenate(dest + [jnp.full((pad_rows, tm), -1.0, F32)], axis=0).astype(jnp.int32)
    wgt_ref[...] = jnp.concatenate(wsel + [jnp.zeros((pad_rows, tm), F32)], axis=0)
    starts_ref[0] = start.astype(jnp.int32)
    plens_ref[0] = plen.astype(jnp.int32)


def _router(x2, mod3, norm_g, rw_t, rb_col, seq):
    t, d = x2.shape
    tm = MOE_SUB
    nsb = t // tm
    return pl.pallas_call(
        _router_kernel, name="router",
        grid=(nsb,),
        in_specs=[pl.BlockSpec((tm, d), lambda i: (i, 0)),
                  pl.BlockSpec((1, 1, d), lambda i: ((i * tm) // seq, 0, 4)),
                  pl.BlockSpec((1, 1, d), lambda i: ((i * tm) // seq, 0, 3)),
                  pl.BlockSpec((1, d), lambda i: (0, 0)),
                  pl.BlockSpec(rw_t.shape, lambda i: (0, 0)),
                  pl.BlockSpec(rb_col.shape, lambda i: (0, 0))],
        out_specs=[pl.BlockSpec((SLOT_ROWS, tm), lambda i: (0, i)),
                   pl.BlockSpec((SLOT_ROWS, tm), lambda i: (0, i)),
                   pl.BlockSpec((1, N_EXPERTS, 1), lambda i: (i, 0, 0)),
                   pl.BlockSpec((1, N_EXPERTS, 1), lambda i: (i, 0, 0))],
        out_shape=[jax.ShapeDtypeStruct((SLOT_ROWS, t), jnp.int32),
                   jax.ShapeDtypeStruct((SLOT_ROWS, t), F32),
                   jax.ShapeDtypeStruct((nsb, N_EXPERTS, 1), jnp.int32),
                   jax.ShapeDtypeStruct((nsb, N_EXPERTS, 1), jnp.int32)],
        compiler_params=_params(("parallel",)),
    )(x2, mod3, mod3, norm_g, rw_t, rb_col)


def _moe_kernel(starts_ref, plens_ref, x_ref, sc_ref, sh_ref, g2_ref, gn_ref, dest_ref, wgt_ref,
                wg_ref, wu_ref, wd_ref, sg_ref, su_ref, sd_ref, nf_ref, o_ref,
                h_ref, xs_ref, stage_ref, *, final):
    blk, step = pl.program_id(0), pl.program_id(1)
    eps = wg_ref.shape[0]
    ne = pl.num_programs(1) * eps
    sub, rt, mt = MOE_SUB, MOE_ROW_TILE, MOE_FFN_TILE
    nsub = x_ref.shape[0] // sub
    rmax = xs_ref.shape[1]

    @pl.when(step == 0)
    def _():
        h = _mx(_rms(x_ref[...], gn_ref[...]) * (1.0 + sc_ref[0]) + sh_ref[0])
        h_ref[...] = h
        stage_ref[...] = jnp.zeros_like(stage_ref)
        for sb in range(nsub):
            dest = dest_ref[:, sb * sub:(sb + 1) * sub]
            hs = h_ref[sb * sub:(sb + 1) * sub, :]
            for r in range(rmax // rt):
                rows = r * rt + lax.broadcasted_iota(jnp.int32, (rt, sub), 0)
                pick = jnp.zeros((rt, sub), F32)
                for j in range(TOP_K):
                    pick = jnp.where(dest[j:j + 1, :] == rows, 1.0, pick)
                xs_ref[sb, r * rt:(r + 1) * rt, :] = _dot(pick, hs).astype(xs_ref.dtype)

    def run(sb, e):
        i = (blk * nsub + sb) * ne + e
        return starts_ref[i], plens_ref[i] // ROW_ALIGN

    def copy_rows(src, src0, dst, dst0, n_chunks):
        def body(k, carry):
            s = pl.multiple_of(src0 + k * ROW_ALIGN, ROW_ALIGN)
            d = pl.multiple_of(dst0 + k * ROW_ALIGN, ROW_ALIGN)
            dst[pl.ds(d, ROW_ALIGN), :] = src[pl.ds(s, ROW_ALIGN), :]
            return carry
        lax.fori_loop(0, n_chunks, body, 0)

    def expert(j, xt):
        return _dot(_silu(_dot(xt, wg_ref[j])) * _dot(xt, wu_ref[j]), wd_ref[j])

    win = MOE_WINDOW
    runs = [[run(sb, step * eps + j) for sb in range(nsub)] for j in range(eps)]
    fits = None
    for per_expert in runs:
        for _, nch in per_expert:
            ok = nch * ROW_ALIGN <= win
            fits = ok if fits is None else jnp.logical_and(fits, ok)

    @pl.when(fits)
    def _():
        def window(sb, st):
            return xs_ref[sb, pl.ds(pl.multiple_of(st, ROW_ALIGN), win), :]
        xin = [jnp.concatenate([window(sb, st) for sb, (st, _) in enumerate(runs[j])], axis=0)
               for j in range(eps)]
        out = [expert(j, xin[j]).astype(xs_ref.dtype) for j in range(eps)]
        row = lax.broadcasted_iota(jnp.int32, (win, 1), 0)
        for j in range(eps):
            for sb, (st, nch) in enumerate(runs[j]):
                rows = slice(sb * win, (sb + 1) * win)
                xs_ref[sb, pl.ds(pl.multiple_of(st, ROW_ALIGN), win), :] = jnp.where(
                    row < nch * ROW_ALIGN, out[j][rows], xin[j][rows])

    @pl.when(jnp.logical_not(fits))
    def _():
        for j in range(eps):
            cursor = 0
            placed = []
            for sb, (st, nch) in enumerate(runs[j]):
                copy_rows(xs_ref.at[sb], st, stage_ref, cursor, nch)
                placed.append((sb, st, nch, cursor))
                cursor = cursor + nch * ROW_ALIGN

            def tile(i, carry, j=j):
                r0 = pl.multiple_of(i * mt, mt)
                stage_ref[pl.ds(r0, mt), :] = expert(j, stage_ref[pl.ds(r0, mt), :]).astype(stage_ref.dtype)
                return carry
            lax.fori_loop(0, (cursor + mt - 1) // mt, tile, 0)
            for sb, st, nch, at in placed:
                copy_rows(stage_ref, at, xs_ref.at[sb], st, nch)

    @pl.when(step == pl.num_programs(1) - 1)
    def _():
        for sb in range(nsub):
            tok = slice(sb * sub, (sb + 1) * sub)
            hs = h_ref[tok, :]
            shared = _dot(_silu(_dot(hs, sg_ref[...])) * _dot(hs, su_ref[...]), sd_ref[...])
            dest_t = jnp.transpose(dest_ref[:, tok].astype(F32))
            wgt_t = jnp.transpose(wgt_ref[:, tok])
            routed = jnp.zeros((sub, x_ref.shape[1]), F32)
            for r in range(rmax // rt):
                cols = (r * rt + lax.broadcasted_iota(jnp.int32, (sub, rt), 1)).astype(F32)
                mix = jnp.zeros((sub, rt), F32)
                for j in range(TOP_K):
                    mix = jnp.where(dest_t[:, j:j + 1] == cols, wgt_t[:, j:j + 1], mix)
                routed = routed + _dot(mix, xs_ref[sb, r * rt:(r + 1) * rt, :])
            xo = x_ref[tok, :] + g2_ref[0] * (shared + routed)
            if final:
                xo = _rms(xo, nf_ref[...])
            o_ref[tok, :] = xo


def _round_up(n, m):
    return (n + m - 1) // m * m


def _moe(x2, mod3, norm_g, dest, wgt, starts, plens, wg, wu, wd, sg, su, sd, nf, seq, final):
    t, d = x2.shape
    tm = min(MOE_BLOCK, seq)
    nsub = tm // MOE_SUB
    ne, ff = wg.shape[0], wg.shape[2]
    rmax = _round_up(TOP_K * MOE_SUB + ne * (ROW_ALIGN - 1) + MOE_WINDOW, MOE_ROW_TILE)
    stage_rows = _round_up(tm + nsub * (ROW_ALIGN - 1), MOE_FFN_TILE)
    eps = MOE_EXPERTS_PER_STEP
    full = lambda a: pl.BlockSpec(a.shape, lambda i, e, *_: (0,) * a.ndim)
    once = lambda a: pl.BlockSpec(a.shape, lambda i, e, *_: (0,) * a.ndim, pipeline_mode=pl.Buffered(1))
    mod = lambda j: pl.BlockSpec((1, 1, d), lambda i, e, *_: ((i * tm) // seq, 0, j))
    slot = pl.BlockSpec((SLOT_ROWS, tm), lambda i, e, *_: (0, i))
    grid_spec = pltpu.PrefetchScalarGridSpec(
        num_scalar_prefetch=2,
        grid=(t // tm, ne // eps),
        in_specs=[pl.BlockSpec((tm, d), lambda i, e, *_: (i, 0), pipeline_mode=pl.Buffered(1)),
                  mod(4), mod(3), mod(5),
                  pl.BlockSpec((1, d), lambda i, e, *_: (0, 0)), slot, slot,
                  pl.BlockSpec((eps, d, ff), lambda i, e, *_: (e, 0, 0)),
                  pl.BlockSpec((eps, d, ff), lambda i, e, *_: (e, 0, 0)),
                  pl.BlockSpec((eps, ff, d), lambda i, e, *_: (e, 0, 0)),
                  once(sg), once(su), once(sd), full(nf)],
        out_specs=pl.BlockSpec((tm, d), lambda i, e, *_: (i, 0), pipeline_mode=pl.Buffered(1)),
        scratch_shapes=[pltpu.VMEM((tm, d), MXU_DTYPE),
                        pltpu.VMEM((nsub, rmax, d), MXU_DTYPE),
                        pltpu.VMEM((stage_rows, d), MXU_DTYPE)])
    return pl.pallas_call(
        functools.partial(_moe_kernel, final=final), name="moe",
        grid_spec=grid_spec,
        out_shape=jax.ShapeDtypeStruct((t, d), F32),
        compiler_params=_params(("parallel", "arbitrary"), vmem_mb=58),
    )(starts.reshape(-1), plens.reshape(-1), x2, mod3, mod3, mod3, norm_g, dest, wgt,
      wg, wu, wd, sg, su, sd, nf)


def _pack_w_in(w):
    d = w.shape[0]
    offs = [0]
    for n in IN_SIZES:
        offs.append(offs[-1] + n)
    (qa, ka, va, ga, alr, cq, ckv, kidx, widx, qc, kc, vc, ic, fc, oc, gates) = [
        w[:, offs[i]:offs[i + 1]] for i in range(len(IN_SIZES))]
    z = lambda n: jnp.zeros((d, n), w.dtype)
    packed = jnp.concatenate(
        [gates, qa, ka, va, ga, vc, oc, qc, kc, cq, ckv,
         kidx, widx, z(LANE - IDX_DIM - IDX_HEADS),
         alr, z(LANE - GLA_GATE_RANK),
         ic, fc, z(LANE - 2 * MLSTM_HEADS)], axis=1)
    assert packed.shape[1] == N_PACK
    return packed.astype(MXU_DTYPE)


def kernel(x, c, ada_w, ada_b, norm_mix, norm_ffn, w_in, gla_w_a2, gla_b_a, gla_norm, dsa_norm_q,
           dsa_norm_kv, dsa_w_uq, dsa_w_uk, dsa_w_uv, dsa_w_qi, mlstm_conv, mlstm_b_i, mlstm_b_f,
           mlstm_norm, w_up_a, w_up_b, w_up_c, w_o, router_w, router_bias, exp_w_gate, exp_w_up,
           exp_w_down, sh_w_gate, sh_w_up, sh_w_down, norm_final):
    b, s, d = x.shape
    depth = ada_w.shape[0]
    t = b * s
    mod = _ada_mod(c, ada_w, ada_b)
    x2 = x.reshape(t, d)
    row = lambda v: v.reshape(1, -1)
    for l in range(depth):
        mod3 = mod[l].reshape(b, 1, 6 * d)
        w_pack = _pack_w_in(w_in[l])
        proj2 = _in_proj(x2, mod3, row(norm_mix[l]), w_pack[:, :N_MAIN], s, 5, ACT_DTYPE)
        tail2 = _in_proj(x2, mod3, row(norm_mix[l]), w_pack[:, N_MAIN:], s, 1, F32)
        proj3 = proj2.reshape(b, s, N_MAIN)
        tail3 = tail2.reshape(b, s, N_TAIL)

        wa2_pad = jnp.zeros((LANE, GLA_QK), F32).at[:GLA_GATE_RANK].set(gla_w_a2[l])
        ya_in = _gla(proj3, tail3, wa2_pad, row(gla_b_a[l]), row(gla_norm[l]))

        bias_row = jnp.zeros((1, LANE), F32).at[0, :MLSTM_HEADS].set(mlstm_b_i[l])
        bias_row = bias_row.at[0, MLSTM_HEADS:2 * MLSTM_HEADS].set(mlstm_b_f[l])
        yc_in = _mlstm(proj3, tail3, mlstm_conv[l], bias_row, row(mlstm_norm[l]))

        wuq = dsa_w_uq[l].reshape(DSA_Q_RANK, DSA_HEADS * DSA_HEAD_DIM).astype(MXU_DTYPE)
        wuk_t = jnp.transpose(dsa_w_uk[l], (1, 2, 0)).astype(MXU_DTYPE)
        wuv_t = jnp.transpose(dsa_w_uv[l], (1, 2, 0)).astype(MXU_DTYPE)
        wqi_pad = jnp.zeros((DSA_Q_RANK, IDX_HEADS, LANE), F32).at[:, :, :IDX_DIM].set(dsa_w_qi[l])
        wqi_pad = wqi_pad.reshape(DSA_Q_RANK, IDX_HEADS * LANE).astype(MXU_DTYPE)
        ckvn, kwb, qlat, qidx, wht = _dsa_prep(proj2, tail2, s, row(dsa_norm_q[l]), row(dsa_norm_kv[l]),
                                               wuq, wuk_t, wqi_pad)
        yb_in = _dsa(ckvn.reshape(b, s, DSA_AUG), kwb.reshape(b, s, LANE), qlat, qidx, wht, wuv_t)

        x2 = _merge(ya_in.reshape(t, GLA_V), yb_in.reshape(t, DSA_V), yc_in.reshape(t, MLSTM_V),
                    proj2, x2, mod3, w_up_a[l].astype(MXU_DTYPE), w_up_b[l].astype(MXU_DTYPE),
                    w_up_c[l].astype(MXU_DTYPE), w_o[l].astype(MXU_DTYPE), s)

        dest, wgt, starts, plens = _router(x2, mod3, row(norm_ffn[l]), jnp.transpose(router_w[l]),
                                           router_bias[l].reshape(-1, 1), s)
        x2 = _moe(x2, mod3, row(norm_ffn[l]), dest, wgt, starts, plens,
                  exp_w_gate[l].astype(MXU_DTYPE), exp_w_up[l].astype(MXU_DTYPE),
                  exp_w_down[l].astype(MXU_DTYPE), sh_w_gate[l].astype(MXU_DTYPE),
                  sh_w_up[l].astype(MXU_DTYPE), sh_w_down[l].astype(MXU_DTYPE), row(norm_final), s,
                  final=(l == depth - 1))
    return x2.reshape(b, s, d)
```

```python
import functools
import struct

import jax
import jax.numpy as jnp
from jax import lax
from jax.experimental import pallas as pl
from jax.experimental.pallas import tpu as pltpu

F32 = jnp.float32
BF16 = jnp.bfloat16
MXU_DTYPE = jnp.bfloat16
ACT_DTYPE = jnp.bfloat16
HIGHEST = lax.Precision.HIGHEST

EPS = 1e-6
D_MODEL = 1024
GLA_HEADS, GLA_DK, GLA_DV, GLA_GATE_RANK, GLA_TAU, GLA_CHUNK = 4, 64, 128, 16, 16.0, 64
GLA_SUB = 16
DSA_HEADS, DSA_Q_RANK, DSA_KV_RANK, DSA_HEAD_DIM, DSA_V_DIM = 8, 256, 128, 64, 64
IDX_HEADS, IDX_DIM, DSA_TOPK_MAX = 8, 32, 256
MLSTM_HEADS, MLSTM_DQK, MLSTM_DV, MLSTM_CONV, MLSTM_CHUNK = 4, 64, 128, 4, 64
N_EXPERTS, TOP_K, N_GROUPS, TOPK_GROUPS, EXPERT_FF, SHARED_FF = 64, 6, 8, 4, 256, 256
ROUTED_SCALE = 2.5

GLA_QK = GLA_HEADS * GLA_DK
GLA_V = GLA_HEADS * GLA_DV
DSA_V = DSA_HEADS * DSA_V_DIM
MLSTM_QK = MLSTM_HEADS * MLSTM_DQK
MLSTM_V = MLSTM_HEADS * MLSTM_DV
IN_SIZES = (GLA_QK, GLA_QK, GLA_V, GLA_V, GLA_GATE_RANK,
            DSA_Q_RANK, DSA_KV_RANK, IDX_DIM, IDX_HEADS,
            MLSTM_QK, MLSTM_QK, MLSTM_V, MLSTM_HEADS, MLSTM_HEADS, MLSTM_V,
            3 * D_MODEL)

LANE = 128
KEY_TILE = 256
Q_TILE = 256
ATT_HEAD_GROUP = 4
NEG_BIG = -1e30
MOE_BLOCK = 1024
MOE_SUB = 256
MOE_ROW_TILE = 256
MOE_WINDOW = 48
MOE_FFN_TILE = 128
MOE_EXPERTS_PER_STEP = 4
CONV_PAD = 8
SEQ_GROUP = 4
ROW_ALIGN = 16
SLOT_ROWS = 8

C_GATES = 0
C_QA = 3072
C_KA = 3328
C_VA = 3584
C_GA = 4096
C_VC = 4608
C_OC = 5120
C_QKC = 5632
C_CQ = 6144
N_MAIN = 6400
T_CKV = 0
T_KW = 128
T_ALR = 256
T_ICFC = 384
N_TAIL = 512
N_PACK = N_MAIN + N_TAIL
INPROJ_PIECES = 5
W_IDX_LANE = IDX_DIM


LOG2E = 1.4426950408889634
DSA_AUG = DSA_KV_RANK + LANE


def _bf16_pieces(x, n):
    out = []
    for _ in range(n):
        bits = struct.unpack("<I", struct.pack("<f", x))[0]
        bits = (bits + 0x7FFF + ((bits >> 16) & 1)) & 0xFFFF0000
        piece = struct.unpack("<f", struct.pack("<I", bits))[0]
        out.append(piece)
        x -= piece
    return out


def _mx(x):
    return x.astype(MXU_DTYPE)


def _dot(a, b):
    return jnp.dot(_mx(a), _mx(b), preferred_element_type=F32)


def _dot_nt(a, b):
    return lax.dot_general(_mx(a), _mx(b), (((1,), (1,)), ((), ())), preferred_element_type=F32)


def _dot_tn(a, b):
    return lax.dot_general(_mx(a), _mx(b), (((0,), (0,)), ((), ())), preferred_element_type=F32)


def _rms(x, g):
    return x * lax.rsqrt(jnp.mean(x * x, axis=-1, keepdims=True) + EPS) * g


def _silu(x):
    return x * jax.nn.sigmoid(x)


def _log_sigmoid(z):
    return jnp.minimum(z, 0.0) - jnp.log1p(jnp.exp(-jnp.abs(z)))


def _cumsum_rows(x):
    n = x.shape[0]
    tri = (lax.broadcasted_iota(jnp.int32, (n, n), 1) <= lax.broadcasted_iota(jnp.int32, (n, n), 0)).astype(F32)
    return jnp.dot(tri, x, preferred_element_type=F32, precision=HIGHEST)


def _truncate_to_bf16(x):
    bits = lax.bitcast_convert_type(x, jnp.int32) & jnp.int32(-65536)
    return lax.bitcast_convert_type(bits, F32).astype(BF16)


def _colreduce(x, op, width=32):
    n, c = x.shape
    return op(op(x.reshape(n // width, width, c), axis=0), axis=0, keepdims=True)


def _params(sem, vmem_mb=40):
    return pltpu.CompilerParams(dimension_semantics=sem, vmem_limit_bytes=vmem_mb * 1024 * 1024)


def _ada_kernel(c_ref, w_ref, b_ref, o_ref):
    cs = _silu(c_ref[...])
    o_ref[0] = jnp.dot(cs, w_ref[0], preferred_element_type=F32, precision=HIGHEST) + b_ref[0]


def _ada_mod(c, ada_w, ada_b):
    depth, d, n = ada_w.shape
    b = c.shape[0]
    return pl.pallas_call(
        _ada_kernel, name="ada_mod",
        grid=(depth, n // d),
        in_specs=[pl.BlockSpec((b, d), lambda l, j: (0, 0)),
                  pl.BlockSpec((1, d, d), lambda l, j: (l, 0, j)),
                  pl.BlockSpec((1, 1, d), lambda l, j: (l, 0, j))],
        out_specs=pl.BlockSpec((1, b, d), lambda l, j: (l, 0, j)),
        out_shape=jax.ShapeDtypeStruct((depth, b, n), F32),
        compiler_params=_params(("parallel", "parallel")),
    )(c, ada_w, ada_b.reshape(depth, 1, n))


def _inproj_kernel(x_ref, sc_ref, sh_ref, g_ref, wm_ref, wt_ref, om_ref, ot_ref):
    h = _mx(_rms(x_ref[...], g_ref[...]) * (1.0 + sc_ref[0]) + sh_ref[0])
    piece = N_MAIN // INPROJ_PIECES
    for j in range(INPROJ_PIECES):
        cols = slice(j * piece, (j + 1) * piece)
        om_ref[:, cols] = jnp.dot(h, wm_ref[:, cols], preferred_element_type=F32).astype(om_ref.dtype)
    ot_ref[...] = jnp.dot(h, wt_ref[...], preferred_element_type=F32)


def _in_proj(x2, mod3, norm_g, w_pack, seq):
    t, d = x2.shape
    tm = min(512, seq)
    once = lambda w: pl.BlockSpec((d, w), lambda i: (0, 0), pipeline_mode=pl.Buffered(1))
    return pl.pallas_call(
        _inproj_kernel, name="in_proj",
        grid=(t // tm,),
        in_specs=[pl.BlockSpec((tm, d), lambda i: (i, 0)),
                  pl.BlockSpec((1, 1, d), lambda i: ((i * tm) // seq, 0, 1)),
                  pl.BlockSpec((1, 1, d), lambda i: ((i * tm) // seq, 0, 0)),
                  pl.BlockSpec((1, d), lambda i: (0, 0)),
                  once(N_MAIN), once(N_TAIL)],
        out_specs=[pl.BlockSpec((tm, N_MAIN), lambda i: (i, 0)),
                   pl.BlockSpec((tm, N_TAIL), lambda i: (i, 0))],
        out_shape=[jax.ShapeDtypeStruct((t, N_MAIN), ACT_DTYPE),
                   jax.ShapeDtypeStruct((t, N_TAIL), F32)],
        compiler_params=_params(("parallel",), vmem_mb=48),
    )(x2, mod3, mod3, norm_g, w_pack[:, :N_MAIN], w_pack[:, N_MAIN:])


def _gla_kernel(q_ref, k_ref, v_ref, g_ref, alr_ref, wa2_ref, ba_ref, gn_ref, o_ref, s_ref, acc_ref):
    @pl.when(pl.program_id(1) == 0)
    def _():
        s_ref[...] = jnp.zeros_like(s_ref)

    L, sub, nh, dk, dv = GLA_CHUNK, GLA_SUB, GLA_HEADS, GLA_DK, GLA_DV
    seqs = range(q_ref.shape[0])
    heads = range(nh)
    hk = lambda h: slice(h * dk, (h + 1) * dk)
    hv = lambda h: slice(h * dv, (h + 1) * dv)

    pre = []
    for g in seqs:
        z = jnp.dot(alr_ref[g], wa2_ref[...], preferred_element_type=F32, precision=HIGHEST) + ba_ref[...]
        cum = _cumsum_rows(_log_sigmoid(z) * (1.0 / GLA_TAU))
        q = q_ref[g].astype(F32) * (dk ** -0.5)
        k = k_ref[g].astype(F32)
        tot = cum[L - 1:L, :]
        pre.append(dict(cum=cum, q=q, k=k, tot=tot, vb=_mx(v_ref[g]),
                        q_in=_mx(q * jnp.exp(cum)), k_dec=_mx(k * jnp.exp(tot - cum))))

    scores = {}
    for i in range(L // sub):
        r0, r1 = i * sub, (i + 1) * sub
        for g in seqs:
            p = pre[g]
            base = p["cum"][r0 - 1:r0, :] if i > 0 else jnp.zeros_like(p["tot"])
            qi = _mx(p["q"][r0:r1] * jnp.exp(p["cum"][r0:r1] - base))
            ka = _mx(p["k"][:r1] * jnp.exp(base - p["cum"][:r1]))
            for h in heads:
                scores[g, i, h] = _dot_nt(qi[:, hk(h)], ka[:, hk(h)])

    for i in range(L // sub):
        r0, r1 = i * sub, (i + 1) * sub
        causal = (lax.broadcasted_iota(jnp.int32, (sub, r1), 1)
                  <= lax.broadcasted_iota(jnp.int32, (sub, r1), 0) + r0)
        for g in seqs:
            for h in heads:
                s = jnp.where(causal, scores[g, i, h], 0.0)
                acc_ref[g, r0:r1, hv(h)] = _dot(s, pre[g]["vb"][:r1, hv(h)])

    inter = {(g, h): _dot(pre[g]["q_in"][:, hk(h)], s_ref[g * nh + h]) for g in seqs for h in heads}
    update = {(g, h): _dot_tn(pre[g]["k_dec"][:, hk(h)], pre[g]["vb"][:, hv(h)]) for g in seqs for h in heads}

    gn = gn_ref[...]
    for g in seqs:
        gate = g_ref[g].astype(F32)
        for h in heads:
            o = acc_ref[g, :, hv(h)] + inter[g, h]
            y = _rms(o, gn[:, hv(h)]) * _silu(gate[:, hv(h)])
            o_ref[g, :, hv(h)] = y.astype(o_ref.dtype)
            decay = jnp.transpose(jnp.exp(pre[g]["tot"][:, hk(h)]))
            s_ref[g * nh + h] = s_ref[g * nh + h] * decay + update[g, h]


def _seq_group(batch):
    for grp in (SEQ_GROUP, 2, 1):
        if batch % grp == 0:
            return grp


def _gla(proj3, tail3, wa2_pad, ba, gn):
    b, s, _ = proj3.shape
    L = GLA_CHUNK
    grp = _seq_group(b)
    blk = lambda w, c0: pl.BlockSpec((grp, L, w), lambda bi, ci: (bi, ci, c0 // w))
    full = lambda a: pl.BlockSpec(a.shape, lambda bi, ci: (0,) * a.ndim)
    return pl.pallas_call(
        _gla_kernel, name="gla",
        grid=(b // grp, s // L),
        in_specs=[blk(GLA_QK, C_QA), blk(GLA_QK, C_KA), blk(GLA_V, C_VA), blk(GLA_V, C_GA),
                  blk(LANE, T_ALR), full(wa2_pad), full(ba), full(gn)],
        out_specs=pl.BlockSpec((grp, L, GLA_V), lambda bi, ci: (bi, ci, 0)),
        out_shape=jax.ShapeDtypeStruct((b, s, GLA_V), ACT_DTYPE),
        scratch_shapes=[pltpu.VMEM((grp * GLA_HEADS, GLA_DK, GLA_DV), F32),
                        pltpu.VMEM((grp, L, GLA_V), F32)],
        compiler_params=_params(("parallel", "arbitrary")),
    )(proj3, proj3, proj3, proj3, tail3, wa2_pad, ba, gn)


def _mlstm_pair_kernel(qk_ref, v_ref, oc_ref, if_ref, conv_ref, bias_ref, gnt_ref, o_ref,
                       xbuf_ref, ct_ref, n_ref, m_ref):
    L, nh, dk, dv, kc = MLSTM_CHUNK, MLSTM_HEADS, MLSTM_DQK, MLSTM_DV, MLSTM_CONV
    pad = CONV_PAD
    npair = nh // 2
    assert 2 * dk == LANE and dv == LANE and L == dk

    @pl.when(pl.program_id(1) == 0)
    def _():
        xbuf_ref[:, 0:pad, :] = jnp.zeros((xbuf_ref.shape[0], pad, 2 * MLSTM_QK), F32)
        ct_ref[...] = jnp.zeros_like(ct_ref)
        n_ref[...] = jnp.zeros_like(n_ref)
        m_ref[...] = jnp.zeros_like(m_ref)

    lane = lax.broadcasted_iota(jnp.int32, (1, LANE), 1)
    half = [(lane < dk).astype(F32), (lane >= dk).astype(F32)]
    s_idx = lax.broadcasted_iota(jnp.int32, (L, LANE), 0)
    t_idx = lax.broadcasted_iota(jnp.int32, (L, LANE), 1) % L
    causal = s_idx <= t_idx
    lane_in = lax.broadcasted_iota(jnp.int32, (LANE, LANE), 0)
    head_of = lax.broadcasted_iota(jnp.int32, (LANE, LANE), 1) // dk
    cw = conv_ref[...]

    combos = [(g, p) for g in range(qk_ref.shape[0]) for p in range(npair)]
    seq = {}
    for g in range(qk_ref.shape[0]):
        xbuf_ref[g, pad:pad + L, :] = qk_ref[g].astype(F32)
        conv = jnp.zeros((L, 2 * MLSTM_QK), F32)
        for j in range(kc):
            conv = conv + cw[j:j + 1, :] * xbuf_ref[g, pl.ds(pad - (kc - 1) + j, L), :]
        xbuf_ref[g, 0:pad, :] = xbuf_ref[g, L:L + pad, :]
        qk = _silu(conv)
        pre = if_ref[g] + bias_ref[...]
        bcum = _cumsum_rows(_log_sigmoid(pre))
        v = v_ref[g].astype(F32)
        seq[g] = dict(q=qk[:, :MLSTM_QK] * (dk ** -0.5), k=qk[:, MLSTM_QK:],
                      vt=[_mx(jnp.transpose(v[:, h * dv:(h + 1) * dv])) for h in range(nh)],
                      gate_mix=jnp.where(lane < nh, pre, -bcum), bcum=bcum, bcum_t=jnp.transpose(bcum))

    st = {}
    for g, p in combos:
        sq = seq[g]
        sel = jnp.where((lane_in == 2 * p + head_of) | (lane_in == nh + 2 * p + head_of), 1.0, 0.0)
        selb = jnp.where(lane_in == nh + 2 * p + head_of, 1.0, 0.0)
        d_mat = jnp.dot(sq["gate_mix"], sel, preferred_element_type=F32, precision=HIGHEST)
        tot = jnp.dot(sq["bcum"][L - 8:L, :], selb, preferred_element_type=F32, precision=HIGHEST)[7:8, :]
        b_row = jnp.concatenate([sq["bcum_t"][nh + 2 * p:nh + 2 * p + 1, :],
                                 sq["bcum_t"][nh + 2 * p + 1:nh + 2 * p + 2, :]], axis=1)
        qt = sq["q"][:, p * LANE:(p + 1) * LANE]
        kt = sq["k"][:, p * LANE:(p + 1) * LANE]
        q_bd = _mx(jnp.concatenate([qt * half[0], qt * half[1]], axis=0))
        st[g, p] = dict(d=d_mat, tot=tot, b_row=b_row, kt=kt, q_bd=q_bd,
                        scores=_dot_nt(kt, q_bd),
                        qn=_dot_nt(jnp.broadcast_to(n_ref[g * npair + p], (8, LANE)), q_bd)[0:1, :],
                        inter=[_dot_nt(ct_ref[g * nh + 2 * p + hh], q_bd) for hh in range(2)])

    for g, p in combos:
        c = st[g, p]
        m_prev = m_ref[g * npair + p]
        dlog = jnp.where(causal, c["b_row"] + c["d"], -jnp.inf)
        inter_log = c["b_row"] + m_prev
        m_t = jnp.maximum(inter_log, jnp.max(dlog, axis=0, keepdims=True))
        g_log = c["tot"] + c["d"]
        m_new = jnp.maximum(c["tot"] + m_prev, jnp.max(g_log, axis=0, keepdims=True))
        c.update(m_t=m_t, w_inter=jnp.exp(inter_log - m_t), sw=c["scores"] * jnp.exp(dlog - m_t),
                 m_new=m_new, w_c=jnp.exp(c["tot"] + m_prev - m_new), ks=c["kt"] * jnp.exp(g_log - m_new))

    for g, p in combos:
        c = st[g, p]
        vt = seq[g]["vt"]
        c["sv"] = [_dot(vt[2 * p + hh], c["sw"] * half[hh]) for hh in range(2)]
        c["kv"] = [_dot(vt[2 * p + hh], c["ks"] * half[hh]) for hh in range(2)]

    for g, p in combos:
        c = st[g, p]
        num = c["sv"][0] + c["sv"][1] + c["w_inter"] * (c["inter"][0] + c["inter"][1])
        den = jnp.sum(c["sw"], axis=0, keepdims=True) + c["w_inter"] * c["qn"]
        hout = num / jnp.maximum(jnp.abs(den), jnp.exp(-c["m_t"]))
        y_t = hout * lax.rsqrt(jnp.mean(hout * hout, axis=0, keepdims=True) + EPS) * gnt_ref[p]
        y = jnp.transpose(y_t)
        for hh in range(2):
            cols = slice((2 * p + hh) * dv, (2 * p + hh + 1) * dv)
            o_ref[g, :, cols] = (y[hh * L:(hh + 1) * L, :] * jax.nn.sigmoid(oc_ref[g, :, cols].astype(F32))).astype(o_ref.dtype)
            ct_ref[g * nh + 2 * p + hh] = ct_ref[g * nh + 2 * p + hh] * c["w_c"] + c["kv"][hh]
        n_ref[g * npair + p] = n_ref[g * npair + p] * c["w_c"] + jnp.sum(c["ks"], axis=0, keepdims=True)
        m_ref[g * npair + p] = c["m_new"]


def _mlstm(proj3, tail3, conv_w, bias_row, gn):
    b, s, _ = proj3.shape
    L = MLSTM_CHUNK
    grp = _seq_group(b)
    blk = lambda w, c0: pl.BlockSpec((grp, L, w), lambda bi, ci: (bi, ci, c0 // w))
    full = lambda a: pl.BlockSpec(a.shape, lambda bi, ci: (0,) * a.ndim)
    npair = MLSTM_HEADS // 2
    gnt = jnp.repeat(jnp.transpose(gn.reshape(npair, 2, MLSTM_DV), (0, 2, 1)), L, axis=2)
    return pl.pallas_call(
        _mlstm_pair_kernel, name="mlstm",
        grid=(b // grp, s // L),
        in_specs=[blk(2 * MLSTM_QK, C_QKC), blk(MLSTM_V, C_VC), blk(MLSTM_V, C_OC), blk(LANE, T_ICFC),
                  full(conv_w), full(bias_row), full(gnt)],
        out_specs=pl.BlockSpec((grp, L, MLSTM_V), lambda bi, ci: (bi, ci, 0)),
        out_shape=jax.ShapeDtypeStruct((b, s, MLSTM_V), ACT_DTYPE),
        scratch_shapes=[pltpu.VMEM((grp, L + CONV_PAD, 2 * MLSTM_QK), F32),
                        pltpu.VMEM((grp * MLSTM_HEADS, MLSTM_DV, LANE), F32),
                        pltpu.VMEM((grp * npair, 1, LANE), F32),
                        pltpu.VMEM((grp * npair, 1, LANE), F32)],
        compiler_params=_params(("parallel", "arbitrary")),
    )(proj3, proj3, proj3, tail3, conv_w, bias_row, gnt)


def _dsa_prep_kernel(cq_ref, ckv_ref, kw_ref, nq_ref, nkv_ref, wuq_ref, wuk_ref, wqi_ref,
                     ckvn_ref, kwb_ref, qlat_ref, qidx_ref, wht_ref, *, tiles_per_seq):
    tm = cq_ref.shape[0]
    r = DSA_KV_RANK
    cqn = _mx(_rms(cq_ref[...].astype(F32), nq_ref[...]))
    pos = (pl.program_id(0) % tiles_per_seq) * tm + lax.broadcasted_iota(jnp.int32, (tm, LANE), 0)
    lane = lax.broadcasted_iota(jnp.int32, (tm, LANE), 1)
    pos_cols = jnp.where(lane < 3, pos >> 6, jnp.where(lane < 6, pos & 63, 0)).astype(F32)
    ckvn_ref[:, :r] = _rms(ckv_ref[...], nkv_ref[...]).astype(ckvn_ref.dtype)
    ckvn_ref[:, r:] = pos_cols.astype(ckvn_ref.dtype)
    kw = kw_ref[...]
    kwb_ref[...] = kw.astype(kwb_ref.dtype)
    wht_ref[0] = jnp.transpose(kw)[W_IDX_LANE:W_IDX_LANE + IDX_HEADS, :] * (IDX_HEADS ** -0.5)
    q = jnp.dot(cqn, wuq_ref[...], preferred_element_type=F32)
    for h in range(DSA_HEADS):
        ql = _dot(q[:, h * DSA_HEAD_DIM:(h + 1) * DSA_HEAD_DIM], wuk_ref[h]) * (DSA_HEAD_DIM ** -0.5 * LOG2E)
        qlat_ref[0, h, :, :r] = ql.astype(qlat_ref.dtype)
        c = _bf16_pieces(2.0 ** (-8.0 * (h + 1) / DSA_HEADS) * LOG2E, 3)
        consts = [64.0 * c[0], 64.0 * c[1], 64.0 * c[2], c[0], c[1], c[2]]
        slope_cols = jnp.zeros((tm, LANE), F32)
        for j, v in enumerate(consts):
            slope_cols = jnp.where(lane == j, v, slope_cols)
        qlat_ref[0, h, :, r:] = slope_cols.astype(qlat_ref.dtype)
    qi = jnp.dot(cqn, wqi_ref[...], preferred_element_type=F32) * (IDX_DIM ** -0.5)
    for h in range(IDX_HEADS):
        qidx_ref[0, h] = qi[:, h * LANE:(h + 1) * LANE].astype(qidx_ref.dtype)


def _dsa_prep(proj2, tail2, seq, nq, nkv, wuq, wuk_t, wqi_pad):
    t = proj2.shape[0]
    b = t // seq
    tm = min(512, seq)
    per = seq // tm
    blk = lambda w, c0: pl.BlockSpec((tm, w), lambda i: (i, c0 // w))
    full = lambda a: pl.BlockSpec(a.shape, lambda i: (0,) * a.ndim)
    hmap = lambda i: (i // per, 0, i % per, 0)
    return pl.pallas_call(
        functools.partial(_dsa_prep_kernel, tiles_per_seq=per), name="dsa_prep",
        grid=(t // tm,),
        in_specs=[blk(DSA_Q_RANK, C_CQ), blk(DSA_KV_RANK, T_CKV), blk(LANE, T_KW),
                  full(nq), full(nkv), full(wuq), full(wuk_t), full(wqi_pad)],
        out_specs=[pl.BlockSpec((tm, DSA_AUG), lambda i: (i, 0)),
                   pl.BlockSpec((tm, LANE), lambda i: (i, 0)),
                   pl.BlockSpec((1, DSA_HEADS, tm, DSA_AUG), hmap),
                   pl.BlockSpec((1, IDX_HEADS, tm, LANE), hmap),
                   pl.BlockSpec((1, IDX_HEADS, tm), lambda i: (i // per, 0, i % per))],
        out_shape=[jax.ShapeDtypeStruct((t, DSA_AUG), ACT_DTYPE),
                   jax.ShapeDtypeStruct((t, LANE), ACT_DTYPE),
                   jax.ShapeDtypeStruct((b, DSA_HEADS, seq, DSA_AUG), ACT_DTYPE),
                   jax.ShapeDtypeStruct((b, IDX_HEADS, seq, LANE), ACT_DTYPE),
                   jax.ShapeDtypeStruct((b, IDX_HEADS, seq), F32)],
        compiler_params=_params(("parallel",)),
    )(proj2, tail2, tail2, nq, nkv, wuq, wuk_t, wqi_pad)


def _dsa_kernel(qlat_ref, qidx_ref, wht_ref, kwk_ref, ckv_ref, wuv_ref, o_ref,
                ibuf_ref, ihi_ref, acc_ref, *, topk):
    nh, tq, tk = DSA_HEADS, Q_TILE, KEY_TILE
    qb = pl.program_id(1)
    n_kt = (qb * tq + tq + tk - 1) // tk
    t_row = qb * tq + lax.broadcasted_iota(jnp.int32, (1, tq), 1)
    s_col = lax.broadcasted_iota(jnp.int32, (tk, 1), 0)

    qi = qidx_ref[0].reshape(IDX_HEADS * tq, LANE)
    wht = wht_ref[0]

    def idx_body(kt, carry):
        kk = kwk_ref[0, pl.ds(pl.multiple_of(kt * tk, tk), tk), :]
        grp = ATT_HEAD_GROUP
        sc = [_dot_nt(kk, qi[g * grp * tq:(g + 1) * grp * tq]) for g in range(IDX_HEADS // grp)]
        tot = None
        for h in range(IDX_HEADS):
            part = wht[h:h + 1, :] * jnp.maximum(sc[h // grp][:, (h % grp) * tq:(h % grp + 1) * tq], 0.0)
            tot = part if tot is None else tot + part
        score = jnp.where(kt * tk + s_col <= t_row, tot, -jnp.inf)
        ibuf_ref[kt] = score
        ihi_ref[kt] = _truncate_to_bf16(score)
        return carry

    lax.fori_loop(0, n_kt, idx_body, 0)

    def count(pred):
        def body(kt, c):
            hit = jnp.where(pred(ibuf_ref[kt]), 1.0, 0.0)
            return c + jnp.sum(hit.reshape(tk // 32, 32, tq), axis=0)
        return jnp.sum(lax.fori_loop(0, n_kt, body, jnp.zeros((32, tq), F32)), axis=0, keepdims=True)

    def count_ge(cand):
        return count(lambda x: x >= cand)

    def count_gt(cand):
        return count(lambda x: x > cand)

    def key_to_float(u):
        key = u ^ jnp.int32(-2 ** 31)
        bits = jnp.where(key >= 0, key, key ^ jnp.int32(0x7FFFFFFF))
        return lax.bitcast_convert_type(bits, F32)

    few = t_row < topk
    n_bits = 32

    def try_bit(i, u, cnt_u, counter):
        cand_u = u | lax.shift_left(jnp.int32(1), n_bits - 1 - i)
        cnt = counter(cand_u)
        ok = cnt >= topk
        return jnp.where(ok, cand_u, u), jnp.where(ok, cnt, cnt_u)

    def count_ge_hi(cand_u):
        cand = _truncate_to_bf16(key_to_float(cand_u))
        def body(kt, c):
            hit = jnp.where(ihi_ref[kt] >= cand, jnp.ones((), BF16), jnp.zeros((), BF16))
            return c + jnp.sum(hit.reshape(tk // 32, 32, tq), axis=0).astype(F32)
        return jnp.sum(lax.fori_loop(0, n_kt, body, jnp.zeros((32, tq), F32)), axis=0, keepdims=True)

    def coarse_body(i, st):
        return try_bit(i, *st, count_ge_hi)

    def search_cond(st):
        i, _, _, pending = st
        return (i < n_bits) & (pending > 0)

    def search_body(st):
        i, u, cnt_u, _ = st
        for _ in range(4):
            u, cnt_u = try_bit(i, u, cnt_u, lambda c: count_ge(key_to_float(c)))
            i = i + 1
        pending = jnp.max(jnp.where(few | (cnt_u == topk), 0, 1))
        return i, u, cnt_u, pending

    u, cnt_u = lax.fori_loop(0, n_bits // 2, coarse_body,
                             (jnp.zeros((1, tq), jnp.int32), jnp.full((1, tq), float(2 ** 30), F32)))
    start = (jnp.int32(n_bits // 2), u, cnt_u, jnp.max(jnp.where(few | (cnt_u == topk), 0, 1)))
    _, u, _, _ = lax.while_loop(search_cond, search_body, start)
    tau = jnp.where(few, -jnp.inf, key_to_float(u))
    need = topk - count_gt(tau)

    acc_ref[...] = jnp.zeros_like(acc_ref)
    ql = qlat_ref[0].reshape(nh * tq, DSA_AUG)
    tri = jnp.where(lax.broadcasted_iota(jnp.int32, (tk, tk), 0) >= lax.broadcasted_iota(jnp.int32, (tk, tk), 1),
                    1.0, 0.0).astype(BF16)

    def keys(kt):
        return ckv_ref[0, pl.ds(pl.multiple_of(kt * tk, tk), tk), :]

    def att_body(kt, carry):
        m, l, eq_seen = carry
        kv_aug = keys(kt)
        kv = kv_aug[:, :DSA_KV_RANK]
        it = ibuf_ref[kt]
        eq = it == tau
        eqf = jnp.where(eq, 1.0, 0.0)
        rank = jnp.dot(tri, eqf.astype(BF16), preferred_element_type=F32) + eq_seen
        valid = ((it > tau) | (eq & (rank <= need))) & (kt * tk + s_col <= t_row)
        bias = jnp.where(valid, 0.0, NEG_BIG)

        n_grp = nh // ATT_HEAD_GROUP
        gcols = lambda g: slice(g * ATT_HEAD_GROUP * tq, (g + 1) * ATT_HEAD_GROUP * tq)
        lg, soft, ms, ls = {}, {}, {}, {}

        def logits(g):
            lg[g] = _dot_nt(kv_aug, ql[gcols(g)])

        def softmax(g):
            als, ps = [], []
            for h in range(g * ATT_HEAD_GROUP, (g + 1) * ATT_HEAD_GROUP):
                cols = slice(h * tq, (h + 1) * tq)
                loc = slice((h - g * ATT_HEAD_GROUP) * tq, (h - g * ATT_HEAD_GROUP + 1) * tq)
                lh = lg[g][:, loc] + bias
                m_old = m[:, cols]
                m_new = jnp.maximum(m_old, _colreduce(lh, jnp.max))
                p = jnp.exp2(lh - m_new)
                alpha = jnp.exp2(m_old - m_new)
                ms[h] = m_new
                ls[h] = alpha * l[:, cols] + _colreduce(p, jnp.sum)
                als.append(alpha)
                ps.append(_mx(p))
            soft[g] = (jnp.concatenate(als, axis=1), jnp.concatenate(ps, axis=1))

        def weighted_values(g):
            alpha, p = soft[g]
            acc_ref[:, gcols(g)] = alpha * acc_ref[:, gcols(g)] + _dot_tn(kv, p)

        logits(0)
        for g in range(n_grp):
            if g + 1 < n_grp:
                logits(g + 1)
            softmax(g)
            if g > 0:
                weighted_values(g - 1)
        weighted_values(n_grp - 1)
        return (jnp.concatenate([ms[h] for h in range(nh)], axis=1),
                jnp.concatenate([ls[h] for h in range(nh)], axis=1),
                eq_seen + jnp.sum(eqf, axis=0, keepdims=True))

    init = (jnp.full((1, nh * tq), NEG_BIG, F32), jnp.zeros((1, nh * tq), F32), jnp.zeros((1, tq), F32))
    _, l, _ = lax.fori_loop(0, n_kt, att_body, init)

    outs = []
    for h in range(nh):
        cols = slice(h * tq, (h + 1) * tq)
        outs.append(_dot(wuv_ref[h], acc_ref[:, cols] / l[:, cols]))
    o_ref[0] = jnp.transpose(jnp.concatenate(outs, axis=0)).astype(o_ref.dtype)


def _dsa(ckvn3, kwb3, qlat, qidx, wht, wuv_t):
    b, s, _ = ckvn3.shape
    topk = min(DSA_TOPK_MAX, s // 4)
    nh, tq = DSA_HEADS, Q_TILE
    return pl.pallas_call(
        functools.partial(_dsa_kernel, topk=topk), name="dsa_attn",
        grid=(b, s // tq),
        in_specs=[pl.BlockSpec((1, nh, tq, DSA_AUG), lambda bi, qi: (bi, 0, qi, 0)),
                  pl.BlockSpec((1, IDX_HEADS, tq, LANE), lambda bi, qi: (bi, 0, qi, 0)),
                  pl.BlockSpec((1, IDX_HEADS, tq), lambda bi, qi: (bi, 0, qi)),
                  pl.BlockSpec((1, s, LANE), lambda bi, qi: (bi, 0, 0)),
                  pl.BlockSpec((1, s, DSA_AUG), lambda bi, qi: (bi, 0, 0)),
                  pl.BlockSpec(wuv_t.shape, lambda bi, qi: (0, 0, 0))],
        out_specs=pl.BlockSpec((1, tq, DSA_V), lambda bi, qi: (bi, qi, 0)),
        out_shape=jax.ShapeDtypeStruct((b, s, DSA_V), ACT_DTYPE),
        scratch_shapes=[pltpu.VMEM((s // KEY_TILE, KEY_TILE, tq), F32),
                        pltpu.VMEM((s // KEY_TILE, KEY_TILE, tq), BF16),
                        pltpu.VMEM((DSA_KV_RANK, nh * tq), F32)],
        compiler_params=_params(("parallel", "arbitrary")),
    )(qlat, qidx, wht, kwb3, ckvn3, wuv_t)


def _merge_kernel(a_ref, b_ref, c_ref, gt_ref, x_ref, g1_ref, wa_ref, wb_ref, wc_ref, wo_ref, o_ref):
    d = D_MODEL
    g = jax.nn.sigmoid(gt_ref[...].astype(F32))
    ya = jnp.dot(a_ref[...], wa_ref[...], preferred_element_type=F32)
    yb = jnp.dot(b_ref[...], wb_ref[...], preferred_element_type=F32)
    yc = jnp.dot(c_ref[...], wc_ref[...], preferred_element_type=F32)
    m = g[:, :d] * ya + g[:, d:2 * d] * yb + g[:, 2 * d:] * yc
    o_ref[...] = x_ref[...] + g1_ref[0] * _dot(m, wo_ref[...])


def _merge(ya_in, yb_in, yc_in, proj2, x2, mod3, wa, wb, wc, wo, seq):
    t, d = x2.shape
    tm = min(512, seq)
    full = lambda a: pl.BlockSpec(a.shape, lambda i: (0,) * a.ndim)
    br = lambda w: pl.BlockSpec((tm, w), lambda i: (i, 0))
    return pl.pallas_call(
        _merge_kernel, name="merge",
        grid=(t // tm,),
        in_specs=[br(GLA_V), br(DSA_V), br(MLSTM_V), br(3 * d), br(d),
                  pl.BlockSpec((1, 1, d), lambda i: ((i * tm) // seq, 0, 2)),
                  full(wa), full(wb), full(wc), full(wo)],
        out_specs=br(d),
        out_shape=jax.ShapeDtypeStruct((t, d), F32),
        compiler_params=_params(("parallel",)),
    )(ya_in, yb_in, yc_in, proj2, x2, mod3, wa, wb, wc, wo)


def _first_argmax_mask(cur, iota, axis, n):
    mx = jnp.max(cur, axis=axis, keepdims=True)
    ix = jnp.min(jnp.where(cur == mx, iota, n), axis=axis, keepdims=True)
    return iota == ix


def _router_kernel(x_ref, sc_ref, sh_ref, g_ref, rwt_ref, rb_ref, dest_ref, wgt_ref, starts_ref, plens_ref):
    ne, ng = N_EXPERTS, N_GROUPS
    eg = ne // ng
    h = _rms(x_ref[...], g_ref[...]) * (1.0 + sc_ref[0]) + sh_ref[0]
    tm = h.shape[0]
    logits = lax.dot_general(rwt_ref[...], h, (((1,), (1,)), ((), ())),
                             preferred_element_type=F32, precision=HIGHEST)
    scores = jax.nn.sigmoid(logits)
    sel = scores + rb_ref[...]
    s3 = sel.reshape(ng, eg, tm)
    io3 = lax.broadcasted_iota(jnp.int32, (ng, eg, tm), 1)
    m1 = jnp.max(s3, axis=1, keepdims=True)
    first = _first_argmax_mask(s3, io3, 1, eg)
    m2 = jnp.max(jnp.where(first, -jnp.inf, s3), axis=1, keepdims=True)
    gs = (m1 + m2).reshape(ng, tm)
    iog = lax.broadcasted_iota(jnp.int32, (ng, tm), 0)
    gkeep = jnp.zeros((ng, tm), F32)
    cur = gs
    for _ in range(TOPK_GROUPS):
        hit = _first_argmax_mask(cur, iog, 0, ng)
        gkeep = jnp.where(hit, 1.0, gkeep)
        cur = jnp.where(hit, -jnp.inf, cur)
    selm = jnp.where(gkeep.reshape(ng, 1, tm) > 0.0, s3, -jnp.inf).reshape(ne, tm)
    ioe = lax.broadcasted_iota(jnp.int32, (ne, tm), 0)
    hits = []
    chosen = jnp.zeros((ne, tm), F32)
    cur = selm
    for _ in range(TOP_K):
        hit = _first_argmax_mask(cur, ioe, 0, ne)
        hits.append(hit)
        chosen = jnp.where(hit, 1.0, chosen)
        cur = jnp.where(hit, -jnp.inf, cur)
    w = chosen * scores
    w = w / jnp.sum(w, axis=0, keepdims=True) * ROUTED_SCALE

    cnt = jnp.sum(chosen, axis=1, keepdims=True)
    plen = jnp.ceil(cnt * (1.0 / ROW_ALIGN)) * ROW_ALIGN
    start = _cumsum_rows(jnp.broadcast_to(plen, (ne, LANE)))[:, :1] - plen
    before = (lax.broadcasted_iota(jnp.int32, (tm, tm), 0)
              < lax.broadcasted_iota(jnp.int32, (tm, tm), 1)).astype(BF16)
    rank = jnp.dot(chosen.astype(BF16), before, preferred_element_type=F32)
    row_of = start + rank
    pad_rows = SLOT_ROWS - TOP_K
    dest = [jnp.sum(jnp.where(hit, row_of, 0.0), axis=0, keepdims=True) for hit in hits]
    wsel = [jnp.sum(jnp.where(hit, w, 0.0), axis=0, keepdims=True) for hit in hits]
    dest_ref[...] = jnp.concatenate(dest + [jnp.full((pad_rows, tm), -1.0, F32)], axis=0).astype(jnp.int32)
    wgt_ref[...] = jnp.concatenate(wsel + [jnp.zeros((pad_rows, tm), F32)], axis=0)
    starts_ref[0] = start.astype(jnp.int32)
    plens_ref[0] = plen.astype(jnp.int32)


def _router(x2, mod3, norm_g, rw_t, rb_col, seq):
    t, d = x2.shape
    tm = MOE_SUB
    nsb = t // tm
    return pl.pallas_call(
        _router_kernel, name="router",
        grid=(nsb,),
        in_specs=[pl.BlockSpec((tm, d), lambda i: (i, 0)),
                  pl.BlockSpec((1, 1, d), lambda i: ((i * tm) // seq, 0, 4)),
                  pl.BlockSpec((1, 1, d), lambda i: ((i * tm) // seq, 0, 3)),
                  pl.BlockSpec((1, d), lambda i: (0, 0)),
                  pl.BlockSpec(rw_t.shape, lambda i: (0, 0)),
                  pl.BlockSpec(rb_col.shape, lambda i: (0, 0))],
        out_specs=[pl.BlockSpec((SLOT_ROWS, tm), lambda i: (0, i)),
                   pl.BlockSpec((SLOT_ROWS, tm), lambda i: (0, i)),
                   pl.BlockSpec((1, N_EXPERTS, 1), lambda i: (i, 0, 0)),
                   pl.BlockSpec((1, N_EXPERTS, 1), lambda i: (i, 0, 0))],
        out_shape=[jax.ShapeDtypeStruct((SLOT_ROWS, t), jnp.int32),
                   jax.ShapeDtypeStruct((SLOT_ROWS, t), F32),
                   jax.ShapeDtypeStruct((nsb, N_EXPERTS, 1), jnp.int32),
                   jax.ShapeDtypeStruct((nsb, N_EXPERTS, 1), jnp.int32)],
        compiler_params=_params(("parallel",)),
    )(x2, mod3, mod3, norm_g, rw_t, rb_col)


def _moe_kernel(starts_ref, plens_ref, x_ref, sc_ref, sh_ref, g2_ref, gn_ref, dest_ref, wgt_ref,
                wg_ref, wu_ref, wd_ref, sg_ref, su_ref, sd_ref, nf_ref, o_ref,
                h_ref, xs_ref, stage_ref, *, final):
    blk, step = pl.program_id(0), pl.program_id(1)
    eps = wg_ref.shape[0]
    ne = pl.num_programs(1) * eps
    sub, rt, mt = MOE_SUB, MOE_ROW_TILE, MOE_FFN_TILE
    nsub = x_ref.shape[0] // sub
    rmax = xs_ref.shape[1]

    @pl.when(step == 0)
    def _():
        h = _mx(_rms(x_ref[...], gn_ref[...]) * (1.0 + sc_ref[0]) + sh_ref[0])
        h_ref[...] = h
        stage_ref[...] = jnp.zeros_like(stage_ref)
        for sb in range(nsub):
            dest = dest_ref[:, sb * sub:(sb + 1) * sub]
            hs = h_ref[sb * sub:(sb + 1) * sub, :]
            for r in range(rmax // rt):
                rows = r * rt + lax.broadcasted_iota(jnp.int32, (rt, sub), 0)
                pick = jnp.zeros((rt, sub), F32)
                for j in range(TOP_K):
                    pick = jnp.where(dest[j:j + 1, :] == rows, 1.0, pick)
                xs_ref[sb, r * rt:(r + 1) * rt, :] = _dot(pick, hs).astype(xs_ref.dtype)

    def run(sb, e):
        i = (blk * nsub + sb) * ne + e
        return starts_ref[i], plens_ref[i] // ROW_ALIGN

    def copy_rows(src, src0, dst, dst0, n_chunks):
        def body(k, carry):
            s = pl.multiple_of(src0 + k * ROW_ALIGN, ROW_ALIGN)
            d = pl.multiple_of(dst0 + k * ROW_ALIGN, ROW_ALIGN)
            dst[pl.ds(d, ROW_ALIGN), :] = src[pl.ds(s, ROW_ALIGN), :]
            return carry
        lax.fori_loop(0, n_chunks, body, 0)

    def expert(j, xt):
        return _dot(_silu(_dot(xt, wg_ref[j])) * _dot(xt, wu_ref[j]), wd_ref[j])

    win = MOE_WINDOW
    runs = [[run(sb, step * eps + j) for sb in range(nsub)] for j in range(eps)]
    fits = None
    for per_expert in runs:
        for _, nch in per_expert:
            ok = nch * ROW_ALIGN <= win
            fits = ok if fits is None else jnp.logical_and(fits, ok)

    @pl.when(fits)
    def _():
        def window(sb, st):
            return xs_ref[sb, pl.ds(pl.multiple_of(st, ROW_ALIGN), win), :]
        xin = [jnp.concatenate([window(sb, st) for sb, (st, _) in enumerate(runs[j])], axis=0)
               for j in range(eps)]
        out = [expert(j, xin[j]).astype(xs_ref.dtype) for j in range(eps)]
        row = lax.broadcasted_iota(jnp.int32, (win, 1), 0)
        for j in range(eps):
            for sb, (st, nch) in enumerate(runs[j]):
                rows = slice(sb * win, (sb + 1) * win)
                xs_ref[sb, pl.ds(pl.multiple_of(st, ROW_ALIGN), win), :] = jnp.where(
                    row < nch * ROW_ALIGN, out[j][rows], xin[j][rows])

    @pl.when(jnp.logical_not(fits))
    def _():
        for j in range(eps):
            cursor = 0
            placed = []
            for sb, (st, nch) in enumerate(runs[j]):
                copy_rows(xs_ref.at[sb], st, stage_ref, cursor, nch)
                placed.append((sb, st, nch, cursor))
                cursor = cursor + nch * ROW_ALIGN

            def tile(i, carry, j=j):
                r0 = pl.multiple_of(i * mt, mt)
                stage_ref[pl.ds(r0, mt), :] = expert(j, stage_ref[pl.ds(r0, mt), :]).astype(stage_ref.dtype)
                return carry
            lax.fori_loop(0, (cursor + mt - 1) // mt, tile, 0)
            for sb, st, nch, at in placed:
                copy_rows(stage_ref, at, xs_ref.at[sb], st, nch)

    @pl.when(step == pl.num_programs(1) - 1)
    def _():
        for sb in range(nsub):
            tok = slice(sb * sub, (sb + 1) * sub)
            hs = h_ref[tok, :]
            shared = _dot(_silu(_dot(hs, sg_ref[...])) * _dot(hs, su_ref[...]), sd_ref[...])
            dest_t = jnp.transpose(dest_ref[:, tok].astype(F32))
            wgt_t = jnp.transpose(wgt_ref[:, tok])
            routed = jnp.zeros((sub, x_ref.shape[1]), F32)
            for r in range(rmax // rt):
                cols = (r * rt + lax.broadcasted_iota(jnp.int32, (sub, rt), 1)).astype(F32)
                mix = jnp.zeros((sub, rt), F32)
                for j in range(TOP_K):
                    mix = jnp.where(dest_t[:, j:j + 1] == cols, wgt_t[:, j:j + 1], mix)
                routed = routed + _dot(mix, xs_ref[sb, r * rt:(r + 1) * rt, :])
            xo = x_ref[tok, :] + g2_ref[0] * (shared + routed)
            if final:
                xo = _rms(xo, nf_ref[...])
            o_ref[tok, :] = xo


def _round_up(n, m):
    return (n + m - 1) // m * m


def _moe(x2, mod3, norm_g, dest, wgt, starts, plens, wg, wu, wd, sg, su, sd, nf, seq, final):
    t, d = x2.shape
    tm = min(MOE_BLOCK, seq)
    nsub = tm // MOE_SUB
    ne, ff = wg.shape[0], wg.shape[2]
    rmax = _round_up(TOP_K * MOE_SUB + ne * (ROW_ALIGN - 1) + MOE_WINDOW, MOE_ROW_TILE)
    stage_rows = _round_up(tm + nsub * (ROW_ALIGN - 1), MOE_FFN_TILE)
    eps = MOE_EXPERTS_PER_STEP
    full = lambda a: pl.BlockSpec(a.shape, lambda i, e, *_: (0,) * a.ndim)
    once = lambda a: pl.BlockSpec(a.shape, lambda i, e, *_: (0,) * a.ndim, pipeline_mode=pl.Buffered(1))
    mod = lambda j: pl.BlockSpec((1, 1, d), lambda i, e, *_: ((i * tm) // seq, 0, j))
    slot = pl.BlockSpec((SLOT_ROWS, tm), lambda i, e, *_: (0, i))
    grid_spec = pltpu.PrefetchScalarGridSpec(
        num_scalar_prefetch=2,
        grid=(t // tm, ne // eps),
        in_specs=[pl.BlockSpec((tm, d), lambda i, e, *_: (i, 0), pipeline_mode=pl.Buffered(1)),
                  mod(4), mod(3), mod(5),
                  pl.BlockSpec((1, d), lambda i, e, *_: (0, 0)), slot, slot,
                  pl.BlockSpec((eps, d, ff), lambda i, e, *_: (e, 0, 0)),
                  pl.BlockSpec((eps, d, ff), lambda i, e, *_: (e, 0, 0)),
                  pl.BlockSpec((eps, ff, d), lambda i, e, *_: (e, 0, 0)),
                  once(sg), once(su), once(sd), full(nf)],
        out_specs=pl.BlockSpec((tm, d), lambda i, e, *_: (i, 0), pipeline_mode=pl.Buffered(1)),
        scratch_shapes=[pltpu.VMEM((tm, d), MXU_DTYPE),
                        pltpu.VMEM((nsub, rmax, d), MXU_DTYPE),
                        pltpu.VMEM((stage_rows, d), MXU_DTYPE)])
    return pl.pallas_call(
        functools.partial(_moe_kernel, final=final), name="moe",
        grid_spec=grid_spec,
        out_shape=jax.ShapeDtypeStruct((t, d), F32),
        compiler_params=_params(("parallel", "arbitrary"), vmem_mb=58),
    )(starts.reshape(-1), plens.reshape(-1), x2, mod3, mod3, mod3, norm_g, dest, wgt,
      wg, wu, wd, sg, su, sd, nf)


def _pack_w_in(w):
    d = w.shape[0]
    offs = [0]
    for n in IN_SIZES:
        offs.append(offs[-1] + n)
    (qa, ka, va, ga, alr, cq, ckv, kidx, widx, qc, kc, vc, ic, fc, oc, gates) = [
        w[:, offs[i]:offs[i + 1]] for i in range(len(IN_SIZES))]
    z = lambda n: jnp.zeros((d, n), w.dtype)
    packed = jnp.concatenate(
        [gates, qa, ka, va, ga, vc, oc, qc, kc, cq, ckv,
         kidx, widx, z(LANE - IDX_DIM - IDX_HEADS),
         alr, z(LANE - GLA_GATE_RANK),
         ic, fc, z(LANE - 2 * MLSTM_HEADS)], axis=1)
    assert packed.shape[1] == N_PACK
    return packed.astype(MXU_DTYPE)


def kernel(x, c, ada_w, ada_b, norm_mix, norm_ffn, w_in, gla_w_a2, gla_b_a, gla_norm, dsa_norm_q,
           dsa_norm_kv, dsa_w_uq, dsa_w_uk, dsa_w_uv, dsa_w_qi, mlstm_conv, mlstm_b_i, mlstm_b_f,
           mlstm_norm, w_up_a, w_up_b, w_up_c, w_o, router_w, router_bias, exp_w_gate, exp_w_up,
           exp_w_down, sh_w_gate, sh_w_up, sh_w_down, norm_final):
    b, s, d = x.shape
    depth = ada_w.shape[0]
    t = b * s
    mod = _ada_mod(c, ada_w, ada_b)
    x2 = x.reshape(t, d)
    row = lambda v: v.reshape(1, -1)
    for l in range(depth):
        mod3 = mod[l].reshape(b, 1, 6 * d)
        w_pack = _pack_w_in(w_in[l])
        proj2, tail2 = _in_proj(x2, mod3, row(norm_mix[l]), w_pack, s)
        proj3 = proj2.reshape(b, s, N_MAIN)
        tail3 = tail2.reshape(b, s, N_TAIL)

        wa2_pad = jnp.zeros((LANE, GLA_QK), F32).at[:GLA_GATE_RANK].set(gla_w_a2[l])
        ya_in = _gla(proj3, tail3, wa2_pad, row(gla_b_a[l]), row(gla_norm[l]))

        bias_row = jnp.zeros((1, LANE), F32).at[0, :MLSTM_HEADS].set(mlstm_b_i[l])
        bias_row = bias_row.at[0, MLSTM_HEADS:2 * MLSTM_HEADS].set(mlstm_b_f[l])
        yc_in = _mlstm(proj3, tail3, mlstm_conv[l], bias_row, row(mlstm_norm[l]))

        wuq = dsa_w_uq[l].reshape(DSA_Q_RANK, DSA_HEADS * DSA_HEAD_DIM).astype(MXU_DTYPE)
        wuk_t = jnp.transpose(dsa_w_uk[l], (1, 2, 0)).astype(MXU_DTYPE)
        wuv_t = jnp.transpose(dsa_w_uv[l], (1, 2, 0)).astype(MXU_DTYPE)
        wqi_pad = jnp.zeros((DSA_Q_RANK, IDX_HEADS, LANE), F32).at[:, :, :IDX_DIM].set(dsa_w_qi[l])
        wqi_pad = wqi_pad.reshape(DSA_Q_RANK, IDX_HEADS * LANE).astype(MXU_DTYPE)
        ckvn, kwb, qlat, qidx, wht = _dsa_prep(proj2, tail2, s, row(dsa_norm_q[l]), row(dsa_norm_kv[l]),
                                               wuq, wuk_t, wqi_pad)
        yb_in = _dsa(ckvn.reshape(b, s, DSA_AUG), kwb.reshape(b, s, LANE), qlat, qidx, wht, wuv_t)

        x2 = _merge(ya_in.reshape(t, GLA_V), yb_in.reshape(t, DSA_V), yc_in.reshape(t, MLSTM_V),
                    proj2, x2, mod3, w_up_a[l].astype(MXU_DTYPE), w_up_b[l].astype(MXU_DTYPE),
                    w_up_c[l].astype(MXU_DTYPE), w_o[l].astype(MXU_DTYPE), s)

        dest, wgt, starts, plens = _router(x2, mod3, row(norm_ffn[l]), jnp.transpose(router_w[l]),
                                           router_bias[l].reshape(-1, 1), s)
        x2 = _moe(x2, mod3, row(norm_ffn[l]), dest, wgt, starts, plens,
                  exp_w_gate[l].astype(MXU_DTYPE), exp_w_up[l].astype(MXU_DTYPE),
                  exp_w_down[l].astype(MXU_DTYPE), sh_w_gate[l].astype(MXU_DTYPE),
                  sh_w_up[l].astype(MXU_DTYPE), sh_w_down[l].astype(MXU_DTYPE), row(norm_final), s,
                  final=(l == depth - 1))
    return x2.reshape(b, s, d)
```

```python
import functools
import struct

import jax
import jax.numpy as jnp
from jax import lax
from jax.experimental import pallas as pl
from jax.experimental.pallas import tpu as pltpu

F32 = jnp.float32
BF16 = jnp.bfloat16
MXU_DTYPE = jnp.bfloat16
ACT_DTYPE = jnp.bfloat16
HIGHEST = lax.Precision.HIGHEST

EPS = 1e-6
D_MODEL = 1024
GLA_HEADS, GLA_DK, GLA_DV, GLA_GATE_RANK, GLA_TAU, GLA_CHUNK = 4, 64, 128, 16, 16.0, 64
GLA_SUB = 16
DSA_HEADS, DSA_Q_RANK, DSA_KV_RANK, DSA_HEAD_DIM, DSA_V_DIM = 8, 256, 128, 64, 64
IDX_HEADS, IDX_DIM, DSA_TOPK_MAX = 8, 32, 256
MLSTM_HEADS, MLSTM_DQK, MLSTM_DV, MLSTM_CONV, MLSTM_CHUNK = 4, 64, 128, 4, 64
N_EXPERTS, TOP_K, N_GROUPS, TOPK_GROUPS, EXPERT_FF, SHARED_FF = 64, 6, 8, 4, 256, 256
ROUTED_SCALE = 2.5

GLA_QK = GLA_HEADS * GLA_DK
GLA_V = GLA_HEADS * GLA_DV
DSA_V = DSA_HEADS * DSA_V_DIM
MLSTM_QK = MLSTM_HEADS * MLSTM_DQK
MLSTM_V = MLSTM_HEADS * MLSTM_DV
IN_SIZES = (GLA_QK, GLA_QK, GLA_V, GLA_V, GLA_GATE_RANK,
            DSA_Q_RANK, DSA_KV_RANK, IDX_DIM, IDX_HEADS,
            MLSTM_QK, MLSTM_QK, MLSTM_V, MLSTM_HEADS, MLSTM_HEADS, MLSTM_V,
            3 * D_MODEL)

LANE = 128
KEY_TILE = 256
Q_TILE = 256
ATT_HEAD_GROUP = 4
NEG_BIG = -1e30
MOE_BLOCK = 1024
MOE_SUB = 256
MOE_ROW_TILE = 256
MOE_WINDOW = 48
MOE_FFN_TILE = 128
MOE_EXPERTS_PER_STEP = 4
CONV_PAD = 8
SEQ_GROUP = 16
SLOT_ROWS = 8
ROW_ALIGN = 16

C_GATES = 0
C_QA = 3072
C_KA = 3328
C_VA = 3584
C_GA = 4096
C_VC = 4608
C_OC = 5120
C_QKC = 5632
C_CQ = 6144
N_MAIN = 6400
T_CKV = 0
T_KW = 128
T_ALR = 256
T_ICFC = 384
N_TAIL = 512
N_PACK = N_MAIN + N_TAIL
INPROJ_PIECES = 5
W_IDX_LANE = IDX_DIM


LOG2E = 1.4426950408889634
DSA_AUG = DSA_KV_RANK + LANE
POS_SPLIT = 64
SLOPE_PIECES = 3
SUBLANE = 8
PARTIAL_ROWS = 32
COUNT_UNSET = float(2 ** 30)


def _bf16_pieces(x, n):
    out = []
    for _ in range(n):
        bits = struct.unpack("<I", struct.pack("<f", x))[0]
        bits = (bits + 0x7FFF + ((bits >> 16) & 1)) & 0xFFFF0000
        piece = struct.unpack("<f", struct.pack("<I", bits))[0]
        out.append(piece)
        x -= piece
    return out


def _mx(x):
    return x.astype(MXU_DTYPE)


def _dot(a, b):
    return jnp.dot(_mx(a), _mx(b), preferred_element_type=F32)


def _dot_nt(a, b):
    return lax.dot_general(_mx(a), _mx(b), (((1,), (1,)), ((), ())), preferred_element_type=F32)


def _dot_tn(a, b):
    return lax.dot_general(_mx(a), _mx(b), (((0,), (0,)), ((), ())), preferred_element_type=F32)


def _rms(x, g):
    return x * lax.rsqrt(jnp.mean(x * x, axis=-1, keepdims=True) + EPS) * g


def _silu(x):
    return x * jax.nn.sigmoid(x)


def _log_sigmoid(z):
    return jnp.minimum(z, 0.0) - jnp.log1p(jnp.exp(-jnp.abs(z)))


def _cumsum_rows(x):
    n = x.shape[0]
    tri = (lax.broadcasted_iota(jnp.int32, (n, n), 1) <= lax.broadcasted_iota(jnp.int32, (n, n), 0)).astype(F32)
    return jnp.dot(tri, x, preferred_element_type=F32, precision=HIGHEST)


def _truncate_to_bf16(x):
    bits = lax.bitcast_convert_type(x, jnp.int32) & jnp.int32(-65536)
    return lax.bitcast_convert_type(bits, F32).astype(BF16)


def _colreduce(x, op, width=PARTIAL_ROWS):
    n, c = x.shape
    return op(op(x.reshape(n // width, width, c), axis=0), axis=0, keepdims=True)


def _params(sem, vmem_mb=40):
    return pltpu.CompilerParams(dimension_semantics=sem, vmem_limit_bytes=vmem_mb * 1024 * 1024)


def _ada_kernel(c_ref, w_ref, b_ref, o_ref):
    cs = _silu(c_ref[...])
    o_ref[0] = jnp.dot(cs, w_ref[0], preferred_element_type=F32, precision=HIGHEST) + b_ref[0]


def _ada_mod(c, ada_w, ada_b):
    depth, d, n = ada_w.shape
    b = c.shape[0]
    return pl.pallas_call(
        _ada_kernel, name="ada_mod",
        grid=(depth, n // d),
        in_specs=[pl.BlockSpec((b, d), lambda l, j: (0, 0)),
                  pl.BlockSpec((1, d, d), lambda l, j: (l, 0, j)),
                  pl.BlockSpec((1, 1, d), lambda l, j: (l, 0, j))],
        out_specs=pl.BlockSpec((1, b, d), lambda l, j: (l, 0, j)),
        out_shape=jax.ShapeDtypeStruct((depth, b, n), F32),
        compiler_params=_params(("parallel", "parallel")),
    )(c, ada_w, ada_b.reshape(depth, 1, n))


def _inproj_kernel(x_ref, sc_ref, sh_ref, g_ref, wm_ref, wt_ref, om_ref, ot_ref):
    h = _mx(_rms(x_ref[...], g_ref[...]) * (1.0 + sc_ref[0]) + sh_ref[0])
    piece = N_MAIN // INPROJ_PIECES
    for j in range(INPROJ_PIECES):
        cols = slice(j * piece, (j + 1) * piece)
        om_ref[:, cols] = jnp.dot(h, wm_ref[:, cols], preferred_element_type=F32).astype(om_ref.dtype)
    ot_ref[...] = jnp.dot(h, wt_ref[...], preferred_element_type=F32)


def _in_proj(x2, mod3, norm_g, w_pack, seq):
    t, d = x2.shape
    tm = min(512, seq)
    once = lambda w: pl.BlockSpec((d, w), lambda i: (0, 0), pipeline_mode=pl.Buffered(1))
    return pl.pallas_call(
        _inproj_kernel, name="in_proj",
        grid=(t // tm,),
        in_specs=[pl.BlockSpec((tm, d), lambda i: (i, 0)),
                  pl.BlockSpec((1, 1, d), lambda i: ((i * tm) // seq, 0, 1)),
                  pl.BlockSpec((1, 1, d), lambda i: ((i * tm) // seq, 0, 0)),
                  pl.BlockSpec((1, d), lambda i: (0, 0)),
                  once(N_MAIN), once(N_TAIL)],
        out_specs=[pl.BlockSpec((tm, N_MAIN), lambda i: (i, 0)),
                   pl.BlockSpec((tm, N_TAIL), lambda i: (i, 0))],
        out_shape=[jax.ShapeDtypeStruct((t, N_MAIN), ACT_DTYPE),
                   jax.ShapeDtypeStruct((t, N_TAIL), F32)],
        compiler_params=_params(("parallel",), vmem_mb=48),
    )(x2, mod3, mod3, norm_g, w_pack[:, :N_MAIN], w_pack[:, N_MAIN:])


def _gla_kernel(q_ref, k_ref, v_ref, g_ref, alr_ref, wa2_ref, ba_ref, gn_ref, o_ref, s_ref, acc_ref):
    @pl.when(pl.program_id(1) == 0)
    def _():
        s_ref[...] = jnp.zeros_like(s_ref)

    L, sub, nh, dk, dv = GLA_CHUNK, GLA_SUB, GLA_HEADS, GLA_DK, GLA_DV
    seqs = range(q_ref.shape[0])
    heads = range(nh)
    hk = lambda h: slice(h * dk, (h + 1) * dk)
    hv = lambda h: slice(h * dv, (h + 1) * dv)

    pre = []
    for g in seqs:
        z = jnp.dot(alr_ref[g], wa2_ref[...], preferred_element_type=F32, precision=HIGHEST) + ba_ref[...]
        cum = _cumsum_rows(_log_sigmoid(z) * (1.0 / GLA_TAU))
        q = q_ref[g].astype(F32) * (dk ** -0.5)
        k = k_ref[g].astype(F32)
        tot = cum[L - 1:L, :]
        pre.append(dict(cum=cum, q=q, k=k, tot=tot, vb=_mx(v_ref[g]),
                        q_in=_mx(q * jnp.exp(cum)), k_dec=_mx(k * jnp.exp(tot - cum))))

    scores = {}
    for i in range(L // sub):
        r0, r1 = i * sub, (i + 1) * sub
        for g in seqs:
            p = pre[g]
            base = p["cum"][r0 - 1:r0, :] if i > 0 else jnp.zeros_like(p["tot"])
            qi = _mx(p["q"][r0:r1] * jnp.exp(p["cum"][r0:r1] - base))
            ka = _mx(p["k"][:r1] * jnp.exp(base - p["cum"][:r1]))
            for h in heads:
                scores[g, i, h] = _dot_nt(qi[:, hk(h)], ka[:, hk(h)])

    for i in range(L // sub):
        r0, r1 = i * sub, (i + 1) * sub
        causal = (lax.broadcasted_iota(jnp.int32, (sub, r1), 1)
                  <= lax.broadcasted_iota(jnp.int32, (sub, r1), 0) + r0)
        for g in seqs:
            for h in heads:
                s = jnp.where(causal, scores[g, i, h], 0.0)
                acc_ref[g, r0:r1, hv(h)] = _dot(s, pre[g]["vb"][:r1, hv(h)])

    inter = {(g, h): _dot(pre[g]["q_in"][:, hk(h)], s_ref[g * nh + h]) for g in seqs for h in heads}
    update = {(g, h): _dot_tn(pre[g]["k_dec"][:, hk(h)], pre[g]["vb"][:, hv(h)]) for g in seqs for h in heads}

    gn = gn_ref[...]
    for g in seqs:
        gate = g_ref[g].astype(F32)
        for h in heads:
            o = acc_ref[g, :, hv(h)] + inter[g, h]
            y = _rms(o, gn[:, hv(h)]) * _silu(gate[:, hv(h)])
            o_ref[g, :, hv(h)] = y.astype(o_ref.dtype)
            decay = jnp.transpose(jnp.exp(pre[g]["tot"][:, hk(h)]))
            s_ref[g * nh + h] = s_ref[g * nh + h] * decay + update[g, h]


def _seq_group(batch):
    for grp in (SEQ_GROUP, 2, 1):
        if batch % grp == 0:
            return grp


def _gla(proj3, tail3, wa2_pad, ba, gn):
    b, s, _ = proj3.shape
    L = GLA_CHUNK
    grp = _seq_group(b)
    blk = lambda w, c0: pl.BlockSpec((grp, L, w), lambda bi, ci: (bi, ci, c0 // w))
    full = lambda a: pl.BlockSpec(a.shape, lambda bi, ci: (0,) * a.ndim)
    return pl.pallas_call(
        _gla_kernel, name="gla",
        grid=(b // grp, s // L),
        in_specs=[blk(GLA_QK, C_QA), blk(GLA_QK, C_KA), blk(GLA_V, C_VA), blk(GLA_V, C_GA),
                  blk(LANE, T_ALR), full(wa2_pad), full(ba), full(gn)],
        out_specs=pl.BlockSpec((grp, L, GLA_V), lambda bi, ci: (bi, ci, 0)),
        out_shape=jax.ShapeDtypeStruct((b, s, GLA_V), ACT_DTYPE),
        scratch_shapes=[pltpu.VMEM((grp * GLA_HEADS, GLA_DK, GLA_DV), F32),
                        pltpu.VMEM((grp, L, GLA_V), F32)],
        compiler_params=_params(("parallel", "arbitrary")),
    )(proj3, proj3, proj3, proj3, tail3, wa2_pad, ba, gn)


def _mlstm_pair_kernel(qk_ref, v_ref, oc_ref, if_ref, conv_ref, bias_ref, gnt_ref, o_ref,
                       xbuf_ref, ct_ref, n_ref, m_ref):
    L, nh, dk, dv, kc = MLSTM_CHUNK, MLSTM_HEADS, MLSTM_DQK, MLSTM_DV, MLSTM_CONV
    pad = CONV_PAD
    npair = nh // 2
    assert 2 * dk == LANE and dv == LANE and L == dk

    @pl.when(pl.program_id(1) == 0)
    def _():
        xbuf_ref[:, 0:pad, :] = jnp.zeros((xbuf_ref.shape[0], pad, 2 * MLSTM_QK), F32)
        ct_ref[...] = jnp.zeros_like(ct_ref)
        n_ref[...] = jnp.zeros_like(n_ref)
        m_ref[...] = jnp.zeros_like(m_ref)

    lane = lax.broadcasted_iota(jnp.int32, (1, LANE), 1)
    half = [(lane < dk).astype(F32), (lane >= dk).astype(F32)]
    s_idx = lax.broadcasted_iota(jnp.int32, (L, LANE), 0)
    t_idx = lax.broadcasted_iota(jnp.int32, (L, LANE), 1) % L
    causal = s_idx <= t_idx
    lane_in = lax.broadcasted_iota(jnp.int32, (LANE, LANE), 0)
    head_of = lax.broadcasted_iota(jnp.int32, (LANE, LANE), 1) // dk
    cw = conv_ref[...]

    combos = [(g, p) for g in range(qk_ref.shape[0]) for p in range(npair)]
    seq = {}
    for g in range(qk_ref.shape[0]):
        xbuf_ref[g, pad:pad + L, :] = qk_ref[g].astype(F32)
        conv = jnp.zeros((L, 2 * MLSTM_QK), F32)
        for j in range(kc):
            conv = conv + cw[j:j + 1, :] * xbuf_ref[g, pl.ds(pad - (kc - 1) + j, L), :]
        xbuf_ref[g, 0:pad, :] = xbuf_ref[g, L:L + pad, :]
        qk = _silu(conv)
        pre = if_ref[g] + bias_ref[...]
        bcum = _cumsum_rows(_log_sigmoid(pre))
        v = v_ref[g].astype(F32)
        seq[g] = dict(q=qk[:, :MLSTM_QK] * (dk ** -0.5), k=qk[:, MLSTM_QK:],
                      vt=[_mx(jnp.transpose(v[:, h * dv:(h + 1) * dv])) for h in range(nh)],
                      gate_mix=jnp.where(lane < nh, pre, -bcum), bcum=bcum, bcum_t=jnp.transpose(bcum))

    st = {}
    for g, p in combos:
        sq = seq[g]
        sel = jnp.where((lane_in == 2 * p + head_of) | (lane_in == nh + 2 * p + head_of), 1.0, 0.0)
        selb = jnp.where(lane_in == nh + 2 * p + head_of, 1.0, 0.0)
        d_mat = jnp.dot(sq["gate_mix"], sel, preferred_element_type=F32, precision=HIGHEST)
        tot = jnp.dot(sq["bcum"][L - SUBLANE:L, :], selb, preferred_element_type=F32,
                      precision=HIGHEST)[SUBLANE - 1:SUBLANE, :]
        b_row = jnp.concatenate([sq["bcum_t"][nh + 2 * p:nh + 2 * p + 1, :],
                                 sq["bcum_t"][nh + 2 * p + 1:nh + 2 * p + 2, :]], axis=1)
        qt = sq["q"][:, p * LANE:(p + 1) * LANE]
        kt = sq["k"][:, p * LANE:(p + 1) * LANE]
        q_bd = _mx(jnp.concatenate([qt * half[0], qt * half[1]], axis=0))
        st[g, p] = dict(d=d_mat, tot=tot, b_row=b_row, kt=kt, q_bd=q_bd,
                        scores=_dot_nt(kt, q_bd),
                        qn=_dot_nt(jnp.broadcast_to(n_ref[g * npair + p], (SUBLANE, LANE)), q_bd)[0:1, :],
                        inter=[_dot_nt(ct_ref[g * nh + 2 * p + hh], q_bd) for hh in range(2)])

    for g, p in combos:
        c = st[g, p]
        m_prev = m_ref[g * npair + p]
        dlog = jnp.where(causal, c["b_row"] + c["d"], -jnp.inf)
        inter_log = c["b_row"] + m_prev
        m_t = jnp.maximum(inter_log, jnp.max(dlog, axis=0, keepdims=True))
        g_log = c["tot"] + c["d"]
        m_new = jnp.maximum(c["tot"] + m_prev, jnp.max(g_log, axis=0, keepdims=True))
        c.update(m_t=m_t, w_inter=jnp.exp(inter_log - m_t), sw=c["scores"] * jnp.exp(dlog - m_t),
                 m_new=m_new, w_c=jnp.exp(c["tot"] + m_prev - m_new), ks=c["kt"] * jnp.exp(g_log - m_new))

    for g, p in combos:
        c = st[g, p]
        vt = seq[g]["vt"]
        c["sv"] = [_dot(vt[2 * p + hh], c["sw"] * half[hh]) for hh in range(2)]
        c["kv"] = [_dot(vt[2 * p + hh], c["ks"] * half[hh]) for hh in range(2)]

    for g, p in combos:
        c = st[g, p]
        num = c["sv"][0] + c["sv"][1] + c["w_inter"] * (c["inter"][0] + c["inter"][1])
        den = jnp.sum(c["sw"], axis=0, keepdims=True) + c["w_inter"] * c["qn"]
        hout = num / jnp.maximum(jnp.abs(den), jnp.exp(-c["m_t"]))
        y_t = hout * lax.rsqrt(jnp.mean(hout * hout, axis=0, keepdims=True) + EPS) * gnt_ref[p]
        y = jnp.transpose(y_t)
        for hh in range(2):
            cols = slice((2 * p + hh) * dv, (2 * p + hh + 1) * dv)
            o_ref[g, :, cols] = (y[hh * L:(hh + 1) * L, :] * jax.nn.sigmoid(oc_ref[g, :, cols].astype(F32))).astype(o_ref.dtype)
            ct_ref[g * nh + 2 * p + hh] = ct_ref[g * nh + 2 * p + hh] * c["w_c"] + c["kv"][hh]
        n_ref[g * npair + p] = n_ref[g * npair + p] * c["w_c"] + jnp.sum(c["ks"], axis=0, keepdims=True)
        m_ref[g * npair + p] = c["m_new"]


def _mlstm(proj3, tail3, conv_w, bias_row, gn):
    b, s, _ = proj3.shape
    L = MLSTM_CHUNK
    grp = _seq_group(b)
    blk = lambda w, c0: pl.BlockSpec((grp, L, w), lambda bi, ci: (bi, ci, c0 // w))
    full = lambda a: pl.BlockSpec(a.shape, lambda bi, ci: (0,) * a.ndim)
    npair = MLSTM_HEADS // 2
    gnt = jnp.repeat(jnp.transpose(gn.reshape(npair, 2, MLSTM_DV), (0, 2, 1)), L, axis=2)
    return pl.pallas_call(
        _mlstm_pair_kernel, name="mlstm",
        grid=(b // grp, s // L),
        in_specs=[blk(2 * MLSTM_QK, C_QKC), blk(MLSTM_V, C_VC), blk(MLSTM_V, C_OC), blk(LANE, T_ICFC),
                  full(conv_w), full(bias_row), full(gnt)],
        out_specs=pl.BlockSpec((grp, L, MLSTM_V), lambda bi, ci: (bi, ci, 0)),
        out_shape=jax.ShapeDtypeStruct((b, s, MLSTM_V), ACT_DTYPE),
        scratch_shapes=[pltpu.VMEM((grp, L + CONV_PAD, 2 * MLSTM_QK), F32),
                        pltpu.VMEM((grp * MLSTM_HEADS, MLSTM_DV, LANE), F32),
                        pltpu.VMEM((grp * npair, 1, LANE), F32),
                        pltpu.VMEM((grp * npair, 1, LANE), F32)],
        compiler_params=_params(("parallel", "arbitrary")),
    )(proj3, proj3, proj3, tail3, conv_w, bias_row, gnt)


def _dsa_prep_kernel(cq_ref, ckv_ref, kw_ref, nq_ref, nkv_ref, wuq_ref, wuk_ref, wqi_ref,
                     ckvn_ref, kwb_ref, qlat_ref, qidx_ref, wht_ref, *, tiles_per_seq):
    tm = cq_ref.shape[0]
    r = DSA_KV_RANK
    cqn = _mx(_rms(cq_ref[...].astype(F32), nq_ref[...]))
    pos = (pl.program_id(0) % tiles_per_seq) * tm + lax.broadcasted_iota(jnp.int32, (tm, LANE), 0)
    lane = lax.broadcasted_iota(jnp.int32, (tm, LANE), 1)
    pos_cols = jnp.where(lane < SLOPE_PIECES, pos >> (POS_SPLIT.bit_length() - 1),
                         jnp.where(lane < 2 * SLOPE_PIECES, pos & (POS_SPLIT - 1), 0)).astype(F32)
    ckvn_ref[:, :r] = _rms(ckv_ref[...], nkv_ref[...]).astype(ckvn_ref.dtype)
    ckvn_ref[:, r:] = pos_cols.astype(ckvn_ref.dtype)
    kw = kw_ref[...]
    kwb_ref[...] = kw.astype(kwb_ref.dtype)
    wht_ref[0] = jnp.transpose(kw)[W_IDX_LANE:W_IDX_LANE + IDX_HEADS, :] * (IDX_HEADS ** -0.5)
    q = jnp.dot(cqn, wuq_ref[...], preferred_element_type=F32)
    for h in range(DSA_HEADS):
        ql = _dot(q[:, h * DSA_HEAD_DIM:(h + 1) * DSA_HEAD_DIM], wuk_ref[h]) * (DSA_HEAD_DIM ** -0.5 * LOG2E)
        qlat_ref[0, h, :, :r] = ql.astype(qlat_ref.dtype)
        c = _bf16_pieces(2.0 ** (-8.0 * (h + 1) / DSA_HEADS) * LOG2E, SLOPE_PIECES)
        consts = [POS_SPLIT * v for v in c] + c
        slope_cols = jnp.zeros((tm, LANE), F32)
        for j, v in enumerate(consts):
            slope_cols = jnp.where(lane == j, v, slope_cols)
        qlat_ref[0, h, :, r:] = slope_cols.astype(qlat_ref.dtype)
    qi = jnp.dot(cqn, wqi_ref[...], preferred_element_type=F32) * (IDX_DIM ** -0.5)
    for h in range(IDX_HEADS):
        qidx_ref[0, h] = qi[:, h * LANE:(h + 1) * LANE].astype(qidx_ref.dtype)


def _dsa_prep(proj2, tail2, seq, nq, nkv, wuq, wuk_t, wqi_pad):
    t = proj2.shape[0]
    b = t // seq
    tm = min(512, seq)
    per = seq // tm
    blk = lambda w, c0: pl.BlockSpec((tm, w), lambda i: (i, c0 // w))
    full = lambda a: pl.BlockSpec(a.shape, lambda i: (0,) * a.ndim)
    hmap = lambda i: (i // per, 0, i % per, 0)
    return pl.pallas_call(
        functools.partial(_dsa_prep_kernel, tiles_per_seq=per), name="dsa_prep",
        grid=(t // tm,),
        in_specs=[blk(DSA_Q_RANK, C_CQ), blk(DSA_KV_RANK, T_CKV), blk(LANE, T_KW),
                  full(nq), full(nkv), full(wuq), full(wuk_t), full(wqi_pad)],
        out_specs=[pl.BlockSpec((tm, DSA_AUG), lambda i: (i, 0)),
                   pl.BlockSpec((tm, LANE), lambda i: (i, 0)),
                   pl.BlockSpec((1, DSA_HEADS, tm, DSA_AUG), hmap),
                   pl.BlockSpec((1, IDX_HEADS, tm, LANE), hmap),
                   pl.BlockSpec((1, IDX_HEADS, tm), lambda i: (i // per, 0, i % per))],
        out_shape=[jax.ShapeDtypeStruct((t, DSA_AUG), ACT_DTYPE),
                   jax.ShapeDtypeStruct((t, LANE), ACT_DTYPE),
                   jax.ShapeDtypeStruct((b, DSA_HEADS, seq, DSA_AUG), ACT_DTYPE),
                   jax.ShapeDtypeStruct((b, IDX_HEADS, seq, LANE), ACT_DTYPE),
                   jax.ShapeDtypeStruct((b, IDX_HEADS, seq), F32)],
        compiler_params=_params(("parallel",)),
    )(proj2, tail2, tail2, nq, nkv, wuq, wuk_t, wqi_pad)


def _dsa_kernel(qlat_ref, qidx_ref, wht_ref, kwk_ref, ckv_ref, wuv_ref, o_ref,
                ibuf_ref, ihi_ref, acc_ref, *, topk):
    nh, tq, tk = DSA_HEADS, Q_TILE, KEY_TILE
    qb = pl.program_id(1)
    n_kt = (qb * tq + tq + tk - 1) // tk
    t_row = qb * tq + lax.broadcasted_iota(jnp.int32, (1, tq), 1)
    s_col = lax.broadcasted_iota(jnp.int32, (tk, 1), 0)

    qi = qidx_ref[0].reshape(IDX_HEADS * tq, LANE)
    wht = wht_ref[0]

    def idx_body(kt, carry):
        kk = kwk_ref[0, pl.ds(pl.multiple_of(kt * tk, tk), tk), :]
        grp = ATT_HEAD_GROUP
        sc = [_dot_nt(kk, qi[g * grp * tq:(g + 1) * grp * tq]) for g in range(IDX_HEADS // grp)]
        tot = None
        for h in range(IDX_HEADS):
            part = wht[h:h + 1, :] * jnp.maximum(sc[h // grp][:, (h % grp) * tq:(h % grp + 1) * tq], 0.0)
            tot = part if tot is None else tot + part
        score = jnp.where(kt * tk + s_col <= t_row, tot, -jnp.inf)
        ibuf_ref[kt] = score
        ihi_ref[kt] = _truncate_to_bf16(score)
        return carry

    lax.fori_loop(0, n_kt, idx_body, 0)

    def count(pred):
        def body(kt, c):
            hit = jnp.where(pred(ibuf_ref[kt]), 1.0, 0.0)
            return c + jnp.sum(hit.reshape(tk // PARTIAL_ROWS, PARTIAL_ROWS, tq), axis=0)
        return jnp.sum(lax.fori_loop(0, n_kt, body, jnp.zeros((PARTIAL_ROWS, tq), F32)), axis=0, keepdims=True)

    def count_ge(cand):
        return count(lambda x: x >= cand)

    def count_gt(cand):
        return count(lambda x: x > cand)

    def key_to_float(u):
        key = u ^ jnp.int32(-2 ** 31)
        bits = jnp.where(key >= 0, key, key ^ jnp.int32(0x7FFFFFFF))
        return lax.bitcast_convert_type(bits, F32)

    few = t_row < topk
    n_bits = 32

    def try_bit(i, u, cnt_u, counter):
        cand_u = u | lax.shift_left(jnp.int32(1), n_bits - 1 - i)
        cnt = counter(cand_u)
        ok = cnt >= topk
        return jnp.where(ok, cand_u, u), jnp.where(ok, cnt, cnt_u)

    def count_ge_hi(cand_u):
        cand = _truncate_to_bf16(key_to_float(cand_u))
        def body(kt, c):
            hit = jnp.where(ihi_ref[kt] >= cand, jnp.ones((), BF16), jnp.zeros((), BF16))
            part = jnp.sum(hit.reshape(tk // PARTIAL_ROWS, PARTIAL_ROWS, tq), axis=0)
            return c + part.astype(F32)
        return jnp.sum(lax.fori_loop(0, n_kt, body, jnp.zeros((PARTIAL_ROWS, tq), F32)), axis=0, keepdims=True)

    def coarse_body(i, st):
        return try_bit(i, *st, count_ge_hi)

    def search_cond(st):
        i, _, _, pending = st
        return (i < n_bits) & (pending > 0)

    def search_body(st):
        i, u, cnt_u, _ = st
        for _ in range(4):
            u, cnt_u = try_bit(i, u, cnt_u, lambda c: count_ge(key_to_float(c)))
            i = i + 1
        pending = jnp.max(jnp.where(few | (cnt_u == topk), 0, 1))
        return i, u, cnt_u, pending

    u, cnt_u = lax.fori_loop(0, n_bits // 2, coarse_body,
                             (jnp.zeros((1, tq), jnp.int32), jnp.full((1, tq), COUNT_UNSET, F32)))
    start = (jnp.int32(n_bits // 2), u, cnt_u, jnp.max(jnp.where(few | (cnt_u == topk), 0, 1)))
    _, u, _, _ = lax.while_loop(search_cond, search_body, start)
    tau = jnp.where(few, -jnp.inf, key_to_float(u))
    need = topk - count_gt(tau)

    acc_ref[...] = jnp.zeros_like(acc_ref)
    ql = qlat_ref[0].reshape(nh * tq, DSA_AUG)
    tri = jnp.where(lax.broadcasted_iota(jnp.int32, (tk, tk), 0) >= lax.broadcasted_iota(jnp.int32, (tk, tk), 1),
                    1.0, 0.0).astype(BF16)

    def keys(kt):
        return ckv_ref[0, pl.ds(pl.multiple_of(kt * tk, tk), tk), :]

    def att_body(kt, carry):
        m, l, eq_seen = carry
        kv_aug = keys(kt)
        kv = kv_aug[:, :DSA_KV_RANK]
        it = ibuf_ref[kt]
        eq = it == tau
        eqf = jnp.where(eq, 1.0, 0.0)
        rank = jnp.dot(tri, eqf.astype(BF16), preferred_element_type=F32) + eq_seen
        valid = ((it > tau) | (eq & (rank <= need))) & (kt * tk + s_col <= t_row)
        bias = jnp.where(valid, 0.0, NEG_BIG)

        n_grp = nh // ATT_HEAD_GROUP
        gcols = lambda g: slice(g * ATT_HEAD_GROUP * tq, (g + 1) * ATT_HEAD_GROUP * tq)
        lg, soft, ms, ls = {}, {}, {}, {}

        def logits(g):
            lg[g] = _dot_nt(kv_aug, ql[gcols(g)])

        def softmax(g):
            als, ps = [], []
            for h in range(g * ATT_HEAD_GROUP, (g + 1) * ATT_HEAD_GROUP):
                cols = slice(h * tq, (h + 1) * tq)
                loc = slice((h - g * ATT_HEAD_GROUP) * tq, (h - g * ATT_HEAD_GROUP + 1) * tq)
                lh = lg[g][:, loc] + bias
                m_old = m[:, cols]
                m_new = jnp.maximum(m_old, _colreduce(lh, jnp.max))
                p = jnp.exp2(lh - m_new)
                alpha = jnp.exp2(m_old - m_new)
                ms[h] = m_new
                ls[h] = alpha * l[:, cols] + _colreduce(p, jnp.sum)
                als.append(alpha)
                ps.append(_mx(p))
            soft[g] = (jnp.concatenate(als, axis=1), jnp.concatenate(ps, axis=1))

        def weighted_values(g):
            alpha, p = soft[g]
            acc_ref[:, gcols(g)] = alpha * acc_ref[:, gcols(g)] + _dot_tn(kv, p)

        logits(0)
        for g in range(n_grp):
            if g + 1 < n_grp:
                logits(g + 1)
            softmax(g)
            if g > 0:
                weighted_values(g - 1)
        weighted_values(n_grp - 1)
        return (jnp.concatenate([ms[h] for h in range(nh)], axis=1),
                jnp.concatenate([ls[h] for h in range(nh)], axis=1),
                eq_seen + jnp.sum(eqf, axis=0, keepdims=True))

    init = (jnp.full((1, nh * tq), NEG_BIG, F32), jnp.zeros((1, nh * tq), F32), jnp.zeros((1, tq), F32))
    _, l, _ = lax.fori_loop(0, n_kt, att_body, init)

    outs = []
    for h in range(nh):
        cols = slice(h * tq, (h + 1) * tq)
        outs.append(_dot(wuv_ref[h], acc_ref[:, cols] / l[:, cols]))
    o_ref[0] = jnp.transpose(jnp.concatenate(outs, axis=0)).astype(o_ref.dtype)


def _dsa(ckvn3, kwb3, qlat, qidx, wht, wuv_t):
    b, s, _ = ckvn3.shape
    topk = min(DSA_TOPK_MAX, s // 4)
    nh, tq = DSA_HEADS, Q_TILE
    return pl.pallas_call(
        functools.partial(_dsa_kernel, topk=topk), name="dsa_attn",
        grid=(b, s // tq),
        in_specs=[pl.BlockSpec((1, nh, tq, DSA_AUG), lambda bi, qi: (bi, 0, qi, 0)),
                  pl.BlockSpec((1, IDX_HEADS, tq, LANE), lambda bi, qi: (bi, 0, qi, 0)),
                  pl.BlockSpec((1, IDX_HEADS, tq), lambda bi, qi: (bi, 0, qi)),
                  pl.BlockSpec((1, s, LANE), lambda bi, qi: (bi, 0, 0)),
                  pl.BlockSpec((1, s, DSA_AUG), lambda bi, qi: (bi, 0, 0)),
                  pl.BlockSpec(wuv_t.shape, lambda bi, qi: (0, 0, 0))],
        out_specs=pl.BlockSpec((1, tq, DSA_V), lambda bi, qi: (bi, qi, 0)),
        out_shape=jax.ShapeDtypeStruct((b, s, DSA_V), ACT_DTYPE),
        scratch_shapes=[pltpu.VMEM((s // KEY_TILE, KEY_TILE, tq), F32),
                        pltpu.VMEM((s // KEY_TILE, KEY_TILE, tq), BF16),
                        pltpu.VMEM((DSA_KV_RANK, nh * tq), F32)],
        compiler_params=_params(("parallel", "arbitrary")),
    )(qlat, qidx, wht, kwb3, ckvn3, wuv_t)


def _merge_kernel(a_ref, b_ref, c_ref, gt_ref, x_ref, g1_ref, wa_ref, wb_ref, wc_ref, wo_ref, o_ref):
    d = D_MODEL
    g = jax.nn.sigmoid(gt_ref[...].astype(F32))
    ya = jnp.dot(a_ref[...], wa_ref[...], preferred_element_type=F32)
    yb = jnp.dot(b_ref[...], wb_ref[...], preferred_element_type=F32)
    yc = jnp.dot(c_ref[...], wc_ref[...], preferred_element_type=F32)
    m = g[:, :d] * ya + g[:, d:2 * d] * yb + g[:, 2 * d:] * yc
    o_ref[...] = x_ref[...] + g1_ref[0] * _dot(m, wo_ref[...])


def _merge(ya_in, yb_in, yc_in, proj2, x2, mod3, wa, wb, wc, wo, seq):
    t, d = x2.shape
    tm = min(512, seq)
    full = lambda a: pl.BlockSpec(a.shape, lambda i: (0,) * a.ndim)
    br = lambda w: pl.BlockSpec((tm, w), lambda i: (i, 0))
    return pl.pallas_call(
        _merge_kernel, name="merge",
        grid=(t // tm,),
        in_specs=[br(GLA_V), br(DSA_V), br(MLSTM_V), br(3 * d), br(d),
                  pl.BlockSpec((1, 1, d), lambda i: ((i * tm) // seq, 0, 2)),
                  full(wa), full(wb), full(wc), full(wo)],
        out_specs=br(d),
        out_shape=jax.ShapeDtypeStruct((t, d), F32),
        compiler_params=_params(("parallel",)),
    )(ya_in, yb_in, yc_in, proj2, x2, mod3, wa, wb, wc, wo)


def _first_argmax_mask(cur, iota, axis, n):
    mx = jnp.max(cur, axis=axis, keepdims=True)
    ix = jnp.min(jnp.where(cur == mx, iota, n), axis=axis, keepdims=True)
    return iota == ix


def _router_kernel(x_ref, sc_ref, sh_ref, g_ref, rwt_ref, rb_ref, dest_ref, wgt_ref, starts_ref, plens_ref):
    ne, ng = N_EXPERTS, N_GROUPS
    eg = ne // ng
    h = _rms(x_ref[...], g_ref[...]) * (1.0 + sc_ref[0]) + sh_ref[0]
    tm = h.shape[0]
    logits = lax.dot_general(rwt_ref[...], h, (((1,), (1,)), ((), ())),
                             preferred_element_type=F32, precision=HIGHEST)
    scores = jax.nn.sigmoid(logits)
    sel = scores + rb_ref[...]
    s3 = sel.reshape(ng, eg, tm)
    io3 = lax.broadcasted_iota(jnp.int32, (ng, eg, tm), 1)
    m1 = jnp.max(s3, axis=1, keepdims=True)
    first = _first_argmax_mask(s3, io3, 1, eg)
    m2 = jnp.max(jnp.where(first, -jnp.inf, s3), axis=1, keepdims=True)
    gs = (m1 + m2).reshape(ng, tm)
    iog = lax.broadcasted_iota(jnp.int32, (ng, tm), 0)
    gkeep = jnp.zeros((ng, tm), F32)
    cur = gs
    for _ in range(TOPK_GROUPS):
        hit = _first_argmax_mask(cur, iog, 0, ng)
        gkeep = jnp.where(hit, 1.0, gkeep)
        cur = jnp.where(hit, -jnp.inf, cur)
    selm = jnp.where(gkeep.reshape(ng, 1, tm) > 0.0, s3, -jnp.inf).reshape(ne, tm)
    ioe = lax.broadcasted_iota(jnp.int32, (ne, tm), 0)
    hits = []
    chosen = jnp.zeros((ne, tm), F32)
    cur = selm
    for _ in range(TOP_K):
        hit = _first_argmax_mask(cur, ioe, 0, ne)
        hits.append(hit)
        chosen = jnp.where(hit, 1.0, chosen)
        cur = jnp.where(hit, -jnp.inf, cur)
    w = chosen * scores
    w = w / jnp.sum(w, axis=0, keepdims=True) * ROUTED_SCALE

    cnt = jnp.sum(chosen, axis=1, keepdims=True)
    plen = jnp.ceil(cnt * (1.0 / ROW_ALIGN)) * ROW_ALIGN
    start = _cumsum_rows(jnp.broadcast_to(plen, (ne, LANE)))[:, :1] - plen
    before = (lax.broadcasted_iota(jnp.int32, (tm, tm), 0)
              < lax.broadcasted_iota(jnp.int32, (tm, tm), 1)).astype(BF16)
    rank = jnp.dot(chosen.astype(BF16), before, preferred_element_type=F32)
    row_of = start + rank
    pad_rows = SLOT_ROWS - TOP_K
    dest = [jnp.sum(jnp.where(hit, row_of, 0.0), axis=0, keepdims=True) for hit in hits]
    wsel = [jnp.sum(jnp.where(hit, w, 0.0), axis=0, keepdims=True) for hit in hits]
    dest_ref[...] = jnp.concatenate(dest + [jnp.full((pad_rows, tm), -1.0, F32)], axis=0).astype(jnp.int32)
    wgt_ref[...] = jnp.concatenate(wsel + [jnp.zeros((pad_rows, tm), F32)], axis=0)
    starts_ref[0] = start.astype(jnp.int32)
    plens_ref[0] = plen.astype(jnp.int32)


def _router(x2, mod3, norm_g, rw_t, rb_col, seq):
    t, d = x2.shape
    tm = MOE_SUB
    nsb = t // tm
    return pl.pallas_call(
        _router_kernel, name="router",
        grid=(nsb,),
        in_specs=[pl.BlockSpec((tm, d), lambda i: (i, 0)),
                  pl.BlockSpec((1, 1, d), lambda i: ((i * tm) // seq, 0, 4)),
                  pl.BlockSpec((1, 1, d), lambda i: ((i * tm) // seq, 0, 3)),
                  pl.BlockSpec((1, d), lambda i: (0, 0)),
                  pl.BlockSpec(rw_t.shape, lambda i: (0, 0)),
                  pl.BlockSpec(rb_col.shape, lambda i: (0, 0))],
        out_specs=[pl.BlockSpec((SLOT_ROWS, tm), lambda i: (0, i)),
                   pl.BlockSpec((SLOT_ROWS, tm), lambda i: (0, i)),
                   pl.BlockSpec((1, N_EXPERTS, 1), lambda i: (i, 0, 0)),
                   pl.BlockSpec((1, N_EXPERTS, 1), lambda i: (i, 0, 0))],
        out_shape=[jax.ShapeDtypeStruct((SLOT_ROWS, t), jnp.int32),
                   jax.ShapeDtypeStruct((SLOT_ROWS, t), F32),
                   jax.ShapeDtypeStruct((nsb, N_EXPERTS, 1), jnp.int32),
                   jax.ShapeDtypeStruct((nsb, N_EXPERTS, 1), jnp.int32)],
        compiler_params=_params(("parallel",)),
    )(x2, mod3, mod3, norm_g, rw_t, rb_col)


def _moe_kernel(starts_ref, plens_ref, x_ref, sc_ref, sh_ref, g2_ref, gn_ref, dest_ref, wgt_ref,
                wg_ref, wu_ref, wd_ref, sg_ref, su_ref, sd_ref, nf_ref, o_ref,
                h_ref, xs_ref, stage_ref, *, final):
    blk, step = pl.program_id(0), pl.program_id(1)
    eps = wg_ref.shape[0]
    ne = pl.num_programs(1) * eps
    sub, rt, mt = MOE_SUB, MOE_ROW_TILE, MOE_FFN_TILE
    nsub = x_ref.shape[0] // sub
    rmax = xs_ref.shape[1]

    @pl.when(step == 0)
    def _():
        h = _mx(_rms(x_ref[...], gn_ref[...]) * (1.0 + sc_ref[0]) + sh_ref[0])
        h_ref[...] = h
        stage_ref[...] = jnp.zeros_like(stage_ref)
        for sb in range(nsub):
            dest = dest_ref[:, sb * sub:(sb + 1) * sub]
            hs = h_ref[sb * sub:(sb + 1) * sub, :]
            for r in range(rmax // rt):
                rows = r * rt + lax.broadcasted_iota(jnp.int32, (rt, sub), 0)
                pick = jnp.zeros((rt, sub), F32)
                for j in range(TOP_K):
                    pick = jnp.where(dest[j:j + 1, :] == rows, 1.0, pick)
                xs_ref[sb, r * rt:(r + 1) * rt, :] = _dot(pick, hs).astype(xs_ref.dtype)

    def run(sb, e):
        i = (blk * nsub + sb) * ne + e
        return starts_ref[i], plens_ref[i] // ROW_ALIGN

    def copy_rows(src, src0, dst, dst0, n_chunks):
        def body(k, carry):
            s = pl.multiple_of(src0 + k * ROW_ALIGN, ROW_ALIGN)
            d = pl.multiple_of(dst0 + k * ROW_ALIGN, ROW_ALIGN)
            dst[pl.ds(d, ROW_ALIGN), :] = src[pl.ds(s, ROW_ALIGN), :]
            return carry
        lax.fori_loop(0, n_chunks, body, 0)

    def expert(j, xt):
        return _dot(_silu(_dot(xt, wg_ref[j])) * _dot(xt, wu_ref[j]), wd_ref[j])

    win = MOE_WINDOW
    runs = [[run(sb, step * eps + j) for sb in range(nsub)] for j in range(eps)]
    fits = None
    for per_expert in runs:
        for _, nch in per_expert:
            ok = nch * ROW_ALIGN <= win
            fits = ok if fits is None else jnp.logical_and(fits, ok)

    @pl.when(fits)
    def _():
        def window(sb, st):
            return xs_ref[sb, pl.ds(pl.multiple_of(st, ROW_ALIGN), win), :]
        xin = [jnp.concatenate([window(sb, st) for sb, (st, _) in enumerate(runs[j])], axis=0)
               for j in range(eps)]
        out = [expert(j, xin[j]).astype(xs_ref.dtype) for j in range(eps)]
        row = lax.broadcasted_iota(jnp.int32, (win, 1), 0)
        for j in range(eps):
            for sb, (st, nch) in enumerate(runs[j]):
                rows = slice(sb * win, (sb + 1) * win)
                xs_ref[sb, pl.ds(pl.multiple_of(st, ROW_ALIGN), win), :] = jnp.where(
                    row < nch * ROW_ALIGN, out[j][rows], xin[j][rows])

    @pl.when(jnp.logical_not(fits))
    def _():
        for j in range(eps):
            cursor = 0
            placed = []
            for sb, (st, nch) in enumerate(runs[j]):
                copy_rows(xs_ref.at[sb], st, stage_ref, cursor, nch)
                placed.append((sb, st, nch, cursor))
                cursor = cursor + nch * ROW_ALIGN

            def tile(i, carry, j=j):
                r0 = pl.multiple_of(i * mt, mt)
                stage_ref[pl.ds(r0, mt), :] = expert(j, stage_ref[pl.ds(r0, mt), :]).astype(stage_ref.dtype)
                return carry
            lax.fori_loop(0, (cursor + mt - 1) // mt, tile, 0)
            for sb, st, nch, at in placed:
                copy_rows(stage_ref, at, xs_ref.at[sb], st, nch)

    @pl.when(step == pl.num_programs(1) - 1)
    def _():
        for sb in range(nsub):
            tok = slice(sb * sub, (sb + 1) * sub)
            hs = h_ref[tok, :]
            shared = _dot(_silu(_dot(hs, sg_ref[...])) * _dot(hs, su_ref[...]), sd_ref[...])
            dest_t = jnp.transpose(dest_ref[:, tok].astype(F32))
            wgt_t = jnp.transpose(wgt_ref[:, tok])
            routed = jnp.zeros((sub, x_ref.shape[1]), F32)
            for r in range(rmax // rt):
                cols = (r * rt + lax.broadcasted_iota(jnp.int32, (sub, rt), 1)).astype(F32)
                mix = jnp.zeros((sub, rt), F32)
                for j in range(TOP_K):
                    mix = jnp.where(dest_t[:, j:j + 1] == cols, wgt_t[:, j:j + 1], mix)
                routed = routed + _dot(mix, xs_ref[sb, r * rt:(r + 1) * rt, :])
            xo = x_ref[tok, :] + g2_ref[0] * (shared + routed)
            if final:
                xo = _rms(xo, nf_ref[...])
            o_ref[tok, :] = xo


def _round_up(n, m):
    return (n + m - 1) // m * m


def _moe(x2, mod3, norm_g, dest, wgt, starts, plens, wg, wu, wd, sg, su, sd, nf, seq, final):
    t, d = x2.shape
    tm = min(MOE_BLOCK, seq)
    nsub = tm // MOE_SUB
    ne, ff = wg.shape[0], wg.shape[2]
    rmax = _round_up(TOP_K * MOE_SUB + ne * (ROW_ALIGN - 1) + MOE_WINDOW, MOE_ROW_TILE)
    stage_rows = _round_up(tm + nsub * (ROW_ALIGN - 1), MOE_FFN_TILE)
    eps = MOE_EXPERTS_PER_STEP
    full = lambda a: pl.BlockSpec(a.shape, lambda i, e, *_: (0,) * a.ndim)
    once = lambda a: pl.BlockSpec(a.shape, lambda i, e, *_: (0,) * a.ndim, pipeline_mode=pl.Buffered(1))
    mod = lambda j: pl.BlockSpec((1, 1, d), lambda i, e, *_: ((i * tm) // seq, 0, j))
    slot = pl.BlockSpec((SLOT_ROWS, tm), lambda i, e, *_: (0, i))
    grid_spec = pltpu.PrefetchScalarGridSpec(
        num_scalar_prefetch=2,
        grid=(t // tm, ne // eps),
        in_specs=[pl.BlockSpec((tm, d), lambda i, e, *_: (i, 0), pipeline_mode=pl.Buffered(1)),
                  mod(4), mod(3), mod(5),
                  pl.BlockSpec((1, d), lambda i, e, *_: (0, 0)), slot, slot,
                  pl.BlockSpec((eps, d, ff), lambda i, e, *_: (e, 0, 0)),
                  pl.BlockSpec((eps, d, ff), lambda i, e, *_: (e, 0, 0)),
                  pl.BlockSpec((eps, ff, d), lambda i, e, *_: (e, 0, 0)),
                  once(sg), once(su), once(sd), full(nf)],
        out_specs=pl.BlockSpec((tm, d), lambda i, e, *_: (i, 0), pipeline_mode=pl.Buffered(1)),
        scratch_shapes=[pltpu.VMEM((tm, d), MXU_DTYPE),
                        pltpu.VMEM((nsub, rmax, d), MXU_DTYPE),
                        pltpu.VMEM((stage_rows, d), MXU_DTYPE)])
    return pl.pallas_call(
        functools.partial(_moe_kernel, final=final), name="moe",
        grid_spec=grid_spec,
        out_shape=jax.ShapeDtypeStruct((t, d), F32),
        compiler_params=_params(("parallel", "arbitrary"), vmem_mb=58),
    )(starts.reshape(-1), plens.reshape(-1), x2, mod3, mod3, mod3, norm_g, dest, wgt,
      wg, wu, wd, sg, su, sd, nf)


def _pack_w_in(w):
    d = w.shape[0]
    offs = [0]
    for n in IN_SIZES:
        offs.append(offs[-1] + n)
    (qa, ka, va, ga, alr, cq, ckv, kidx, widx, qc, kc, vc, ic, fc, oc, gates) = [
        w[:, offs[i]:offs[i + 1]] for i in range(len(IN_SIZES))]
    z = lambda n: jnp.zeros((d, n), w.dtype)
    packed = jnp.concatenate(
        [gates, qa, ka, va, ga, vc, oc, qc, kc, cq, ckv,
         kidx, widx, z(LANE - IDX_DIM - IDX_HEADS),
         alr, z(LANE - GLA_GATE_RANK),
         ic, fc, z(LANE - 2 * MLSTM_HEADS)], axis=1)
    assert packed.shape[1] == N_PACK
    return packed.astype(MXU_DTYPE)


def kernel(x, c, ada_w, ada_b, norm_mix, norm_ffn, w_in, gla_w_a2, gla_b_a, gla_norm, dsa_norm_q,
           dsa_norm_kv, dsa_w_uq, dsa_w_uk, dsa_w_uv, dsa_w_qi, mlstm_conv, mlstm_b_i, mlstm_b_f,
           mlstm_norm, w_up_a, w_up_b, w_up_c, w_o, router_w, router_bias, exp_w_gate, exp_w_up,
           exp_w_down, sh_w_gate, sh_w_up, sh_w_down, norm_final):
    b, s, d = x.shape
    depth = ada_w.shape[0]
    t = b * s
    mod = _ada_mod(c, ada_w, ada_b)
    x2 = x.reshape(t, d)
    row = lambda v: v.reshape(1, -1)
    for l in range(depth):
        mod3 = mod[l].reshape(b, 1, 6 * d)
        w_pack = _pack_w_in(w_in[l])
        proj2, tail2 = _in_proj(x2, mod3, row(norm_mix[l]), w_pack, s)
        proj3 = proj2.reshape(b, s, N_MAIN)
        tail3 = tail2.reshape(b, s, N_TAIL)

        wa2_pad = jnp.zeros((LANE, GLA_QK), F32).at[:GLA_GATE_RANK].set(gla_w_a2[l])
        ya_in = _gla(proj3, tail3, wa2_pad, row(gla_b_a[l]), row(gla_norm[l]))

        bias_row = jnp.zeros((1, LANE), F32).at[0, :MLSTM_HEADS].set(mlstm_b_i[l])
        bias_row = bias_row.at[0, MLSTM_HEADS:2 * MLSTM_HEADS].set(mlstm_b_f[l])
        yc_in = _mlstm(proj3, tail3, mlstm_conv[l], bias_row, row(mlstm_norm[l]))

        wuq = dsa_w_uq[l].reshape(DSA_Q_RANK, DSA_HEADS * DSA_HEAD_DIM).astype(MXU_DTYPE)
        wuk_t = jnp.transpose(dsa_w_uk[l], (1, 2, 0)).astype(MXU_DTYPE)
        wuv_t = jnp.transpose(dsa_w_uv[l], (1, 2, 0)).astype(MXU_DTYPE)
        wqi_pad = jnp.zeros((DSA_Q_RANK, IDX_HEADS, LANE), F32).at[:, :, :IDX_DIM].set(dsa_w_qi[l])
        wqi_pad = wqi_pad.reshape(DSA_Q_RANK, IDX_HEADS * LANE).astype(MXU_DTYPE)
        ckvn, kwb, qlat, qidx, wht = _dsa_prep(proj2, tail2, s, row(dsa_norm_q[l]), row(dsa_norm_kv[l]),
                                               wuq, wuk_t, wqi_pad)
        yb_in = _dsa(ckvn.reshape(b, s, DSA_AUG), kwb.reshape(b, s, LANE), qlat, qidx, wht, wuv_t)

        x2 = _merge(ya_in.reshape(t, GLA_V), yb_in.reshape(t, DSA_V), yc_in.reshape(t, MLSTM_V),
                    proj2, x2, mod3, w_up_a[l].astype(MXU_DTYPE), w_up_b[l].astype(MXU_DTYPE),
                    w_up_c[l].astype(MXU_DTYPE), w_o[l].astype(MXU_DTYPE), s)

        dest, wgt, starts, plens = _router(x2, mod3, row(norm_ffn[l]), jnp.transpose(router_w[l]),
                                           router_bias[l].reshape(-1, 1), s)
        x2 = _moe(x2, mod3, row(norm_ffn[l]), dest, wgt, starts, plens,
                  exp_w_gate[l].astype(MXU_DTYPE), exp_w_up[l].astype(MXU_DTYPE),
                  exp_w_down[l].astype(MXU_DTYPE), sh_w_gate[l].astype(MXU_DTYPE),
                  sh_w_up[l].astype(MXU_DTYPE), sh_w_down[l].astype(MXU_DTYPE), row(norm_final), s,
                  final=(l == depth - 1))
    return x2.reshape(b, s, d)
```

```python
import functools
import struct

import jax
import jax.numpy as jnp
from jax import lax
from jax.experimental import pallas as pl
from jax.experimental.pallas import tpu as pltpu

F32 = jnp.float32
BF16 = jnp.bfloat16
MXU_DTYPE = jnp.bfloat16
ACT_DTYPE = jnp.bfloat16
HIGHEST = lax.Precision.HIGHEST

EPS = 1e-6
D_MODEL = 1024
GLA_HEADS, GLA_DK, GLA_DV, GLA_GATE_RANK, GLA_TAU, GLA_CHUNK = 4, 64, 128, 16, 16.0, 64
GLA_SUB = 16
DSA_HEADS, DSA_Q_RANK, DSA_KV_RANK, DSA_HEAD_DIM, DSA_V_DIM = 8, 256, 128, 64, 64
IDX_HEADS, IDX_DIM, DSA_TOPK_MAX = 8, 32, 256
MLSTM_HEADS, MLSTM_DQK, MLSTM_DV, MLSTM_CONV, MLSTM_CHUNK = 4, 64, 128, 4, 64
N_EXPERTS, TOP_K, N_GROUPS, TOPK_GROUPS, EXPERT_FF, SHARED_FF = 64, 6, 8, 4, 256, 256
ROUTED_SCALE = 2.5

GLA_QK = GLA_HEADS * GLA_DK
GLA_V = GLA_HEADS * GLA_DV
DSA_V = DSA_HEADS * DSA_V_DIM
MLSTM_QK = MLSTM_HEADS * MLSTM_DQK
MLSTM_V = MLSTM_HEADS * MLSTM_DV
IN_SIZES = (GLA_QK, GLA_QK, GLA_V, GLA_V, GLA_GATE_RANK,
            DSA_Q_RANK, DSA_KV_RANK, IDX_DIM, IDX_HEADS,
            MLSTM_QK, MLSTM_QK, MLSTM_V, MLSTM_HEADS, MLSTM_HEADS, MLSTM_V,
            3 * D_MODEL)

LANE = 128
KEY_TILE = 256
Q_TILE = 256
ATT_HEAD_GROUP = 4
NEG_BIG = -1e30
MOE_BLOCK = 1024
MOE_SUB = 256
MOE_ROW_TILE = 256
MOE_WINDOW = 48
MOE_WINDOW_MAIN = 32
MOE_FFN_TILE = 128
MOE_EXPERTS_PER_STEP = 4
CONV_PAD = 8
SEQ_GROUP = 16
SLOT_ROWS = 8
ROW_ALIGN = 16

C_GATES = 0
C_QA = 3072
C_KA = 3328
C_VA = 3584
C_GA = 4096
C_VC = 4608
C_OC = 5120
C_QKC = 5632
C_CQ = 6144
N_MAIN = 6400
T_CKV = 0
T_KW = 128
T_ALR = 256
T_ICFC = 384
N_TAIL = 512
N_PACK = N_MAIN + N_TAIL
INPROJ_PIECES = 5
W_IDX_LANE = IDX_DIM


LOG2E = 1.4426950408889634
DSA_AUG = DSA_KV_RANK + LANE
POS_SPLIT = 64
SLOPE_PIECES = 3
SUBLANE = 8
PARTIAL_ROWS = 32
COUNT_UNSET = float(2 ** 30)


def _bf16_pieces(x, n):
    out = []
    for _ in range(n):
        bits = struct.unpack("<I", struct.pack("<f", x))[0]
        bits = (bits + 0x7FFF + ((bits >> 16) & 1)) & 0xFFFF0000
        piece = struct.unpack("<f", struct.pack("<I", bits))[0]
        out.append(piece)
        x -= piece
    return out


def _mx(x):
    return x.astype(MXU_DTYPE)


def _dot(a, b):
    return jnp.dot(_mx(a), _mx(b), preferred_element_type=F32)


def _dot_nt(a, b):
    return lax.dot_general(_mx(a), _mx(b), (((1,), (1,)), ((), ())), preferred_element_type=F32)


def _dot_tn(a, b):
    return lax.dot_general(_mx(a), _mx(b), (((0,), (0,)), ((), ())), preferred_element_type=F32)


def _rms(x, g):
    return x * lax.rsqrt(jnp.mean(x * x, axis=-1, keepdims=True) + EPS) * g


def _silu(x):
    return x * jax.nn.sigmoid(x)


def _log_sigmoid(z):
    return jnp.minimum(z, 0.0) - jnp.log1p(jnp.exp(-jnp.abs(z)))


def _cumsum_rows(x):
    n = x.shape[0]
    tri = (lax.broadcasted_iota(jnp.int32, (n, n), 1) <= lax.broadcasted_iota(jnp.int32, (n, n), 0)).astype(F32)
    return jnp.dot(tri, x, preferred_element_type=F32, precision=HIGHEST)


def _truncate_to_bf16(x):
    bits = lax.bitcast_convert_type(x, jnp.int32) & jnp.int32(-65536)
    return lax.bitcast_convert_type(bits, F32).astype(BF16)


def _colreduce(x, op, width=PARTIAL_ROWS):
    n, c = x.shape
    return op(op(x.reshape(n // width, width, c), axis=0), axis=0, keepdims=True)


def _params(sem, vmem_mb=40):
    return pltpu.CompilerParams(dimension_semantics=sem, vmem_limit_bytes=vmem_mb * 1024 * 1024)


def _ada_kernel(c_ref, w_ref, b_ref, o_ref):
    cs = _silu(c_ref[...])
    o_ref[0] = jnp.dot(cs, w_ref[0], preferred_element_type=F32, precision=HIGHEST) + b_ref[0]


def _ada_mod(c, ada_w, ada_b):
    depth, d, n = ada_w.shape
    b = c.shape[0]
    return pl.pallas_call(
        _ada_kernel, name="ada_mod",
        grid=(depth, n // d),
        in_specs=[pl.BlockSpec((b, d), lambda l, j: (0, 0)),
                  pl.BlockSpec((1, d, d), lambda l, j: (l, 0, j)),
                  pl.BlockSpec((1, 1, d), lambda l, j: (l, 0, j))],
        out_specs=pl.BlockSpec((1, b, d), lambda l, j: (l, 0, j)),
        out_shape=jax.ShapeDtypeStruct((depth, b, n), F32),
        compiler_params=_params(("parallel", "parallel")),
    )(c, ada_w, ada_b.reshape(depth, 1, n))


def _inproj_kernel(x_ref, sc_ref, sh_ref, g_ref, wm_ref, wt_ref, om_ref, ot_ref):
    h = _mx(_rms(x_ref[...], g_ref[...]) * (1.0 + sc_ref[0]) + sh_ref[0])
    piece = N_MAIN // INPROJ_PIECES
    for j in range(INPROJ_PIECES):
        cols = slice(j * piece, (j + 1) * piece)
        om_ref[:, cols] = jnp.dot(h, wm_ref[:, cols], preferred_element_type=F32).astype(om_ref.dtype)
    ot_ref[...] = jnp.dot(h, wt_ref[...], preferred_element_type=F32)


def _in_proj(x2, mod3, norm_g, w_pack, seq):
    t, d = x2.shape
    tm = min(512, seq)
    once = lambda w: pl.BlockSpec((d, w), lambda i: (0, 0), pipeline_mode=pl.Buffered(1))
    return pl.pallas_call(
        _inproj_kernel, name="in_proj",
        grid=(t // tm,),
        in_specs=[pl.BlockSpec((tm, d), lambda i: (i, 0)),
                  pl.BlockSpec((1, 1, d), lambda i: ((i * tm) // seq, 0, 1)),
                  pl.BlockSpec((1, 1, d), lambda i: ((i * tm) // seq, 0, 0)),
                  pl.BlockSpec((1, d), lambda i: (0, 0)),
                  once(N_MAIN), once(N_TAIL)],
        out_specs=[pl.BlockSpec((tm, N_MAIN), lambda i: (i, 0)),
                   pl.BlockSpec((tm, N_TAIL), lambda i: (i, 0))],
        out_shape=[jax.ShapeDtypeStruct((t, N_MAIN), ACT_DTYPE),
                   jax.ShapeDtypeStruct((t, N_TAIL), F32)],
        compiler_params=_params(("parallel",), vmem_mb=48),
    )(x2, mod3, mod3, norm_g, w_pack[:, :N_MAIN], w_pack[:, N_MAIN:])


def _gla_kernel(q_ref, k_ref, v_ref, g_ref, alr_ref, wa2_ref, ba_ref, gn_ref, o_ref, s_ref, acc_ref):
    @pl.when(pl.program_id(1) == 0)
    def _():
        s_ref[...] = jnp.zeros_like(s_ref)

    L, sub, nh, dk, dv = GLA_CHUNK, GLA_SUB, GLA_HEADS, GLA_DK, GLA_DV
    seqs = range(q_ref.shape[0])
    heads = range(nh)
    hk = lambda h: slice(h * dk, (h + 1) * dk)
    hv = lambda h: slice(h * dv, (h + 1) * dv)

    pre = []
    for g in seqs:
        z = jnp.dot(alr_ref[g], wa2_ref[...], preferred_element_type=F32, precision=HIGHEST) + ba_ref[...]
        cum = _cumsum_rows(_log_sigmoid(z) * (1.0 / GLA_TAU))
        q = q_ref[g].astype(F32) * (dk ** -0.5)
        k = k_ref[g].astype(F32)
        tot = cum[L - 1:L, :]
        pre.append(dict(cum=cum, q=q, k=k, tot=tot, vb=_mx(v_ref[g]),
                        q_in=_mx(q * jnp.exp(cum)), k_dec=_mx(k * jnp.exp(tot - cum))))

    scores = {}
    for i in range(L // sub):
        r0, r1 = i * sub, (i + 1) * sub
        for g in seqs:
            p = pre[g]
            base = p["cum"][r0 - 1:r0, :] if i > 0 else jnp.zeros_like(p["tot"])
            qi = _mx(p["q"][r0:r1] * jnp.exp(p["cum"][r0:r1] - base))
            ka = _mx(p["k"][:r1] * jnp.exp(base - p["cum"][:r1]))
            for h in heads:
                scores[g, i, h] = _dot_nt(qi[:, hk(h)], ka[:, hk(h)])

    for i in range(L // sub):
        r0, r1 = i * sub, (i + 1) * sub
        causal = (lax.broadcasted_iota(jnp.int32, (sub, r1), 1)
                  <= lax.broadcasted_iota(jnp.int32, (sub, r1), 0) + r0)
        for g in seqs:
            for h in heads:
                s = jnp.where(causal, scores[g, i, h], 0.0)
                acc_ref[g, r0:r1, hv(h)] = _dot(s, pre[g]["vb"][:r1, hv(h)])

    inter = {(g, h): _dot(pre[g]["q_in"][:, hk(h)], s_ref[g * nh + h]) for g in seqs for h in heads}
    update = {(g, h): _dot_tn(pre[g]["k_dec"][:, hk(h)], pre[g]["vb"][:, hv(h)]) for g in seqs for h in heads}

    gn = gn_ref[...]
    for g in seqs:
        gate = g_ref[g].astype(F32)
        for h in heads:
            o = acc_ref[g, :, hv(h)] + inter[g, h]
            y = _rms(o, gn[:, hv(h)]) * _silu(gate[:, hv(h)])
            o_ref[g, :, hv(h)] = y.astype(o_ref.dtype)
            decay = jnp.transpose(jnp.exp(pre[g]["tot"][:, hk(h)]))
            s_ref[g * nh + h] = s_ref[g * nh + h] * decay + update[g, h]


def _seq_group(batch):
    for grp in (SEQ_GROUP, 2, 1):
        if batch % grp == 0:
            return grp


def _gla(proj3, tail3, wa2_pad, ba, gn):
    b, s, _ = proj3.shape
    L = GLA_CHUNK
    grp = _seq_group(b)
    blk = lambda w, c0: pl.BlockSpec((grp, L, w), lambda bi, ci: (bi, ci, c0 // w))
    full = lambda a: pl.BlockSpec(a.shape, lambda bi, ci: (0,) * a.ndim)
    return pl.pallas_call(
        _gla_kernel, name="gla",
        grid=(b // grp, s // L),
        in_specs=[blk(GLA_QK, C_QA), blk(GLA_QK, C_KA), blk(GLA_V, C_VA), blk(GLA_V, C_GA),
                  blk(LANE, T_ALR), full(wa2_pad), full(ba), full(gn)],
        out_specs=pl.BlockSpec((grp, L, GLA_V), lambda bi, ci: (bi, ci, 0)),
        out_shape=jax.ShapeDtypeStruct((b, s, GLA_V), ACT_DTYPE),
        scratch_shapes=[pltpu.VMEM((grp * GLA_HEADS, GLA_DK, GLA_DV), F32),
                        pltpu.VMEM((grp, L, GLA_V), F32)],
        compiler_params=_params(("parallel", "arbitrary")),
    )(proj3, proj3, proj3, proj3, tail3, wa2_pad, ba, gn)


def _mlstm_pair_kernel(qk_ref, v_ref, oc_ref, if_ref, conv_ref, bias_ref, gnt_ref, o_ref,
                       xbuf_ref, ct_ref, n_ref, m_ref):
    L, nh, dk, dv, kc = MLSTM_CHUNK, MLSTM_HEADS, MLSTM_DQK, MLSTM_DV, MLSTM_CONV
    pad = CONV_PAD
    npair = nh // 2
    assert 2 * dk == LANE and dv == LANE and L == dk

    @pl.when(pl.program_id(1) == 0)
    def _():
        xbuf_ref[:, 0:pad, :] = jnp.zeros((xbuf_ref.shape[0], pad, 2 * MLSTM_QK), F32)
        ct_ref[...] = jnp.zeros_like(ct_ref)
        n_ref[...] = jnp.zeros_like(n_ref)
        m_ref[...] = jnp.zeros_like(m_ref)

    lane = lax.broadcasted_iota(jnp.int32, (1, LANE), 1)
    half = [(lane < dk).astype(F32), (lane >= dk).astype(F32)]
    s_idx = lax.broadcasted_iota(jnp.int32, (L, LANE), 0)
    t_idx = lax.broadcasted_iota(jnp.int32, (L, LANE), 1) % L
    causal = s_idx <= t_idx
    lane_in = lax.broadcasted_iota(jnp.int32, (LANE, LANE), 0)
    head_of = lax.broadcasted_iota(jnp.int32, (LANE, LANE), 1) // dk
    cw = conv_ref[...]

    combos = [(g, p) for g in range(qk_ref.shape[0]) for p in range(npair)]
    seq = {}
    for g in range(qk_ref.shape[0]):
        xbuf_ref[g, pad:pad + L, :] = qk_ref[g].astype(F32)
        conv = jnp.zeros((L, 2 * MLSTM_QK), F32)
        for j in range(kc):
            conv = conv + cw[j:j + 1, :] * xbuf_ref[g, pl.ds(pad - (kc - 1) + j, L), :]
        xbuf_ref[g, 0:pad, :] = xbuf_ref[g, L:L + pad, :]
        qk = _silu(conv)
        pre = if_ref[g] + bias_ref[...]
        bcum = _cumsum_rows(_log_sigmoid(pre))
        v = v_ref[g].astype(F32)
        seq[g] = dict(q=qk[:, :MLSTM_QK] * (dk ** -0.5), k=qk[:, MLSTM_QK:],
                      vt=[_mx(jnp.transpose(v[:, h * dv:(h + 1) * dv])) for h in range(nh)],
                      gate_mix=jnp.where(lane < nh, pre, -bcum), bcum=bcum, bcum_t=jnp.transpose(bcum))

    st = {}
    for g, p in combos:
        sq = seq[g]
        sel = jnp.where((lane_in == 2 * p + head_of) | (lane_in == nh + 2 * p + head_of), 1.0, 0.0)
        selb = jnp.where(lane_in == nh + 2 * p + head_of, 1.0, 0.0)
        d_mat = jnp.dot(sq["gate_mix"], sel, preferred_element_type=F32, precision=HIGHEST)
        tot = jnp.dot(sq["bcum"][L - SUBLANE:L, :], selb, preferred_element_type=F32,
                      precision=HIGHEST)[SUBLANE - 1:SUBLANE, :]
        b_row = jnp.concatenate([sq["bcum_t"][nh + 2 * p:nh + 2 * p + 1, :],
                                 sq["bcum_t"][nh + 2 * p + 1:nh + 2 * p + 2, :]], axis=1)
        qt = sq["q"][:, p * LANE:(p + 1) * LANE]
        kt = sq["k"][:, p * LANE:(p + 1) * LANE]
        q_bd = _mx(jnp.concatenate([qt * half[0], qt * half[1]], axis=0))
        st[g, p] = dict(d=d_mat, tot=tot, b_row=b_row, kt=kt, q_bd=q_bd,
                        scores=_dot_nt(kt, q_bd),
                        qn=_dot_nt(jnp.broadcast_to(n_ref[g * npair + p], (SUBLANE, LANE)), q_bd)[0:1, :],
                        inter=[_dot_nt(ct_ref[g * nh + 2 * p + hh], q_bd) for hh in range(2)])

    for g, p in combos:
        c = st[g, p]
        m_prev = m_ref[g * npair + p]
        dlog = jnp.where(causal, c["b_row"] + c["d"], -jnp.inf)
        inter_log = c["b_row"] + m_prev
        m_t = jnp.maximum(inter_log, jnp.max(dlog, axis=0, keepdims=True))
        g_log = c["tot"] + c["d"]
        m_new = jnp.maximum(c["tot"] + m_prev, jnp.max(g_log, axis=0, keepdims=True))
        c.update(m_t=m_t, w_inter=jnp.exp(inter_log - m_t), sw=c["scores"] * jnp.exp(dlog - m_t),
                 m_new=m_new, w_c=jnp.exp(c["tot"] + m_prev - m_new), ks=c["kt"] * jnp.exp(g_log - m_new))

    for g, p in combos:
        c = st[g, p]
        vt = seq[g]["vt"]
        c["sv"] = [_dot(vt[2 * p + hh], c["sw"] * half[hh]) for hh in range(2)]
        c["kv"] = [_dot(vt[2 * p + hh], c["ks"] * half[hh]) for hh in range(2)]

    for g, p in combos:
        c = st[g, p]
        num = c["sv"][0] + c["sv"][1] + c["w_inter"] * (c["inter"][0] + c["inter"][1])
        den = jnp.sum(c["sw"], axis=0, keepdims=True) + c["w_inter"] * c["qn"]
        hout = num / jnp.maximum(jnp.abs(den), jnp.exp(-c["m_t"]))
        y_t = hout * lax.rsqrt(jnp.mean(hout * hout, axis=0, keepdims=True) + EPS) * gnt_ref[p]
        y = jnp.transpose(y_t)
        for hh in range(2):
            cols = slice((2 * p + hh) * dv, (2 * p + hh + 1) * dv)
            o_ref[g, :, cols] = (y[hh * L:(hh + 1) * L, :] * jax.nn.sigmoid(oc_ref[g, :, cols].astype(F32))).astype(o_ref.dtype)
            ct_ref[g * nh + 2 * p + hh] = ct_ref[g * nh + 2 * p + hh] * c["w_c"] + c["kv"][hh]
        n_ref[g * npair + p] = n_ref[g * npair + p] * c["w_c"] + jnp.sum(c["ks"], axis=0, keepdims=True)
        m_ref[g * npair + p] = c["m_new"]


def _mlstm(proj3, tail3, conv_w, bias_row, gn):
    b, s, _ = proj3.shape
    L = MLSTM_CHUNK
    grp = _seq_group(b)
    blk = lambda w, c0: pl.BlockSpec((grp, L, w), lambda bi, ci: (bi, ci, c0 // w))
    full = lambda a: pl.BlockSpec(a.shape, lambda bi, ci: (0,) * a.ndim)
    npair = MLSTM_HEADS // 2
    gnt = jnp.repeat(jnp.transpose(gn.reshape(npair, 2, MLSTM_DV), (0, 2, 1)), L, axis=2)
    return pl.pallas_call(
        _mlstm_pair_kernel, name="mlstm",
        grid=(b // grp, s // L),
        in_specs=[blk(2 * MLSTM_QK, C_QKC), blk(MLSTM_V, C_VC), blk(MLSTM_V, C_OC), blk(LANE, T_ICFC),
                  full(conv_w), full(bias_row), full(gnt)],
        out_specs=pl.BlockSpec((grp, L, MLSTM_V), lambda bi, ci: (bi, ci, 0)),
        out_shape=jax.ShapeDtypeStruct((b, s, MLSTM_V), ACT_DTYPE),
        scratch_shapes=[pltpu.VMEM((grp, L + CONV_PAD, 2 * MLSTM_QK), F32),
                        pltpu.VMEM((grp * MLSTM_HEADS, MLSTM_DV, LANE), F32),
                        pltpu.VMEM((grp * npair, 1, LANE), F32),
                        pltpu.VMEM((grp * npair, 1, LANE), F32)],
        compiler_params=_params(("parallel", "arbitrary")),
    )(proj3, proj3, proj3, tail3, conv_w, bias_row, gnt)


def _dsa_prep_kernel(cq_ref, ckv_ref, kw_ref, nq_ref, nkv_ref, wuq_ref, wuk_ref, wqi_ref,
                     ckvn_ref, kwb_ref, qlat_ref, qidx_ref, wht_ref, *, tiles_per_seq):
    tm = cq_ref.shape[0]
    r = DSA_KV_RANK
    cqn = _mx(_rms(cq_ref[...].astype(F32), nq_ref[...]))
    pos = (pl.program_id(0) % tiles_per_seq) * tm + lax.broadcasted_iota(jnp.int32, (tm, LANE), 0)
    lane = lax.broadcasted_iota(jnp.int32, (tm, LANE), 1)
    pos_cols = jnp.where(lane < SLOPE_PIECES, pos >> (POS_SPLIT.bit_length() - 1),
                         jnp.where(lane < 2 * SLOPE_PIECES, pos & (POS_SPLIT - 1), 0)).astype(F32)
    ckvn_ref[:, :r] = _rms(ckv_ref[...], nkv_ref[...]).astype(ckvn_ref.dtype)
    ckvn_ref[:, r:] = pos_cols.astype(ckvn_ref.dtype)
    kw = kw_ref[...]
    kwb_ref[...] = kw.astype(kwb_ref.dtype)
    wht_ref[0] = jnp.transpose(kw)[W_IDX_LANE:W_IDX_LANE + IDX_HEADS, :] * (IDX_HEADS ** -0.5)
    q = jnp.dot(cqn, wuq_ref[...], preferred_element_type=F32)
    for h in range(DSA_HEADS):
        ql = _dot(q[:, h * DSA_HEAD_DIM:(h + 1) * DSA_HEAD_DIM], wuk_ref[h]) * (DSA_HEAD_DIM ** -0.5 * LOG2E)
        qlat_ref[0, h, :, :r] = ql.astype(qlat_ref.dtype)
        c = _bf16_pieces(2.0 ** (-8.0 * (h + 1) / DSA_HEADS) * LOG2E, SLOPE_PIECES)
        consts = [POS_SPLIT * v for v in c] + c
        slope_cols = jnp.zeros((tm, LANE), F32)
        for j, v in enumerate(consts):
            slope_cols = jnp.where(lane == j, v, slope_cols)
        qlat_ref[0, h, :, r:] = slope_cols.astype(qlat_ref.dtype)
    qi = jnp.dot(cqn, wqi_ref[...], preferred_element_type=F32) * (IDX_DIM ** -0.5)
    for h in range(IDX_HEADS):
        qidx_ref[0, h] = qi[:, h * LANE:(h + 1) * LANE].astype(qidx_ref.dtype)


def _dsa_prep(proj2, tail2, seq, nq, nkv, wuq, wuk_t, wqi_pad):
    t = proj2.shape[0]
    b = t // seq
    tm = min(512, seq)
    per = seq // tm
    blk = lambda w, c0: pl.BlockSpec((tm, w), lambda i: (i, c0 // w))
    full = lambda a: pl.BlockSpec(a.shape, lambda i: (0,) * a.ndim)
    hmap = lambda i: (i // per, 0, i % per, 0)
    return pl.pallas_call(
        functools.partial(_dsa_prep_kernel, tiles_per_seq=per), name="dsa_prep",
        grid=(t // tm,),
        in_specs=[blk(DSA_Q_RANK, C_CQ), blk(DSA_KV_RANK, T_CKV), blk(LANE, T_KW),
                  full(nq), full(nkv), full(wuq), full(wuk_t), full(wqi_pad)],
        out_specs=[pl.BlockSpec((tm, DSA_AUG), lambda i: (i, 0)),
                   pl.BlockSpec((tm, LANE), lambda i: (i, 0)),
                   pl.BlockSpec((1, DSA_HEADS, tm, DSA_AUG), hmap),
                   pl.BlockSpec((1, IDX_HEADS, tm, LANE), hmap),
                   pl.BlockSpec((1, IDX_HEADS, tm), lambda i: (i // per, 0, i % per))],
        out_shape=[jax.ShapeDtypeStruct((t, DSA_AUG), ACT_DTYPE),
                   jax.ShapeDtypeStruct((t, LANE), ACT_DTYPE),
                   jax.ShapeDtypeStruct((b, DSA_HEADS, seq, DSA_AUG), ACT_DTYPE),
                   jax.ShapeDtypeStruct((b, IDX_HEADS, seq, LANE), ACT_DTYPE),
                   jax.ShapeDtypeStruct((b, IDX_HEADS, seq), F32)],
        compiler_params=_params(("parallel",)),
    )(proj2, tail2, tail2, nq, nkv, wuq, wuk_t, wqi_pad)


def _dsa_kernel(qlat_ref, qidx_ref, wht_ref, kwk_ref, ckv_ref, wuv_ref, o_ref,
                ibuf_ref, ihi_ref, acc_ref, *, topk):
    nh, tq, tk = DSA_HEADS, Q_TILE, KEY_TILE
    qb = pl.program_id(1)
    n_kt = (qb * tq + tq + tk - 1) // tk
    t_row = qb * tq + lax.broadcasted_iota(jnp.int32, (1, tq), 1)
    s_col = lax.broadcasted_iota(jnp.int32, (tk, 1), 0)

    qi = qidx_ref[0].reshape(IDX_HEADS * tq, LANE)
    wht = wht_ref[0]

    def idx_body(kt, carry):
        kk = kwk_ref[0, pl.ds(pl.multiple_of(kt * tk, tk), tk), :]
        grp = ATT_HEAD_GROUP
        sc = [_dot_nt(kk, qi[g * grp * tq:(g + 1) * grp * tq]) for g in range(IDX_HEADS // grp)]
        tot = None
        for h in range(IDX_HEADS):
            part = wht[h:h + 1, :] * jnp.maximum(sc[h // grp][:, (h % grp) * tq:(h % grp + 1) * tq], 0.0)
            tot = part if tot is None else tot + part
        score = jnp.where(kt * tk + s_col <= t_row, tot, -jnp.inf)
        ibuf_ref[kt] = score
        ihi_ref[kt] = _truncate_to_bf16(score)
        return carry

    lax.fori_loop(0, n_kt, idx_body, 0)

    def count(pred):
        def body(kt, c):
            hit = jnp.where(pred(ibuf_ref[kt]), 1.0, 0.0)
            return c + jnp.sum(hit.reshape(tk // PARTIAL_ROWS, PARTIAL_ROWS, tq), axis=0)
        return jnp.sum(lax.fori_loop(0, n_kt, body, jnp.zeros((PARTIAL_ROWS, tq), F32)), axis=0, keepdims=True)

    def count_ge(cand):
        return count(lambda x: x >= cand)

    def count_gt(cand):
        return count(lambda x: x > cand)

    def key_to_float(u):
        key = u ^ jnp.int32(-2 ** 31)
        bits = jnp.where(key >= 0, key, key ^ jnp.int32(0x7FFFFFFF))
        return lax.bitcast_convert_type(bits, F32)

    few = t_row < topk
    n_bits = 32

    def try_bit(i, u, cnt_u, counter):
        cand_u = u | lax.shift_left(jnp.int32(1), n_bits - 1 - i)
        cnt = counter(cand_u)
        ok = cnt >= topk
        return jnp.where(ok, cand_u, u), jnp.where(ok, cnt, cnt_u)

    def count_ge_hi(cand_u):
        cand = _truncate_to_bf16(key_to_float(cand_u))
        def body(kt, c):
            hit = jnp.where(ihi_ref[kt] >= cand, jnp.ones((), BF16), jnp.zeros((), BF16))
            part = jnp.sum(hit.reshape(tk // PARTIAL_ROWS, PARTIAL_ROWS, tq), axis=0)
            return c + part.astype(F32)
        return jnp.sum(lax.fori_loop(0, n_kt, body, jnp.zeros((PARTIAL_ROWS, tq), F32)), axis=0, keepdims=True)

    def coarse_body(i, st):
        return try_bit(i, *st, count_ge_hi)

    def search_cond(st):
        i, _, _, pending = st
        return (i < n_bits) & (pending > 0)

    def search_body(st):
        i, u, cnt_u, _ = st
        for _ in range(4):
            u, cnt_u = try_bit(i, u, cnt_u, lambda c: count_ge(key_to_float(c)))
            i = i + 1
        pending = jnp.max(jnp.where(few | (cnt_u == topk), 0, 1))
        return i, u, cnt_u, pending

    u, cnt_u = lax.fori_loop(0, n_bits // 2, coarse_body,
                             (jnp.zeros((1, tq), jnp.int32), jnp.full((1, tq), COUNT_UNSET, F32)))
    start = (jnp.int32(n_bits // 2), u, cnt_u, jnp.max(jnp.where(few | (cnt_u == topk), 0, 1)))
    _, u, _, _ = lax.while_loop(search_cond, search_body, start)
    tau = jnp.where(few, -jnp.inf, key_to_float(u))
    need = topk - count_gt(tau)

    acc_ref[...] = jnp.zeros_like(acc_ref)
    ql = qlat_ref[0].reshape(nh * tq, DSA_AUG)
    tri = jnp.where(lax.broadcasted_iota(jnp.int32, (tk, tk), 0) >= lax.broadcasted_iota(jnp.int32, (tk, tk), 1),
                    1.0, 0.0).astype(BF16)

    def keys(kt):
        return ckv_ref[0, pl.ds(pl.multiple_of(kt * tk, tk), tk), :]

    def att_body(kt, carry):
        m, l, eq_seen = carry
        kv_aug = keys(kt)
        kv = kv_aug[:, :DSA_KV_RANK]
        it = ibuf_ref[kt]
        eq = it == tau
        eqf = jnp.where(eq, 1.0, 0.0)
        rank = jnp.dot(tri, eqf.astype(BF16), preferred_element_type=F32) + eq_seen
        valid = ((it > tau) | (eq & (rank <= need))) & (kt * tk + s_col <= t_row)
        bias = jnp.where(valid, 0.0, NEG_BIG)

        n_grp = nh // ATT_HEAD_GROUP
        gcols = lambda g: slice(g * ATT_HEAD_GROUP * tq, (g + 1) * ATT_HEAD_GROUP * tq)
        lg, soft, ms, ls = {}, {}, {}, {}

        def logits(g):
            lg[g] = _dot_nt(kv_aug, ql[gcols(g)])

        def softmax(g):
            als, ps = [], []
            for h in range(g * ATT_HEAD_GROUP, (g + 1) * ATT_HEAD_GROUP):
                cols = slice(h * tq, (h + 1) * tq)
                loc = slice((h - g * ATT_HEAD_GROUP) * tq, (h - g * ATT_HEAD_GROUP + 1) * tq)
                lh = lg[g][:, loc] + bias
                m_old = m[:, cols]
                m_new = jnp.maximum(m_old, _colreduce(lh, jnp.max))
                p = jnp.exp2(lh - m_new)
                alpha = jnp.exp2(m_old - m_new)
                ms[h] = m_new
                ls[h] = alpha * l[:, cols] + _colreduce(p, jnp.sum)
                als.append(alpha)
                ps.append(_mx(p))
            soft[g] = (jnp.concatenate(als, axis=1), jnp.concatenate(ps, axis=1))

        def weighted_values(g):
            alpha, p = soft[g]
            acc_ref[:, gcols(g)] = alpha * acc_ref[:, gcols(g)] + _dot_tn(kv, p)

        logits(0)
        for g in range(n_grp):
            if g + 1 < n_grp:
                logits(g + 1)
            softmax(g)
            if g > 0:
                weighted_values(g - 1)
        weighted_values(n_grp - 1)
        return (jnp.concatenate([ms[h] for h in range(nh)], axis=1),
                jnp.concatenate([ls[h] for h in range(nh)], axis=1),
                eq_seen + jnp.sum(eqf, axis=0, keepdims=True))

    init = (jnp.full((1, nh * tq), NEG_BIG, F32), jnp.zeros((1, nh * tq), F32), jnp.zeros((1, tq), F32))
    _, l, _ = lax.fori_loop(0, n_kt, att_body, init)

    outs = []
    for h in range(nh):
        cols = slice(h * tq, (h + 1) * tq)
        outs.append(_dot(wuv_ref[h], acc_ref[:, cols] / l[:, cols]))
    o_ref[0] = jnp.transpose(jnp.concatenate(outs, axis=0)).astype(o_ref.dtype)


def _dsa(ckvn3, kwb3, qlat, qidx, wht, wuv_t):
    b, s, _ = ckvn3.shape
    topk = min(DSA_TOPK_MAX, s // 4)
    nh, tq = DSA_HEADS, Q_TILE
    return pl.pallas_call(
        functools.partial(_dsa_kernel, topk=topk), name="dsa_attn",
        grid=(b, s // tq),
        in_specs=[pl.BlockSpec((1, nh, tq, DSA_AUG), lambda bi, qi: (bi, 0, qi, 0)),
                  pl.BlockSpec((1, IDX_HEADS, tq, LANE), lambda bi, qi: (bi, 0, qi, 0)),
                  pl.BlockSpec((1, IDX_HEADS, tq), lambda bi, qi: (bi, 0, qi)),
                  pl.BlockSpec((1, s, LANE), lambda bi, qi: (bi, 0, 0)),
                  pl.BlockSpec((1, s, DSA_AUG), lambda bi, qi: (bi, 0, 0)),
                  pl.BlockSpec(wuv_t.shape, lambda bi, qi: (0, 0, 0))],
        out_specs=pl.BlockSpec((1, tq, DSA_V), lambda bi, qi: (bi, qi, 0)),
        out_shape=jax.ShapeDtypeStruct((b, s, DSA_V), ACT_DTYPE),
        scratch_shapes=[pltpu.VMEM((s // KEY_TILE, KEY_TILE, tq), F32),
                        pltpu.VMEM((s // KEY_TILE, KEY_TILE, tq), BF16),
                        pltpu.VMEM((DSA_KV_RANK, nh * tq), F32)],
        compiler_params=_params(("parallel", "arbitrary")),
    )(qlat, qidx, wht, kwb3, ckvn3, wuv_t)


def _merge_kernel(a_ref, b_ref, c_ref, gt_ref, x_ref, g1_ref, wa_ref, wb_ref, wc_ref, wo_ref, o_ref):
    d = D_MODEL
    g = jax.nn.sigmoid(gt_ref[...].astype(F32))
    ya = jnp.dot(a_ref[...], wa_ref[...], preferred_element_type=F32)
    yb = jnp.dot(b_ref[...], wb_ref[...], preferred_element_type=F32)
    yc = jnp.dot(c_ref[...], wc_ref[...], preferred_element_type=F32)
    m = g[:, :d] * ya + g[:, d:2 * d] * yb + g[:, 2 * d:] * yc
    o_ref[...] = x_ref[...] + g1_ref[0] * _dot(m, wo_ref[...])


def _merge(ya_in, yb_in, yc_in, proj2, x2, mod3, wa, wb, wc, wo, seq):
    t, d = x2.shape
    tm = min(512, seq)
    full = lambda a: pl.BlockSpec(a.shape, lambda i: (0,) * a.ndim)
    br = lambda w: pl.BlockSpec((tm, w), lambda i: (i, 0))
    return pl.pallas_call(
        _merge_kernel, name="merge",
        grid=(t // tm,),
        in_specs=[br(GLA_V), br(DSA_V), br(MLSTM_V), br(3 * d), br(d),
                  pl.BlockSpec((1, 1, d), lambda i: ((i * tm) // seq, 0, 2)),
                  full(wa), full(wb), full(wc), full(wo)],
        out_specs=br(d),
        out_shape=jax.ShapeDtypeStruct((t, d), F32),
        compiler_params=_params(("parallel",)),
    )(ya_in, yb_in, yc_in, proj2, x2, mod3, wa, wb, wc, wo)


def _first_argmax_mask(cur, iota, axis, n):
    mx = jnp.max(cur, axis=axis, keepdims=True)
    ix = jnp.min(jnp.where(cur == mx, iota, n), axis=axis, keepdims=True)
    return iota == ix


def _router_kernel(x_ref, sc_ref, sh_ref, g_ref, rwt_ref, rb_ref, dest_ref, wgt_ref, starts_ref, plens_ref):
    ne, ng = N_EXPERTS, N_GROUPS
    eg = ne // ng
    h = _rms(x_ref[...], g_ref[...]) * (1.0 + sc_ref[0]) + sh_ref[0]
    tm = h.shape[0]
    logits = lax.dot_general(rwt_ref[...], h, (((1,), (1,)), ((), ())),
                             preferred_element_type=F32, precision=HIGHEST)
    scores = jax.nn.sigmoid(logits)
    sel = scores + rb_ref[...]
    s3 = sel.reshape(ng, eg, tm)
    io3 = lax.broadcasted_iota(jnp.int32, (ng, eg, tm), 1)
    m1 = jnp.max(s3, axis=1, keepdims=True)
    first = _first_argmax_mask(s3, io3, 1, eg)
    m2 = jnp.max(jnp.where(first, -jnp.inf, s3), axis=1, keepdims=True)
    gs = (m1 + m2).reshape(ng, tm)
    iog = lax.broadcasted_iota(jnp.int32, (ng, tm), 0)
    gkeep = jnp.zeros((ng, tm), F32)
    cur = gs
    for _ in range(TOPK_GROUPS):
        hit = _first_argmax_mask(cur, iog, 0, ng)
        gkeep = jnp.where(hit, 1.0, gkeep)
        cur = jnp.where(hit, -jnp.inf, cur)
    selm = jnp.where(gkeep.reshape(ng, 1, tm) > 0.0, s3, -jnp.inf).reshape(ne, tm)
    ioe = lax.broadcasted_iota(jnp.int32, (ne, tm), 0)
    hits = []
    chosen = jnp.zeros((ne, tm), F32)
    cur = selm
    for _ in range(TOP_K):
        hit = _first_argmax_mask(cur, ioe, 0, ne)
        hits.append(hit)
        chosen = jnp.where(hit, 1.0, chosen)
        cur = jnp.where(hit, -jnp.inf, cur)
    w = chosen * scores
    w = w / jnp.sum(w, axis=0, keepdims=True) * ROUTED_SCALE

    cnt = jnp.sum(chosen, axis=1, keepdims=True)
    plen = jnp.ceil(cnt * (1.0 / ROW_ALIGN)) * ROW_ALIGN
    start = _cumsum_rows(jnp.broadcast_to(plen, (ne, LANE)))[:, :1] - plen
    before = (lax.broadcasted_iota(jnp.int32, (tm, tm), 0)
              < lax.broadcasted_iota(jnp.int32, (tm, tm), 1)).astype(BF16)
    rank = jnp.dot(chosen.astype(BF16), before, preferred_element_type=F32)
    row_of = start + rank
    pad_rows = SLOT_ROWS - TOP_K
    dest = [jnp.sum(jnp.where(hit, row_of, 0.0), axis=0, keepdims=True) for hit in hits]
    wsel = [jnp.sum(jnp.where(hit, w, 0.0), axis=0, keepdims=True) for hit in hits]
    dest_ref[...] = jnp.concatenate(dest + [jnp.full((pad_rows, tm), -1.0, F32)], axis=0).astype(jnp.int32)
    wgt_ref[...] = jnp.concatenate(wsel + [jnp.zeros((pad_rows, tm), F32)], axis=0)
    starts_ref[0] = start.astype(jnp.int32)
    plens_ref[0] = plen.astype(jnp.int32)


def _router(x2, mod3, norm_g, rw_t, rb_col, seq):
    t, d = x2.shape
    tm = MOE_SUB
    nsb = t // tm
    return pl.pallas_call(
        _router_kernel, name="router",
        grid=(nsb,),
        in_specs=[pl.BlockSpec((tm, d), lambda i: (i, 0)),
                  pl.BlockSpec((1, 1, d), lambda i: ((i * tm) // seq, 0, 4)),
                  pl.BlockSpec((1, 1, d), lambda i: ((i * tm) // seq, 0, 3)),
                  pl.BlockSpec((1, d), lambda i: (0, 0)),
                  pl.BlockSpec(rw_t.shape, lambda i: (0, 0)),
                  pl.BlockSpec(rb_col.shape, lambda i: (0, 0))],
        out_specs=[pl.BlockSpec((SLOT_ROWS, tm), lambda i: (0, i)),
                   pl.BlockSpec((SLOT_ROWS, tm), lambda i: (0, i)),
                   pl.BlockSpec((1, N_EXPERTS, 1), lambda i: (i, 0, 0)),
                   pl.BlockSpec((1, N_EXPERTS, 1), lambda i: (i, 0, 0))],
        out_shape=[jax.ShapeDtypeStruct((SLOT_ROWS, t), jnp.int32),
                   jax.ShapeDtypeStruct((SLOT_ROWS, t), F32),
                   jax.ShapeDtypeStruct((nsb, N_EXPERTS, 1), jnp.int32),
                   jax.ShapeDtypeStruct((nsb, N_EXPERTS, 1), jnp.int32)],
        compiler_params=_params(("parallel",)),
    )(x2, mod3, mod3, norm_g, rw_t, rb_col)


def _moe_kernel(starts_ref, plens_ref, x_ref, sc_ref, sh_ref, g2_ref, gn_ref, dest_ref, wgt_ref,
                wg_ref, wu_ref, wd_ref, sg_ref, su_ref, sd_ref, nf_ref, o_ref,
                h_ref, xs_ref, stage_ref, *, final):
    blk, step = pl.program_id(0), pl.program_id(1)
    eps = wg_ref.shape[0]
    ne = pl.num_programs(1) * eps
    sub, rt, mt = MOE_SUB, MOE_ROW_TILE, MOE_FFN_TILE
    nsub = x_ref.shape[0] // sub
    rmax = xs_ref.shape[1]

    @pl.when(step == 0)
    def _():
        h = _mx(_rms(x_ref[...], gn_ref[...]) * (1.0 + sc_ref[0]) + sh_ref[0])
        h_ref[...] = h
        stage_ref[...] = jnp.zeros_like(stage_ref)
        for sb in range(nsub):
            dest = dest_ref[:, sb * sub:(sb + 1) * sub]
            hs = h_ref[sb * sub:(sb + 1) * sub, :]
            for r in range(rmax // rt):
                rows = r * rt + lax.broadcasted_iota(jnp.int32, (rt, sub), 0)
                pick = jnp.zeros((rt, sub), F32)
                for j in range(TOP_K):
                    pick = jnp.where(dest[j:j + 1, :] == rows, 1.0, pick)
                xs_ref[sb, r * rt:(r + 1) * rt, :] = _dot(pick, hs).astype(xs_ref.dtype)

    def run(sb, e):
        i = (blk * nsub + sb) * ne + e
        return starts_ref[i], plens_ref[i] // ROW_ALIGN

    def copy_rows(src, src0, dst, dst0, n_chunks):
        def body(k, carry):
            s = pl.multiple_of(src0 + k * ROW_ALIGN, ROW_ALIGN)
            d = pl.multiple_of(dst0 + k * ROW_ALIGN, ROW_ALIGN)
            dst[pl.ds(d, ROW_ALIGN), :] = src[pl.ds(s, ROW_ALIGN), :]
            return carry
        lax.fori_loop(0, n_chunks, body, 0)

    def expert(j, xt):
        return _dot(_silu(_dot(xt, wg_ref[j])) * _dot(xt, wu_ref[j]), wd_ref[j])

    win = MOE_WINDOW
    runs = [[run(sb, step * eps + j) for sb in range(nsub)] for j in range(eps)]
    fits = None
    for per_expert in runs:
        for _, nch in per_expert:
            ok = nch * ROW_ALIGN <= win
            fits = ok if fits is None else jnp.logical_and(fits, ok)

    @pl.when(fits)
    def _():
        def apply(experts, first, size):
            def window(sb, st):
                return xs_ref[sb, pl.ds(pl.multiple_of(st + first, ROW_ALIGN), size), :]
            xin = {j: jnp.concatenate([window(sb, st) for sb, (st, _) in enumerate(runs[j])], axis=0)
                   for j in experts}
            out = {j: expert(j, xin[j]).astype(xs_ref.dtype) for j in experts}
            row = first + lax.broadcasted_iota(jnp.int32, (size, 1), 0)
            for j in experts:
                for sb, (st, nch) in enumerate(runs[j]):
                    rows = slice(sb * size, (sb + 1) * size)
                    xs_ref[sb, pl.ds(pl.multiple_of(st + first, ROW_ALIGN), size), :] = jnp.where(
                        row < nch * ROW_ALIGN, out[j][rows], xin[j][rows])

        apply(range(eps), 0, MOE_WINDOW_MAIN)
        for j in range(eps):
            longer = None
            for _, nch in runs[j]:
                more = nch * ROW_ALIGN > MOE_WINDOW_MAIN
                longer = more if longer is None else jnp.logical_or(longer, more)

            @pl.when(longer)
            def _(j=j):
                apply([j], MOE_WINDOW_MAIN, win - MOE_WINDOW_MAIN)

    @pl.when(jnp.logical_not(fits))
    def _():
        for j in range(eps):
            cursor = 0
            placed = []
            for sb, (st, nch) in enumerate(runs[j]):
                copy_rows(xs_ref.at[sb], st, stage_ref, cursor, nch)
                placed.append((sb, st, nch, cursor))
                cursor = cursor + nch * ROW_ALIGN

            def tile(i, carry, j=j):
                r0 = pl.multiple_of(i * mt, mt)
                stage_ref[pl.ds(r0, mt), :] = expert(j, stage_ref[pl.ds(r0, mt), :]).astype(stage_ref.dtype)
                return carry
            lax.fori_loop(0, (cursor + mt - 1) // mt, tile, 0)
            for sb, st, nch, at in placed:
                copy_rows(stage_ref, at, xs_ref.at[sb], st, nch)

    @pl.when(step == pl.num_programs(1) - 1)
    def _():
        for sb in range(nsub):
            tok = slice(sb * sub, (sb + 1) * sub)
            hs = h_ref[tok, :]
            shared = _dot(_silu(_dot(hs, sg_ref[...])) * _dot(hs, su_ref[...]), sd_ref[...])
            dest_t = jnp.transpose(dest_ref[:, tok].astype(F32))
            wgt_t = jnp.transpose(wgt_ref[:, tok])
            routed = jnp.zeros((sub, x_ref.shape[1]), F32)
            for r in range(rmax // rt):
                cols = (r * rt + lax.broadcasted_iota(jnp.int32, (sub, rt), 1)).astype(F32)
                mix = jnp.zeros((sub, rt), F32)
                for j in range(TOP_K):
                    mix = jnp.where(dest_t[:, j:j + 1] == cols, wgt_t[:, j:j + 1], mix)
                routed = routed + _dot(mix, xs_ref[sb, r * rt:(r + 1) * rt, :])
            xo = x_ref[tok, :] + g2_ref[0] * (shared + routed)
            if final:
                xo = _rms(xo, nf_ref[...])
            o_ref[tok, :] = xo


def _round_up(n, m):
    return (n + m - 1) // m * m


def _moe(x2, mod3, norm_g, dest, wgt, starts, plens, wg, wu, wd, sg, su, sd, nf, seq, final):
    t, d = x2.shape
    tm = min(MOE_BLOCK, seq)
    nsub = tm // MOE_SUB
    ne, ff = wg.shape[0], wg.shape[2]
    rmax = _round_up(TOP_K * MOE_SUB + ne * (ROW_ALIGN - 1) + MOE_WINDOW, MOE_ROW_TILE)
    stage_rows = _round_up(tm + nsub * (ROW_ALIGN - 1), MOE_FFN_TILE)
    eps = MOE_EXPERTS_PER_STEP
    full = lambda a: pl.BlockSpec(a.shape, lambda i, e, *_: (0,) * a.ndim)
    once = lambda a: pl.BlockSpec(a.shape, lambda i, e, *_: (0,) * a.ndim, pipeline_mode=pl.Buffered(1))
    mod = lambda j: pl.BlockSpec((1, 1, d), lambda i, e, *_: ((i * tm) // seq, 0, j))
    slot = pl.BlockSpec((SLOT_ROWS, tm), lambda i, e, *_: (0, i))
    grid_spec = pltpu.PrefetchScalarGridSpec(
        num_scalar_prefetch=2,
        grid=(t // tm, ne // eps),
        in_specs=[pl.BlockSpec((tm, d), lambda i, e, *_: (i, 0), pipeline_mode=pl.Buffered(1)),
                  mod(4), mod(3), mod(5),
                  pl.BlockSpec((1, d), lambda i, e, *_: (0, 0)), slot, slot,
                  pl.BlockSpec((eps, d, ff), lambda i, e, *_: (e, 0, 0)),
                  pl.BlockSpec((eps, d, ff), lambda i, e, *_: (e, 0, 0)),
                  pl.BlockSpec((eps, ff, d), lambda i, e, *_: (e, 0, 0)),
                  once(sg), once(su), once(sd), full(nf)],
        out_specs=pl.BlockSpec((tm, d), lambda i, e, *_: (i, 0), pipeline_mode=pl.Buffered(1)),
        scratch_shapes=[pltpu.VMEM((tm, d), MXU_DTYPE),
                        pltpu.VMEM((nsub, rmax, d), MXU_DTYPE),
                        pltpu.VMEM((stage_rows, d), MXU_DTYPE)])
    return pl.pallas_call(
        functools.partial(_moe_kernel, final=final), name="moe",
        grid_spec=grid_spec,
        out_shape=jax.ShapeDtypeStruct((t, d), F32),
        compiler_params=_params(("parallel", "arbitrary"), vmem_mb=58),
    )(starts.reshape(-1), plens.reshape(-1), x2, mod3, mod3, mod3, norm_g, dest, wgt,
      wg, wu, wd, sg, su, sd, nf)


def _pack_w_in(w):
    d = w.shape[0]
    offs = [0]
    for n in IN_SIZES:
        offs.append(offs[-1] + n)
    (qa, ka, va, ga, alr, cq, ckv, kidx, widx, qc, kc, vc, ic, fc, oc, gates) = [
        w[:, offs[i]:offs[i + 1]] for i in range(len(IN_SIZES))]
    z = lambda n: jnp.zeros((d, n), w.dtype)
    packed = jnp.concatenate(
        [gates, qa, ka, va, ga, vc, oc, qc, kc, cq, ckv,
         kidx, widx, z(LANE - IDX_DIM - IDX_HEADS),
         alr, z(LANE - GLA_GATE_RANK),
         ic, fc, z(LANE - 2 * MLSTM_HEADS)], axis=1)
    assert packed.shape[1] == N_PACK
    return packed.astype(MXU_DTYPE)


def kernel(x, c, ada_w, ada_b, norm_mix, norm_ffn, w_in, gla_w_a2, gla_b_a, gla_norm, dsa_norm_q,
           dsa_norm_kv, dsa_w_uq, dsa_w_uk, dsa_w_uv, dsa_w_qi, mlstm_conv, mlstm_b_i, mlstm_b_f,
           mlstm_norm, w_up_a, w_up_b, w_up_c, w_o, router_w, router_bias, exp_w_gate, exp_w_up,
           exp_w_down, sh_w_gate, sh_w_up, sh_w_down, norm_final):
    b, s, d = x.shape
    depth = ada_w.shape[0]
    t = b * s
    mod = _ada_mod(c, ada_w, ada_b)
    x2 = x.reshape(t, d)
    row = lambda v: v.reshape(1, -1)
    for l in range(depth):
        mod3 = mod[l].reshape(b, 1, 6 * d)
        w_pack = _pack_w_in(w_in[l])
        proj2, tail2 = _in_proj(x2, mod3, row(norm_mix[l]), w_pack, s)
        proj3 = proj2.reshape(b, s, N_MAIN)
        tail3 = tail2.reshape(b, s, N_TAIL)

        wa2_pad = jnp.zeros((LANE, GLA_QK), F32).at[:GLA_GATE_RANK].set(gla_w_a2[l])
        ya_in = _gla(proj3, tail3, wa2_pad, row(gla_b_a[l]), row(gla_norm[l]))

        bias_row = jnp.zeros((1, LANE), F32).at[0, :MLSTM_HEADS].set(mlstm_b_i[l])
        bias_row = bias_row.at[0, MLSTM_HEADS:2 * MLSTM_HEADS].set(mlstm_b_f[l])
        yc_in = _mlstm(proj3, tail3, mlstm_conv[l], bias_row, row(mlstm_norm[l]))

        wuq = dsa_w_uq[l].reshape(DSA_Q_RANK, DSA_HEADS * DSA_HEAD_DIM).astype(MXU_DTYPE)
        wuk_t = jnp.transpose(dsa_w_uk[l], (1, 2, 0)).astype(MXU_DTYPE)
        wuv_t = jnp.transpose(dsa_w_uv[l], (1, 2, 0)).astype(MXU_DTYPE)
        wqi_pad = jnp.zeros((DSA_Q_RANK, IDX_HEADS, LANE), F32).at[:, :, :IDX_DIM].set(dsa_w_qi[l])
        wqi_pad = wqi_pad.reshape(DSA_Q_RANK, IDX_HEADS * LANE).astype(MXU_DTYPE)
        ckvn, kwb, qlat, qidx, wht = _dsa_prep(proj2, tail2, s, row(dsa_norm_q[l]), row(dsa_norm_kv[l]),
                                               wuq, wuk_t, wqi_pad)
        yb_in = _dsa(ckvn.reshape(b, s, DSA_AUG), kwb.reshape(b, s, LANE), qlat, qidx, wht, wuv_t)

        x2 = _merge(ya_in.reshape(t, GLA_V), yb_in.reshape(t, DSA_V), yc_in.reshape(t, MLSTM_V),
                    proj2, x2, mod3, w_up_a[l].astype(MXU_DTYPE), w_up_b[l].astype(MXU_DTYPE),
                    w_up_c[l].astype(MXU_DTYPE), w_o[l].astype(MXU_DTYPE), s)

        dest, wgt, starts, plens = _router(x2, mod3, row(norm_ffn[l]), jnp.transpose(router_w[l]),
                                           router_bias[l].reshape(-1, 1), s)
        x2 = _moe(x2, mod3, row(norm_ffn[l]), dest, wgt, starts, plens,
                  exp_w_gate[l].astype(MXU_DTYPE), exp_w_up[l].astype(MXU_DTYPE),
                  exp_w_down[l].astype(MXU_DTYPE), sh_w_gate[l].astype(MXU_DTYPE),
                  sh_w_up[l].astype(MXU_DTYPE), sh_w_down[l].astype(MXU_DTYPE), row(norm_final), s,
                  final=(l == depth - 1))
    return x2.reshape(b, s, d)
```

```python
import functools
import struct

import jax
import jax.numpy as jnp
from jax import lax
from jax.experimental import pallas as pl
from jax.experimental.pallas import tpu as pltpu

F32 = jnp.float32
BF16 = jnp.bfloat16
MXU_DTYPE = jnp.bfloat16
ACT_DTYPE = jnp.bfloat16
HIGHEST = lax.Precision.HIGHEST

EPS = 1e-6
D_MODEL = 1024
GLA_HEADS, GLA_DK, GLA_DV, GLA_GATE_RANK, GLA_TAU, GLA_CHUNK = 4, 64, 128, 16, 16.0, 64
GLA_SUB = 16
DSA_HEADS, DSA_Q_RANK, DSA_KV_RANK, DSA_HEAD_DIM, DSA_V_DIM = 8, 256, 128, 64, 64
IDX_HEADS, IDX_DIM, DSA_TOPK_MAX = 8, 32, 256
MLSTM_HEADS, MLSTM_DQK, MLSTM_DV, MLSTM_CONV, MLSTM_CHUNK = 4, 64, 128, 4, 64
N_EXPERTS, TOP_K, N_GROUPS, TOPK_GROUPS, EXPERT_FF, SHARED_FF = 64, 6, 8, 4, 256, 256
ROUTED_SCALE = 2.5

GLA_QK = GLA_HEADS * GLA_DK
GLA_V = GLA_HEADS * GLA_DV
DSA_V = DSA_HEADS * DSA_V_DIM
MLSTM_QK = MLSTM_HEADS * MLSTM_DQK
MLSTM_V = MLSTM_HEADS * MLSTM_DV
IN_SIZES = (GLA_QK, GLA_QK, GLA_V, GLA_V, GLA_GATE_RANK,
            DSA_Q_RANK, DSA_KV_RANK, IDX_DIM, IDX_HEADS,
            MLSTM_QK, MLSTM_QK, MLSTM_V, MLSTM_HEADS, MLSTM_HEADS, MLSTM_V,
            3 * D_MODEL)

LANE = 128
KEY_TILE = 256
Q_TILE = 256
ATT_HEAD_GROUP = 4
NEG_BIG = -1e30
MOE_BLOCK = 1024
MOE_STREAM_BLOCK = 2048
MOE_SUB = 256
MOE_ROW_TILE = 256
MOE_WINDOW = 48
MOE_FFN_TILE = 128
MOE_EXPERTS_PER_STEP = 4
CONV_PAD = 8
SEQ_GROUP = 16
SLOT_ROWS = 8
ROW_ALIGN = 16

C_GATES = 0
C_QA = 3072
C_KA = 3328
C_VA = 3584
C_GA = 4096
C_VC = 4608
C_OC = 5120
C_QKC = 5632
C_CQ = 6144
N_MAIN = 6400
T_CKV = 0
T_KW = 128
T_ALR = 256
T_ICFC = 384
N_TAIL = 512
N_PACK = N_MAIN + N_TAIL
INPROJ_PIECES = 5
W_IDX_LANE = IDX_DIM


LOG2E = 1.4426950408889634
DSA_AUG = DSA_KV_RANK + LANE
POS_SPLIT = 64
SLOPE_PIECES = 3
SUBLANE = 8
PARTIAL_ROWS = 32
COUNT_UNSET = float(2 ** 30)


def _bf16_pieces(x, n):
    out = []
    for _ in range(n):
        bits = struct.unpack("<I", struct.pack("<f", x))[0]
        bits = (bits + 0x7FFF + ((bits >> 16) & 1)) & 0xFFFF0000
        piece = struct.unpack("<f", struct.pack("<I", bits))[0]
        out.append(piece)
        x -= piece
    return out


def _mx(x):
    return x.astype(MXU_DTYPE)


def _dot(a, b):
    return jnp.dot(_mx(a), _mx(b), preferred_element_type=F32)


def _dot_nt(a, b):
    return lax.dot_general(_mx(a), _mx(b), (((1,), (1,)), ((), ())), preferred_element_type=F32)


def _dot_tn(a, b):
    return lax.dot_general(_mx(a), _mx(b), (((0,), (0,)), ((), ())), preferred_element_type=F32)


def _rms(x, g):
    return x * lax.rsqrt(jnp.mean(x * x, axis=-1, keepdims=True) + EPS) * g


def _silu(x):
    return x * jax.nn.sigmoid(x)


def _log_sigmoid(z):
    return jnp.minimum(z, 0.0) - jnp.log1p(jnp.exp(-jnp.abs(z)))


def _cumsum_rows(x):
    n = x.shape[0]
    tri = (lax.broadcasted_iota(jnp.int32, (n, n), 1) <= lax.broadcasted_iota(jnp.int32, (n, n), 0)).astype(F32)
    return jnp.dot(tri, x, preferred_element_type=F32, precision=HIGHEST)


def _truncate_to_bf16(x):
    bits = lax.bitcast_convert_type(x, jnp.int32) & jnp.int32(-65536)
    return lax.bitcast_convert_type(bits, F32).astype(BF16)


def _colreduce(x, op, width=PARTIAL_ROWS):
    n, c = x.shape
    return op(op(x.reshape(n // width, width, c), axis=0), axis=0, keepdims=True)


def _params(sem, vmem_mb=40):
    return pltpu.CompilerParams(dimension_semantics=sem, vmem_limit_bytes=vmem_mb * 1024 * 1024)


def _ada_kernel(c_ref, w_ref, b_ref, o_ref):
    cs = _silu(c_ref[...])
    o_ref[0] = jnp.dot(cs, w_ref[0], preferred_element_type=F32, precision=HIGHEST) + b_ref[0]


def _ada_mod(c, ada_w, ada_b):
    depth, d, n = ada_w.shape
    b = c.shape[0]
    return pl.pallas_call(
        _ada_kernel, name="ada_mod",
        grid=(depth, n // d),
        in_specs=[pl.BlockSpec((b, d), lambda l, j: (0, 0)),
                  pl.BlockSpec((1, d, d), lambda l, j: (l, 0, j)),
                  pl.BlockSpec((1, 1, d), lambda l, j: (l, 0, j))],
        out_specs=pl.BlockSpec((1, b, d), lambda l, j: (l, 0, j)),
        out_shape=jax.ShapeDtypeStruct((depth, b, n), F32),
        compiler_params=_params(("parallel", "parallel")),
    )(c, ada_w, ada_b.reshape(depth, 1, n))


def _inproj_kernel(x_ref, sc_ref, sh_ref, g_ref, wm_ref, wt_ref, om_ref, ot_ref):
    h = _mx(_rms(x_ref[...], g_ref[...]) * (1.0 + sc_ref[0]) + sh_ref[0])
    piece = N_MAIN // INPROJ_PIECES
    for j in range(INPROJ_PIECES):
        cols = slice(j * piece, (j + 1) * piece)
        om_ref[:, cols] = jnp.dot(h, wm_ref[:, cols], preferred_element_type=F32).astype(om_ref.dtype)
    ot_ref[...] = jnp.dot(h, wt_ref[...], preferred_element_type=F32)


def _in_proj(x2, mod3, norm_g, w_pack, seq):
    t, d = x2.shape
    tm = min(512, seq)
    once = lambda w: pl.BlockSpec((d, w), lambda i: (0, 0), pipeline_mode=pl.Buffered(1))
    return pl.pallas_call(
        _inproj_kernel, name="in_proj",
        grid=(t // tm,),
        in_specs=[pl.BlockSpec((tm, d), lambda i: (i, 0)),
                  pl.BlockSpec((1, 1, d), lambda i: ((i * tm) // seq, 0, 1)),
                  pl.BlockSpec((1, 1, d), lambda i: ((i * tm) // seq, 0, 0)),
                  pl.BlockSpec((1, d), lambda i: (0, 0)),
                  once(N_MAIN), once(N_TAIL)],
        out_specs=[pl.BlockSpec((tm, N_MAIN), lambda i: (i, 0)),
                   pl.BlockSpec((tm, N_TAIL), lambda i: (i, 0))],
        out_shape=[jax.ShapeDtypeStruct((t, N_MAIN), ACT_DTYPE),
                   jax.ShapeDtypeStruct((t, N_TAIL), F32)],
        compiler_params=_params(("parallel",), vmem_mb=48),
    )(x2, mod3, mod3, norm_g, w_pack[:, :N_MAIN], w_pack[:, N_MAIN:])


def _gla_kernel(q_ref, k_ref, v_ref, g_ref, alr_ref, wa2_ref, ba_ref, gn_ref, o_ref, s_ref, acc_ref):
    @pl.when(pl.program_id(1) == 0)
    def _():
        s_ref[...] = jnp.zeros_like(s_ref)

    L, sub, nh, dk, dv = GLA_CHUNK, GLA_SUB, GLA_HEADS, GLA_DK, GLA_DV
    seqs = range(q_ref.shape[0])
    heads = range(nh)
    hk = lambda h: slice(h * dk, (h + 1) * dk)
    hv = lambda h: slice(h * dv, (h + 1) * dv)

    pre = []
    for g in seqs:
        z = jnp.dot(alr_ref[g], wa2_ref[...], preferred_element_type=F32, precision=HIGHEST) + ba_ref[...]
        cum = _cumsum_rows(_log_sigmoid(z) * (1.0 / GLA_TAU))
        q = q_ref[g].astype(F32) * (dk ** -0.5)
        k = k_ref[g].astype(F32)
        tot = cum[L - 1:L, :]
        pre.append(dict(cum=cum, q=q, k=k, tot=tot, vb=_mx(v_ref[g]),
                        q_in=_mx(q * jnp.exp(cum)), k_dec=_mx(k * jnp.exp(tot - cum))))

    scores = {}
    for i in range(L // sub):
        r0, r1 = i * sub, (i + 1) * sub
        for g in seqs:
            p = pre[g]
            base = p["cum"][r0 - 1:r0, :] if i > 0 else jnp.zeros_like(p["tot"])
            qi = _mx(p["q"][r0:r1] * jnp.exp(p["cum"][r0:r1] - base))
            ka = _mx(p["k"][:r1] * jnp.exp(base - p["cum"][:r1]))
            for h in heads:
                scores[g, i, h] = _dot_nt(qi[:, hk(h)], ka[:, hk(h)])

    for i in range(L // sub):
        r0, r1 = i * sub, (i + 1) * sub
        causal = (lax.broadcasted_iota(jnp.int32, (sub, r1), 1)
                  <= lax.broadcasted_iota(jnp.int32, (sub, r1), 0) + r0)
        for g in seqs:
            for h in heads:
                s = jnp.where(causal, scores[g, i, h], 0.0)
                acc_ref[g, r0:r1, hv(h)] = _dot(s, pre[g]["vb"][:r1, hv(h)])

    inter = {(g, h): _dot(pre[g]["q_in"][:, hk(h)], s_ref[g * nh + h]) for g in seqs for h in heads}
    update = {(g, h): _dot_tn(pre[g]["k_dec"][:, hk(h)], pre[g]["vb"][:, hv(h)]) for g in seqs for h in heads}

    gn = gn_ref[...]
    for g in seqs:
        gate = g_ref[g].astype(F32)
        for h in heads:
            o = acc_ref[g, :, hv(h)] + inter[g, h]
            y = _rms(o, gn[:, hv(h)]) * _silu(gate[:, hv(h)])
            o_ref[g, :, hv(h)] = y.astype(o_ref.dtype)
            decay = jnp.transpose(jnp.exp(pre[g]["tot"][:, hk(h)]))
            s_ref[g * nh + h] = s_ref[g * nh + h] * decay + update[g, h]


def _seq_group(batch):
    for grp in (SEQ_GROUP, 2, 1):
        if batch % grp == 0:
            return grp


def _gla(proj3, tail3, wa2_pad, ba, gn):
    b, s, _ = proj3.shape
    L = GLA_CHUNK
    grp = _seq_group(b)
    blk = lambda w, c0: pl.BlockSpec((grp, L, w), lambda bi, ci: (bi, ci, c0 // w))
    full = lambda a: pl.BlockSpec(a.shape, lambda bi, ci: (0,) * a.ndim)
    return pl.pallas_call(
        _gla_kernel, name="gla",
        grid=(b // grp, s // L),
        in_specs=[blk(GLA_QK, C_QA), blk(GLA_QK, C_KA), blk(GLA_V, C_VA), blk(GLA_V, C_GA),
                  blk(LANE, T_ALR), full(wa2_pad), full(ba), full(gn)],
        out_specs=pl.BlockSpec((grp, L, GLA_V), lambda bi, ci: (bi, ci, 0)),
        out_shape=jax.ShapeDtypeStruct((b, s, GLA_V), ACT_DTYPE),
        scratch_shapes=[pltpu.VMEM((grp * GLA_HEADS, GLA_DK, GLA_DV), F32),
                        pltpu.VMEM((grp, L, GLA_V), F32)],
        compiler_params=_params(("parallel", "arbitrary")),
    )(proj3, proj3, proj3, proj3, tail3, wa2_pad, ba, gn)


def _mlstm_pair_kernel(qk_ref, v_ref, oc_ref, if_ref, conv_ref, bias_ref, gnt_ref, o_ref,
                       xbuf_ref, ct_ref, n_ref, m_ref):
    L, nh, dk, dv, kc = MLSTM_CHUNK, MLSTM_HEADS, MLSTM_DQK, MLSTM_DV, MLSTM_CONV
    pad = CONV_PAD
    npair = nh // 2
    assert 2 * dk == LANE and dv == LANE and L == dk

    @pl.when(pl.program_id(1) == 0)
    def _():
        xbuf_ref[:, 0:pad, :] = jnp.zeros((xbuf_ref.shape[0], pad, 2 * MLSTM_QK), F32)
        ct_ref[...] = jnp.zeros_like(ct_ref)
        n_ref[...] = jnp.zeros_like(n_ref)
        m_ref[...] = jnp.zeros_like(m_ref)

    lane = lax.broadcasted_iota(jnp.int32, (1, LANE), 1)
    half = [(lane < dk).astype(F32), (lane >= dk).astype(F32)]
    s_idx = lax.broadcasted_iota(jnp.int32, (L, LANE), 0)
    t_idx = lax.broadcasted_iota(jnp.int32, (L, LANE), 1) % L
    causal = s_idx <= t_idx
    lane_in = lax.broadcasted_iota(jnp.int32, (LANE, LANE), 0)
    head_of = lax.broadcasted_iota(jnp.int32, (LANE, LANE), 1) // dk
    cw = conv_ref[...]

    combos = [(g, p) for g in range(qk_ref.shape[0]) for p in range(npair)]
    seq = {}
    for g in range(qk_ref.shape[0]):
        xbuf_ref[g, pad:pad + L, :] = qk_ref[g].astype(F32)
        conv = jnp.zeros((L, 2 * MLSTM_QK), F32)
        for j in range(kc):
            conv = conv + cw[j:j + 1, :] * xbuf_ref[g, pl.ds(pad - (kc - 1) + j, L), :]
        xbuf_ref[g, 0:pad, :] = xbuf_ref[g, L:L + pad, :]
        qk = _silu(conv)
        pre = if_ref[g] + bias_ref[...]
        bcum = _cumsum_rows(_log_sigmoid(pre))
        v = v_ref[g].astype(F32)
        seq[g] = dict(q=qk[:, :MLSTM_QK] * (dk ** -0.5), k=qk[:, MLSTM_QK:],
                      vt=[_mx(jnp.transpose(v[:, h * dv:(h + 1) * dv])) for h in range(nh)],
                      gate_mix=jnp.where(lane < nh, pre, -bcum), bcum=bcum, bcum_t=jnp.transpose(bcum))

    st = {}
    for g, p in combos:
        sq = seq[g]
        sel = jnp.where((lane_in == 2 * p + head_of) | (lane_in == nh + 2 * p + head_of), 1.0, 0.0)
        selb = jnp.where(lane_in == nh + 2 * p + head_of, 1.0, 0.0)
        d_mat = jnp.dot(sq["gate_mix"], sel, preferred_element_type=F32, precision=HIGHEST)
        tot = jnp.dot(sq["bcum"][L - SUBLANE:L, :], selb, preferred_element_type=F32,
                      precision=HIGHEST)[SUBLANE - 1:SUBLANE, :]
        b_row = jnp.concatenate([sq["bcum_t"][nh + 2 * p:nh + 2 * p + 1, :],
                                 sq["bcum_t"][nh + 2 * p + 1:nh + 2 * p + 2, :]], axis=1)
        qt = sq["q"][:, p * LANE:(p + 1) * LANE]
        kt = sq["k"][:, p * LANE:(p + 1) * LANE]
        q_bd = _mx(jnp.concatenate([qt * half[0], qt * half[1]], axis=0))
        st[g, p] = dict(d=d_mat, tot=tot, b_row=b_row, kt=kt, q_bd=q_bd,
                        scores=_dot_nt(kt, q_bd),
                        qn=_dot_nt(jnp.broadcast_to(n_ref[g * npair + p], (SUBLANE, LANE)), q_bd)[0:1, :],
                        inter=[_dot_nt(ct_ref[g * nh + 2 * p + hh], q_bd) for hh in range(2)])

    for g, p in combos:
        c = st[g, p]
        m_prev = m_ref[g * npair + p]
        dlog = jnp.where(causal, c["b_row"] + c["d"], -jnp.inf)
        inter_log = c["b_row"] + m_prev
        m_t = jnp.maximum(inter_log, jnp.max(dlog, axis=0, keepdims=True))
        g_log = c["tot"] + c["d"]
        m_new = jnp.maximum(c["tot"] + m_prev, jnp.max(g_log, axis=0, keepdims=True))
        c.update(m_t=m_t, w_inter=jnp.exp(inter_log - m_t), sw=c["scores"] * jnp.exp(dlog - m_t),
                 m_new=m_new, w_c=jnp.exp(c["tot"] + m_prev - m_new), ks=c["kt"] * jnp.exp(g_log - m_new))

    for g, p in combos:
        c = st[g, p]
        vt = seq[g]["vt"]
        c["sv"] = [_dot(vt[2 * p + hh], c["sw"] * half[hh]) for hh in range(2)]
        c["kv"] = [_dot(vt[2 * p + hh], c["ks"] * half[hh]) for hh in range(2)]

    for g, p in combos:
        c = st[g, p]
        num = c["sv"][0] + c["sv"][1] + c["w_inter"] * (c["inter"][0] + c["inter"][1])
        den = jnp.sum(c["sw"], axis=0, keepdims=True) + c["w_inter"] * c["qn"]
        hout = num / jnp.maximum(jnp.abs(den), jnp.exp(-c["m_t"]))
        y_t = hout * lax.rsqrt(jnp.mean(hout * hout, axis=0, keepdims=True) + EPS) * gnt_ref[p]
        y = jnp.transpose(y_t)
        for hh in range(2):
            cols = slice((2 * p + hh) * dv, (2 * p + hh + 1) * dv)
            o_ref[g, :, cols] = (y[hh * L:(hh + 1) * L, :] * jax.nn.sigmoid(oc_ref[g, :, cols].astype(F32))).astype(o_ref.dtype)
            ct_ref[g * nh + 2 * p + hh] = ct_ref[g * nh + 2 * p + hh] * c["w_c"] + c["kv"][hh]
        n_ref[g * npair + p] = n_ref[g * npair + p] * c["w_c"] + jnp.sum(c["ks"], axis=0, keepdims=True)
        m_ref[g * npair + p] = c["m_new"]


def _mlstm(proj3, tail3, conv_w, bias_row, gn):
    b, s, _ = proj3.shape
    L = MLSTM_CHUNK
    grp = _seq_group(b)
    blk = lambda w, c0: pl.BlockSpec((grp, L, w), lambda bi, ci: (bi, ci, c0 // w))
    full = lambda a: pl.BlockSpec(a.shape, lambda bi, ci: (0,) * a.ndim)
    npair = MLSTM_HEADS // 2
    gnt = jnp.repeat(jnp.transpose(gn.reshape(npair, 2, MLSTM_DV), (0, 2, 1)), L, axis=2)
    return pl.pallas_call(
        _mlstm_pair_kernel, name="mlstm",
        grid=(b // grp, s // L),
        in_specs=[blk(2 * MLSTM_QK, C_QKC), blk(MLSTM_V, C_VC), blk(MLSTM_V, C_OC), blk(LANE, T_ICFC),
                  full(conv_w), full(bias_row), full(gnt)],
        out_specs=pl.BlockSpec((grp, L, MLSTM_V), lambda bi, ci: (bi, ci, 0)),
        out_shape=jax.ShapeDtypeStruct((b, s, MLSTM_V), ACT_DTYPE),
        scratch_shapes=[pltpu.VMEM((grp, L + CONV_PAD, 2 * MLSTM_QK), F32),
                        pltpu.VMEM((grp * MLSTM_HEADS, MLSTM_DV, LANE), F32),
                        pltpu.VMEM((grp * npair, 1, LANE), F32),
                        pltpu.VMEM((grp * npair, 1, LANE), F32)],
        compiler_params=_params(("parallel", "arbitrary")),
    )(proj3, proj3, proj3, tail3, conv_w, bias_row, gnt)


def _dsa_prep_kernel(cq_ref, ckv_ref, kw_ref, nq_ref, nkv_ref, wuq_ref, wuk_ref, wqi_ref,
                     ckvn_ref, kwb_ref, qlat_ref, qidx_ref, wht_ref, *, tiles_per_seq):
    tm = cq_ref.shape[0]
    r = DSA_KV_RANK
    cqn = _mx(_rms(cq_ref[...].astype(F32), nq_ref[...]))
    pos = (pl.program_id(0) % tiles_per_seq) * tm + lax.broadcasted_iota(jnp.int32, (tm, LANE), 0)
    lane = lax.broadcasted_iota(jnp.int32, (tm, LANE), 1)
    pos_cols = jnp.where(lane < SLOPE_PIECES, pos >> (POS_SPLIT.bit_length() - 1),
                         jnp.where(lane < 2 * SLOPE_PIECES, pos & (POS_SPLIT - 1), 0)).astype(F32)
    ckvn_ref[:, :r] = _rms(ckv_ref[...], nkv_ref[...]).astype(ckvn_ref.dtype)
    ckvn_ref[:, r:] = pos_cols.astype(ckvn_ref.dtype)
    kw = kw_ref[...]
    kwb_ref[...] = kw.astype(kwb_ref.dtype)
    wht_ref[0] = jnp.transpose(kw)[W_IDX_LANE:W_IDX_LANE + IDX_HEADS, :] * (IDX_HEADS ** -0.5)
    q = jnp.dot(cqn, wuq_ref[...], preferred_element_type=F32)
    for h in range(DSA_HEADS):
        ql = _dot(q[:, h * DSA_HEAD_DIM:(h + 1) * DSA_HEAD_DIM], wuk_ref[h]) * (DSA_HEAD_DIM ** -0.5 * LOG2E)
        qlat_ref[0, h, :, :r] = ql.astype(qlat_ref.dtype)
        c = _bf16_pieces(2.0 ** (-8.0 * (h + 1) / DSA_HEADS) * LOG2E, SLOPE_PIECES)
        consts = [POS_SPLIT * v for v in c] + c
        slope_cols = jnp.zeros((tm, LANE), F32)
        for j, v in enumerate(consts):
            slope_cols = jnp.where(lane == j, v, slope_cols)
        qlat_ref[0, h, :, r:] = slope_cols.astype(qlat_ref.dtype)
    qi = jnp.dot(cqn, wqi_ref[...], preferred_element_type=F32) * (IDX_DIM ** -0.5)
    for h in range(IDX_HEADS):
        qidx_ref[0, h] = qi[:, h * LANE:(h + 1) * LANE].astype(qidx_ref.dtype)


def _dsa_prep(proj2, tail2, seq, nq, nkv, wuq, wuk_t, wqi_pad):
    t = proj2.shape[0]
    b = t // seq
    tm = min(512, seq)
    per = seq // tm
    blk = lambda w, c0: pl.BlockSpec((tm, w), lambda i: (i, c0 // w))
    full = lambda a: pl.BlockSpec(a.shape, lambda i: (0,) * a.ndim)
    hmap = lambda i: (i // per, 0, i % per, 0)
    return pl.pallas_call(
        functools.partial(_dsa_prep_kernel, tiles_per_seq=per), name="dsa_prep",
        grid=(t // tm,),
        in_specs=[blk(DSA_Q_RANK, C_CQ), blk(DSA_KV_RANK, T_CKV), blk(LANE, T_KW),
                  full(nq), full(nkv), full(wuq), full(wuk_t), full(wqi_pad)],
        out_specs=[pl.BlockSpec((tm, DSA_AUG), lambda i: (i, 0)),
                   pl.BlockSpec((tm, LANE), lambda i: (i, 0)),
                   pl.BlockSpec((1, DSA_HEADS, tm, DSA_AUG), hmap),
                   pl.BlockSpec((1, IDX_HEADS, tm, LANE), hmap),
                   pl.BlockSpec((1, IDX_HEADS, tm), lambda i: (i // per, 0, i % per))],
        out_shape=[jax.ShapeDtypeStruct((t, DSA_AUG), ACT_DTYPE),
                   jax.ShapeDtypeStruct((t, LANE), ACT_DTYPE),
                   jax.ShapeDtypeStruct((b, DSA_HEADS, seq, DSA_AUG), ACT_DTYPE),
                   jax.ShapeDtypeStruct((b, IDX_HEADS, seq, LANE), ACT_DTYPE),
                   jax.ShapeDtypeStruct((b, IDX_HEADS, seq), F32)],
        compiler_params=_params(("parallel",)),
    )(proj2, tail2, tail2, nq, nkv, wuq, wuk_t, wqi_pad)


def _dsa_kernel(qlat_ref, qidx_ref, wht_ref, kwk_ref, ckv_ref, wuv_ref, o_ref,
                ibuf_ref, ihi_ref, acc_ref, *, topk):
    nh, tq, tk = DSA_HEADS, Q_TILE, KEY_TILE
    qb = pl.program_id(1)
    n_kt = (qb * tq + tq + tk - 1) // tk
    t_row = qb * tq + lax.broadcasted_iota(jnp.int32, (1, tq), 1)
    s_col = lax.broadcasted_iota(jnp.int32, (tk, 1), 0)

    qi = qidx_ref[0].reshape(IDX_HEADS * tq, LANE)
    wht = wht_ref[0]

    def idx_body(kt, carry):
        kk = kwk_ref[0, pl.ds(pl.multiple_of(kt * tk, tk), tk), :]
        grp = ATT_HEAD_GROUP
        sc = [_dot_nt(kk, qi[g * grp * tq:(g + 1) * grp * tq]) for g in range(IDX_HEADS // grp)]
        tot = None
        for h in range(IDX_HEADS):
            part = wht[h:h + 1, :] * jnp.maximum(sc[h // grp][:, (h % grp) * tq:(h % grp + 1) * tq], 0.0)
            tot = part if tot is None else tot + part
        score = jnp.where(kt * tk + s_col <= t_row, tot, -jnp.inf)
        ibuf_ref[kt] = score
        ihi_ref[kt] = _truncate_to_bf16(score)
        return carry

    lax.fori_loop(0, n_kt, idx_body, 0)

    def count(pred):
        def body(kt, c):
            hit = jnp.where(pred(ibuf_ref[kt]), 1.0, 0.0)
            return c + jnp.sum(hit.reshape(tk // PARTIAL_ROWS, PARTIAL_ROWS, tq), axis=0)
        return jnp.sum(lax.fori_loop(0, n_kt, body, jnp.zeros((PARTIAL_ROWS, tq), F32)), axis=0, keepdims=True)

    def count_ge(cand):
        return count(lambda x: x >= cand)

    def count_gt(cand):
        return count(lambda x: x > cand)

    def key_to_float(u):
        key = u ^ jnp.int32(-2 ** 31)
        bits = jnp.where(key >= 0, key, key ^ jnp.int32(0x7FFFFFFF))
        return lax.bitcast_convert_type(bits, F32)

    few = t_row < topk
    n_bits = 32

    def try_bit(i, u, cnt_u, counter):
        cand_u = u | lax.shift_left(jnp.int32(1), n_bits - 1 - i)
        cnt = counter(cand_u)
        ok = cnt >= topk
        return jnp.where(ok, cand_u, u), jnp.where(ok, cnt, cnt_u)

    def count_ge_hi(cand_u):
        cand = _truncate_to_bf16(key_to_float(cand_u))
        def body(kt, c):
            hit = jnp.where(ihi_ref[kt] >= cand, jnp.ones((), BF16), jnp.zeros((), BF16))
            part = jnp.sum(hit.reshape(tk // PARTIAL_ROWS, PARTIAL_ROWS, tq), axis=0)
            return c + part.astype(F32)
        return jnp.sum(lax.fori_loop(0, n_kt, body, jnp.zeros((PARTIAL_ROWS, tq), F32)), axis=0, keepdims=True)

    def coarse_body(i, st):
        return try_bit(i, *st, count_ge_hi)

    def search_cond(st):
        i, _, _, pending = st
        return (i < n_bits) & (pending > 0)

    def search_body(st):
        i, u, cnt_u, _ = st
        for _ in range(4):
            u, cnt_u = try_bit(i, u, cnt_u, lambda c: count_ge(key_to_float(c)))
            i = i + 1
        pending = jnp.max(jnp.where(few | (cnt_u == topk), 0, 1))
        return i, u, cnt_u, pending

    u, cnt_u = lax.fori_loop(0, n_bits // 2, coarse_body,
                             (jnp.zeros((1, tq), jnp.int32), jnp.full((1, tq), COUNT_UNSET, F32)))
    start = (jnp.int32(n_bits // 2), u, cnt_u, jnp.max(jnp.where(few | (cnt_u == topk), 0, 1)))
    _, u, _, _ = lax.while_loop(search_cond, search_body, start)
    tau = jnp.where(few, -jnp.inf, key_to_float(u))
    need = topk - count_gt(tau)

    acc_ref[...] = jnp.zeros_like(acc_ref)
    ql = qlat_ref[0].reshape(nh * tq, DSA_AUG)
    tri = jnp.where(lax.broadcasted_iota(jnp.int32, (tk, tk), 0) >= lax.broadcasted_iota(jnp.int32, (tk, tk), 1),
                    1.0, 0.0).astype(BF16)

    def keys(kt):
        return ckv_ref[0, pl.ds(pl.multiple_of(kt * tk, tk), tk), :]

    def att_body(kt, carry):
        m, l, eq_seen = carry
        kv_aug = keys(kt)
        kv = kv_aug[:, :DSA_KV_RANK]
        it = ibuf_ref[kt]
        eq = it == tau
        eqf = jnp.where(eq, 1.0, 0.0)
        rank = jnp.dot(tri, eqf.astype(BF16), preferred_element_type=F32) + eq_seen
        valid = ((it > tau) | (eq & (rank <= need))) & (kt * tk + s_col <= t_row)
        bias = jnp.where(valid, 0.0, NEG_BIG)

        n_grp = nh // ATT_HEAD_GROUP
        gcols = lambda g: slice(g * ATT_HEAD_GROUP * tq, (g + 1) * ATT_HEAD_GROUP * tq)
        lg, soft, ms, ls = {}, {}, {}, {}

        def logits(g):
            lg[g] = _dot_nt(kv_aug, ql[gcols(g)])

        def softmax(g):
            als, ps = [], []
            for h in range(g * ATT_HEAD_GROUP, (g + 1) * ATT_HEAD_GROUP):
                cols = slice(h * tq, (h + 1) * tq)
                loc = slice((h - g * ATT_HEAD_GROUP) * tq, (h - g * ATT_HEAD_GROUP + 1) * tq)
                lh = lg[g][:, loc] + bias
                m_old = m[:, cols]
                m_new = jnp.maximum(m_old, _colreduce(lh, jnp.max))
                p = jnp.exp2(lh - m_new)
                alpha = jnp.exp2(m_old - m_new)
                ms[h] = m_new
                ls[h] = alpha * l[:, cols] + _colreduce(p, jnp.sum)
                als.append(alpha)
                ps.append(_mx(p))
            soft[g] = (jnp.concatenate(als, axis=1), jnp.concatenate(ps, axis=1))

        def weighted_values(g):
            alpha, p = soft[g]
            acc_ref[:, gcols(g)] = alpha * acc_ref[:, gcols(g)] + _dot_tn(kv, p)

        logits(0)
        for g in range(n_grp):
            if g + 1 < n_grp:
                logits(g + 1)
            softmax(g)
            if g > 0:
                weighted_values(g - 1)
        weighted_values(n_grp - 1)
        return (jnp.concatenate([ms[h] for h in range(nh)], axis=1),
                jnp.concatenate([ls[h] for h in range(nh)], axis=1),
                eq_seen + jnp.sum(eqf, axis=0, keepdims=True))

    init = (jnp.full((1, nh * tq), NEG_BIG, F32), jnp.zeros((1, nh * tq), F32), jnp.zeros((1, tq), F32))
    _, l, _ = lax.fori_loop(0, n_kt, att_body, init)

    outs = []
    for h in range(nh):
        cols = slice(h * tq, (h + 1) * tq)
        outs.append(_dot(wuv_ref[h], acc_ref[:, cols] / l[:, cols]))
    o_ref[0] = jnp.transpose(jnp.concatenate(outs, axis=0)).astype(o_ref.dtype)


def _dsa(ckvn3, kwb3, qlat, qidx, wht, wuv_t):
    b, s, _ = ckvn3.shape
    topk = min(DSA_TOPK_MAX, s // 4)
    nh, tq = DSA_HEADS, Q_TILE
    return pl.pallas_call(
        functools.partial(_dsa_kernel, topk=topk), name="dsa_attn",
        grid=(b, s // tq),
        in_specs=[pl.BlockSpec((1, nh, tq, DSA_AUG), lambda bi, qi: (bi, 0, qi, 0)),
                  pl.BlockSpec((1, IDX_HEADS, tq, LANE), lambda bi, qi: (bi, 0, qi, 0)),
                  pl.BlockSpec((1, IDX_HEADS, tq), lambda bi, qi: (bi, 0, qi)),
                  pl.BlockSpec((1, s, LANE), lambda bi, qi: (bi, 0, 0)),
                  pl.BlockSpec((1, s, DSA_AUG), lambda bi, qi: (bi, 0, 0)),
                  pl.BlockSpec(wuv_t.shape, lambda bi, qi: (0, 0, 0))],
        out_specs=pl.BlockSpec((1, tq, DSA_V), lambda bi, qi: (bi, qi, 0)),
        out_shape=jax.ShapeDtypeStruct((b, s, DSA_V), ACT_DTYPE),
        scratch_shapes=[pltpu.VMEM((s // KEY_TILE, KEY_TILE, tq), F32),
                        pltpu.VMEM((s // KEY_TILE, KEY_TILE, tq), BF16),
                        pltpu.VMEM((DSA_KV_RANK, nh * tq), F32)],
        compiler_params=_params(("parallel", "arbitrary")),
    )(qlat, qidx, wht, kwb3, ckvn3, wuv_t)


def _merge_kernel(a_ref, b_ref, c_ref, gt_ref, x_ref, g1_ref, wa_ref, wb_ref, wc_ref, wo_ref, o_ref):
    d = D_MODEL
    g = jax.nn.sigmoid(gt_ref[...].astype(F32))
    ya = jnp.dot(a_ref[...], wa_ref[...], preferred_element_type=F32)
    yb = jnp.dot(b_ref[...], wb_ref[...], preferred_element_type=F32)
    yc = jnp.dot(c_ref[...], wc_ref[...], preferred_element_type=F32)
    m = g[:, :d] * ya + g[:, d:2 * d] * yb + g[:, 2 * d:] * yc
    o_ref[...] = x_ref[...] + g1_ref[0] * _dot(m, wo_ref[...])


def _merge(ya_in, yb_in, yc_in, proj2, x2, mod3, wa, wb, wc, wo, seq):
    t, d = x2.shape
    tm = min(512, seq)
    full = lambda a: pl.BlockSpec(a.shape, lambda i: (0,) * a.ndim)
    br = lambda w: pl.BlockSpec((tm, w), lambda i: (i, 0))
    return pl.pallas_call(
        _merge_kernel, name="merge",
        grid=(t // tm,),
        in_specs=[br(GLA_V), br(DSA_V), br(MLSTM_V), br(3 * d), br(d),
                  pl.BlockSpec((1, 1, d), lambda i: ((i * tm) // seq, 0, 2)),
                  full(wa), full(wb), full(wc), full(wo)],
        out_specs=br(d),
        out_shape=jax.ShapeDtypeStruct((t, d), F32),
        compiler_params=_params(("parallel",)),
    )(ya_in, yb_in, yc_in, proj2, x2, mod3, wa, wb, wc, wo)


def _first_argmax_mask(cur, iota, axis, n):
    mx = jnp.max(cur, axis=axis, keepdims=True)
    ix = jnp.min(jnp.where(cur == mx, iota, n), axis=axis, keepdims=True)
    return iota == ix


def _router_kernel(x_ref, sc_ref, sh_ref, g_ref, rwt_ref, rb_ref, dest_ref, wgt_ref, starts_ref, plens_ref):
    ne, ng = N_EXPERTS, N_GROUPS
    eg = ne // ng
    h = _rms(x_ref[...], g_ref[...]) * (1.0 + sc_ref[0]) + sh_ref[0]
    tm = h.shape[0]
    logits = lax.dot_general(rwt_ref[...], h, (((1,), (1,)), ((), ())),
                             preferred_element_type=F32, precision=HIGHEST)
    scores = jax.nn.sigmoid(logits)
    sel = scores + rb_ref[...]
    s3 = sel.reshape(ng, eg, tm)
    io3 = lax.broadcasted_iota(jnp.int32, (ng, eg, tm), 1)
    m1 = jnp.max(s3, axis=1, keepdims=True)
    first = _first_argmax_mask(s3, io3, 1, eg)
    m2 = jnp.max(jnp.where(first, -jnp.inf, s3), axis=1, keepdims=True)
    gs = (m1 + m2).reshape(ng, tm)
    iog = lax.broadcasted_iota(jnp.int32, (ng, tm), 0)
    gkeep = jnp.zeros((ng, tm), F32)
    cur = gs
    for _ in range(TOPK_GROUPS):
        hit = _first_argmax_mask(cur, iog, 0, ng)
        gkeep = jnp.where(hit, 1.0, gkeep)
        cur = jnp.where(hit, -jnp.inf, cur)
    selm = jnp.where(gkeep.reshape(ng, 1, tm) > 0.0, s3, -jnp.inf).reshape(ne, tm)
    ioe = lax.broadcasted_iota(jnp.int32, (ne, tm), 0)
    hits = []
    chosen = jnp.zeros((ne, tm), F32)
    cur = selm
    for _ in range(TOP_K):
        hit = _first_argmax_mask(cur, ioe, 0, ne)
        hits.append(hit)
        chosen = jnp.where(hit, 1.0, chosen)
        cur = jnp.where(hit, -jnp.inf, cur)
    w = chosen * scores
    w = w / jnp.sum(w, axis=0, keepdims=True) * ROUTED_SCALE

    cnt = jnp.sum(chosen, axis=1, keepdims=True)
    plen = jnp.ceil(cnt * (1.0 / ROW_ALIGN)) * ROW_ALIGN
    start = _cumsum_rows(jnp.broadcast_to(plen, (ne, LANE)))[:, :1] - plen
    before = (lax.broadcasted_iota(jnp.int32, (tm, tm), 0)
              < lax.broadcasted_iota(jnp.int32, (tm, tm), 1)).astype(BF16)
    rank = jnp.dot(chosen.astype(BF16), before, preferred_element_type=F32)
    row_of = start + rank
    pad_rows = SLOT_ROWS - TOP_K
    dest = [jnp.sum(jnp.where(hit, row_of, 0.0), axis=0, keepdims=True) for hit in hits]
    wsel = [jnp.sum(jnp.where(hit, w, 0.0), axis=0, keepdims=True) for hit in hits]
    dest_ref[...] = jnp.concatenate(dest + [jnp.full((pad_rows, tm), -1.0, F32)], axis=0).astype(jnp.int32)
    wgt_ref[...] = jnp.concatenate(wsel + [jnp.zeros((pad_rows, tm), F32)], axis=0)
    starts_ref[0] = start.astype(jnp.int32)
    plens_ref[0] = plen.astype(jnp.int32)


def _router(x2, mod3, norm_g, rw_t, rb_col, seq):
    t, d = x2.shape
    tm = MOE_SUB
    nsb = t // tm
    return pl.pallas_call(
        _router_kernel, name="router",
        grid=(nsb,),
        in_specs=[pl.BlockSpec((tm, d), lambda i: (i, 0)),
                  pl.BlockSpec((1, 1, d), lambda i: ((i * tm) // seq, 0, 4)),
                  pl.BlockSpec((1, 1, d), lambda i: ((i * tm) // seq, 0, 3)),
                  pl.BlockSpec((1, d), lambda i: (0, 0)),
                  pl.BlockSpec(rw_t.shape, lambda i: (0, 0)),
                  pl.BlockSpec(rb_col.shape, lambda i: (0, 0))],
        out_specs=[pl.BlockSpec((SLOT_ROWS, tm), lambda i: (0, i)),
                   pl.BlockSpec((SLOT_ROWS, tm), lambda i: (0, i)),
                   pl.BlockSpec((1, N_EXPERTS, 1), lambda i: (i, 0, 0)),
                   pl.BlockSpec((1, N_EXPERTS, 1), lambda i: (i, 0, 0))],
        out_shape=[jax.ShapeDtypeStruct((SLOT_ROWS, t), jnp.int32),
                   jax.ShapeDtypeStruct((SLOT_ROWS, t), F32),
                   jax.ShapeDtypeStruct((nsb, N_EXPERTS, 1), jnp.int32),
                   jax.ShapeDtypeStruct((nsb, N_EXPERTS, 1), jnp.int32)],
        compiler_params=_params(("parallel",)),
    )(x2, mod3, mod3, norm_g, rw_t, rb_col)


def _moe_kernel(starts_ref, plens_ref, x_ref, sc_ref, sh_ref, g2_ref, gn_ref, dest_ref, wgt_ref,
                wg_ref, wu_ref, wd_ref, sg_ref, su_ref, sd_ref, nf_ref, o_ref,
                h_ref, xs_ref, stage_ref, *, final):
    blk, step = pl.program_id(0), pl.program_id(1)
    eps = wg_ref.shape[0]
    ne = pl.num_programs(1) * eps
    sub, rt, mt = MOE_SUB, MOE_ROW_TILE, MOE_FFN_TILE
    nsub = x_ref.shape[0] // sub
    rmax = xs_ref.shape[1]

    @pl.when(step == 0)
    def _():
        h = _mx(_rms(x_ref[...], gn_ref[...]) * (1.0 + sc_ref[0]) + sh_ref[0])
        h_ref[...] = h
        stage_ref[...] = jnp.zeros_like(stage_ref)
        for sb in range(nsub):
            dest = dest_ref[:, sb * sub:(sb + 1) * sub]
            hs = h_ref[sb * sub:(sb + 1) * sub, :]
            for r in range(rmax // rt):
                rows = r * rt + lax.broadcasted_iota(jnp.int32, (rt, sub), 0)
                pick = jnp.zeros((rt, sub), F32)
                for j in range(TOP_K):
                    pick = jnp.where(dest[j:j + 1, :] == rows, 1.0, pick)
                xs_ref[sb, r * rt:(r + 1) * rt, :] = _dot(pick, hs).astype(xs_ref.dtype)

    def run(sb, e):
        i = (blk * nsub + sb) * ne + e
        return starts_ref[i], plens_ref[i] // ROW_ALIGN

    def copy_rows(src, src0, dst, dst0, n_chunks):
        def body(k, carry):
            s = pl.multiple_of(src0 + k * ROW_ALIGN, ROW_ALIGN)
            d = pl.multiple_of(dst0 + k * ROW_ALIGN, ROW_ALIGN)
            dst[pl.ds(d, ROW_ALIGN), :] = src[pl.ds(s, ROW_ALIGN), :]
            return carry
        lax.fori_loop(0, n_chunks, body, 0)

    def expert(j, xt):
        return _dot(_silu(_dot(xt, wg_ref[j])) * _dot(xt, wu_ref[j]), wd_ref[j])

    win = MOE_WINDOW
    runs = [[run(sb, step * eps + j) for sb in range(nsub)] for j in range(eps)]
    fits = None
    for per_expert in runs:
        for _, nch in per_expert:
            ok = nch * ROW_ALIGN <= win
            fits = ok if fits is None else jnp.logical_and(fits, ok)

    @pl.when(fits)
    def _():
        def window(sb, st):
            return xs_ref[sb, pl.ds(pl.multiple_of(st, ROW_ALIGN), win), :]
        xin = [jnp.concatenate([window(sb, st) for sb, (st, _) in enumerate(runs[j])], axis=0)
               for j in range(eps)]
        out = [expert(j, xin[j]).astype(xs_ref.dtype) for j in range(eps)]
        row = lax.broadcasted_iota(jnp.int32, (win, 1), 0)
        for j in range(eps):
            for sb, (st, nch) in enumerate(runs[j]):
                rows = slice(sb * win, (sb + 1) * win)
                xs_ref[sb, pl.ds(pl.multiple_of(st, ROW_ALIGN), win), :] = jnp.where(
                    row < nch * ROW_ALIGN, out[j][rows], xin[j][rows])

    @pl.when(jnp.logical_not(fits))
    def _():
        for j in range(eps):
            cursor = 0
            placed = []
            for sb, (st, nch) in enumerate(runs[j]):
                copy_rows(xs_ref.at[sb], st, stage_ref, cursor, nch)
                placed.append((sb, st, nch, cursor))
                cursor = cursor + nch * ROW_ALIGN

            def tile(i, carry, j=j):
                r0 = pl.multiple_of(i * mt, mt)
                stage_ref[pl.ds(r0, mt), :] = expert(j, stage_ref[pl.ds(r0, mt), :]).astype(stage_ref.dtype)
                return carry
            lax.fori_loop(0, (cursor + mt - 1) // mt, tile, 0)
            for sb, st, nch, at in placed:
                copy_rows(stage_ref, at, xs_ref.at[sb], st, nch)

    @pl.when(step == pl.num_programs(1) - 1)
    def _():
        for sb in range(nsub):
            tok = slice(sb * sub, (sb + 1) * sub)
            hs = h_ref[tok, :]
            shared = _dot(_silu(_dot(hs, sg_ref[...])) * _dot(hs, su_ref[...]), sd_ref[...])
            dest_t = jnp.transpose(dest_ref[:, tok].astype(F32))
            wgt_t = jnp.transpose(wgt_ref[:, tok])
            routed = jnp.zeros((sub, x_ref.shape[1]), F32)
            for r in range(rmax // rt):
                cols = (r * rt + lax.broadcasted_iota(jnp.int32, (sub, rt), 1)).astype(F32)
                mix = jnp.zeros((sub, rt), F32)
                for j in range(TOP_K):
                    mix = jnp.where(dest_t[:, j:j + 1] == cols, wgt_t[:, j:j + 1], mix)
                routed = routed + _dot(mix, xs_ref[sb, r * rt:(r + 1) * rt, :])
            xo = x_ref[tok, :] + g2_ref[0] * (shared + routed)
            if final:
                xo = _rms(xo, nf_ref[...])
            o_ref[tok, :] = xo


def _round_up(n, m):
    return (n + m - 1) // m * m


def _moe(x2, mod3, norm_g, dest, wgt, starts, plens, wg, wu, wd, sg, su, sd, nf, seq, final):
    t, d = x2.shape
    tm = min(MOE_BLOCK, seq)
    nsub = tm // MOE_SUB
    ne, ff = wg.shape[0], wg.shape[2]
    rmax = _round_up(TOP_K * MOE_SUB + ne * (ROW_ALIGN - 1) + MOE_WINDOW, MOE_ROW_TILE)
    stage_rows = _round_up(tm + nsub * (ROW_ALIGN - 1), MOE_FFN_TILE)
    eps = MOE_EXPERTS_PER_STEP
    full = lambda a: pl.BlockSpec(a.shape, lambda i, e, *_: (0,) * a.ndim)
    once = lambda a: pl.BlockSpec(a.shape, lambda i, e, *_: (0,) * a.ndim, pipeline_mode=pl.Buffered(1))
    mod = lambda j: pl.BlockSpec((1, 1, d), lambda i, e, *_: ((i * tm) // seq, 0, j))
    slot = pl.BlockSpec((SLOT_ROWS, tm), lambda i, e, *_: (0, i))
    grid_spec = pltpu.PrefetchScalarGridSpec(
        num_scalar_prefetch=2,
        grid=(t // tm, ne // eps),
        in_specs=[pl.BlockSpec((tm, d), lambda i, e, *_: (i, 0), pipeline_mode=pl.Buffered(1)),
                  mod(4), mod(3), mod(5),
                  pl.BlockSpec((1, d), lambda i, e, *_: (0, 0)), slot, slot,
                  pl.BlockSpec((eps, d, ff), lambda i, e, *_: (e, 0, 0)),
                  pl.BlockSpec((eps, d, ff), lambda i, e, *_: (e, 0, 0)),
                  pl.BlockSpec((eps, ff, d), lambda i, e, *_: (e, 0, 0)),
                  once(sg), once(su), once(sd), full(nf)],
        out_specs=pl.BlockSpec((tm, d), lambda i, e, *_: (i, 0), pipeline_mode=pl.Buffered(1)),
        scratch_shapes=[pltpu.VMEM((tm, d), MXU_DTYPE),
                        pltpu.VMEM((nsub, rmax, d), MXU_DTYPE),
                        pltpu.VMEM((stage_rows, d), MXU_DTYPE)])
    return pl.pallas_call(
        functools.partial(_moe_kernel, final=final), name="moe",
        grid_spec=grid_spec,
        out_shape=jax.ShapeDtypeStruct((t, d), F32),
        compiler_params=_params(("parallel", "arbitrary"), vmem_mb=58),
    )(starts.reshape(-1), plens.reshape(-1), x2, mod3, mod3, mod3, norm_g, dest, wgt,
      wg, wu, wd, sg, su, sd, nf)


def _moe_stream_kernel(starts_ref, plens_ref, x_ref, sc_ref, sh_ref, g2_ref, gn_ref, dest_ref, wgt_ref,
                       wg_ref, wu_ref, wd_ref, sg_ref, su_ref, sd_ref, nf_ref, o_ref, h_ref, *, final):
    blk, step = pl.program_id(0), pl.program_id(1)
    eps = wg_ref.shape[0]
    ne = pl.num_programs(1) * eps
    sub, win = MOE_SUB, MOE_WINDOW
    nsub = x_ref.shape[0] // sub

    @pl.when(step == 0)
    def _():
        h_ref[...] = _mx(_rms(x_ref[...], gn_ref[...]) * (1.0 + sc_ref[0]) + sh_ref[0])
        o_ref[...] = jnp.zeros_like(o_ref)

    def run(sb, j):
        i = (blk * nsub + sb) * ne + step * eps + j
        return starts_ref[i], plens_ref[i]

    runs = [[run(sb, j) for j in range(eps)] for sb in range(nsub)]
    r_col = lax.broadcasted_iota(jnp.int32, (win, 1), 0)
    r_row = lax.broadcasted_iota(jnp.int32, (1, win), 1)

    def window_pass(k):
        first = k * win
        picked = []
        for sb in range(nsub):
            tok = slice(sb * sub, (sb + 1) * sub)
            dest = dest_ref[:, tok]
            want = jnp.concatenate([st + first + r_col for st, _ in runs[sb]], axis=0)
            pick = jnp.zeros((eps * win, sub), F32)
            for s in range(TOP_K):
                pick = jnp.where(dest[s:s + 1, :] == want, 1.0, pick)
            picked.append(_mx(_dot(pick, h_ref[tok, :])))
        outs = []
        for j in range(eps):
            xj = jnp.concatenate([p[j * win:(j + 1) * win] for p in picked], axis=0)
            outs.append(_mx(_dot(_silu(_dot(xj, wg_ref[j])) * _dot(xj, wu_ref[j]), wd_ref[j])))
        for sb in range(nsub):
            tok = slice(sb * sub, (sb + 1) * sub)
            res = jnp.concatenate([o[sb * win:(sb + 1) * win] for o in outs], axis=0)
            have = jnp.concatenate([jnp.where(first + r_row < ln, st + first + r_row, -2)
                                    for st, ln in runs[sb]], axis=1).astype(F32)
            dest_t = jnp.transpose(dest_ref[:, tok].astype(F32))
            wgt_t = jnp.transpose(wgt_ref[:, tok])
            mix = jnp.zeros((sub, eps * win), F32)
            for s in range(TOP_K):
                mix = jnp.where(dest_t[:, s:s + 1] == have, wgt_t[:, s:s + 1], mix)
            o_ref[tok, :] += _dot(mix, res)

    window_pass(0)

    longest = runs[0][0][1]
    for per_sub in runs:
        for _, ln in per_sub:
            longest = jnp.maximum(longest, ln)

    def more(k):
        window_pass(k)
        return k + 1
    lax.while_loop(lambda k: k * win < longest, more, jnp.int32(1))

    @pl.when(step == pl.num_programs(1) - 1)
    def _():
        for sb in range(nsub):
            tok = slice(sb * sub, (sb + 1) * sub)
            hs = h_ref[tok, :]
            shared = _dot(_silu(_dot(hs, sg_ref[...])) * _dot(hs, su_ref[...]), sd_ref[...])
            xo = x_ref[tok, :] + g2_ref[0] * (shared + o_ref[tok, :])
            if final:
                xo = _rms(xo, nf_ref[...])
            o_ref[tok, :] = xo


def _moe_stream(x2, mod3, norm_g, dest, wgt, starts, plens, wg, wu, wd, sg, su, sd, nf, seq, final):
    t, d = x2.shape
    tm = min(MOE_STREAM_BLOCK, seq)
    ne, ff = wg.shape[0], wg.shape[2]
    eps = MOE_EXPERTS_PER_STEP
    full = lambda a: pl.BlockSpec(a.shape, lambda i, e, *_: (0,) * a.ndim)
    once = lambda a: pl.BlockSpec(a.shape, lambda i, e, *_: (0,) * a.ndim, pipeline_mode=pl.Buffered(1))
    mod = lambda j: pl.BlockSpec((1, 1, d), lambda i, e, *_: ((i * tm) // seq, 0, j))
    slot = pl.BlockSpec((SLOT_ROWS, tm), lambda i, e, *_: (0, i))
    grid_spec = pltpu.PrefetchScalarGridSpec(
        num_scalar_prefetch=2,
        grid=(t // tm, ne // eps),
        in_specs=[pl.BlockSpec((tm, d), lambda i, e, *_: (i, 0), pipeline_mode=pl.Buffered(1)),
                  mod(4), mod(3), mod(5),
                  pl.BlockSpec((1, d), lambda i, e, *_: (0, 0)), slot, slot,
                  pl.BlockSpec((eps, d, ff), lambda i, e, *_: (e, 0, 0)),
                  pl.BlockSpec((eps, d, ff), lambda i, e, *_: (e, 0, 0)),
                  pl.BlockSpec((eps, ff, d), lambda i, e, *_: (e, 0, 0)),
                  once(sg), once(su), once(sd), full(nf)],
        out_specs=pl.BlockSpec((tm, d), lambda i, e, *_: (i, 0), pipeline_mode=pl.Buffered(1)),
        scratch_shapes=[pltpu.VMEM((tm, d), MXU_DTYPE)])
    return pl.pallas_call(
        functools.partial(_moe_stream_kernel, final=final), name="moe",
        grid_spec=grid_spec,
        out_shape=jax.ShapeDtypeStruct((t, d), F32),
        compiler_params=_params(("parallel", "arbitrary"), vmem_mb=61),
    )(starts.reshape(-1), plens.reshape(-1), x2, mod3, mod3, mod3, norm_g, dest, wgt,
      wg, wu, wd, sg, su, sd, nf)


def _pack_w_in(w):
    d = w.shape[0]
    offs = [0]
    for n in IN_SIZES:
        offs.append(offs[-1] + n)
    (qa, ka, va, ga, alr, cq, ckv, kidx, widx, qc, kc, vc, ic, fc, oc, gates) = [
        w[:, offs[i]:offs[i + 1]] for i in range(len(IN_SIZES))]
    z = lambda n: jnp.zeros((d, n), w.dtype)
    packed = jnp.concatenate(
        [gates, qa, ka, va, ga, vc, oc, qc, kc, cq, ckv,
         kidx, widx, z(LANE - IDX_DIM - IDX_HEADS),
         alr, z(LANE - GLA_GATE_RANK),
         ic, fc, z(LANE - 2 * MLSTM_HEADS)], axis=1)
    assert packed.shape[1] == N_PACK
    return packed.astype(MXU_DTYPE)


def kernel(x, c, ada_w, ada_b, norm_mix, norm_ffn, w_in, gla_w_a2, gla_b_a, gla_norm, dsa_norm_q,
           dsa_norm_kv, dsa_w_uq, dsa_w_uk, dsa_w_uv, dsa_w_qi, mlstm_conv, mlstm_b_i, mlstm_b_f,
           mlstm_norm, w_up_a, w_up_b, w_up_c, w_o, router_w, router_bias, exp_w_gate, exp_w_up,
           exp_w_down, sh_w_gate, sh_w_up, sh_w_down, norm_final):
    b, s, d = x.shape
    depth = ada_w.shape[0]
    t = b * s
    mod = _ada_mod(c, ada_w, ada_b)
    x2 = x.reshape(t, d)
    row = lambda v: v.reshape(1, -1)
    for l in range(depth):
        mod3 = mod[l].reshape(b, 1, 6 * d)
        w_pack = _pack_w_in(w_in[l])
        proj2, tail2 = _in_proj(x2, mod3, row(norm_mix[l]), w_pack, s)
        proj3 = proj2.reshape(b, s, N_MAIN)
        tail3 = tail2.reshape(b, s, N_TAIL)

        wa2_pad = jnp.zeros((LANE, GLA_QK), F32).at[:GLA_GATE_RANK].set(gla_w_a2[l])
        ya_in = _gla(proj3, tail3, wa2_pad, row(gla_b_a[l]), row(gla_norm[l]))

        bias_row = jnp.zeros((1, LANE), F32).at[0, :MLSTM_HEADS].set(mlstm_b_i[l])
        bias_row = bias_row.at[0, MLSTM_HEADS:2 * MLSTM_HEADS].set(mlstm_b_f[l])
        yc_in = _mlstm(proj3, tail3, mlstm_conv[l], bias_row, row(mlstm_norm[l]))

        wuq = dsa_w_uq[l].reshape(DSA_Q_RANK, DSA_HEADS * DSA_HEAD_DIM).astype(MXU_DTYPE)
        wuk_t = jnp.transpose(dsa_w_uk[l], (1, 2, 0)).astype(MXU_DTYPE)
        wuv_t = jnp.transpose(dsa_w_uv[l], (1, 2, 0)).astype(MXU_DTYPE)
        wqi_pad = jnp.zeros((DSA_Q_RANK, IDX_HEADS, LANE), F32).at[:, :, :IDX_DIM].set(dsa_w_qi[l])
        wqi_pad = wqi_pad.reshape(DSA_Q_RANK, IDX_HEADS * LANE).astype(MXU_DTYPE)
        ckvn, kwb, qlat, qidx, wht = _dsa_prep(proj2, tail2, s, row(dsa_norm_q[l]), row(dsa_norm_kv[l]),
                                               wuq, wuk_t, wqi_pad)
        yb_in = _dsa(ckvn.reshape(b, s, DSA_AUG), kwb.reshape(b, s, LANE), qlat, qidx, wht, wuv_t)

        x2 = _merge(ya_in.reshape(t, GLA_V), yb_in.reshape(t, DSA_V), yc_in.reshape(t, MLSTM_V),
                    proj2, x2, mod3, w_up_a[l].astype(MXU_DTYPE), w_up_b[l].astype(MXU_DTYPE),
                    w_up_c[l].astype(MXU_DTYPE), w_o[l].astype(MXU_DTYPE), s)

        dest, wgt, starts, plens = _router(x2, mod3, row(norm_ffn[l]), jnp.transpose(router_w[l]),
                                           router_bias[l].reshape(-1, 1), s)
        x2 = _moe_stream(x2, mod3, row(norm_ffn[l]), dest, wgt, starts, plens,
                  exp_w_gate[l].astype(MXU_DTYPE), exp_w_up[l].astype(MXU_DTYPE),
                  exp_w_down[l].astype(MXU_DTYPE), sh_w_gate[l].astype(MXU_DTYPE),
                  sh_w_up[l].astype(MXU_DTYPE), sh_w_down[l].astype(MXU_DTYPE), row(norm_final), s,
                  final=(l == depth - 1))
    return x2.reshape(b, s, d)
```

```python
import functools
import struct

import jax
import jax.numpy as jnp
from jax import lax
from jax.experimental import pallas as pl
from jax.experimental.pallas import tpu as pltpu

F32 = jnp.float32
BF16 = jnp.bfloat16
MXU_DTYPE = jnp.bfloat16
ACT_DTYPE = jnp.bfloat16
HIGHEST = lax.Precision.HIGHEST

EPS = 1e-6
D_MODEL = 1024
GLA_HEADS, GLA_DK, GLA_DV, GLA_GATE_RANK, GLA_TAU, GLA_CHUNK = 4, 64, 128, 16, 16.0, 64
GLA_SUB = 16
DSA_HEADS, DSA_Q_RANK, DSA_KV_RANK, DSA_HEAD_DIM, DSA_V_DIM = 8, 256, 128, 64, 64
IDX_HEADS, IDX_DIM, DSA_TOPK_MAX = 8, 32, 256
MLSTM_HEADS, MLSTM_DQK, MLSTM_DV, MLSTM_CONV, MLSTM_CHUNK = 4, 64, 128, 4, 64
N_EXPERTS, TOP_K, N_GROUPS, TOPK_GROUPS, EXPERT_FF, SHARED_FF = 64, 6, 8, 4, 256, 256
ROUTED_SCALE = 2.5

GLA_QK = GLA_HEADS * GLA_DK
GLA_V = GLA_HEADS * GLA_DV
DSA_V = DSA_HEADS * DSA_V_DIM
MLSTM_QK = MLSTM_HEADS * MLSTM_DQK
MLSTM_V = MLSTM_HEADS * MLSTM_DV
IN_SIZES = (GLA_QK, GLA_QK, GLA_V, GLA_V, GLA_GATE_RANK,
            DSA_Q_RANK, DSA_KV_RANK, IDX_DIM, IDX_HEADS,
            MLSTM_QK, MLSTM_QK, MLSTM_V, MLSTM_HEADS, MLSTM_HEADS, MLSTM_V,
            3 * D_MODEL)

LANE = 128
KEY_TILE = 256
Q_TILE = 256
ATT_HEAD_GROUP = 4
NEG_BIG = -1e30
MOE_STREAM_BLOCK = 2048
MOE_SUB = 256
MOE_WINDOW = 48
MOE_EXPERTS_PER_STEP = 4
CONV_PAD = 8
SEQ_GROUP = 16
SLOT_ROWS = 8
ROW_ALIGN = 16

C_GATES = 0
C_QA = 3072
C_KA = 3328
C_VA = 3584
C_GA = 4096
C_VC = 4608
C_OC = 5120
C_QKC = 5632
C_CQ = 6144
N_MAIN = 6400
T_CKV = 0
T_KW = 128
T_ALR = 256
T_ICFC = 384
N_TAIL = 512
N_PACK = N_MAIN + N_TAIL
INPROJ_PIECES = 5
W_IDX_LANE = IDX_DIM


LOG2E = 1.4426950408889634
DSA_AUG = DSA_KV_RANK + LANE
POS_SPLIT = 64
SLOPE_PIECES = 3
SUBLANE = 8
PARTIAL_ROWS = 32
COUNT_UNSET = float(2 ** 30)


def _bf16_pieces(x, n):
    out = []
    for _ in range(n):
        bits = struct.unpack("<I", struct.pack("<f", x))[0]
        bits = (bits + 0x7FFF + ((bits >> 16) & 1)) & 0xFFFF0000
        piece = struct.unpack("<f", struct.pack("<I", bits))[0]
        out.append(piece)
        x -= piece
    return out


def _mx(x):
    return x.astype(MXU_DTYPE)


def _dot(a, b):
    return jnp.dot(_mx(a), _mx(b), preferred_element_type=F32)


def _dot_nt(a, b):
    return lax.dot_general(_mx(a), _mx(b), (((1,), (1,)), ((), ())), preferred_element_type=F32)


def _dot_tn(a, b):
    return lax.dot_general(_mx(a), _mx(b), (((0,), (0,)), ((), ())), preferred_element_type=F32)


def _rms(x, g):
    return x * lax.rsqrt(jnp.mean(x * x, axis=-1, keepdims=True) + EPS) * g


def _silu(x):
    return x * jax.nn.sigmoid(x)


def _log_sigmoid(z):
    return jnp.minimum(z, 0.0) - jnp.log1p(jnp.exp(-jnp.abs(z)))


def _cumsum_rows(x):
    n = x.shape[0]
    tri = (lax.broadcasted_iota(jnp.int32, (n, n), 1) <= lax.broadcasted_iota(jnp.int32, (n, n), 0)).astype(F32)
    return jnp.dot(tri, x, preferred_element_type=F32, precision=HIGHEST)


def _truncate_to_bf16(x):
    bits = lax.bitcast_convert_type(x, jnp.int32) & jnp.int32(-65536)
    return lax.bitcast_convert_type(bits, F32).astype(BF16)


def _colreduce(x, op, width=PARTIAL_ROWS):
    n, c = x.shape
    return op(op(x.reshape(n // width, width, c), axis=0), axis=0, keepdims=True)


def _params(sem, vmem_mb=40):
    return pltpu.CompilerParams(dimension_semantics=sem, vmem_limit_bytes=vmem_mb * 1024 * 1024)


def _ada_kernel(c_ref, w_ref, b_ref, o_ref):
    cs = _silu(c_ref[...])
    o_ref[0] = jnp.dot(cs, w_ref[0], preferred_element_type=F32, precision=HIGHEST) + b_ref[0]


def _ada_mod(c, ada_w, ada_b):
    depth, d, n = ada_w.shape
    b = c.shape[0]
    return pl.pallas_call(
        _ada_kernel, name="ada_mod",
        grid=(depth, n // d),
        in_specs=[pl.BlockSpec((b, d), lambda l, j: (0, 0)),
                  pl.BlockSpec((1, d, d), lambda l, j: (l, 0, j)),
                  pl.BlockSpec((1, 1, d), lambda l, j: (l, 0, j))],
        out_specs=pl.BlockSpec((1, b, d), lambda l, j: (l, 0, j)),
        out_shape=jax.ShapeDtypeStruct((depth, b, n), F32),
        compiler_params=_params(("parallel", "parallel")),
    )(c, ada_w, ada_b.reshape(depth, 1, n))


def _inproj_kernel(x_ref, sc_ref, sh_ref, g_ref, wm_ref, wt_ref, om_ref, ot_ref):
    h = _mx(_rms(x_ref[...], g_ref[...]) * (1.0 + sc_ref[0]) + sh_ref[0])
    piece = N_MAIN // INPROJ_PIECES
    for j in range(INPROJ_PIECES):
        cols = slice(j * piece, (j + 1) * piece)
        om_ref[:, cols] = jnp.dot(h, wm_ref[:, cols], preferred_element_type=F32).astype(om_ref.dtype)
    ot_ref[...] = jnp.dot(h, wt_ref[...], preferred_element_type=F32)


def _in_proj(x2, mod3, norm_g, w_pack, seq):
    t, d = x2.shape
    tm = min(512, seq)
    once = lambda w: pl.BlockSpec((d, w), lambda i: (0, 0), pipeline_mode=pl.Buffered(1))
    return pl.pallas_call(
        _inproj_kernel, name="in_proj",
        grid=(t // tm,),
        in_specs=[pl.BlockSpec((tm, d), lambda i: (i, 0)),
                  pl.BlockSpec((1, 1, d), lambda i: ((i * tm) // seq, 0, 1)),
                  pl.BlockSpec((1, 1, d), lambda i: ((i * tm) // seq, 0, 0)),
                  pl.BlockSpec((1, d), lambda i: (0, 0)),
                  once(N_MAIN), once(N_TAIL)],
        out_specs=[pl.BlockSpec((tm, N_MAIN), lambda i: (i, 0)),
                   pl.BlockSpec((tm, N_TAIL), lambda i: (i, 0))],
        out_shape=[jax.ShapeDtypeStruct((t, N_MAIN), ACT_DTYPE),
                   jax.ShapeDtypeStruct((t, N_TAIL), F32)],
        compiler_params=_params(("parallel",), vmem_mb=48),
    )(x2, mod3, mod3, norm_g, w_pack[:, :N_MAIN], w_pack[:, N_MAIN:])


def _gla_kernel(q_ref, k_ref, v_ref, g_ref, alr_ref, wa2_ref, ba_ref, gn_ref, o_ref, s_ref, acc_ref):
    @pl.when(pl.program_id(1) == 0)
    def _():
        s_ref[...] = jnp.zeros_like(s_ref)

    L, sub, nh, dk, dv = GLA_CHUNK, GLA_SUB, GLA_HEADS, GLA_DK, GLA_DV
    seqs = range(q_ref.shape[0])
    heads = range(nh)
    hk = lambda h: slice(h * dk, (h + 1) * dk)
    hv = lambda h: slice(h * dv, (h + 1) * dv)

    pre = []
    for g in seqs:
        z = jnp.dot(alr_ref[g], wa2_ref[...], preferred_element_type=F32, precision=HIGHEST) + ba_ref[...]
        cum = _cumsum_rows(_log_sigmoid(z) * (1.0 / GLA_TAU))
        q = q_ref[g].astype(F32) * (dk ** -0.5)
        k = k_ref[g].astype(F32)
        tot = cum[L - 1:L, :]
        pre.append(dict(cum=cum, q=q, k=k, tot=tot, vb=_mx(v_ref[g]),
                        q_in=_mx(q * jnp.exp(cum)), k_dec=_mx(k * jnp.exp(tot - cum))))

    scores = {}
    for i in range(L // sub):
        r0, r1 = i * sub, (i + 1) * sub
        for g in seqs:
            p = pre[g]
            base = p["cum"][r0 - 1:r0, :] if i > 0 else jnp.zeros_like(p["tot"])
            qi = _mx(p["q"][r0:r1] * jnp.exp(p["cum"][r0:r1] - base))
            ka = _mx(p["k"][:r1] * jnp.exp(base - p["cum"][:r1]))
            for h in heads:
                scores[g, i, h] = _dot_nt(qi[:, hk(h)], ka[:, hk(h)])

    for i in range(L // sub):
        r0, r1 = i * sub, (i + 1) * sub
        causal = (lax.broadcasted_iota(jnp.int32, (sub, r1), 1)
                  <= lax.broadcasted_iota(jnp.int32, (sub, r1), 0) + r0)
        for g in seqs:
            for h in heads:
                s = jnp.where(causal, scores[g, i, h], 0.0)
                acc_ref[g, r0:r1, hv(h)] = _dot(s, pre[g]["vb"][:r1, hv(h)])

    inter = {(g, h): _dot(pre[g]["q_in"][:, hk(h)], s_ref[g * nh + h]) for g in seqs for h in heads}
    update = {(g, h): _dot_tn(pre[g]["k_dec"][:, hk(h)], pre[g]["vb"][:, hv(h)]) for g in seqs for h in heads}

    gn = gn_ref[...]
    for g in seqs:
        gate = g_ref[g].astype(F32)
        for h in heads:
            o = acc_ref[g, :, hv(h)] + inter[g, h]
            y = _rms(o, gn[:, hv(h)]) * _silu(gate[:, hv(h)])
            o_ref[g, :, hv(h)] = y.astype(o_ref.dtype)
            decay = jnp.transpose(jnp.exp(pre[g]["tot"][:, hk(h)]))
            s_ref[g * nh + h] = s_ref[g * nh + h] * decay + update[g, h]


def _seq_group(batch):
    for grp in (SEQ_GROUP, 2, 1):
        if batch % grp == 0:
            return grp


def _gla(proj3, tail3, wa2_pad, ba, gn):
    b, s, _ = proj3.shape
    L = GLA_CHUNK
    grp = _seq_group(b)
    blk = lambda w, c0: pl.BlockSpec((grp, L, w), lambda bi, ci: (bi, ci, c0 // w))
    full = lambda a: pl.BlockSpec(a.shape, lambda bi, ci: (0,) * a.ndim)
    return pl.pallas_call(
        _gla_kernel, name="gla",
        grid=(b // grp, s // L),
        in_specs=[blk(GLA_QK, C_QA), blk(GLA_QK, C_KA), blk(GLA_V, C_VA), blk(GLA_V, C_GA),
                  blk(LANE, T_ALR), full(wa2_pad), full(ba), full(gn)],
        out_specs=pl.BlockSpec((grp, L, GLA_V), lambda bi, ci: (bi, ci, 0)),
        out_shape=jax.ShapeDtypeStruct((b, s, GLA_V), ACT_DTYPE),
        scratch_shapes=[pltpu.VMEM((grp * GLA_HEADS, GLA_DK, GLA_DV), F32),
                        pltpu.VMEM((grp, L, GLA_V), F32)],
        compiler_params=_params(("parallel", "arbitrary")),
    )(proj3, proj3, proj3, proj3, tail3, wa2_pad, ba, gn)


def _mlstm_pair_kernel(qk_ref, v_ref, oc_ref, if_ref, conv_ref, bias_ref, gnt_ref, o_ref,
                       xbuf_ref, ct_ref, n_ref, m_ref):
    L, nh, dk, dv, kc = MLSTM_CHUNK, MLSTM_HEADS, MLSTM_DQK, MLSTM_DV, MLSTM_CONV
    pad = CONV_PAD
    npair = nh // 2
    assert 2 * dk == LANE and dv == LANE and L == dk

    @pl.when(pl.program_id(1) == 0)
    def _():
        xbuf_ref[:, 0:pad, :] = jnp.zeros((xbuf_ref.shape[0], pad, 2 * MLSTM_QK), F32)
        ct_ref[...] = jnp.zeros_like(ct_ref)
        n_ref[...] = jnp.zeros_like(n_ref)
        m_ref[...] = jnp.zeros_like(m_ref)

    lane = lax.broadcasted_iota(jnp.int32, (1, LANE), 1)
    half = [(lane < dk).astype(F32), (lane >= dk).astype(F32)]
    s_idx = lax.broadcasted_iota(jnp.int32, (L, LANE), 0)
    t_idx = lax.broadcasted_iota(jnp.int32, (L, LANE), 1) % L
    causal = s_idx <= t_idx
    lane_in = lax.broadcasted_iota(jnp.int32, (LANE, LANE), 0)
    head_of = lax.broadcasted_iota(jnp.int32, (LANE, LANE), 1) // dk
    cw = conv_ref[...]

    combos = [(g, p) for g in range(qk_ref.shape[0]) for p in range(npair)]
    seq = {}
    for g in range(qk_ref.shape[0]):
        xbuf_ref[g, pad:pad + L, :] = qk_ref[g].astype(F32)
        conv = jnp.zeros((L, 2 * MLSTM_QK), F32)
        for j in range(kc):
            conv = conv + cw[j:j + 1, :] * xbuf_ref[g, pl.ds(pad - (kc - 1) + j, L), :]
        xbuf_ref[g, 0:pad, :] = xbuf_ref[g, L:L + pad, :]
        qk = _silu(conv)
        pre = if_ref[g] + bias_ref[...]
        bcum = _cumsum_rows(_log_sigmoid(pre))
        v = v_ref[g].astype(F32)
        seq[g] = dict(q=qk[:, :MLSTM_QK] * (dk ** -0.5), k=qk[:, MLSTM_QK:],
                      vt=[_mx(jnp.transpose(v[:, h * dv:(h + 1) * dv])) for h in range(nh)],
                      gate_mix=jnp.where(lane < nh, pre, -bcum), bcum=bcum, bcum_t=jnp.transpose(bcum))

    st = {}
    for g, p in combos:
        sq = seq[g]
        sel = jnp.where((lane_in == 2 * p + head_of) | (lane_in == nh + 2 * p + head_of), 1.0, 0.0)
        selb = jnp.where(lane_in == nh + 2 * p + head_of, 1.0, 0.0)
        d_mat = jnp.dot(sq["gate_mix"], sel, preferred_element_type=F32, precision=HIGHEST)
        tot = jnp.dot(sq["bcum"][L - SUBLANE:L, :], selb, preferred_element_type=F32,
                      precision=HIGHEST)[SUBLANE - 1:SUBLANE, :]
        b_row = jnp.concatenate([sq["bcum_t"][nh + 2 * p:nh + 2 * p + 1, :],
                                 sq["bcum_t"][nh + 2 * p + 1:nh + 2 * p + 2, :]], axis=1)
        qt = sq["q"][:, p * LANE:(p + 1) * LANE]
        kt = sq["k"][:, p * LANE:(p + 1) * LANE]
        q_bd = _mx(jnp.concatenate([qt * half[0], qt * half[1]], axis=0))
        st[g, p] = dict(d=d_mat, tot=tot, b_row=b_row, kt=kt, q_bd=q_bd,
                        scores=_dot_nt(kt, q_bd),
                        qn=_dot_nt(jnp.broadcast_to(n_ref[g * npair + p], (SUBLANE, LANE)), q_bd)[0:1, :],
                        inter=[_dot_nt(ct_ref[g * nh + 2 * p + hh], q_bd) for hh in range(2)])

    for g, p in combos:
        c = st[g, p]
        m_prev = m_ref[g * npair + p]
        dlog = jnp.where(causal, c["b_row"] + c["d"], -jnp.inf)
        inter_log = c["b_row"] + m_prev
        m_t = jnp.maximum(inter_log, jnp.max(dlog, axis=0, keepdims=True))
        g_log = c["tot"] + c["d"]
        m_new = jnp.maximum(c["tot"] + m_prev, jnp.max(g_log, axis=0, keepdims=True))
        c.update(m_t=m_t, w_inter=jnp.exp(inter_log - m_t), sw=c["scores"] * jnp.exp(dlog - m_t),
                 m_new=m_new, w_c=jnp.exp(c["tot"] + m_prev - m_new), ks=c["kt"] * jnp.exp(g_log - m_new))

    for g, p in combos:
        c = st[g, p]
        vt = seq[g]["vt"]
        c["sv"] = [_dot(vt[2 * p + hh], c["sw"] * half[hh]) for hh in range(2)]
        c["kv"] = [_dot(vt[2 * p + hh], c["ks"] * half[hh]) for hh in range(2)]

    for g, p in combos:
        c = st[g, p]
        num = c["sv"][0] + c["sv"][1] + c["w_inter"] * (c["inter"][0] + c["inter"][1])
        den = jnp.sum(c["sw"], axis=0, keepdims=True) + c["w_inter"] * c["qn"]
        hout = num / jnp.maximum(jnp.abs(den), jnp.exp(-c["m_t"]))
        y_t = hout * lax.rsqrt(jnp.mean(hout * hout, axis=0, keepdims=True) + EPS) * gnt_ref[p]
        y = jnp.transpose(y_t)
        for hh in range(2):
            cols = slice((2 * p + hh) * dv, (2 * p + hh + 1) * dv)
            o_ref[g, :, cols] = (y[hh * L:(hh + 1) * L, :] * jax.nn.sigmoid(oc_ref[g, :, cols].astype(F32))).astype(o_ref.dtype)
            ct_ref[g * nh + 2 * p + hh] = ct_ref[g * nh + 2 * p + hh] * c["w_c"] + c["kv"][hh]
        n_ref[g * npair + p] = n_ref[g * npair + p] * c["w_c"] + jnp.sum(c["ks"], axis=0, keepdims=True)
        m_ref[g * npair + p] = c["m_new"]


def _mlstm(proj3, tail3, conv_w, bias_row, gn):
    b, s, _ = proj3.shape
    L = MLSTM_CHUNK
    grp = _seq_group(b)
    blk = lambda w, c0: pl.BlockSpec((grp, L, w), lambda bi, ci: (bi, ci, c0 // w))
    full = lambda a: pl.BlockSpec(a.shape, lambda bi, ci: (0,) * a.ndim)
    npair = MLSTM_HEADS // 2
    gnt = jnp.repeat(jnp.transpose(gn.reshape(npair, 2, MLSTM_DV), (0, 2, 1)), L, axis=2)
    return pl.pallas_call(
        _mlstm_pair_kernel, name="mlstm",
        grid=(b // grp, s // L),
        in_specs=[blk(2 * MLSTM_QK, C_QKC), blk(MLSTM_V, C_VC), blk(MLSTM_V, C_OC), blk(LANE, T_ICFC),
                  full(conv_w), full(bias_row), full(gnt)],
        out_specs=pl.BlockSpec((grp, L, MLSTM_V), lambda bi, ci: (bi, ci, 0)),
        out_shape=jax.ShapeDtypeStruct((b, s, MLSTM_V), ACT_DTYPE),
        scratch_shapes=[pltpu.VMEM((grp, L + CONV_PAD, 2 * MLSTM_QK), F32),
                        pltpu.VMEM((grp * MLSTM_HEADS, MLSTM_DV, LANE), F32),
                        pltpu.VMEM((grp * npair, 1, LANE), F32),
                        pltpu.VMEM((grp * npair, 1, LANE), F32)],
        compiler_params=_params(("parallel", "arbitrary")),
    )(proj3, proj3, proj3, tail3, conv_w, bias_row, gnt)


def _dsa_prep_kernel(cq_ref, ckv_ref, kw_ref, nq_ref, nkv_ref, wuq_ref, wuk_ref, wqi_ref,
                     ckvn_ref, kwb_ref, qlat_ref, qidx_ref, wht_ref, *, tiles_per_seq):
    tm = cq_ref.shape[0]
    r = DSA_KV_RANK
    cqn = _mx(_rms(cq_ref[...].astype(F32), nq_ref[...]))
    pos = (pl.program_id(0) % tiles_per_seq) * tm + lax.broadcasted_iota(jnp.int32, (tm, LANE), 0)
    lane = lax.broadcasted_iota(jnp.int32, (tm, LANE), 1)
    pos_cols = jnp.where(lane < SLOPE_PIECES, pos >> (POS_SPLIT.bit_length() - 1),
                         jnp.where(lane < 2 * SLOPE_PIECES, pos & (POS_SPLIT - 1), 0)).astype(F32)
    ckvn_ref[:, :r] = _rms(ckv_ref[...], nkv_ref[...]).astype(ckvn_ref.dtype)
    ckvn_ref[:, r:] = pos_cols.astype(ckvn_ref.dtype)
    kw = kw_ref[...]
    kwb_ref[...] = kw.astype(kwb_ref.dtype)
    wht_ref[0] = jnp.transpose(kw)[W_IDX_LANE:W_IDX_LANE + IDX_HEADS, :] * (IDX_HEADS ** -0.5)
    q = jnp.dot(cqn, wuq_ref[...], preferred_element_type=F32)
    for h in range(DSA_HEADS):
        ql = _dot(q[:, h * DSA_HEAD_DIM:(h + 1) * DSA_HEAD_DIM], wuk_ref[h]) * (DSA_HEAD_DIM ** -0.5 * LOG2E)
        qlat_ref[0, h, :, :r] = ql.astype(qlat_ref.dtype)
        c = _bf16_pieces(2.0 ** (-8.0 * (h + 1) / DSA_HEADS) * LOG2E, SLOPE_PIECES)
        consts = [POS_SPLIT * v for v in c] + c
        slope_cols = jnp.zeros((tm, LANE), F32)
        for j, v in enumerate(consts):
            slope_cols = jnp.where(lane == j, v, slope_cols)
        qlat_ref[0, h, :, r:] = slope_cols.astype(qlat_ref.dtype)
    qi = jnp.dot(cqn, wqi_ref[...], preferred_element_type=F32) * (IDX_DIM ** -0.5)
    for h in range(IDX_HEADS):
        qidx_ref[0, h] = qi[:, h * LANE:(h + 1) * LANE].astype(qidx_ref.dtype)


def _dsa_prep(proj2, tail2, seq, nq, nkv, wuq, wuk_t, wqi_pad):
    t = proj2.shape[0]
    b = t // seq
    tm = min(512, seq)
    per = seq // tm
    blk = lambda w, c0: pl.BlockSpec((tm, w), lambda i: (i, c0 // w))
    full = lambda a: pl.BlockSpec(a.shape, lambda i: (0,) * a.ndim)
    hmap = lambda i: (i // per, 0, i % per, 0)
    return pl.pallas_call(
        functools.partial(_dsa_prep_kernel, tiles_per_seq=per), name="dsa_prep",
        grid=(t // tm,),
        in_specs=[blk(DSA_Q_RANK, C_CQ), blk(DSA_KV_RANK, T_CKV), blk(LANE, T_KW),
                  full(nq), full(nkv), full(wuq), full(wuk_t), full(wqi_pad)],
        out_specs=[pl.BlockSpec((tm, DSA_AUG), lambda i: (i, 0)),
                   pl.BlockSpec((tm, LANE), lambda i: (i, 0)),
                   pl.BlockSpec((1, DSA_HEADS, tm, DSA_AUG), hmap),
                   pl.BlockSpec((1, IDX_HEADS, tm, LANE), hmap),
                   pl.BlockSpec((1, IDX_HEADS, tm), lambda i: (i // per, 0, i % per))],
        out_shape=[jax.ShapeDtypeStruct((t, DSA_AUG), ACT_DTYPE),
                   jax.ShapeDtypeStruct((t, LANE), ACT_DTYPE),
                   jax.ShapeDtypeStruct((b, DSA_HEADS, seq, DSA_AUG), ACT_DTYPE),
                   jax.ShapeDtypeStruct((b, IDX_HEADS, seq, LANE), ACT_DTYPE),
                   jax.ShapeDtypeStruct((b, IDX_HEADS, seq), F32)],
        compiler_params=_params(("parallel",)),
    )(proj2, tail2, tail2, nq, nkv, wuq, wuk_t, wqi_pad)


def _dsa_kernel(qlat_ref, qidx_ref, wht_ref, kwk_ref, ckv_ref, wuv_ref, o_ref,
                ibuf_ref, ihi_ref, acc_ref, *, topk):
    nh, tq, tk = DSA_HEADS, Q_TILE, KEY_TILE
    qb = pl.program_id(1)
    n_kt = (qb * tq + tq + tk - 1) // tk
    t_row = qb * tq + lax.broadcasted_iota(jnp.int32, (1, tq), 1)
    s_col = lax.broadcasted_iota(jnp.int32, (tk, 1), 0)

    qi = qidx_ref[0].reshape(IDX_HEADS * tq, LANE)
    wht = wht_ref[0]

    def idx_body(kt, carry):
        kk = kwk_ref[0, pl.ds(pl.multiple_of(kt * tk, tk), tk), :]
        grp = ATT_HEAD_GROUP
        sc = [_dot_nt(kk, qi[g * grp * tq:(g + 1) * grp * tq]) for g in range(IDX_HEADS // grp)]
        tot = None
        for h in range(IDX_HEADS):
            part = wht[h:h + 1, :] * jnp.maximum(sc[h // grp][:, (h % grp) * tq:(h % grp + 1) * tq], 0.0)
            tot = part if tot is None else tot + part
        score = jnp.where(kt * tk + s_col <= t_row, tot, -jnp.inf)
        ibuf_ref[kt] = score
        ihi_ref[kt] = _truncate_to_bf16(score)
        return carry

    lax.fori_loop(0, n_kt, idx_body, 0)

    def count(pred):
        def body(kt, c):
            hit = jnp.where(pred(ibuf_ref[kt]), 1.0, 0.0)
            return c + jnp.sum(hit.reshape(tk // PARTIAL_ROWS, PARTIAL_ROWS, tq), axis=0)
        return jnp.sum(lax.fori_loop(0, n_kt, body, jnp.zeros((PARTIAL_ROWS, tq), F32)), axis=0, keepdims=True)

    def count_ge(cand):
        return count(lambda x: x >= cand)

    def count_gt(cand):
        return count(lambda x: x > cand)

    def key_to_float(u):
        key = u ^ jnp.int32(-2 ** 31)
        bits = jnp.where(key >= 0, key, key ^ jnp.int32(0x7FFFFFFF))
        return lax.bitcast_convert_type(bits, F32)

    few = t_row < topk
    n_bits = 32

    def try_bit(i, u, cnt_u, counter):
        cand_u = u | lax.shift_left(jnp.int32(1), n_bits - 1 - i)
        cnt = counter(cand_u)
        ok = cnt >= topk
        return jnp.where(ok, cand_u, u), jnp.where(ok, cnt, cnt_u)

    def count_ge_hi(cand_u):
        cand = _truncate_to_bf16(key_to_float(cand_u))
        def body(kt, c):
            hit = jnp.where(ihi_ref[kt] >= cand, jnp.ones((), BF16), jnp.zeros((), BF16))
            part = jnp.sum(hit.reshape(tk // PARTIAL_ROWS, PARTIAL_ROWS, tq), axis=0)
            return c + part.astype(F32)
        return jnp.sum(lax.fori_loop(0, n_kt, body, jnp.zeros((PARTIAL_ROWS, tq), F32)), axis=0, keepdims=True)

    def coarse_body(i, st):
        return try_bit(i, *st, count_ge_hi)

    def search_cond(st):
        i, _, _, pending = st
        return (i < n_bits) & (pending > 0)

    def search_body(st):
        i, u, cnt_u, _ = st
        for _ in range(4):
            u, cnt_u = try_bit(i, u, cnt_u, lambda c: count_ge(key_to_float(c)))
            i = i + 1
        pending = jnp.max(jnp.where(few | (cnt_u == topk), 0, 1))
        return i, u, cnt_u, pending

    u, cnt_u = lax.fori_loop(0, n_bits // 2, coarse_body,
                             (jnp.zeros((1, tq), jnp.int32), jnp.full((1, tq), COUNT_UNSET, F32)))
    start = (jnp.int32(n_bits // 2), u, cnt_u, jnp.max(jnp.where(few | (cnt_u == topk), 0, 1)))
    _, u, _, _ = lax.while_loop(search_cond, search_body, start)
    tau = jnp.where(few, -jnp.inf, key_to_float(u))
    need = topk - count_gt(tau)

    acc_ref[...] = jnp.zeros_like(acc_ref)
    ql = qlat_ref[0].reshape(nh * tq, DSA_AUG)
    tri = jnp.where(lax.broadcasted_iota(jnp.int32, (tk, tk), 0) >= lax.broadcasted_iota(jnp.int32, (tk, tk), 1),
                    1.0, 0.0).astype(BF16)

    def keys(kt):
        return ckv_ref[0, pl.ds(pl.multiple_of(kt * tk, tk), tk), :]

    def att_body(kt, carry):
        m, l, eq_seen = carry
        kv_aug = keys(kt)
        kv = kv_aug[:, :DSA_KV_RANK]
        it = ibuf_ref[kt]
        eq = it == tau
        eqf = jnp.where(eq, 1.0, 0.0)
        rank = jnp.dot(tri, eqf.astype(BF16), preferred_element_type=F32) + eq_seen
        valid = ((it > tau) | (eq & (rank <= need))) & (kt * tk + s_col <= t_row)
        bias = jnp.where(valid, 0.0, NEG_BIG)

        n_grp = nh // ATT_HEAD_GROUP
        gcols = lambda g: slice(g * ATT_HEAD_GROUP * tq, (g + 1) * ATT_HEAD_GROUP * tq)
        lg, soft, ms, ls = {}, {}, {}, {}

        def logits(g):
            lg[g] = _dot_nt(kv_aug, ql[gcols(g)])

        def softmax(g):
            als, ps = [], []
            for h in range(g * ATT_HEAD_GROUP, (g + 1) * ATT_HEAD_GROUP):
                cols = slice(h * tq, (h + 1) * tq)
                loc = slice((h - g * ATT_HEAD_GROUP) * tq, (h - g * ATT_HEAD_GROUP + 1) * tq)
                lh = lg[g][:, loc] + bias
                m_old = m[:, cols]
                m_new = jnp.maximum(m_old, _colreduce(lh, jnp.max))
                p = jnp.exp2(lh - m_new)
                alpha = jnp.exp2(m_old - m_new)
                ms[h] = m_new
                ls[h] = alpha * l[:, cols] + _colreduce(p, jnp.sum)
                als.append(alpha)
                ps.append(_mx(p))
            soft[g] = (jnp.concatenate(als, axis=1), jnp.concatenate(ps, axis=1))

        def weighted_values(g):
            alpha, p = soft[g]
            acc_ref[:, gcols(g)] = alpha * acc_ref[:, gcols(g)] + _dot_tn(kv, p)

        logits(0)
        for g in range(n_grp):
            if g + 1 < n_grp:
                logits(g + 1)
            softmax(g)
            if g > 0:
                weighted_values(g - 1)
        weighted_values(n_grp - 1)
        return (jnp.concatenate([ms[h] for h in range(nh)], axis=1),
                jnp.concatenate([ls[h] for h in range(nh)], axis=1),
                eq_seen + jnp.sum(eqf, axis=0, keepdims=True))

    init = (jnp.full((1, nh * tq), NEG_BIG, F32), jnp.zeros((1, nh * tq), F32), jnp.zeros((1, tq), F32))
    _, l, _ = lax.fori_loop(0, n_kt, att_body, init)

    outs = []
    for h in range(nh):
        cols = slice(h * tq, (h + 1) * tq)
        outs.append(_dot(wuv_ref[h], acc_ref[:, cols] / l[:, cols]))
    o_ref[0] = jnp.transpose(jnp.concatenate(outs, axis=0)).astype(o_ref.dtype)


def _dsa(ckvn3, kwb3, qlat, qidx, wht, wuv_t):
    b, s, _ = ckvn3.shape
    topk = min(DSA_TOPK_MAX, s // 4)
    nh, tq = DSA_HEADS, Q_TILE
    return pl.pallas_call(
        functools.partial(_dsa_kernel, topk=topk), name="dsa_attn",
        grid=(b, s // tq),
        in_specs=[pl.BlockSpec((1, nh, tq, DSA_AUG), lambda bi, qi: (bi, 0, qi, 0)),
                  pl.BlockSpec((1, IDX_HEADS, tq, LANE), lambda bi, qi: (bi, 0, qi, 0)),
                  pl.BlockSpec((1, IDX_HEADS, tq), lambda bi, qi: (bi, 0, qi)),
                  pl.BlockSpec((1, s, LANE), lambda bi, qi: (bi, 0, 0)),
                  pl.BlockSpec((1, s, DSA_AUG), lambda bi, qi: (bi, 0, 0)),
                  pl.BlockSpec(wuv_t.shape, lambda bi, qi: (0, 0, 0))],
        out_specs=pl.BlockSpec((1, tq, DSA_V), lambda bi, qi: (bi, qi, 0)),
        out_shape=jax.ShapeDtypeStruct((b, s, DSA_V), ACT_DTYPE),
        scratch_shapes=[pltpu.VMEM((s // KEY_TILE, KEY_TILE, tq), F32),
                        pltpu.VMEM((s // KEY_TILE, KEY_TILE, tq), BF16),
                        pltpu.VMEM((DSA_KV_RANK, nh * tq), F32)],
        compiler_params=_params(("parallel", "arbitrary")),
    )(qlat, qidx, wht, kwb3, ckvn3, wuv_t)


def _merge_kernel(a_ref, b_ref, c_ref, gt_ref, x_ref, g1_ref, wa_ref, wb_ref, wc_ref, wo_ref, o_ref):
    d = D_MODEL
    g = jax.nn.sigmoid(gt_ref[...].astype(F32))
    ya = jnp.dot(a_ref[...], wa_ref[...], preferred_element_type=F32)
    yb = jnp.dot(b_ref[...], wb_ref[...], preferred_element_type=F32)
    yc = jnp.dot(c_ref[...], wc_ref[...], preferred_element_type=F32)
    m = g[:, :d] * ya + g[:, d:2 * d] * yb + g[:, 2 * d:] * yc
    o_ref[...] = x_ref[...] + g1_ref[0] * _dot(m, wo_ref[...])


def _merge(ya_in, yb_in, yc_in, proj2, x2, mod3, wa, wb, wc, wo, seq):
    t, d = x2.shape
    tm = min(512, seq)
    full = lambda a: pl.BlockSpec(a.shape, lambda i: (0,) * a.ndim)
    br = lambda w: pl.BlockSpec((tm, w), lambda i: (i, 0))
    return pl.pallas_call(
        _merge_kernel, name="merge",
        grid=(t // tm,),
        in_specs=[br(GLA_V), br(DSA_V), br(MLSTM_V), br(3 * d), br(d),
                  pl.BlockSpec((1, 1, d), lambda i: ((i * tm) // seq, 0, 2)),
                  full(wa), full(wb), full(wc), full(wo)],
        out_specs=br(d),
        out_shape=jax.ShapeDtypeStruct((t, d), F32),
        compiler_params=_params(("parallel",)),
    )(ya_in, yb_in, yc_in, proj2, x2, mod3, wa, wb, wc, wo)


def _first_argmax_mask(cur, iota, axis, n):
    mx = jnp.max(cur, axis=axis, keepdims=True)
    ix = jnp.min(jnp.where(cur == mx, iota, n), axis=axis, keepdims=True)
    return iota == ix


def _router_kernel(x_ref, sc_ref, sh_ref, g_ref, rwt_ref, rb_ref, dest_ref, wgt_ref, starts_ref, plens_ref):
    ne, ng = N_EXPERTS, N_GROUPS
    eg = ne // ng
    h = _rms(x_ref[...], g_ref[...]) * (1.0 + sc_ref[0]) + sh_ref[0]
    tm = h.shape[0]
    logits = lax.dot_general(rwt_ref[...], h, (((1,), (1,)), ((), ())),
                             preferred_element_type=F32, precision=HIGHEST)
    scores = jax.nn.sigmoid(logits)
    sel = scores + rb_ref[...]
    s3 = sel.reshape(ng, eg, tm)
    io3 = lax.broadcasted_iota(jnp.int32, (ng, eg, tm), 1)
    m1 = jnp.max(s3, axis=1, keepdims=True)
    first = _first_argmax_mask(s3, io3, 1, eg)
    m2 = jnp.max(jnp.where(first, -jnp.inf, s3), axis=1, keepdims=True)
    gs = (m1 + m2).reshape(ng, tm)
    iog = lax.broadcasted_iota(jnp.int32, (ng, tm), 0)
    gkeep = jnp.zeros((ng, tm), F32)
    cur = gs
    for _ in range(TOPK_GROUPS):
        hit = _first_argmax_mask(cur, iog, 0, ng)
        gkeep = jnp.where(hit, 1.0, gkeep)
        cur = jnp.where(hit, -jnp.inf, cur)
    selm = jnp.where(gkeep.reshape(ng, 1, tm) > 0.0, s3, -jnp.inf).reshape(ne, tm)
    ioe = lax.broadcasted_iota(jnp.int32, (ne, tm), 0)
    hits = []
    chosen = jnp.zeros((ne, tm), F32)
    cur = selm
    for _ in range(TOP_K):
        hit = _first_argmax_mask(cur, ioe, 0, ne)
        hits.append(hit)
        chosen = jnp.where(hit, 1.0, chosen)
        cur = jnp.where(hit, -jnp.inf, cur)
    w = chosen * scores
    w = w / jnp.sum(w, axis=0, keepdims=True) * ROUTED_SCALE

    cnt = jnp.sum(chosen, axis=1, keepdims=True)
    plen = jnp.ceil(cnt * (1.0 / ROW_ALIGN)) * ROW_ALIGN
    start = _cumsum_rows(jnp.broadcast_to(plen, (ne, LANE)))[:, :1] - plen
    before = (lax.broadcasted_iota(jnp.int32, (tm, tm), 0)
              < lax.broadcasted_iota(jnp.int32, (tm, tm), 1)).astype(BF16)
    rank = jnp.dot(chosen.astype(BF16), before, preferred_element_type=F32)
    row_of = start + rank
    pad_rows = SLOT_ROWS - TOP_K
    dest = [jnp.sum(jnp.where(hit, row_of, 0.0), axis=0, keepdims=True) for hit in hits]
    wsel = [jnp.sum(jnp.where(hit, w, 0.0), axis=0, keepdims=True) for hit in hits]
    dest_ref[...] = jnp.concatenate(dest + [jnp.full((pad_rows, tm), -1.0, F32)], axis=0).astype(jnp.int32)
    wgt_ref[...] = jnp.concatenate(wsel + [jnp.zeros((pad_rows, tm), F32)], axis=0)
    starts_ref[0] = start.astype(jnp.int32)
    plens_ref[0] = plen.astype(jnp.int32)


def _router(x2, mod3, norm_g, rw_t, rb_col, seq):
    t, d = x2.shape
    tm = MOE_SUB
    nsb = t // tm
    return pl.pallas_call(
        _router_kernel, name="router",
        grid=(nsb,),
        in_specs=[pl.BlockSpec((tm, d), lambda i: (i, 0)),
                  pl.BlockSpec((1, 1, d), lambda i: ((i * tm) // seq, 0, 4)),
                  pl.BlockSpec((1, 1, d), lambda i: ((i * tm) // seq, 0, 3)),
                  pl.BlockSpec((1, d), lambda i: (0, 0)),
                  pl.BlockSpec(rw_t.shape, lambda i: (0, 0)),
                  pl.BlockSpec(rb_col.shape, lambda i: (0, 0))],
        out_specs=[pl.BlockSpec((SLOT_ROWS, tm), lambda i: (0, i)),
                   pl.BlockSpec((SLOT_ROWS, tm), lambda i: (0, i)),
                   pl.BlockSpec((1, N_EXPERTS, 1), lambda i: (i, 0, 0)),
                   pl.BlockSpec((1, N_EXPERTS, 1), lambda i: (i, 0, 0))],
        out_shape=[jax.ShapeDtypeStruct((SLOT_ROWS, t), jnp.int32),
                   jax.ShapeDtypeStruct((SLOT_ROWS, t), F32),
                   jax.ShapeDtypeStruct((nsb, N_EXPERTS, 1), jnp.int32),
                   jax.ShapeDtypeStruct((nsb, N_EXPERTS, 1), jnp.int32)],
        compiler_params=_params(("parallel",)),
    )(x2, mod3, mod3, norm_g, rw_t, rb_col)


def _moe_stream_kernel(starts_ref, plens_ref, x_ref, sc_ref, sh_ref, g2_ref, gn_ref, dest_ref, wgt_ref,
                       wg_ref, wu_ref, wd_ref, sg_ref, su_ref, sd_ref, nf_ref, o_ref, h_ref, *, final):
    blk, step = pl.program_id(0), pl.program_id(1)
    eps = wg_ref.shape[0]
    ne = pl.num_programs(1) * eps
    sub, win = MOE_SUB, MOE_WINDOW
    nsub = x_ref.shape[0] // sub

    @pl.when(step == 0)
    def _():
        h_ref[...] = _mx(_rms(x_ref[...], gn_ref[...]) * (1.0 + sc_ref[0]) + sh_ref[0])
        o_ref[...] = jnp.zeros_like(o_ref)

    def run(sb, j):
        i = (blk * nsub + sb) * ne + step * eps + j
        return starts_ref[i], plens_ref[i]

    runs = [[run(sb, j) for j in range(eps)] for sb in range(nsub)]
    r_col = lax.broadcasted_iota(jnp.int32, (win, 1), 0)
    r_row = lax.broadcasted_iota(jnp.int32, (1, win), 1)

    def window_pass(k):
        first = k * win
        picked = []
        for sb in range(nsub):
            tok = slice(sb * sub, (sb + 1) * sub)
            dest = dest_ref[:, tok]
            want = jnp.concatenate([st + first + r_col for st, _ in runs[sb]], axis=0)
            pick = jnp.zeros((eps * win, sub), F32)
            for s in range(TOP_K):
                pick = jnp.where(dest[s:s + 1, :] == want, 1.0, pick)
            picked.append(_mx(_dot(pick, h_ref[tok, :])))
        outs = []
        for j in range(eps):
            xj = jnp.concatenate([p[j * win:(j + 1) * win] for p in picked], axis=0)
            outs.append(_mx(_dot(_silu(_dot(xj, wg_ref[j])) * _dot(xj, wu_ref[j]), wd_ref[j])))
        for sb in range(nsub):
            tok = slice(sb * sub, (sb + 1) * sub)
            res = jnp.concatenate([o[sb * win:(sb + 1) * win] for o in outs], axis=0)
            have = jnp.concatenate([jnp.where(first + r_row < ln, st + first + r_row, -2)
                                    for st, ln in runs[sb]], axis=1).astype(F32)
            dest_t = jnp.transpose(dest_ref[:, tok].astype(F32))
            wgt_t = jnp.transpose(wgt_ref[:, tok])
            mix = jnp.zeros((sub, eps * win), F32)
            for s in range(TOP_K):
                mix = jnp.where(dest_t[:, s:s + 1] == have, wgt_t[:, s:s + 1], mix)
            o_ref[tok, :] += _dot(mix, res)

    window_pass(0)

    longest = runs[0][0][1]
    for per_sub in runs:
        for _, ln in per_sub:
            longest = jnp.maximum(longest, ln)

    def more(k):
        window_pass(k)
        return k + 1
    lax.while_loop(lambda k: k * win < longest, more, jnp.int32(1))

    @pl.when(step == pl.num_programs(1) - 1)
    def _():
        for sb in range(nsub):
            tok = slice(sb * sub, (sb + 1) * sub)
            hs = h_ref[tok, :]
            shared = _dot(_silu(_dot(hs, sg_ref[...])) * _dot(hs, su_ref[...]), sd_ref[...])
            xo = x_ref[tok, :] + g2_ref[0] * (shared + o_ref[tok, :])
            if final:
                xo = _rms(xo, nf_ref[...])
            o_ref[tok, :] = xo


def _moe_stream(x2, mod3, norm_g, dest, wgt, starts, plens, wg, wu, wd, sg, su, sd, nf, seq, final):
    t, d = x2.shape
    tm = min(MOE_STREAM_BLOCK, seq)
    ne, ff = wg.shape[0], wg.shape[2]
    eps = MOE_EXPERTS_PER_STEP
    full = lambda a: pl.BlockSpec(a.shape, lambda i, e, *_: (0,) * a.ndim)
    once = lambda a: pl.BlockSpec(a.shape, lambda i, e, *_: (0,) * a.ndim, pipeline_mode=pl.Buffered(1))
    mod = lambda j: pl.BlockSpec((1, 1, d), lambda i, e, *_: ((i * tm) // seq, 0, j))
    slot = pl.BlockSpec((SLOT_ROWS, tm), lambda i, e, *_: (0, i))
    grid_spec = pltpu.PrefetchScalarGridSpec(
        num_scalar_prefetch=2,
        grid=(t // tm, ne // eps),
        in_specs=[pl.BlockSpec((tm, d), lambda i, e, *_: (i, 0), pipeline_mode=pl.Buffered(1)),
                  mod(4), mod(3), mod(5),
                  pl.BlockSpec((1, d), lambda i, e, *_: (0, 0)), slot, slot,
                  pl.BlockSpec((eps, d, ff), lambda i, e, *_: (e, 0, 0)),
                  pl.BlockSpec((eps, d, ff), lambda i, e, *_: (e, 0, 0)),
                  pl.BlockSpec((eps, ff, d), lambda i, e, *_: (e, 0, 0)),
                  once(sg), once(su), once(sd), full(nf)],
        out_specs=pl.BlockSpec((tm, d), lambda i, e, *_: (i, 0), pipeline_mode=pl.Buffered(1)),
        scratch_shapes=[pltpu.VMEM((tm, d), MXU_DTYPE)])
    return pl.pallas_call(
        functools.partial(_moe_stream_kernel, final=final), name="moe",
        grid_spec=grid_spec,
        out_shape=jax.ShapeDtypeStruct((t, d), F32),
        compiler_params=_params(("parallel", "arbitrary"), vmem_mb=61),
    )(starts.reshape(-1), plens.reshape(-1), x2, mod3, mod3, mod3, norm_g, dest, wgt,
      wg, wu, wd, sg, su, sd, nf)


def _pack_w_in(w):
    d = w.shape[0]
    offs = [0]
    for n in IN_SIZES:
        offs.append(offs[-1] + n)
    (qa, ka, va, ga, alr, cq, ckv, kidx, widx, qc, kc, vc, ic, fc, oc, gates) = [
        w[:, offs[i]:offs[i + 1]] for i in range(len(IN_SIZES))]
    z = lambda n: jnp.zeros((d, n), w.dtype)
    packed = jnp.concatenate(
        [gates, qa, ka, va, ga, vc, oc, qc, kc, cq, ckv,
         kidx, widx, z(LANE - IDX_DIM - IDX_HEADS),
         alr, z(LANE - GLA_GATE_RANK),
         ic, fc, z(LANE - 2 * MLSTM_HEADS)], axis=1)
    assert packed.shape[1] == N_PACK
    return packed.astype(MXU_DTYPE)


def kernel(x, c, ada_w, ada_b, norm_mix, norm_ffn, w_in, gla_w_a2, gla_b_a, gla_norm, dsa_norm_q,
           dsa_norm_kv, dsa_w_uq, dsa_w_uk, dsa_w_uv, dsa_w_qi, mlstm_conv, mlstm_b_i, mlstm_b_f,
           mlstm_norm, w_up_a, w_up_b, w_up_c, w_o, router_w, router_bias, exp_w_gate, exp_w_up,
           exp_w_down, sh_w_gate, sh_w_up, sh_w_down, norm_final):
    b, s, d = x.shape
    depth = ada_w.shape[0]
    t = b * s
    mod = _ada_mod(c, ada_w, ada_b)
    x2 = x.reshape(t, d)
    row = lambda v: v.reshape(1, -1)
    for l in range(depth):
        mod3 = mod[l].reshape(b, 1, 6 * d)
        w_pack = _pack_w_in(w_in[l])
        proj2, tail2 = _in_proj(x2, mod3, row(norm_mix[l]), w_pack, s)
        proj3 = proj2.reshape(b, s, N_MAIN)
        tail3 = tail2.reshape(b, s, N_TAIL)

        wa2_pad = jnp.zeros((LANE, GLA_QK), F32).at[:GLA_GATE_RANK].set(gla_w_a2[l])
        ya_in = _gla(proj3, tail3, wa2_pad, row(gla_b_a[l]), row(gla_norm[l]))

        bias_row = jnp.zeros((1, LANE), F32).at[0, :MLSTM_HEADS].set(mlstm_b_i[l])
        bias_row = bias_row.at[0, MLSTM_HEADS:2 * MLSTM_HEADS].set(mlstm_b_f[l])
        yc_in = _mlstm(proj3, tail3, mlstm_conv[l], bias_row, row(mlstm_norm[l]))

        wuq = dsa_w_uq[l].reshape(DSA_Q_RANK, DSA_HEADS * DSA_HEAD_DIM).astype(MXU_DTYPE)
        wuk_t = jnp.transpose(dsa_w_uk[l], (1, 2, 0)).astype(MXU_DTYPE)
        wuv_t = jnp.transpose(dsa_w_uv[l], (1, 2, 0)).astype(MXU_DTYPE)
        wqi_pad = jnp.zeros((DSA_Q_RANK, IDX_HEADS, LANE), F32).at[:, :, :IDX_DIM].set(dsa_w_qi[l])
        wqi_pad = wqi_pad.reshape(DSA_Q_RANK, IDX_HEADS * LANE).astype(MXU_DTYPE)
        ckvn, kwb, qlat, qidx, wht = _dsa_prep(proj2, tail2, s, row(dsa_norm_q[l]), row(dsa_norm_kv[l]),
                                               wuq, wuk_t, wqi_pad)
        yb_in = _dsa(ckvn.reshape(b, s, DSA_AUG), kwb.reshape(b, s, LANE), qlat, qidx, wht, wuv_t)

        x2 = _merge(ya_in.reshape(t, GLA_V), yb_in.reshape(t, DSA_V), yc_in.reshape(t, MLSTM_V),
                    proj2, x2, mod3, w_up_a[l].astype(MXU_DTYPE), w_up_b[l].astype(MXU_DTYPE),
                    w_up_c[l].astype(MXU_DTYPE), w_o[l].astype(MXU_DTYPE), s)

        dest, wgt, starts, plens = _router(x2, mod3, row(norm_ffn[l]), jnp.transpose(router_w[l]),
                                           router_bias[l].reshape(-1, 1), s)
        x2 = _moe_stream(x2, mod3, row(norm_ffn[l]), dest, wgt, starts, plens,
                  exp_w_gate[l].astype(MXU_DTYPE), exp_w_up[l].astype(MXU_DTYPE),
                  exp_w_down[l].astype(MXU_DTYPE), sh_w_gate[l].astype(MXU_DTYPE),
                  sh_w_up[l].astype(MXU_DTYPE), sh_w_down[l].astype(MXU_DTYPE), row(norm_final), s,
                  final=(l == depth - 1))
    return x2.reshape(b, s, d)
```

```python
import functools
import struct

import jax
import jax.numpy as jnp
from jax import lax
from jax.experimental import pallas as pl
from jax.experimental.pallas import tpu as pltpu

F32 = jnp.float32
BF16 = jnp.bfloat16
MXU_DTYPE = jnp.bfloat16
ACT_DTYPE = jnp.bfloat16
HIGHEST = lax.Precision.HIGHEST

EPS = 1e-6
D_MODEL = 1024
GLA_HEADS, GLA_DK, GLA_DV, GLA_GATE_RANK, GLA_TAU, GLA_CHUNK = 4, 64, 128, 16, 16.0, 64
GLA_SUB = 16
DSA_HEADS, DSA_Q_RANK, DSA_KV_RANK, DSA_HEAD_DIM, DSA_V_DIM = 8, 256, 128, 64, 64
IDX_HEADS, IDX_DIM, DSA_TOPK_MAX = 8, 32, 256
MLSTM_HEADS, MLSTM_DQK, MLSTM_DV, MLSTM_CONV, MLSTM_CHUNK = 4, 64, 128, 4, 64
N_EXPERTS, TOP_K, N_GROUPS, TOPK_GROUPS, EXPERT_FF, SHARED_FF = 64, 6, 8, 4, 256, 256
ROUTED_SCALE = 2.5

GLA_QK = GLA_HEADS * GLA_DK
GLA_V = GLA_HEADS * GLA_DV
DSA_V = DSA_HEADS * DSA_V_DIM
MLSTM_QK = MLSTM_HEADS * MLSTM_DQK
MLSTM_V = MLSTM_HEADS * MLSTM_DV
IN_SIZES = (GLA_QK, GLA_QK, GLA_V, GLA_V, GLA_GATE_RANK,
            DSA_Q_RANK, DSA_KV_RANK, IDX_DIM, IDX_HEADS,
            MLSTM_QK, MLSTM_QK, MLSTM_V, MLSTM_HEADS, MLSTM_HEADS, MLSTM_V,
            3 * D_MODEL)

LANE = 128
KEY_TILE = 256
Q_TILE = 256
ATT_HEAD_GROUP = 4
NEG_BIG = -1e30
MOE_STREAM_BLOCK = 2048
MOE_SUB = 256
MOE_WINDOW = 40
MOE_EXPERTS_PER_STEP = 4
CONV_PAD = 8
SEQ_GROUP = 16
SLOT_ROWS = 8
ROW_ALIGN = 8

C_GATES = 0
C_QA = 3072
C_KA = 3328
C_VA = 3584
C_GA = 4096
C_VC = 4608
C_OC = 5120
C_QKC = 5632
C_CQ = 6144
N_MAIN = 6400
T_CKV = 0
T_KW = 128
T_ALR = 256
T_ICFC = 384
N_TAIL = 512
N_PACK = N_MAIN + N_TAIL
INPROJ_PIECES = 5
W_IDX_LANE = IDX_DIM


LOG2E = 1.4426950408889634
DSA_AUG = DSA_KV_RANK + LANE
POS_SPLIT = 64
SLOPE_PIECES = 3
SUBLANE = 8
PARTIAL_ROWS = 32
COUNT_UNSET = float(2 ** 30)


def _bf16_pieces(x, n):
    out = []
    for _ in range(n):
        bits = struct.unpack("<I", struct.pack("<f", x))[0]
        bits = (bits + 0x7FFF + ((bits >> 16) & 1)) & 0xFFFF0000
        piece = struct.unpack("<f", struct.pack("<I", bits))[0]
        out.append(piece)
        x -= piece
    return out


def _mx(x):
    return x.astype(MXU_DTYPE)


def _dot(a, b):
    return jnp.dot(_mx(a), _mx(b), preferred_element_type=F32)


def _dot_nt(a, b):
    return lax.dot_general(_mx(a), _mx(b), (((1,), (1,)), ((), ())), preferred_element_type=F32)


def _dot_tn(a, b):
    return lax.dot_general(_mx(a), _mx(b), (((0,), (0,)), ((), ())), preferred_element_type=F32)


def _rms(x, g):
    return x * lax.rsqrt(jnp.mean(x * x, axis=-1, keepdims=True) + EPS) * g


def _silu(x):
    return x * jax.nn.sigmoid(x)


def _log_sigmoid(z):
    return jnp.minimum(z, 0.0) - jnp.log1p(jnp.exp(-jnp.abs(z)))


def _cumsum_rows(x):
    n = x.shape[0]
    tri = (lax.broadcasted_iota(jnp.int32, (n, n), 1) <= lax.broadcasted_iota(jnp.int32, (n, n), 0)).astype(F32)
    return jnp.dot(tri, x, preferred_element_type=F32, precision=HIGHEST)


def _truncate_to_bf16(x):
    bits = lax.bitcast_convert_type(x, jnp.int32) & jnp.int32(-65536)
    return lax.bitcast_convert_type(bits, F32).astype(BF16)


def _colreduce(x, op, width=PARTIAL_ROWS):
    n, c = x.shape
    return op(op(x.reshape(n // width, width, c), axis=0), axis=0, keepdims=True)


def _params(sem, vmem_mb=40):
    return pltpu.CompilerParams(dimension_semantics=sem, vmem_limit_bytes=vmem_mb * 1024 * 1024)


def _ada_kernel(c_ref, w_ref, b_ref, o_ref):
    cs = _silu(c_ref[...])
    o_ref[0] = jnp.dot(cs, w_ref[0], preferred_element_type=F32, precision=HIGHEST) + b_ref[0]


def _ada_mod(c, ada_w, ada_b):
    depth, d, n = ada_w.shape
    b = c.shape[0]
    return pl.pallas_call(
        _ada_kernel, name="ada_mod",
        grid=(depth, n // d),
        in_specs=[pl.BlockSpec((b, d), lambda l, j: (0, 0)),
                  pl.BlockSpec((1, d, d), lambda l, j: (l, 0, j)),
                  pl.BlockSpec((1, 1, d), lambda l, j: (l, 0, j))],
        out_specs=pl.BlockSpec((1, b, d), lambda l, j: (l, 0, j)),
        out_shape=jax.ShapeDtypeStruct((depth, b, n), F32),
        compiler_params=_params(("parallel", "parallel")),
    )(c, ada_w, ada_b.reshape(depth, 1, n))


def _inproj_kernel(x_ref, sc_ref, sh_ref, g_ref, wm_ref, wt_ref, om_ref, ot_ref):
    h = _mx(_rms(x_ref[...], g_ref[...]) * (1.0 + sc_ref[0]) + sh_ref[0])
    piece = N_MAIN // INPROJ_PIECES
    for j in range(INPROJ_PIECES):
        cols = slice(j * piece, (j + 1) * piece)
        om_ref[:, cols] = jnp.dot(h, wm_ref[:, cols], preferred_element_type=F32).astype(om_ref.dtype)
    ot_ref[...] = jnp.dot(h, wt_ref[...], preferred_element_type=F32)


def _in_proj(x2, mod3, norm_g, w_pack, seq):
    t, d = x2.shape
    tm = min(512, seq)
    once = lambda w: pl.BlockSpec((d, w), lambda i: (0, 0), pipeline_mode=pl.Buffered(1))
    return pl.pallas_call(
        _inproj_kernel, name="in_proj",
        grid=(t // tm,),
        in_specs=[pl.BlockSpec((tm, d), lambda i: (i, 0)),
                  pl.BlockSpec((1, 1, d), lambda i: ((i * tm) // seq, 0, 1)),
                  pl.BlockSpec((1, 1, d), lambda i: ((i * tm) // seq, 0, 0)),
                  pl.BlockSpec((1, d), lambda i: (0, 0)),
                  once(N_MAIN), once(N_TAIL)],
        out_specs=[pl.BlockSpec((tm, N_MAIN), lambda i: (i, 0)),
                   pl.BlockSpec((tm, N_TAIL), lambda i: (i, 0))],
        out_shape=[jax.ShapeDtypeStruct((t, N_MAIN), ACT_DTYPE),
                   jax.ShapeDtypeStruct((t, N_TAIL), F32)],
        compiler_params=_params(("parallel",), vmem_mb=48),
    )(x2, mod3, mod3, norm_g, w_pack[:, :N_MAIN], w_pack[:, N_MAIN:])


def _gla_kernel(q_ref, k_ref, v_ref, g_ref, alr_ref, wa2_ref, ba_ref, gn_ref, o_ref, s_ref, acc_ref):
    @pl.when(pl.program_id(1) == 0)
    def _():
        s_ref[...] = jnp.zeros_like(s_ref)

    L, sub, nh, dk, dv = GLA_CHUNK, GLA_SUB, GLA_HEADS, GLA_DK, GLA_DV
    seqs = range(q_ref.shape[0])
    heads = range(nh)
    hk = lambda h: slice(h * dk, (h + 1) * dk)
    hv = lambda h: slice(h * dv, (h + 1) * dv)

    pre = []
    for g in seqs:
        z = jnp.dot(alr_ref[g], wa2_ref[...], preferred_element_type=F32, precision=HIGHEST) + ba_ref[...]
        cum = _cumsum_rows(_log_sigmoid(z) * (1.0 / GLA_TAU))
        q = q_ref[g].astype(F32) * (dk ** -0.5)
        k = k_ref[g].astype(F32)
        tot = cum[L - 1:L, :]
        pre.append(dict(cum=cum, q=q, k=k, tot=tot, vb=_mx(v_ref[g]),
                        q_in=_mx(q * jnp.exp(cum)), k_dec=_mx(k * jnp.exp(tot - cum))))

    scores = {}
    for i in range(L // sub):
        r0, r1 = i * sub, (i + 1) * sub
        for g in seqs:
            p = pre[g]
            base = p["cum"][r0 - 1:r0, :] if i > 0 else jnp.zeros_like(p["tot"])
            qi = _mx(p["q"][r0:r1] * jnp.exp(p["cum"][r0:r1] - base))
            ka = _mx(p["k"][:r1] * jnp.exp(base - p["cum"][:r1]))
            for h in heads:
                scores[g, i, h] = _dot_nt(qi[:, hk(h)], ka[:, hk(h)])

    for i in range(L // sub):
        r0, r1 = i * sub, (i + 1) * sub
        causal = (lax.broadcasted_iota(jnp.int32, (sub, r1), 1)
                  <= lax.broadcasted_iota(jnp.int32, (sub, r1), 0) + r0)
        for g in seqs:
            for h in heads:
                s = jnp.where(causal, scores[g, i, h], 0.0)
                acc_ref[g, r0:r1, hv(h)] = _dot(s, pre[g]["vb"][:r1, hv(h)])

    inter = {(g, h): _dot(pre[g]["q_in"][:, hk(h)], s_ref[g * nh + h]) for g in seqs for h in heads}
    update = {(g, h): _dot_tn(pre[g]["k_dec"][:, hk(h)], pre[g]["vb"][:, hv(h)]) for g in seqs for h in heads}

    gn = gn_ref[...]
    for g in seqs:
        gate = g_ref[g].astype(F32)
        for h in heads:
            o = acc_ref[g, :, hv(h)] + inter[g, h]
            y = _rms(o, gn[:, hv(h)]) * _silu(gate[:, hv(h)])
            o_ref[g, :, hv(h)] = y.astype(o_ref.dtype)
            decay = jnp.transpose(jnp.exp(pre[g]["tot"][:, hk(h)]))
            s_ref[g * nh + h] = s_ref[g * nh + h] * decay + update[g, h]


def _seq_group(batch):
    for grp in (SEQ_GROUP, 2, 1):
        if batch % grp == 0:
            return grp


def _gla(proj3, tail3, wa2_pad, ba, gn):
    b, s, _ = proj3.shape
    L = GLA_CHUNK
    grp = _seq_group(b)
    blk = lambda w, c0: pl.BlockSpec((grp, L, w), lambda bi, ci: (bi, ci, c0 // w))
    full = lambda a: pl.BlockSpec(a.shape, lambda bi, ci: (0,) * a.ndim)
    return pl.pallas_call(
        _gla_kernel, name="gla",
        grid=(b // grp, s // L),
        in_specs=[blk(GLA_QK, C_QA), blk(GLA_QK, C_KA), blk(GLA_V, C_VA), blk(GLA_V, C_GA),
                  blk(LANE, T_ALR), full(wa2_pad), full(ba), full(gn)],
        out_specs=pl.BlockSpec((grp, L, GLA_V), lambda bi, ci: (bi, ci, 0)),
        out_shape=jax.ShapeDtypeStruct((b, s, GLA_V), ACT_DTYPE),
        scratch_shapes=[pltpu.VMEM((grp * GLA_HEADS, GLA_DK, GLA_DV), F32),
                        pltpu.VMEM((grp, L, GLA_V), F32)],
        compiler_params=_params(("parallel", "arbitrary")),
    )(proj3, proj3, proj3, proj3, tail3, wa2_pad, ba, gn)


def _mlstm_pair_kernel(qk_ref, v_ref, oc_ref, if_ref, conv_ref, bias_ref, gnt_ref, o_ref,
                       xbuf_ref, ct_ref, n_ref, m_ref):
    L, nh, dk, dv, kc = MLSTM_CHUNK, MLSTM_HEADS, MLSTM_DQK, MLSTM_DV, MLSTM_CONV
    pad = CONV_PAD
    npair = nh // 2
    assert 2 * dk == LANE and dv == LANE and L == dk

    @pl.when(pl.program_id(1) == 0)
    def _():
        xbuf_ref[:, 0:pad, :] = jnp.zeros((xbuf_ref.shape[0], pad, 2 * MLSTM_QK), F32)
        ct_ref[...] = jnp.zeros_like(ct_ref)
        n_ref[...] = jnp.zeros_like(n_ref)
        m_ref[...] = jnp.zeros_like(m_ref)

    lane = lax.broadcasted_iota(jnp.int32, (1, LANE), 1)
    half = [(lane < dk).astype(F32), (lane >= dk).astype(F32)]
    s_idx = lax.broadcasted_iota(jnp.int32, (L, LANE), 0)
    t_idx = lax.broadcasted_iota(jnp.int32, (L, LANE), 1) % L
    causal = s_idx <= t_idx
    lane_in = lax.broadcasted_iota(jnp.int32, (LANE, LANE), 0)
    head_of = lax.broadcasted_iota(jnp.int32, (LANE, LANE), 1) // dk
    cw = conv_ref[...]

    combos = [(g, p) for g in range(qk_ref.shape[0]) for p in range(npair)]
    seq = {}
    for g in range(qk_ref.shape[0]):
        xbuf_ref[g, pad:pad + L, :] = qk_ref[g].astype(F32)
        conv = jnp.zeros((L, 2 * MLSTM_QK), F32)
        for j in range(kc):
            conv = conv + cw[j:j + 1, :] * xbuf_ref[g, pl.ds(pad - (kc - 1) + j, L), :]
        xbuf_ref[g, 0:pad, :] = xbuf_ref[g, L:L + pad, :]
        qk = _silu(conv)
        pre = if_ref[g] + bias_ref[...]
        bcum = _cumsum_rows(_log_sigmoid(pre))
        v = v_ref[g].astype(F32)
        seq[g] = dict(q=qk[:, :MLSTM_QK] * (dk ** -0.5), k=qk[:, MLSTM_QK:],
                      vt=[_mx(jnp.transpose(v[:, h * dv:(h + 1) * dv])) for h in range(nh)],
                      gate_mix=jnp.where(lane < nh, pre, -bcum), bcum=bcum, bcum_t=jnp.transpose(bcum))

    st = {}
    for g, p in combos:
        sq = seq[g]
        sel = jnp.where((lane_in == 2 * p + head_of) | (lane_in == nh + 2 * p + head_of), 1.0, 0.0)
        selb = jnp.where(lane_in == nh + 2 * p + head_of, 1.0, 0.0)
        d_mat = jnp.dot(sq["gate_mix"], sel, preferred_element_type=F32, precision=HIGHEST)
        tot = jnp.dot(sq["bcum"][L - SUBLANE:L, :], selb, preferred_element_type=F32,
                      precision=HIGHEST)[SUBLANE - 1:SUBLANE, :]
        b_row = jnp.concatenate([sq["bcum_t"][nh + 2 * p:nh + 2 * p + 1, :],
                                 sq["bcum_t"][nh + 2 * p + 1:nh + 2 * p + 2, :]], axis=1)
        qt = sq["q"][:, p * LANE:(p + 1) * LANE]
        kt = sq["k"][:, p * LANE:(p + 1) * LANE]
        q_bd = _mx(jnp.concatenate([qt * half[0], qt * half[1]], axis=0))
        st[g, p] = dict(d=d_mat, tot=tot, b_row=b_row, kt=kt, q_bd=q_bd,
                        scores=_dot_nt(kt, q_bd),
                        qn=_dot_nt(jnp.broadcast_to(n_ref[g * npair + p], (SUBLANE, LANE)), q_bd)[0:1, :],
                        inter=[_dot_nt(ct_ref[g * nh + 2 * p + hh], q_bd) for hh in range(2)])

    for g, p in combos:
        c = st[g, p]
        m_prev = m_ref[g * npair + p]
        dlog = jnp.where(causal, c["b_row"] + c["d"], -jnp.inf)
        inter_log = c["b_row"] + m_prev
        m_t = jnp.maximum(inter_log, jnp.max(dlog, axis=0, keepdims=True))
        g_log = c["tot"] + c["d"]
        m_new = jnp.maximum(c["tot"] + m_prev, jnp.max(g_log, axis=0, keepdims=True))
        c.update(m_t=m_t, w_inter=jnp.exp(inter_log - m_t), sw=c["scores"] * jnp.exp(dlog - m_t),
                 m_new=m_new, w_c=jnp.exp(c["tot"] + m_prev - m_new), ks=c["kt"] * jnp.exp(g_log - m_new))

    for g, p in combos:
        c = st[g, p]
        vt = seq[g]["vt"]
        c["sv"] = [_dot(vt[2 * p + hh], c["sw"] * half[hh]) for hh in range(2)]
        c["kv"] = [_dot(vt[2 * p + hh], c["ks"] * half[hh]) for hh in range(2)]

    for g, p in combos:
        c = st[g, p]
        num = c["sv"][0] + c["sv"][1] + c["w_inter"] * (c["inter"][0] + c["inter"][1])
        den = jnp.sum(c["sw"], axis=0, keepdims=True) + c["w_inter"] * c["qn"]
        hout = num / jnp.maximum(jnp.abs(den), jnp.exp(-c["m_t"]))
        y_t = hout * lax.rsqrt(jnp.mean(hout * hout, axis=0, keepdims=True) + EPS) * gnt_ref[p]
        y = jnp.transpose(y_t)
        for hh in range(2):
            cols = slice((2 * p + hh) * dv, (2 * p + hh + 1) * dv)
            o_ref[g, :, cols] = (y[hh * L:(hh + 1) * L, :] * jax.nn.sigmoid(oc_ref[g, :, cols].astype(F32))).astype(o_ref.dtype)
            ct_ref[g * nh + 2 * p + hh] = ct_ref[g * nh + 2 * p + hh] * c["w_c"] + c["kv"][hh]
        n_ref[g * npair + p] = n_ref[g * npair + p] * c["w_c"] + jnp.sum(c["ks"], axis=0, keepdims=True)
        m_ref[g * npair + p] = c["m_new"]


def _mlstm(proj3, tail3, conv_w, bias_row, gn):
    b, s, _ = proj3.shape
    L = MLSTM_CHUNK
    grp = _seq_group(b)
    blk = lambda w, c0: pl.BlockSpec((grp, L, w), lambda bi, ci: (bi, ci, c0 // w))
    full = lambda a: pl.BlockSpec(a.shape, lambda bi, ci: (0,) * a.ndim)
    npair = MLSTM_HEADS // 2
    gnt = jnp.repeat(jnp.transpose(gn.reshape(npair, 2, MLSTM_DV), (0, 2, 1)), L, axis=2)
    return pl.pallas_call(
        _mlstm_pair_kernel, name="mlstm",
        grid=(b // grp, s // L),
        in_specs=[blk(2 * MLSTM_QK, C_QKC), blk(MLSTM_V, C_VC), blk(MLSTM_V, C_OC), blk(LANE, T_ICFC),
                  full(conv_w), full(bias_row), full(gnt)],
        out_specs=pl.BlockSpec((grp, L, MLSTM_V), lambda bi, ci: (bi, ci, 0)),
        out_shape=jax.ShapeDtypeStruct((b, s, MLSTM_V), ACT_DTYPE),
        scratch_shapes=[pltpu.VMEM((grp, L + CONV_PAD, 2 * MLSTM_QK), F32),
                        pltpu.VMEM((grp * MLSTM_HEADS, MLSTM_DV, LANE), F32),
                        pltpu.VMEM((grp * npair, 1, LANE), F32),
                        pltpu.VMEM((grp * npair, 1, LANE), F32)],
        compiler_params=_params(("parallel", "arbitrary")),
    )(proj3, proj3, proj3, tail3, conv_w, bias_row, gnt)


def _dsa_prep_kernel(cq_ref, ckv_ref, kw_ref, nq_ref, nkv_ref, wuq_ref, wuk_ref, wqi_ref,
                     ckvn_ref, kwb_ref, qlat_ref, qidx_ref, wht_ref, *, tiles_per_seq):
    tm = cq_ref.shape[0]
    r = DSA_KV_RANK
    cqn = _mx(_rms(cq_ref[...].astype(F32), nq_ref[...]))
    pos = (pl.program_id(0) % tiles_per_seq) * tm + lax.broadcasted_iota(jnp.int32, (tm, LANE), 0)
    lane = lax.broadcasted_iota(jnp.int32, (tm, LANE), 1)
    pos_cols = jnp.where(lane < SLOPE_PIECES, pos >> (POS_SPLIT.bit_length() - 1),
                         jnp.where(lane < 2 * SLOPE_PIECES, pos & (POS_SPLIT - 1), 0)).astype(F32)
    ckvn_ref[:, :r] = _rms(ckv_ref[...], nkv_ref[...]).astype(ckvn_ref.dtype)
    ckvn_ref[:, r:] = pos_cols.astype(ckvn_ref.dtype)
    kw = kw_ref[...]
    kwb_ref[...] = kw.astype(kwb_ref.dtype)
    wht_ref[0] = jnp.transpose(kw)[W_IDX_LANE:W_IDX_LANE + IDX_HEADS, :] * (IDX_HEADS ** -0.5)
    q = jnp.dot(cqn, wuq_ref[...], preferred_element_type=F32)
    for h in range(DSA_HEADS):
        ql = _dot(q[:, h * DSA_HEAD_DIM:(h + 1) * DSA_HEAD_DIM], wuk_ref[h]) * (DSA_HEAD_DIM ** -0.5 * LOG2E)
        qlat_ref[0, h, :, :r] = ql.astype(qlat_ref.dtype)
        c = _bf16_pieces(2.0 ** (-8.0 * (h + 1) / DSA_HEADS) * LOG2E, SLOPE_PIECES)
        consts = [POS_SPLIT * v for v in c] + c
        slope_cols = jnp.zeros((tm, LANE), F32)
        for j, v in enumerate(consts):
            slope_cols = jnp.where(lane == j, v, slope_cols)
        qlat_ref[0, h, :, r:] = slope_cols.astype(qlat_ref.dtype)
    qi = jnp.dot(cqn, wqi_ref[...], preferred_element_type=F32) * (IDX_DIM ** -0.5)
    for h in range(IDX_HEADS):
        qidx_ref[0, h] = qi[:, h * LANE:(h + 1) * LANE].astype(qidx_ref.dtype)


def _dsa_prep(proj2, tail2, seq, nq, nkv, wuq, wuk_t, wqi_pad):
    t = proj2.shape[0]
    b = t // seq
    tm = min(512, seq)
    per = seq // tm
    blk = lambda w, c0: pl.BlockSpec((tm, w), lambda i: (i, c0 // w))
    full = lambda a: pl.BlockSpec(a.shape, lambda i: (0,) * a.ndim)
    hmap = lambda i: (i // per, 0, i % per, 0)
    return pl.pallas_call(
        functools.partial(_dsa_prep_kernel, tiles_per_seq=per), name="dsa_prep",
        grid=(t // tm,),
        in_specs=[blk(DSA_Q_RANK, C_CQ), blk(DSA_KV_RANK, T_CKV), blk(LANE, T_KW),
                  full(nq), full(nkv), full(wuq), full(wuk_t), full(wqi_pad)],
        out_specs=[pl.BlockSpec((tm, DSA_AUG), lambda i: (i, 0)),
                   pl.BlockSpec((tm, LANE), lambda i: (i, 0)),
                   pl.BlockSpec((1, DSA_HEADS, tm, DSA_AUG), hmap),
                   pl.BlockSpec((1, IDX_HEADS, tm, LANE), hmap),
                   pl.BlockSpec((1, IDX_HEADS, tm), lambda i: (i // per, 0, i % per))],
        out_shape=[jax.ShapeDtypeStruct((t, DSA_AUG), ACT_DTYPE),
                   jax.ShapeDtypeStruct((t, LANE), ACT_DTYPE),
                   jax.ShapeDtypeStruct((b, DSA_HEADS, seq, DSA_AUG), ACT_DTYPE),
                   jax.ShapeDtypeStruct((b, IDX_HEADS, seq, LANE), ACT_DTYPE),
                   jax.ShapeDtypeStruct((b, IDX_HEADS, seq), F32)],
        compiler_params=_params(("parallel",)),
    )(proj2, tail2, tail2, nq, nkv, wuq, wuk_t, wqi_pad)


def _dsa_kernel(qlat_ref, qidx_ref, wht_ref, kwk_ref, ckv_ref, wuv_ref, o_ref,
                ibuf_ref, ihi_ref, acc_ref, *, topk):
    nh, tq, tk = DSA_HEADS, Q_TILE, KEY_TILE
    qb = pl.program_id(1)
    n_kt = (qb * tq + tq + tk - 1) // tk
    t_row = qb * tq + lax.broadcasted_iota(jnp.int32, (1, tq), 1)
    s_col = lax.broadcasted_iota(jnp.int32, (tk, 1), 0)

    qi = qidx_ref[0].reshape(IDX_HEADS * tq, LANE)
    wht = wht_ref[0]

    def idx_body(kt, carry):
        kk = kwk_ref[0, pl.ds(pl.multiple_of(kt * tk, tk), tk), :]
        grp = ATT_HEAD_GROUP
        sc = [_dot_nt(kk, qi[g * grp * tq:(g + 1) * grp * tq]) for g in range(IDX_HEADS // grp)]
        tot = None
        for h in range(IDX_HEADS):
            part = wht[h:h + 1, :] * jnp.maximum(sc[h // grp][:, (h % grp) * tq:(h % grp + 1) * tq], 0.0)
            tot = part if tot is None else tot + part
        score = jnp.where(kt * tk + s_col <= t_row, tot, -jnp.inf)
        ibuf_ref[kt] = score
        ihi_ref[kt] = _truncate_to_bf16(score)
        return carry

    lax.fori_loop(0, n_kt, idx_body, 0)

    def count(pred):
        def body(kt, c):
            hit = jnp.where(pred(ibuf_ref[kt]), 1.0, 0.0)
            return c + jnp.sum(hit.reshape(tk // PARTIAL_ROWS, PARTIAL_ROWS, tq), axis=0)
        return jnp.sum(lax.fori_loop(0, n_kt, body, jnp.zeros((PARTIAL_ROWS, tq), F32)), axis=0, keepdims=True)

    def count_ge(cand):
        return count(lambda x: x >= cand)

    def count_gt(cand):
        return count(lambda x: x > cand)

    def key_to_float(u):
        key = u ^ jnp.int32(-2 ** 31)
        bits = jnp.where(key >= 0, key, key ^ jnp.int32(0x7FFFFFFF))
        return lax.bitcast_convert_type(bits, F32)

    few = t_row < topk
    n_bits = 32

    def try_bit(i, u, cnt_u, counter):
        cand_u = u | lax.shift_left(jnp.int32(1), n_bits - 1 - i)
        cnt = counter(cand_u)
        ok = cnt >= topk
        return jnp.where(ok, cand_u, u), jnp.where(ok, cnt, cnt_u)

    def count_ge_hi(cand_u):
        cand = _truncate_to_bf16(key_to_float(cand_u))
        def body(kt, c):
            hit = jnp.where(ihi_ref[kt] >= cand, jnp.ones((), BF16), jnp.zeros((), BF16))
            part = jnp.sum(hit.reshape(tk // PARTIAL_ROWS, PARTIAL_ROWS, tq), axis=0)
            return c + part.astype(F32)
        return jnp.sum(lax.fori_loop(0, n_kt, body, jnp.zeros((PARTIAL_ROWS, tq), F32)), axis=0, keepdims=True)

    def coarse_body(i, st):
        return try_bit(i, *st, count_ge_hi)

    def search_cond(st):
        i, _, _, pending = st
        return (i < n_bits) & (pending > 0)

    def search_body(st):
        i, u, cnt_u, _ = st
        for _ in range(4):
            u, cnt_u = try_bit(i, u, cnt_u, lambda c: count_ge(key_to_float(c)))
            i = i + 1
        pending = jnp.max(jnp.where(few | (cnt_u == topk), 0, 1))
        return i, u, cnt_u, pending

    u, cnt_u = lax.fori_loop(0, n_bits // 2, coarse_body,
                             (jnp.zeros((1, tq), jnp.int32), jnp.full((1, tq), COUNT_UNSET, F32)))
    start = (jnp.int32(n_bits // 2), u, cnt_u, jnp.max(jnp.where(few | (cnt_u == topk), 0, 1)))
    _, u, _, _ = lax.while_loop(search_cond, search_body, start)
    tau = jnp.where(few, -jnp.inf, key_to_float(u))
    need = topk - count_gt(tau)

    acc_ref[...] = jnp.zeros_like(acc_ref)
    ql = qlat_ref[0].reshape(nh * tq, DSA_AUG)
    tri = jnp.where(lax.broadcasted_iota(jnp.int32, (tk, tk), 0) >= lax.broadcasted_iota(jnp.int32, (tk, tk), 1),
                    1.0, 0.0).astype(BF16)

    def keys(kt):
        return ckv_ref[0, pl.ds(pl.multiple_of(kt * tk, tk), tk), :]

    def att_body(kt, carry):
        m, l, eq_seen = carry
        kv_aug = keys(kt)
        kv = kv_aug[:, :DSA_KV_RANK]
        it = ibuf_ref[kt]
        eq = it == tau
        eqf = jnp.where(eq, 1.0, 0.0)
        rank = jnp.dot(tri, eqf.astype(BF16), preferred_element_type=F32) + eq_seen
        valid = ((it > tau) | (eq & (rank <= need))) & (kt * tk + s_col <= t_row)
        bias = jnp.where(valid, 0.0, NEG_BIG)

        n_grp = nh // ATT_HEAD_GROUP
        gcols = lambda g: slice(g * ATT_HEAD_GROUP * tq, (g + 1) * ATT_HEAD_GROUP * tq)
        lg, soft, ms, ls = {}, {}, {}, {}

        def logits(g):
            lg[g] = _dot_nt(kv_aug, ql[gcols(g)])

        def softmax(g):
            als, ps = [], []
            for h in range(g * ATT_HEAD_GROUP, (g + 1) * ATT_HEAD_GROUP):
                cols = slice(h * tq, (h + 1) * tq)
                loc = slice((h - g * ATT_HEAD_GROUP) * tq, (h - g * ATT_HEAD_GROUP + 1) * tq)
                lh = lg[g][:, loc] + bias
                m_old = m[:, cols]
                m_new = jnp.maximum(m_old, _colreduce(lh, jnp.max))
                p = jnp.exp2(lh - m_new)
                alpha = jnp.exp2(m_old - m_new)
                ms[h] = m_new
                ls[h] = alpha * l[:, cols] + _colreduce(p, jnp.sum)
                als.append(alpha)
                ps.append(_mx(p))
            soft[g] = (jnp.concatenate(als, axis=1), jnp.concatenate(ps, axis=1))

        def weighted_values(g):
            alpha, p = soft[g]
            acc_ref[:, gcols(g)] = alpha * acc_ref[:, gcols(g)] + _dot_tn(kv, p)

        logits(0)
        for g in range(n_grp):
            if g + 1 < n_grp:
                logits(g + 1)
            softmax(g)
            if g > 0:
                weighted_values(g - 1)
        weighted_values(n_grp - 1)
        return (jnp.concatenate([ms[h] for h in range(nh)], axis=1),
                jnp.concatenate([ls[h] for h in range(nh)], axis=1),
                eq_seen + jnp.sum(eqf, axis=0, keepdims=True))

    init = (jnp.full((1, nh * tq), NEG_BIG, F32), jnp.zeros((1, nh * tq), F32), jnp.zeros((1, tq), F32))
    _, l, _ = lax.fori_loop(0, n_kt, att_body, init)

    outs = []
    for h in range(nh):
        cols = slice(h * tq, (h + 1) * tq)
        outs.append(_dot(wuv_ref[h], acc_ref[:, cols] / l[:, cols]))
    o_ref[0] = jnp.transpose(jnp.concatenate(outs, axis=0)).astype(o_ref.dtype)


def _dsa(ckvn3, kwb3, qlat, qidx, wht, wuv_t):
    b, s, _ = ckvn3.shape
    topk = min(DSA_TOPK_MAX, s // 4)
    nh, tq = DSA_HEADS, Q_TILE
    return pl.pallas_call(
        functools.partial(_dsa_kernel, topk=topk), name="dsa_attn",
        grid=(b, s // tq),
        in_specs=[pl.BlockSpec((1, nh, tq, DSA_AUG), lambda bi, qi: (bi, 0, qi, 0)),
                  pl.BlockSpec((1, IDX_HEADS, tq, LANE), lambda bi, qi: (bi, 0, qi, 0)),
                  pl.BlockSpec((1, IDX_HEADS, tq), lambda bi, qi: (bi, 0, qi)),
                  pl.BlockSpec((1, s, LANE), lambda bi, qi: (bi, 0, 0)),
                  pl.BlockSpec((1, s, DSA_AUG), lambda bi, qi: (bi, 0, 0)),
                  pl.BlockSpec(wuv_t.shape, lambda bi, qi: (0, 0, 0))],
        out_specs=pl.BlockSpec((1, tq, DSA_V), lambda bi, qi: (bi, qi, 0)),
        out_shape=jax.ShapeDtypeStruct((b, s, DSA_V), ACT_DTYPE),
        scratch_shapes=[pltpu.VMEM((s // KEY_TILE, KEY_TILE, tq), F32),
                        pltpu.VMEM((s // KEY_TILE, KEY_TILE, tq), BF16),
                        pltpu.VMEM((DSA_KV_RANK, nh * tq), F32)],
        compiler_params=_params(("parallel", "arbitrary")),
    )(qlat, qidx, wht, kwb3, ckvn3, wuv_t)


def _merge_kernel(a_ref, b_ref, c_ref, gt_ref, x_ref, g1_ref, wa_ref, wb_ref, wc_ref, wo_ref, o_ref):
    d = D_MODEL
    g = jax.nn.sigmoid(gt_ref[...].astype(F32))
    ya = jnp.dot(a_ref[...], wa_ref[...], preferred_element_type=F32)
    yb = jnp.dot(b_ref[...], wb_ref[...], preferred_element_type=F32)
    yc = jnp.dot(c_ref[...], wc_ref[...], preferred_element_type=F32)
    m = g[:, :d] * ya + g[:, d:2 * d] * yb + g[:, 2 * d:] * yc
    o_ref[...] = x_ref[...] + g1_ref[0] * _dot(m, wo_ref[...])


def _merge(ya_in, yb_in, yc_in, proj2, x2, mod3, wa, wb, wc, wo, seq):
    t, d = x2.shape
    tm = min(512, seq)
    full = lambda a: pl.BlockSpec(a.shape, lambda i: (0,) * a.ndim)
    br = lambda w: pl.BlockSpec((tm, w), lambda i: (i, 0))
    return pl.pallas_call(
        _merge_kernel, name="merge",
        grid=(t // tm,),
        in_specs=[br(GLA_V), br(DSA_V), br(MLSTM_V), br(3 * d), br(d),
                  pl.BlockSpec((1, 1, d), lambda i: ((i * tm) // seq, 0, 2)),
                  full(wa), full(wb), full(wc), full(wo)],
        out_specs=br(d),
        out_shape=jax.ShapeDtypeStruct((t, d), F32),
        compiler_params=_params(("parallel",)),
    )(ya_in, yb_in, yc_in, proj2, x2, mod3, wa, wb, wc, wo)


def _first_argmax_mask(cur, iota, axis, n):
    mx = jnp.max(cur, axis=axis, keepdims=True)
    ix = jnp.min(jnp.where(cur == mx, iota, n), axis=axis, keepdims=True)
    return iota == ix


def _router_kernel(x_ref, sc_ref, sh_ref, g_ref, rwt_ref, rb_ref, dest_ref, wgt_ref, starts_ref, plens_ref):
    ne, ng = N_EXPERTS, N_GROUPS
    eg = ne // ng
    h = _rms(x_ref[...], g_ref[...]) * (1.0 + sc_ref[0]) + sh_ref[0]
    tm = h.shape[0]
    logits = lax.dot_general(rwt_ref[...], h, (((1,), (1,)), ((), ())),
                             preferred_element_type=F32, precision=HIGHEST)
    scores = jax.nn.sigmoid(logits)
    sel = scores + rb_ref[...]
    s3 = sel.reshape(ng, eg, tm)
    io3 = lax.broadcasted_iota(jnp.int32, (ng, eg, tm), 1)
    m1 = jnp.max(s3, axis=1, keepdims=True)
    first = _first_argmax_mask(s3, io3, 1, eg)
    m2 = jnp.max(jnp.where(first, -jnp.inf, s3), axis=1, keepdims=True)
    gs = (m1 + m2).reshape(ng, tm)
    iog = lax.broadcasted_iota(jnp.int32, (ng, tm), 0)
    gkeep = jnp.zeros((ng, tm), F32)
    cur = gs
    for _ in range(TOPK_GROUPS):
        hit = _first_argmax_mask(cur, iog, 0, ng)
        gkeep = jnp.where(hit, 1.0, gkeep)
        cur = jnp.where(hit, -jnp.inf, cur)
    selm = jnp.where(gkeep.reshape(ng, 1, tm) > 0.0, s3, -jnp.inf).reshape(ne, tm)
    ioe = lax.broadcasted_iota(jnp.int32, (ne, tm), 0)
    hits = []
    chosen = jnp.zeros((ne, tm), F32)
    cur = selm
    for _ in range(TOP_K):
        hit = _first_argmax_mask(cur, ioe, 0, ne)
        hits.append(hit)
        chosen = jnp.where(hit, 1.0, chosen)
        cur = jnp.where(hit, -jnp.inf, cur)
    w = chosen * scores
    w = w / jnp.sum(w, axis=0, keepdims=True) * ROUTED_SCALE

    cnt = jnp.sum(chosen, axis=1, keepdims=True)
    plen = jnp.ceil(cnt * (1.0 / ROW_ALIGN)) * ROW_ALIGN
    start = _cumsum_rows(jnp.broadcast_to(plen, (ne, LANE)))[:, :1] - plen
    before = (lax.broadcasted_iota(jnp.int32, (tm, tm), 0)
              < lax.broadcasted_iota(jnp.int32, (tm, tm), 1)).astype(BF16)
    rank = jnp.dot(chosen.astype(BF16), before, preferred_element_type=F32)
    row_of = start + rank
    pad_rows = SLOT_ROWS - TOP_K
    dest = [jnp.sum(jnp.where(hit, row_of, 0.0), axis=0, keepdims=True) for hit in hits]
    wsel = [jnp.sum(jnp.where(hit, w, 0.0), axis=0, keepdims=True) for hit in hits]
    dest_ref[...] = jnp.concatenate(dest + [jnp.full((pad_rows, tm), -1.0, F32)], axis=0).astype(jnp.int32)
    wgt_ref[...] = jnp.concatenate(wsel + [jnp.zeros((pad_rows, tm), F32)], axis=0)
    starts_ref[0] = start.astype(jnp.int32)
    plens_ref[0] = plen.astype(jnp.int32)


def _router(x2, mod3, norm_g, rw_t, rb_col, seq):
    t, d = x2.shape
    tm = MOE_SUB
    nsb = t // tm
    return pl.pallas_call(
        _router_kernel, name="router",
        grid=(nsb,),
        in_specs=[pl.BlockSpec((tm, d), lambda i: (i, 0)),
                  pl.BlockSpec((1, 1, d), lambda i: ((i * tm) // seq, 0, 4)),
                  pl.BlockSpec((1, 1, d), lambda i: ((i * tm) // seq, 0, 3)),
                  pl.BlockSpec((1, d), lambda i: (0, 0)),
                  pl.BlockSpec(rw_t.shape, lambda i: (0, 0)),
                  pl.BlockSpec(rb_col.shape, lambda i: (0, 0))],
        out_specs=[pl.BlockSpec((SLOT_ROWS, tm), lambda i: (0, i)),
                   pl.BlockSpec((SLOT_ROWS, tm), lambda i: (0, i)),
                   pl.BlockSpec((1, N_EXPERTS, 1), lambda i: (i, 0, 0)),
                   pl.BlockSpec((1, N_EXPERTS, 1), lambda i: (i, 0, 0))],
        out_shape=[jax.ShapeDtypeStruct((SLOT_ROWS, t), jnp.int32),
                   jax.ShapeDtypeStruct((SLOT_ROWS, t), F32),
                   jax.ShapeDtypeStruct((nsb, N_EXPERTS, 1), jnp.int32),
                   jax.ShapeDtypeStruct((nsb, N_EXPERTS, 1), jnp.int32)],
        compiler_params=_params(("parallel",)),
    )(x2, mod3, mod3, norm_g, rw_t, rb_col)


def _moe_stream_kernel(starts_ref, plens_ref, x_ref, sc_ref, sh_ref, g2_ref, gn_ref, dest_ref, wgt_ref,
                       wg_ref, wu_ref, wd_ref, sg_ref, su_ref, sd_ref, nf_ref, o_ref, h_ref, *, final):
    blk, step = pl.program_id(0), pl.program_id(1)
    eps = wg_ref.shape[0]
    ne = pl.num_programs(1) * eps
    sub, win = MOE_SUB, MOE_WINDOW
    nsub = x_ref.shape[0] // sub

    @pl.when(step == 0)
    def _():
        h_ref[...] = _mx(_rms(x_ref[...], gn_ref[...]) * (1.0 + sc_ref[0]) + sh_ref[0])
        o_ref[...] = jnp.zeros_like(o_ref)

    def run(sb, j):
        i = (blk * nsub + sb) * ne + step * eps + j
        return starts_ref[i], plens_ref[i]

    runs = [[run(sb, j) for j in range(eps)] for sb in range(nsub)]
    r_col = lax.broadcasted_iota(jnp.int32, (win, 1), 0)
    r_row = lax.broadcasted_iota(jnp.int32, (1, win), 1)

    def window_pass(k):
        first = k * win
        picked = []
        for sb in range(nsub):
            tok = slice(sb * sub, (sb + 1) * sub)
            dest = dest_ref[:, tok]
            want = jnp.concatenate([st + first + r_col for st, _ in runs[sb]], axis=0)
            pick = jnp.zeros((eps * win, sub), F32)
            for s in range(TOP_K):
                pick = jnp.where(dest[s:s + 1, :] == want, 1.0, pick)
            picked.append(_dot(pick, h_ref[tok, :]))
        outs = []
        for j in range(eps):
            xj = jnp.concatenate([p[j * win:(j + 1) * win] for p in picked], axis=0)
            outs.append(_dot(_silu(_dot(xj, wg_ref[j])) * _dot(xj, wu_ref[j]), wd_ref[j]))
        for sb in range(nsub):
            tok = slice(sb * sub, (sb + 1) * sub)
            res = jnp.concatenate([o[sb * win:(sb + 1) * win] for o in outs], axis=0)
            have = jnp.concatenate([jnp.where(first + r_row < ln, st + first + r_row, -2)
                                    for st, ln in runs[sb]], axis=1).astype(F32)
            dest_t = jnp.transpose(dest_ref[:, tok].astype(F32))
            wgt_t = jnp.transpose(wgt_ref[:, tok])
            mix = jnp.zeros((sub, eps * win), F32)
            for s in range(TOP_K):
                mix = jnp.where(dest_t[:, s:s + 1] == have, wgt_t[:, s:s + 1], mix)
            o_ref[tok, :] += _dot(mix, res)

    window_pass(0)

    longest = runs[0][0][1]
    for per_sub in runs:
        for _, ln in per_sub:
            longest = jnp.maximum(longest, ln)

    def more(k):
        window_pass(k)
        return k + 1
    lax.while_loop(lambda k: k * win < longest, more, jnp.int32(1))

    @pl.when(step == pl.num_programs(1) - 1)
    def _():
        for sb in range(nsub):
            tok = slice(sb * sub, (sb + 1) * sub)
            hs = h_ref[tok, :]
            shared = _dot(_silu(_dot(hs, sg_ref[...])) * _dot(hs, su_ref[...]), sd_ref[...])
            xo = x_ref[tok, :] + g2_ref[0] * (shared + o_ref[tok, :])
            if final:
                xo = _rms(xo, nf_ref[...])
            o_ref[tok, :] = xo


def _moe_stream(x2, mod3, norm_g, dest, wgt, starts, plens, wg, wu, wd, sg, su, sd, nf, seq, final):
    t, d = x2.shape
    tm = min(MOE_STREAM_BLOCK, seq)
    ne, ff = wg.shape[0], wg.shape[2]
    eps = MOE_EXPERTS_PER_STEP
    full = lambda a: pl.BlockSpec(a.shape, lambda i, e, *_: (0,) * a.ndim)
    once = lambda a: pl.BlockSpec(a.shape, lambda i, e, *_: (0,) * a.ndim, pipeline_mode=pl.Buffered(1))
    mod = lambda j: pl.BlockSpec((1, 1, d), lambda i, e, *_: ((i * tm) // seq, 0, j))
    slot = pl.BlockSpec((SLOT_ROWS, tm), lambda i, e, *_: (0, i))
    grid_spec = pltpu.PrefetchScalarGridSpec(
        num_scalar_prefetch=2,
        grid=(t // tm, ne // eps),
        in_specs=[pl.BlockSpec((tm, d), lambda i, e, *_: (i, 0), pipeline_mode=pl.Buffered(1)),
                  mod(4), mod(3), mod(5),
                  pl.BlockSpec((1, d), lambda i, e, *_: (0, 0)), slot, slot,
                  pl.BlockSpec((eps, d, ff), lambda i, e, *_: (e, 0, 0)),
                  pl.BlockSpec((eps, d, ff), lambda i, e, *_: (e, 0, 0)),
                  pl.BlockSpec((eps, ff, d), lambda i, e, *_: (e, 0, 0)),
                  once(sg), once(su), once(sd), full(nf)],
        out_specs=pl.BlockSpec((tm, d), lambda i, e, *_: (i, 0), pipeline_mode=pl.Buffered(1)),
        scratch_shapes=[pltpu.VMEM((tm, d), MXU_DTYPE)])
    return pl.pallas_call(
        functools.partial(_moe_stream_kernel, final=final), name="moe",
        grid_spec=grid_spec,
        out_shape=jax.ShapeDtypeStruct((t, d), F32),
        compiler_params=_params(("parallel", "arbitrary"), vmem_mb=61),
    )(starts.reshape(-1), plens.reshape(-1), x2, mod3, mod3, mod3, norm_g, dest, wgt,
      wg, wu, wd, sg, su, sd, nf)


def _pack_w_in(w):
    d = w.shape[0]
    offs = [0]
    for n in IN_SIZES:
        offs.append(offs[-1] + n)
    (qa, ka, va, ga, alr, cq, ckv, kidx, widx, qc, kc, vc, ic, fc, oc, gates) = [
        w[:, offs[i]:offs[i + 1]] for i in range(len(IN_SIZES))]
    z = lambda n: jnp.zeros((d, n), w.dtype)
    packed = jnp.concatenate(
        [gates, qa, ka, va, ga, vc, oc, qc, kc, cq, ckv,
         kidx, widx, z(LANE - IDX_DIM - IDX_HEADS),
         alr, z(LANE - GLA_GATE_RANK),
         ic, fc, z(LANE - 2 * MLSTM_HEADS)], axis=1)
    assert packed.shape[1] == N_PACK
    return packed.astype(MXU_DTYPE)


def kernel(x, c, ada_w, ada_b, norm_mix, norm_ffn, w_in, gla_w_a2, gla_b_a, gla_norm, dsa_norm_q,
           dsa_norm_kv, dsa_w_uq, dsa_w_uk, dsa_w_uv, dsa_w_qi, mlstm_conv, mlstm_b_i, mlstm_b_f,
           mlstm_norm, w_up_a, w_up_b, w_up_c, w_o, router_w, router_bias, exp_w_gate, exp_w_up,
           exp_w_down, sh_w_gate, sh_w_up, sh_w_down, norm_final):
    b, s, d = x.shape
    depth = ada_w.shape[0]
    t = b * s
    mod = _ada_mod(c, ada_w, ada_b)
    x2 = x.reshape(t, d)
    row = lambda v: v.reshape(1, -1)
    for l in range(depth):
        mod3 = mod[l].reshape(b, 1, 6 * d)
        w_pack = _pack_w_in(w_in[l])
        proj2, tail2 = _in_proj(x2, mod3, row(norm_mix[l]), w_pack, s)
        proj3 = proj2.reshape(b, s, N_MAIN)
        tail3 = tail2.reshape(b, s, N_TAIL)

        wa2_pad = jnp.zeros((LANE, GLA_QK), F32).at[:GLA_GATE_RANK].set(gla_w_a2[l])
        ya_in = _gla(proj3, tail3, wa2_pad, row(gla_b_a[l]), row(gla_norm[l]))

        bias_row = jnp.zeros((1, LANE), F32).at[0, :MLSTM_HEADS].set(mlstm_b_i[l])
        bias_row = bias_row.at[0, MLSTM_HEADS:2 * MLSTM_HEADS].set(mlstm_b_f[l])
        yc_in = _mlstm(proj3, tail3, mlstm_conv[l], bias_row, row(mlstm_norm[l]))

        wuq = dsa_w_uq[l].reshape(DSA_Q_RANK, DSA_HEADS * DSA_HEAD_DIM).astype(MXU_DTYPE)
        wuk_t = jnp.transpose(dsa_w_uk[l], (1, 2, 0)).astype(MXU_DTYPE)
        wuv_t = jnp.transpose(dsa_w_uv[l], (1, 2, 0)).astype(MXU_DTYPE)
        wqi_pad = jnp.zeros((DSA_Q_RANK, IDX_HEADS, LANE), F32).at[:, :, :IDX_DIM].set(dsa_w_qi[l])
        wqi_pad = wqi_pad.reshape(DSA_Q_RANK, IDX_HEADS * LANE).astype(MXU_DTYPE)
        ckvn, kwb, qlat, qidx, wht = _dsa_prep(proj2, tail2, s, row(dsa_norm_q[l]), row(dsa_norm_kv[l]),
                                               wuq, wuk_t, wqi_pad)
        yb_in = _dsa(ckvn.reshape(b, s, DSA_AUG), kwb.reshape(b, s, LANE), qlat, qidx, wht, wuv_t)

        x2 = _merge(ya_in.reshape(t, GLA_V), yb_in.reshape(t, DSA_V), yc_in.reshape(t, MLSTM_V),
                    proj2, x2, mod3, w_up_a[l].astype(MXU_DTYPE), w_up_b[l].astype(MXU_DTYPE),
                    w_up_c[l].astype(MXU_DTYPE), w_o[l].astype(MXU_DTYPE), s)

        dest, wgt, starts, plens = _router(x2, mod3, row(norm_ffn[l]), jnp.transpose(router_w[l]),
                                           router_bias[l].reshape(-1, 1), s)
        x2 = _moe_stream(x2, mod3, row(norm_ffn[l]), dest, wgt, starts, plens,
                  exp_w_gate[l].astype(MXU_DTYPE), exp_w_up[l].astype(MXU_DTYPE),
                  exp_w_down[l].astype(MXU_DTYPE), sh_w_gate[l].astype(MXU_DTYPE),
                  sh_w_up[l].astype(MXU_DTYPE), sh_w_down[l].astype(MXU_DTYPE), row(norm_final), s,
                  final=(l == depth - 1))
    return x2.reshape(b, s, d)
```

```python
import functools
import struct

import jax
import jax.numpy as jnp
from jax import lax
from jax.experimental import pallas as pl
from jax.experimental.pallas import tpu as pltpu

F32 = jnp.float32
BF16 = jnp.bfloat16
MXU_DTYPE = jnp.bfloat16
ACT_DTYPE = jnp.bfloat16
HIGHEST = lax.Precision.HIGHEST

EPS = 1e-6
D_MODEL = 1024
GLA_HEADS, GLA_DK, GLA_DV, GLA_GATE_RANK, GLA_TAU, GLA_CHUNK = 4, 64, 128, 16, 16.0, 64
GLA_SUB = 16
DSA_HEADS, DSA_Q_RANK, DSA_KV_RANK, DSA_HEAD_DIM, DSA_V_DIM = 8, 256, 128, 64, 64
IDX_HEADS, IDX_DIM, DSA_TOPK_MAX = 8, 32, 256
MLSTM_HEADS, MLSTM_DQK, MLSTM_DV, MLSTM_CONV, MLSTM_CHUNK = 4, 64, 128, 4, 64
N_EXPERTS, TOP_K, N_GROUPS, TOPK_GROUPS, EXPERT_FF, SHARED_FF = 64, 6, 8, 4, 256, 256
ROUTED_SCALE = 2.5

GLA_QK = GLA_HEADS * GLA_DK
GLA_V = GLA_HEADS * GLA_DV
DSA_V = DSA_HEADS * DSA_V_DIM
MLSTM_QK = MLSTM_HEADS * MLSTM_DQK
MLSTM_V = MLSTM_HEADS * MLSTM_DV
IN_SIZES = (GLA_QK, GLA_QK, GLA_V, GLA_V, GLA_GATE_RANK,
            DSA_Q_RANK, DSA_KV_RANK, IDX_DIM, IDX_HEADS,
            MLSTM_QK, MLSTM_QK, MLSTM_V, MLSTM_HEADS, MLSTM_HEADS, MLSTM_V,
            3 * D_MODEL)

LANE = 128
KEY_TILE = 256
Q_TILE = 256
ATT_HEAD_GROUP = 4
NEG_BIG = -1e30
MOE_STREAM_BLOCK = 2048
MOE_SUB = 256
MOE_WINDOW = 48
MOE_EXPERTS_PER_STEP = 4
CONV_PAD = 8
SEQ_GROUP = 16
SLOT_ROWS = 8
ROW_ALIGN = 16

C_GATES = 0
C_QA = 3072
C_KA = 3328
C_VA = 3584
C_GA = 4096
C_VC = 4608
C_OC = 5120
C_QKC = 5632
C_CQ = 6144
N_MAIN = 6400
T_CKV = 0
T_KW = 128
T_ALR = 256
T_ICFC = 384
N_TAIL = 512
N_PACK = N_MAIN + N_TAIL
INPROJ_PIECES = 5
W_IDX_LANE = IDX_DIM


LOG2E = 1.4426950408889634
DSA_AUG = DSA_KV_RANK + LANE
POS_SPLIT = 64
SLOPE_PIECES = 3
SUBLANE = 8
PARTIAL_ROWS = 32
COUNT_UNSET = float(2 ** 30)


def _bf16_pieces(x, n):
    out = []
    for _ in range(n):
        bits = struct.unpack("<I", struct.pack("<f", x))[0]
        bits = (bits + 0x7FFF + ((bits >> 16) & 1)) & 0xFFFF0000
        piece = struct.unpack("<f", struct.pack("<I", bits))[0]
        out.append(piece)
        x -= piece
    return out


def _mx(x):
    return x.astype(MXU_DTYPE)


def _dot(a, b):
    return jnp.dot(_mx(a), _mx(b), preferred_element_type=F32)


def _dot_nt(a, b):
    return lax.dot_general(_mx(a), _mx(b), (((1,), (1,)), ((), ())), preferred_element_type=F32)


def _dot_tn(a, b):
    return lax.dot_general(_mx(a), _mx(b), (((0,), (0,)), ((), ())), preferred_element_type=F32)


def _rms(x, g):
    return x * lax.rsqrt(jnp.mean(x * x, axis=-1, keepdims=True) + EPS) * g


def _silu(x):
    return x * jax.nn.sigmoid(x)


def _log_sigmoid(z):
    return jnp.minimum(z, 0.0) - jnp.log1p(jnp.exp(-jnp.abs(z)))


def _cumsum_rows(x):
    n = x.shape[0]
    tri = (lax.broadcasted_iota(jnp.int32, (n, n), 1) <= lax.broadcasted_iota(jnp.int32, (n, n), 0)).astype(F32)
    return jnp.dot(tri, x, preferred_element_type=F32, precision=HIGHEST)


def _truncate_to_bf16(x):
    bits = lax.bitcast_convert_type(x, jnp.int32) & jnp.int32(-65536)
    return lax.bitcast_convert_type(bits, F32).astype(BF16)


def _colreduce(x, op, width=PARTIAL_ROWS):
    n, c = x.shape
    return op(op(x.reshape(n // width, width, c), axis=0), axis=0, keepdims=True)


def _params(sem, vmem_mb=40):
    return pltpu.CompilerParams(dimension_semantics=sem, vmem_limit_bytes=vmem_mb * 1024 * 1024)


def _ada_kernel(c_ref, w_ref, b_ref, o_ref):
    cs = _silu(c_ref[...])
    o_ref[0] = jnp.dot(cs, w_ref[0], preferred_element_type=F32, precision=HIGHEST) + b_ref[0]


def _ada_mod(c, ada_w, ada_b):
    depth, d, n = ada_w.shape
    b = c.shape[0]
    return pl.pallas_call(
        _ada_kernel, name="ada_mod",
        grid=(depth, n // d),
        in_specs=[pl.BlockSpec((b, d), lambda l, j: (0, 0)),
                  pl.BlockSpec((1, d, d), lambda l, j: (l, 0, j)),
                  pl.BlockSpec((1, 1, d), lambda l, j: (l, 0, j))],
        out_specs=pl.BlockSpec((1, b, d), lambda l, j: (l, 0, j)),
        out_shape=jax.ShapeDtypeStruct((depth, b, n), F32),
        compiler_params=_params(("parallel", "parallel")),
    )(c, ada_w, ada_b.reshape(depth, 1, n))


def _inproj_kernel(x_ref, sc_ref, sh_ref, g_ref, wm_ref, wt_ref, om_ref, ot_ref):
    h = _mx(_rms(x_ref[...], g_ref[...]) * (1.0 + sc_ref[0]) + sh_ref[0])
    piece = N_MAIN // INPROJ_PIECES
    for j in range(INPROJ_PIECES):
        cols = slice(j * piece, (j + 1) * piece)
        om_ref[:, cols] = jnp.dot(h, wm_ref[:, cols], preferred_element_type=F32).astype(om_ref.dtype)
    ot_ref[...] = jnp.dot(h, wt_ref[...], preferred_element_type=F32)


def _in_proj(x2, mod3, norm_g, w_pack, seq):
    t, d = x2.shape
    tm = min(512, seq)
    once = lambda w: pl.BlockSpec((d, w), lambda i: (0, 0), pipeline_mode=pl.Buffered(1))
    return pl.pallas_call(
        _inproj_kernel, name="in_proj",
        grid=(t // tm,),
        in_specs=[pl.BlockSpec((tm, d), lambda i: (i, 0)),
                  pl.BlockSpec((1, 1, d), lambda i: ((i * tm) // seq, 0, 1)),
                  pl.BlockSpec((1, 1, d), lambda i: ((i * tm) // seq, 0, 0)),
                  pl.BlockSpec((1, d), lambda i: (0, 0)),
                  once(N_MAIN), once(N_TAIL)],
        out_specs=[pl.BlockSpec((tm, N_MAIN), lambda i: (i, 0)),
                   pl.BlockSpec((tm, N_TAIL), lambda i: (i, 0))],
        out_shape=[jax.ShapeDtypeStruct((t, N_MAIN), ACT_DTYPE),
                   jax.ShapeDtypeStruct((t, N_TAIL), F32)],
        compiler_params=_params(("parallel",), vmem_mb=48),
    )(x2, mod3, mod3, norm_g, w_pack[:, :N_MAIN], w_pack[:, N_MAIN:])


def _gla_kernel(q_ref, k_ref, v_ref, g_ref, alr_ref, wa2_ref, ba_ref, gn_ref, o_ref, s_ref, acc_ref):
    @pl.when(pl.program_id(1) == 0)
    def _():
        s_ref[...] = jnp.zeros_like(s_ref)

    L, sub, nh, dk, dv = GLA_CHUNK, GLA_SUB, GLA_HEADS, GLA_DK, GLA_DV
    seqs = range(q_ref.shape[0])
    heads = range(nh)
    hk = lambda h: slice(h * dk, (h + 1) * dk)
    hv = lambda h: slice(h * dv, (h + 1) * dv)

    pre = []
    for g in seqs:
        z = jnp.dot(alr_ref[g], wa2_ref[...], preferred_element_type=F32, precision=HIGHEST) + ba_ref[...]
        cum = _cumsum_rows(_log_sigmoid(z) * (1.0 / GLA_TAU))
        q = q_ref[g].astype(F32) * (dk ** -0.5)
        k = k_ref[g].astype(F32)
        tot = cum[L - 1:L, :]
        pre.append(dict(cum=cum, q=q, k=k, tot=tot, vb=_mx(v_ref[g]),
                        q_in=_mx(q * jnp.exp(cum)), k_dec=_mx(k * jnp.exp(tot - cum))))

    scores = {}
    for i in range(L // sub):
        r0, r1 = i * sub, (i + 1) * sub
        for g in seqs:
            p = pre[g]
            base = p["cum"][r0 - 1:r0, :] if i > 0 else jnp.zeros_like(p["tot"])
            qi = _mx(p["q"][r0:r1] * jnp.exp(p["cum"][r0:r1] - base))
            ka = _mx(p["k"][:r1] * jnp.exp(base - p["cum"][:r1]))
            for h in heads:
                scores[g, i, h] = _dot_nt(qi[:, hk(h)], ka[:, hk(h)])

    for i in range(L // sub):
        r0, r1 = i * sub, (i + 1) * sub
        causal = (lax.broadcasted_iota(jnp.int32, (sub, r1), 1)
                  <= lax.broadcasted_iota(jnp.int32, (sub, r1), 0) + r0)
        for g in seqs:
            for h in heads:
                s = jnp.where(causal, scores[g, i, h], 0.0)
                acc_ref[g, r0:r1, hv(h)] = _dot(s, pre[g]["vb"][:r1, hv(h)])

    inter = {(g, h): _dot(pre[g]["q_in"][:, hk(h)], s_ref[g * nh + h]) for g in seqs for h in heads}
    update = {(g, h): _dot_tn(pre[g]["k_dec"][:, hk(h)], pre[g]["vb"][:, hv(h)]) for g in seqs for h in heads}

    gn = gn_ref[...]
    for g in seqs:
        gate = g_ref[g].astype(F32)
        for h in heads:
            o = acc_ref[g, :, hv(h)] + inter[g, h]
            y = _rms(o, gn[:, hv(h)]) * _silu(gate[:, hv(h)])
            o_ref[g, :, hv(h)] = y.astype(o_ref.dtype)
            decay = jnp.transpose(jnp.exp(pre[g]["tot"][:, hk(h)]))
            s_ref[g * nh + h] = s_ref[g * nh + h] * decay + update[g, h]


def _seq_group(batch):
    for grp in (SEQ_GROUP, 2, 1):
        if batch % grp == 0:
            return grp


def _gla(proj3, tail3, wa2_pad, ba, gn):
    b, s, _ = proj3.shape
    L = GLA_CHUNK
    grp = _seq_group(b)
    blk = lambda w, c0: pl.BlockSpec((grp, L, w), lambda bi, ci: (bi, ci, c0 // w))
    full = lambda a: pl.BlockSpec(a.shape, lambda bi, ci: (0,) * a.ndim)
    return pl.pallas_call(
        _gla_kernel, name="gla",
        grid=(b // grp, s // L),
        in_specs=[blk(GLA_QK, C_QA), blk(GLA_QK, C_KA), blk(GLA_V, C_VA), blk(GLA_V, C_GA),
                  blk(LANE, T_ALR), full(wa2_pad), full(ba), full(gn)],
        out_specs=pl.BlockSpec((grp, L, GLA_V), lambda bi, ci: (bi, ci, 0)),
        out_shape=jax.ShapeDtypeStruct((b, s, GLA_V), ACT_DTYPE),
        scratch_shapes=[pltpu.VMEM((grp * GLA_HEADS, GLA_DK, GLA_DV), F32),
                        pltpu.VMEM((grp, L, GLA_V), F32)],
        compiler_params=_params(("parallel", "arbitrary")),
    )(proj3, proj3, proj3, proj3, tail3, wa2_pad, ba, gn)


def _mlstm_pair_kernel(qk_ref, v_ref, oc_ref, if_ref, conv_ref, bias_ref, gnt_ref, o_ref,
                       xbuf_ref, ct_ref, n_ref, m_ref):
    L, nh, dk, dv, kc = MLSTM_CHUNK, MLSTM_HEADS, MLSTM_DQK, MLSTM_DV, MLSTM_CONV
    pad = CONV_PAD
    npair = nh // 2
    assert 2 * dk == LANE and dv == LANE and L == dk

    @pl.when(pl.program_id(1) == 0)
    def _():
        xbuf_ref[:, 0:pad, :] = jnp.zeros((xbuf_ref.shape[0], pad, 2 * MLSTM_QK), F32)
        ct_ref[...] = jnp.zeros_like(ct_ref)
        n_ref[...] = jnp.zeros_like(n_ref)
        m_ref[...] = jnp.zeros_like(m_ref)

    lane = lax.broadcasted_iota(jnp.int32, (1, LANE), 1)
    half = [(lane < dk).astype(F32), (lane >= dk).astype(F32)]
    s_idx = lax.broadcasted_iota(jnp.int32, (L, LANE), 0)
    t_idx = lax.broadcasted_iota(jnp.int32, (L, LANE), 1) % L
    causal = s_idx <= t_idx
    lane_in = lax.broadcasted_iota(jnp.int32, (LANE, LANE), 0)
    head_of = lax.broadcasted_iota(jnp.int32, (LANE, LANE), 1) // dk
    cw = conv_ref[...]

    combos = [(g, p) for g in range(qk_ref.shape[0]) for p in range(npair)]
    seq = {}
    for g in range(qk_ref.shape[0]):
        xbuf_ref[g, pad:pad + L, :] = qk_ref[g].astype(F32)
        conv = jnp.zeros((L, 2 * MLSTM_QK), F32)
        for j in range(kc):
            conv = conv + cw[j:j + 1, :] * xbuf_ref[g, pl.ds(pad - (kc - 1) + j, L), :]
        xbuf_ref[g, 0:pad, :] = xbuf_ref[g, L:L + pad, :]
        qk = _silu(conv)
        pre = if_ref[g] + bias_ref[...]
        bcum = _cumsum_rows(_log_sigmoid(pre))
        v = v_ref[g].astype(F32)
        seq[g] = dict(q=qk[:, :MLSTM_QK] * (dk ** -0.5), k=qk[:, MLSTM_QK:],
                      vt=[_mx(jnp.transpose(v[:, h * dv:(h + 1) * dv])) for h in range(nh)],
                      gate_mix=jnp.where(lane < nh, pre, -bcum), bcum=bcum, bcum_t=jnp.transpose(bcum))

    st = {}
    for g, p in combos:
        sq = seq[g]
        sel = jnp.where((lane_in == 2 * p + head_of) | (lane_in == nh + 2 * p + head_of), 1.0, 0.0)
        selb = jnp.where(lane_in == nh + 2 * p + head_of, 1.0, 0.0)
        d_mat = jnp.dot(sq["gate_mix"], sel, preferred_element_type=F32, precision=HIGHEST)
        tot = jnp.dot(sq["bcum"][L - SUBLANE:L, :], selb, preferred_element_type=F32,
                      precision=HIGHEST)[SUBLANE - 1:SUBLANE, :]
        b_row = jnp.concatenate([sq["bcum_t"][nh + 2 * p:nh + 2 * p + 1, :],
                                 sq["bcum_t"][nh + 2 * p + 1:nh + 2 * p + 2, :]], axis=1)
        qt = sq["q"][:, p * LANE:(p + 1) * LANE]
        kt = sq["k"][:, p * LANE:(p + 1) * LANE]
        q_bd = _mx(jnp.concatenate([qt * half[0], qt * half[1]], axis=0))
        st[g, p] = dict(d=d_mat, tot=tot, b_row=b_row, kt=kt, q_bd=q_bd,
                        scores=_dot_nt(kt, q_bd),
                        qn=_dot_nt(jnp.broadcast_to(n_ref[g * npair + p], (SUBLANE, LANE)), q_bd)[0:1, :],
                        inter=[_dot_nt(ct_ref[g * nh + 2 * p + hh], q_bd) for hh in range(2)])

    for g, p in combos:
        c = st[g, p]
        m_prev = m_ref[g * npair + p]
        dlog = jnp.where(causal, c["b_row"] + c["d"], -jnp.inf)
        inter_log = c["b_row"] + m_prev
        m_t = jnp.maximum(inter_log, jnp.max(dlog, axis=0, keepdims=True))
        g_log = c["tot"] + c["d"]
        m_new = jnp.maximum(c["tot"] + m_prev, jnp.max(g_log, axis=0, keepdims=True))
        c.update(m_t=m_t, w_inter=jnp.exp(inter_log - m_t), sw=c["scores"] * jnp.exp(dlog - m_t),
                 m_new=m_new, w_c=jnp.exp(c["tot"] + m_prev - m_new), ks=c["kt"] * jnp.exp(g_log - m_new))

    for g, p in combos:
        c = st[g, p]
        vt = seq[g]["vt"]
        c["sv"] = [_dot(vt[2 * p + hh], c["sw"] * half[hh]) for hh in range(2)]
        c["kv"] = [_dot(vt[2 * p + hh], c["ks"] * half[hh]) for hh in range(2)]

    for g, p in combos:
        c = st[g, p]
        num = c["sv"][0] + c["sv"][1] + c["w_inter"] * (c["inter"][0] + c["inter"][1])
        den = jnp.sum(c["sw"], axis=0, keepdims=True) + c["w_inter"] * c["qn"]
        hout = num / jnp.maximum(jnp.abs(den), jnp.exp(-c["m_t"]))
        y_t = hout * lax.rsqrt(jnp.mean(hout * hout, axis=0, keepdims=True) + EPS) * gnt_ref[p]
        y = jnp.transpose(y_t)
        for hh in range(2):
            cols = slice((2 * p + hh) * dv, (2 * p + hh + 1) * dv)
            o_ref[g, :, cols] = (y[hh * L:(hh + 1) * L, :] * jax.nn.sigmoid(oc_ref[g, :, cols].astype(F32))).astype(o_ref.dtype)
            ct_ref[g * nh + 2 * p + hh] = ct_ref[g * nh + 2 * p + hh] * c["w_c"] + c["kv"][hh]
        n_ref[g * npair + p] = n_ref[g * npair + p] * c["w_c"] + jnp.sum(c["ks"], axis=0, keepdims=True)
        m_ref[g * npair + p] = c["m_new"]


def _mlstm(proj3, tail3, conv_w, bias_row, gn):
    b, s, _ = proj3.shape
    L = MLSTM_CHUNK
    grp = _seq_group(b)
    blk = lambda w, c0: pl.BlockSpec((grp, L, w), lambda bi, ci: (bi, ci, c0 // w))
    full = lambda a: pl.BlockSpec(a.shape, lambda bi, ci: (0,) * a.ndim)
    npair = MLSTM_HEADS // 2
    gnt = jnp.repeat(jnp.transpose(gn.reshape(npair, 2, MLSTM_DV), (0, 2, 1)), L, axis=2)
    return pl.pallas_call(
        _mlstm_pair_kernel, name="mlstm",
        grid=(b // grp, s // L),
        in_specs=[blk(2 * MLSTM_QK, C_QKC), blk(MLSTM_V, C_VC), blk(MLSTM_V, C_OC), blk(LANE, T_ICFC),
                  full(conv_w), full(bias_row), full(gnt)],
        out_specs=pl.BlockSpec((grp, L, MLSTM_V), lambda bi, ci: (bi, ci, 0)),
        out_shape=jax.ShapeDtypeStruct((b, s, MLSTM_V), ACT_DTYPE),
        scratch_shapes=[pltpu.VMEM((grp, L + CONV_PAD, 2 * MLSTM_QK), F32),
                        pltpu.VMEM((grp * MLSTM_HEADS, MLSTM_DV, LANE), F32),
                        pltpu.VMEM((grp * npair, 1, LANE), F32),
                        pltpu.VMEM((grp * npair, 1, LANE), F32)],
        compiler_params=_params(("parallel", "arbitrary")),
    )(proj3, proj3, proj3, tail3, conv_w, bias_row, gnt)


def _dsa_prep_kernel(cq_ref, ckv_ref, kw_ref, nq_ref, nkv_ref, wuq_ref, wuk_ref, wqi_ref,
                     ckvn_ref, kwb_ref, qlat_ref, qidx_ref, wht_ref, *, tiles_per_seq):
    tm = cq_ref.shape[0]
    r = DSA_KV_RANK
    cqn = _mx(_rms(cq_ref[...].astype(F32), nq_ref[...]))
    pos = (pl.program_id(0) % tiles_per_seq) * tm + lax.broadcasted_iota(jnp.int32, (tm, LANE), 0)
    lane = lax.broadcasted_iota(jnp.int32, (tm, LANE), 1)
    pos_cols = jnp.where(lane < SLOPE_PIECES, pos >> (POS_SPLIT.bit_length() - 1),
                         jnp.where(lane < 2 * SLOPE_PIECES, pos & (POS_SPLIT - 1), 0)).astype(F32)
    ckvn_ref[:, :r] = _rms(ckv_ref[...], nkv_ref[...]).astype(ckvn_ref.dtype)
    ckvn_ref[:, r:] = pos_cols.astype(ckvn_ref.dtype)
    kw = kw_ref[...]
    kwb_ref[...] = kw.astype(kwb_ref.dtype)
    wht_ref[0] = jnp.transpose(kw)[W_IDX_LANE:W_IDX_LANE + IDX_HEADS, :] * (IDX_HEADS ** -0.5)
    q = jnp.dot(cqn, wuq_ref[...], preferred_element_type=F32)
    for h in range(DSA_HEADS):
        ql = _dot(q[:, h * DSA_HEAD_DIM:(h + 1) * DSA_HEAD_DIM], wuk_ref[h]) * (DSA_HEAD_DIM ** -0.5 * LOG2E)
        qlat_ref[0, h, :, :r] = ql.astype(qlat_ref.dtype)
        c = _bf16_pieces(2.0 ** (-8.0 * (h + 1) / DSA_HEADS) * LOG2E, SLOPE_PIECES)
        consts = [POS_SPLIT * v for v in c] + c
        slope_cols = jnp.zeros((tm, LANE), F32)
        for j, v in enumerate(consts):
            slope_cols = jnp.where(lane == j, v, slope_cols)
        qlat_ref[0, h, :, r:] = slope_cols.astype(qlat_ref.dtype)
    qi = jnp.dot(cqn, wqi_ref[...], preferred_element_type=F32) * (IDX_DIM ** -0.5)
    for h in range(IDX_HEADS):
        qidx_ref[0, h] = qi[:, h * LANE:(h + 1) * LANE].astype(qidx_ref.dtype)


def _dsa_prep(proj2, tail2, seq, nq, nkv, wuq, wuk_t, wqi_pad):
    t = proj2.shape[0]
    b = t // seq
    tm = min(512, seq)
    per = seq // tm
    blk = lambda w, c0: pl.BlockSpec((tm, w), lambda i: (i, c0 // w))
    full = lambda a: pl.BlockSpec(a.shape, lambda i: (0,) * a.ndim)
    hmap = lambda i: (i // per, 0, i % per, 0)
    return pl.pallas_call(
        functools.partial(_dsa_prep_kernel, tiles_per_seq=per), name="dsa_prep",
        grid=(t // tm,),
        in_specs=[blk(DSA_Q_RANK, C_CQ), blk(DSA_KV_RANK, T_CKV), blk(LANE, T_KW),
                  full(nq), full(nkv), full(wuq), full(wuk_t), full(wqi_pad)],
        out_specs=[pl.BlockSpec((tm, DSA_AUG), lambda i: (i, 0)),
                   pl.BlockSpec((tm, LANE), lambda i: (i, 0)),
                   pl.BlockSpec((1, DSA_HEADS, tm, DSA_AUG), hmap),
                   pl.BlockSpec((1, IDX_HEADS, tm, LANE), hmap),
                   pl.BlockSpec((1, IDX_HEADS, tm), lambda i: (i // per, 0, i % per))],
        out_shape=[jax.ShapeDtypeStruct((t, DSA_AUG), ACT_DTYPE),
                   jax.ShapeDtypeStruct((t, LANE), ACT_DTYPE),
                   jax.ShapeDtypeStruct((b, DSA_HEADS, seq, DSA_AUG), ACT_DTYPE),
                   jax.ShapeDtypeStruct((b, IDX_HEADS, seq, LANE), ACT_DTYPE),
                   jax.ShapeDtypeStruct((b, IDX_HEADS, seq), F32)],
        compiler_params=_params(("parallel",)),
    )(proj2, tail2, tail2, nq, nkv, wuq, wuk_t, wqi_pad)


def _dsa_kernel(qlat_ref, qidx_ref, wht_ref, kwk_ref, ckv_ref, wuv_ref, o_ref,
                ibuf_ref, ihi_ref, acc_ref, *, topk):
    nh, tq, tk = DSA_HEADS, Q_TILE, KEY_TILE
    qb = pl.program_id(1)
    n_kt = (qb * tq + tq + tk - 1) // tk
    t_row = qb * tq + lax.broadcasted_iota(jnp.int32, (1, tq), 1)
    s_col = lax.broadcasted_iota(jnp.int32, (tk, 1), 0)

    qi = qidx_ref[0].reshape(IDX_HEADS * tq, LANE)
    wht = wht_ref[0]

    def idx_body(kt, carry):
        kk = kwk_ref[0, pl.ds(pl.multiple_of(kt * tk, tk), tk), :]
        grp = ATT_HEAD_GROUP
        sc = [_dot_nt(kk, qi[g * grp * tq:(g + 1) * grp * tq]) for g in range(IDX_HEADS // grp)]
        tot = None
        for h in range(IDX_HEADS):
            part = wht[h:h + 1, :] * jnp.maximum(sc[h // grp][:, (h % grp) * tq:(h % grp + 1) * tq], 0.0)
            tot = part if tot is None else tot + part
        score = jnp.where(kt * tk + s_col <= t_row, tot, -jnp.inf)
        ibuf_ref[kt] = score
        ihi_ref[kt] = _truncate_to_bf16(score)
        return carry

    lax.fori_loop(0, n_kt, idx_body, 0)

    def count(pred):
        def body(kt, c):
            hit = jnp.where(pred(ibuf_ref[kt]), 1.0, 0.0)
            return c + jnp.sum(hit.reshape(tk // PARTIAL_ROWS, PARTIAL_ROWS, tq), axis=0)
        return jnp.sum(lax.fori_loop(0, n_kt, body, jnp.zeros((PARTIAL_ROWS, tq), F32)), axis=0, keepdims=True)

    def count_ge(cand):
        return count(lambda x: x >= cand)

    def count_gt(cand):
        return count(lambda x: x > cand)

    def key_to_float(u):
        key = u ^ jnp.int32(-2 ** 31)
        bits = jnp.where(key >= 0, key, key ^ jnp.int32(0x7FFFFFFF))
        return lax.bitcast_convert_type(bits, F32)

    few = t_row < topk
    n_bits = 32

    def try_bit(i, u, cnt_u, counter):
        cand_u = u | lax.shift_left(jnp.int32(1), n_bits - 1 - i)
        cnt = counter(cand_u)
        ok = cnt >= topk
        return jnp.where(ok, cand_u, u), jnp.where(ok, cnt, cnt_u)

    def count_ge_hi(cand_u):
        cand = _truncate_to_bf16(key_to_float(cand_u))
        def body(kt, c):
            hit = jnp.where(ihi_ref[kt] >= cand, jnp.ones((), BF16), jnp.zeros((), BF16))
            part = jnp.sum(hit.reshape(tk // PARTIAL_ROWS, PARTIAL_ROWS, tq), axis=0)
            return c + part.astype(F32)
        return jnp.sum(lax.fori_loop(0, n_kt, body, jnp.zeros((PARTIAL_ROWS, tq), F32)), axis=0, keepdims=True)

    def coarse_body(i, st):
        return try_bit(i, *st, count_ge_hi)

    def search_cond(st):
        i, _, _, pending = st
        return (i < n_bits) & (pending > 0)

    def search_body(st):
        i, u, cnt_u, _ = st
        for _ in range(4):
            u, cnt_u = try_bit(i, u, cnt_u, lambda c: count_ge(key_to_float(c)))
            i = i + 1
        pending = jnp.max(jnp.where(few | (cnt_u == topk), 0, 1))
        return i, u, cnt_u, pending

    u, cnt_u = lax.fori_loop(0, n_bits // 2, coarse_body,
                             (jnp.zeros((1, tq), jnp.int32), jnp.full((1, tq), COUNT_UNSET, F32)))
    start = (jnp.int32(n_bits // 2), u, cnt_u, jnp.max(jnp.where(few | (cnt_u == topk), 0, 1)))
    _, u, _, _ = lax.while_loop(search_cond, search_body, start)
    tau = jnp.where(few, -jnp.inf, key_to_float(u))
    need = topk - count_gt(tau)

    acc_ref[...] = jnp.zeros_like(acc_ref)
    ql = qlat_ref[0].reshape(nh * tq, DSA_AUG)
    tri = jnp.where(lax.broadcasted_iota(jnp.int32, (tk, tk), 0) >= lax.broadcasted_iota(jnp.int32, (tk, tk), 1),
                    1.0, 0.0).astype(BF16)

    def keys(kt):
        return ckv_ref[0, pl.ds(pl.multiple_of(kt * tk, tk), tk), :]

    def att_body(kt, carry):
        m, l, eq_seen = carry
        kv_aug = keys(kt)
        kv = kv_aug[:, :DSA_KV_RANK]
        it = ibuf_ref[kt]
        eq = it == tau
        eqf = jnp.where(eq, 1.0, 0.0)
        rank = jnp.dot(tri, eqf.astype(BF16), preferred_element_type=F32) + eq_seen
        valid = ((it > tau) | (eq & (rank <= need))) & (kt * tk + s_col <= t_row)
        bias = jnp.where(valid, 0.0, NEG_BIG)

        n_grp = nh // ATT_HEAD_GROUP
        gcols = lambda g: slice(g * ATT_HEAD_GROUP * tq, (g + 1) * ATT_HEAD_GROUP * tq)
        lg, soft, ms, ls = {}, {}, {}, {}

        def logits(g):
            lg[g] = _dot_nt(kv_aug, ql[gcols(g)])

        def softmax(g):
            als, ps = [], []
            for h in range(g * ATT_HEAD_GROUP, (g + 1) * ATT_HEAD_GROUP):
                cols = slice(h * tq, (h + 1) * tq)
                loc = slice((h - g * ATT_HEAD_GROUP) * tq, (h - g * ATT_HEAD_GROUP + 1) * tq)
                lh = lg[g][:, loc] + bias
                m_old = m[:, cols]
                m_new = jnp.maximum(m_old, _colreduce(lh, jnp.max))
                p = jnp.exp2(lh - m_new)
                alpha = jnp.exp2(m_old - m_new)
                ms[h] = m_new
                ls[h] = alpha * l[:, cols] + _colreduce(p, jnp.sum)
                als.append(alpha)
                ps.append(_mx(p))
            soft[g] = (jnp.concatenate(als, axis=1), jnp.concatenate(ps, axis=1))

        def weighted_values(g):
            alpha, p = soft[g]
            acc_ref[:, gcols(g)] = alpha * acc_ref[:, gcols(g)] + _dot_tn(kv, p)

        logits(0)
        for g in range(n_grp):
            if g + 1 < n_grp:
                logits(g + 1)
            softmax(g)
            if g > 0:
                weighted_values(g - 1)
        weighted_values(n_grp - 1)
        return (jnp.concatenate([ms[h] for h in range(nh)], axis=1),
                jnp.concatenate([ls[h] for h in range(nh)], axis=1),
                eq_seen + jnp.sum(eqf, axis=0, keepdims=True))

    init = (jnp.full((1, nh * tq), NEG_BIG, F32), jnp.zeros((1, nh * tq), F32), jnp.zeros((1, tq), F32))
    _, l, _ = lax.fori_loop(0, n_kt, att_body, init)

    outs = []
    for h in range(nh):
        cols = slice(h * tq, (h + 1) * tq)
        outs.append(_dot(wuv_ref[h], acc_ref[:, cols] / l[:, cols]))
    o_ref[0] = jnp.transpose(jnp.concatenate(outs, axis=0)).astype(o_ref.dtype)


def _dsa(ckvn3, kwb3, qlat, qidx, wht, wuv_t):
    b, s, _ = ckvn3.shape
    topk = min(DSA_TOPK_MAX, s // 4)
    nh, tq = DSA_HEADS, Q_TILE
    return pl.pallas_call(
        functools.partial(_dsa_kernel, topk=topk), name="dsa_attn",
        grid=(b, s // tq),
        in_specs=[pl.BlockSpec((1, nh, tq, DSA_AUG), lambda bi, qi: (bi, 0, qi, 0)),
                  pl.BlockSpec((1, IDX_HEADS, tq, LANE), lambda bi, qi: (bi, 0, qi, 0)),
                  pl.BlockSpec((1, IDX_HEADS, tq), lambda bi, qi: (bi, 0, qi)),
                  pl.BlockSpec((1, s, LANE), lambda bi, qi: (bi, 0, 0)),
                  pl.BlockSpec((1, s, DSA_AUG), lambda bi, qi: (bi, 0, 0)),
                  pl.BlockSpec(wuv_t.shape, lambda bi, qi: (0, 0, 0))],
        out_specs=pl.BlockSpec((1, tq, DSA_V), lambda bi, qi: (bi, qi, 0)),
        out_shape=jax.ShapeDtypeStruct((b, s, DSA_V), ACT_DTYPE),
        scratch_shapes=[pltpu.VMEM((s // KEY_TILE, KEY_TILE, tq), F32),
                        pltpu.VMEM((s // KEY_TILE, KEY_TILE, tq), BF16),
                        pltpu.VMEM((DSA_KV_RANK, nh * tq), F32)],
        compiler_params=_params(("parallel", "arbitrary")),
    )(qlat, qidx, wht, kwb3, ckvn3, wuv_t)


def _merge_kernel(a_ref, b_ref, c_ref, gt_ref, x_ref, g1_ref, wa_ref, wb_ref, wc_ref, wo_ref, o_ref):
    d = D_MODEL
    g = jax.nn.sigmoid(gt_ref[...].astype(F32))
    ya = jnp.dot(a_ref[...], wa_ref[...], preferred_element_type=F32)
    yb = jnp.dot(b_ref[...], wb_ref[...], preferred_element_type=F32)
    yc = jnp.dot(c_ref[...], wc_ref[...], preferred_element_type=F32)
    m = g[:, :d] * ya + g[:, d:2 * d] * yb + g[:, 2 * d:] * yc
    o_ref[...] = x_ref[...] + g1_ref[0] * _dot(m, wo_ref[...])


def _merge(ya_in, yb_in, yc_in, proj2, x2, mod3, wa, wb, wc, wo, seq):
    t, d = x2.shape
    tm = min(512, seq)
    full = lambda a: pl.BlockSpec(a.shape, lambda i: (0,) * a.ndim)
    br = lambda w: pl.BlockSpec((tm, w), lambda i: (i, 0))
    return pl.pallas_call(
        _merge_kernel, name="merge",
        grid=(t // tm,),
        in_specs=[br(GLA_V), br(DSA_V), br(MLSTM_V), br(3 * d), br(d),
                  pl.BlockSpec((1, 1, d), lambda i: ((i * tm) // seq, 0, 2)),
                  full(wa), full(wb), full(wc), full(wo)],
        out_specs=br(d),
        out_shape=jax.ShapeDtypeStruct((t, d), F32),
        compiler_params=_params(("parallel",)),
    )(ya_in, yb_in, yc_in, proj2, x2, mod3, wa, wb, wc, wo)


def _first_argmax_mask(cur, iota, axis, n):
    mx = jnp.max(cur, axis=axis, keepdims=True)
    ix = jnp.min(jnp.where(cur == mx, iota, n), axis=axis, keepdims=True)
    return iota == ix


def _router_kernel(x_ref, sc_ref, sh_ref, g_ref, rwt_ref, rb_ref, dest_ref, wgt_ref, starts_ref, plens_ref):
    ne, ng = N_EXPERTS, N_GROUPS
    eg = ne // ng
    h = _rms(x_ref[...], g_ref[...]) * (1.0 + sc_ref[0]) + sh_ref[0]
    tm = h.shape[0]
    logits = lax.dot_general(rwt_ref[...], h, (((1,), (1,)), ((), ())),
                             preferred_element_type=F32, precision=HIGHEST)
    scores = jax.nn.sigmoid(logits)
    sel = scores + rb_ref[...]
    s3 = sel.reshape(ng, eg, tm)
    io3 = lax.broadcasted_iota(jnp.int32, (ng, eg, tm), 1)
    m1 = jnp.max(s3, axis=1, keepdims=True)
    first = _first_argmax_mask(s3, io3, 1, eg)
    m2 = jnp.max(jnp.where(first, -jnp.inf, s3), axis=1, keepdims=True)
    gs = (m1 + m2).reshape(ng, tm)
    iog = lax.broadcasted_iota(jnp.int32, (ng, tm), 0)
    gkeep = jnp.zeros((ng, tm), F32)
    cur = gs
    for _ in range(TOPK_GROUPS):
        hit = _first_argmax_mask(cur, iog, 0, ng)
        gkeep = jnp.where(hit, 1.0, gkeep)
        cur = jnp.where(hit, -jnp.inf, cur)
    selm = jnp.where(gkeep.reshape(ng, 1, tm) > 0.0, s3, -jnp.inf).reshape(ne, tm)
    ioe = lax.broadcasted_iota(jnp.int32, (ne, tm), 0)
    hits = []
    chosen = jnp.zeros((ne, tm), F32)
    cur = selm
    for _ in range(TOP_K):
        hit = _first_argmax_mask(cur, ioe, 0, ne)
        hits.append(hit)
        chosen = jnp.where(hit, 1.0, chosen)
        cur = jnp.where(hit, -jnp.inf, cur)
    w = chosen * scores
    w = w / jnp.sum(w, axis=0, keepdims=True) * ROUTED_SCALE

    cnt = jnp.sum(chosen, axis=1, keepdims=True)
    plen = jnp.ceil(cnt * (1.0 / ROW_ALIGN)) * ROW_ALIGN
    start = _cumsum_rows(jnp.broadcast_to(plen, (ne, LANE)))[:, :1] - plen
    before = (lax.broadcasted_iota(jnp.int32, (tm, tm), 0)
              < lax.broadcasted_iota(jnp.int32, (tm, tm), 1)).astype(BF16)
    rank = jnp.dot(chosen.astype(BF16), before, preferred_element_type=F32)
    row_of = start + rank
    pad_rows = SLOT_ROWS - TOP_K
    dest = [jnp.sum(jnp.where(hit, row_of, 0.0), axis=0, keepdims=True) for hit in hits]
    wsel = [jnp.sum(jnp.where(hit, w, 0.0), axis=0, keepdims=True) for hit in hits]
    dest_ref[...] = jnp.concatenate(dest + [jnp.full((pad_rows, tm), -1.0, F32)], axis=0).astype(jnp.int32)
    wgt_ref[...] = jnp.concatenate(wsel + [jnp.zeros((pad_rows, tm), F32)], axis=0)
    starts_ref[0] = start.astype(jnp.int32)
    plens_ref[0] = plen.astype(jnp.int32)


def _router(x2, mod3, norm_g, rw_t, rb_col, seq):
    t, d = x2.shape
    tm = MOE_SUB
    nsb = t // tm
    return pl.pallas_call(
        _router_kernel, name="router",
        grid=(nsb,),
        in_specs=[pl.BlockSpec((tm, d), lambda i: (i, 0)),
                  pl.BlockSpec((1, 1, d), lambda i: ((i * tm) // seq, 0, 4)),
                  pl.BlockSpec((1, 1, d), lambda i: ((i * tm) // seq, 0, 3)),
                  pl.BlockSpec((1, d), lambda i: (0, 0)),
                  pl.BlockSpec(rw_t.shape, lambda i: (0, 0)),
                  pl.BlockSpec(rb_col.shape, lambda i: (0, 0))],
        out_specs=[pl.BlockSpec((SLOT_ROWS, tm), lambda i: (0, i)),
                   pl.BlockSpec((SLOT_ROWS, tm), lambda i: (0, i)),
                   pl.BlockSpec((1, N_EXPERTS, 1), lambda i: (i, 0, 0)),
                   pl.BlockSpec((1, N_EXPERTS, 1), lambda i: (i, 0, 0))],
        out_shape=[jax.ShapeDtypeStruct((SLOT_ROWS, t), jnp.int32),
                   jax.ShapeDtypeStruct((SLOT_ROWS, t), F32),
                   jax.ShapeDtypeStruct((nsb, N_EXPERTS, 1), jnp.int32),
                   jax.ShapeDtypeStruct((nsb, N_EXPERTS, 1), jnp.int32)],
        compiler_params=_params(("parallel",)),
    )(x2, mod3, mod3, norm_g, rw_t, rb_col)


def _moe_stream_kernel(starts_ref, plens_ref, x_ref, sc_ref, sh_ref, g2_ref, gn_ref, dest_ref, wgt_ref,
                       wg_ref, wu_ref, wd_ref, sg_ref, su_ref, sd_ref, nf_ref, o_ref, h_ref, *, final):
    blk, step = pl.program_id(0), pl.program_id(1)
    eps = wg_ref.shape[0]
    ne = pl.num_programs(1) * eps
    sub, win = MOE_SUB, MOE_WINDOW
    nsub = x_ref.shape[0] // sub

    @pl.when(step == 0)
    def _():
        h_ref[...] = _mx(_rms(x_ref[...], gn_ref[...]) * (1.0 + sc_ref[0]) + sh_ref[0])
        o_ref[...] = jnp.zeros_like(o_ref)

    def run(sb, j):
        i = (blk * nsub + sb) * ne + step * eps + j
        return starts_ref[i], plens_ref[i]

    runs = [[run(sb, j) for j in range(eps)] for sb in range(nsub)]
    r_col = lax.broadcasted_iota(jnp.int32, (win, 1), 0)
    r_row = lax.broadcasted_iota(jnp.int32, (1, win), 1)

    def window_pass(k):
        first = k * win
        picked = []
        for sb in range(nsub):
            tok = slice(sb * sub, (sb + 1) * sub)
            dest = dest_ref[:, tok]
            want = jnp.concatenate([st + first + r_col for st, _ in runs[sb]], axis=0)
            pick = jnp.zeros((eps * win, sub), F32)
            for s in range(TOP_K):
                pick = jnp.where(dest[s:s + 1, :] == want, 1.0, pick)
            picked.append(_mx(_dot(pick, h_ref[tok, :])))
        outs = []
        for j in range(eps):
            xj = jnp.concatenate([p[j * win:(j + 1) * win] for p in picked], axis=0)
            outs.append(_mx(_dot(_silu(_dot(xj, wg_ref[j])) * _dot(xj, wu_ref[j]), wd_ref[j])))
        for sb in range(nsub):
            tok = slice(sb * sub, (sb + 1) * sub)
            res = jnp.concatenate([o[sb * win:(sb + 1) * win] for o in outs], axis=0)
            have = jnp.concatenate([jnp.where(first + r_row < ln, st + first + r_row, -2)
                                    for st, ln in runs[sb]], axis=1).astype(F32)
            dest_t = jnp.transpose(dest_ref[:, tok].astype(F32))
            wgt_t = jnp.transpose(wgt_ref[:, tok])
            mix = jnp.zeros((sub, eps * win), F32)
            for s in range(TOP_K):
                mix = jnp.where(dest_t[:, s:s + 1] == have, wgt_t[:, s:s + 1], mix)
            o_ref[tok, :] += _dot(mix, res)

    window_pass(0)

    longest = runs[0][0][1]
    for per_sub in runs:
        for _, ln in per_sub:
            longest = jnp.maximum(longest, ln)

    def more(k):
        window_pass(k)
        return k + 1
    lax.while_loop(lambda k: k * win < longest, more, jnp.int32(1))

    @pl.when(step == pl.num_programs(1) - 1)
    def _():
        for sb in range(nsub):
            tok = slice(sb * sub, (sb + 1) * sub)
            hs = h_ref[tok, :]
            shared = _dot(_silu(_dot(hs, sg_ref[...])) * _dot(hs, su_ref[...]), sd_ref[...])
            xo = x_ref[tok, :] + g2_ref[0] * (shared + o_ref[tok, :])
            if final:
                xo = _rms(xo, nf_ref[...])
            o_ref[tok, :] = xo


def _moe_stream(x2, mod3, norm_g, dest, wgt, starts, plens, wg, wu, wd, sg, su, sd, nf, seq, final):
    t, d = x2.shape
    tm = min(MOE_STREAM_BLOCK, seq)
    ne, ff = wg.shape[0], wg.shape[2]
    eps = MOE_EXPERTS_PER_STEP
    full = lambda a: pl.BlockSpec(a.shape, lambda i, e, *_: (0,) * a.ndim)
    once = lambda a: pl.BlockSpec(a.shape, lambda i, e, *_: (0,) * a.ndim, pipeline_mode=pl.Buffered(1))
    mod = lambda j: pl.BlockSpec((1, 1, d), lambda i, e, *_: ((i * tm) // seq, 0, j))
    slot = pl.BlockSpec((SLOT_ROWS, tm), lambda i, e, *_: (0, i))
    grid_spec = pltpu.PrefetchScalarGridSpec(
        num_scalar_prefetch=2,
        grid=(t // tm, ne // eps),
        in_specs=[pl.BlockSpec((tm, d), lambda i, e, *_: (i, 0), pipeline_mode=pl.Buffered(1)),
                  mod(4), mod(3), mod(5),
                  pl.BlockSpec((1, d), lambda i, e, *_: (0, 0)), slot, slot,
                  pl.BlockSpec((eps, d, ff), lambda i, e, *_: (e, 0, 0)),
                  pl.BlockSpec((eps, d, ff), lambda i, e, *_: (e, 0, 0)),
                  pl.BlockSpec((eps, ff, d), lambda i, e, *_: (e, 0, 0)),
                  once(sg), once(su), once(sd), full(nf)],
        out_specs=pl.BlockSpec((tm, d), lambda i, e, *_: (i, 0), pipeline_mode=pl.Buffered(1)),
        scratch_shapes=[pltpu.VMEM((tm, d), MXU_DTYPE)])
    return pl.pallas_call(
        functools.partial(_moe_stream_kernel, final=final), name="moe",
        grid_spec=grid_spec,
        out_shape=jax.ShapeDtypeStruct((t, d), F32),
        compiler_params=_params(("parallel", "arbitrary"), vmem_mb=61),
    )(starts.reshape(-1), plens.reshape(-1), x2, mod3, mod3, mod3, norm_g, dest, wgt,
      wg, wu, wd, sg, su, sd, nf)


def _pack_w_in(w):
    d = w.shape[0]
    offs = [0]
    for n in IN_SIZES:
        offs.append(offs[-1] + n)
    (qa, ka, va, ga, alr, cq, ckv, kidx, widx, qc, kc, vc, ic, fc, oc, gates) = [
        w[:, offs[i]:offs[i + 1]] for i in range(len(IN_SIZES))]
    z = lambda n: jnp.zeros((d, n), w.dtype)
    packed = jnp.concatenate(
        [gates, qa, ka, va, ga, vc, oc, qc, kc, cq, ckv,
         kidx, widx, z(LANE - IDX_DIM - IDX_HEADS),
         alr, z(LANE - GLA_GATE_RANK),
         ic, fc, z(LANE - 2 * MLSTM_HEADS)], axis=1)
    assert packed.shape[1] == N_PACK
    return packed.astype(MXU_DTYPE)


def kernel(x, c, ada_w, ada_b, norm_mix, norm_ffn, w_in, gla_w_a2, gla_b_a, gla_norm, dsa_norm_q,
           dsa_norm_kv, dsa_w_uq, dsa_w_uk, dsa_w_uv, dsa_w_qi, mlstm_conv, mlstm_b_i, mlstm_b_f,
           mlstm_norm, w_up_a, w_up_b, w_up_c, w_o, router_w, router_bias, exp_w_gate, exp_w_up,
           exp_w_down, sh_w_gate, sh_w_up, sh_w_down, norm_final):
    b, s, d = x.shape
    depth = ada_w.shape[0]
    t = b * s
    mod = _ada_mod(c, ada_w, ada_b)
    x2 = x.reshape(t, d)
    row = lambda v: v.reshape(1, -1)
    for l in range(depth):
        mod3 = mod[l].reshape(b, 1, 6 * d)
        w_pack = _pack_w_in(w_in[l])
        proj2, tail2 = _in_proj(x2, mod3, row(norm_mix[l]), w_pack, s)
        proj3 = proj2.reshape(b, s, N_MAIN)
        tail3 = tail2.reshape(b, s, N_TAIL)

        wa2_pad = jnp.zeros((LANE, GLA_QK), F32).at[:GLA_GATE_RANK].set(gla_w_a2[l])
        ya_in = _gla(proj3, tail3, wa2_pad, row(gla_b_a[l]), row(gla_norm[l]))

        bias_row = jnp.zeros((1, LANE), F32).at[0, :MLSTM_HEADS].set(mlstm_b_i[l])
        bias_row = bias_row.at[0, MLSTM_HEADS:2 * MLSTM_HEADS].set(mlstm_b_f[l])
        yc_in = _mlstm(proj3, tail3, mlstm_conv[l], bias_row, row(mlstm_norm[l]))

        wuq = dsa_w_uq[l].reshape(DSA_Q_RANK, DSA_HEADS * DSA_HEAD_DIM).astype(MXU_DTYPE)
        wuk_t = jnp.transpose(dsa_w_uk[l], (1, 2, 0)).astype(MXU_DTYPE)
        wuv_t = jnp.transpose(dsa_w_uv[l], (1, 2, 0)).astype(MXU_DTYPE)
        wqi_pad = jnp.zeros((DSA_Q_RANK, IDX_HEADS, LANE), F32).at[:, :, :IDX_DIM].set(dsa_w_qi[l])
        wqi_pad = wqi_pad.reshape(DSA_Q_RANK, IDX_HEADS * LANE).astype(MXU_DTYPE)
        ckvn, kwb, qlat, qidx, wht = _dsa_prep(proj2, tail2, s, row(dsa_norm_q[l]), row(dsa_norm_kv[l]),
                                               wuq, wuk_t, wqi_pad)
        yb_in = _dsa(ckvn.reshape(b, s, DSA_AUG), kwb.reshape(b, s, LANE), qlat, qidx, wht, wuv_t)

        x2 = _merge(ya_in.reshape(t, GLA_V), yb_in.reshape(t, DSA_V), yc_in.reshape(t, MLSTM_V),
                    proj2, x2, mod3, w_up_a[l].astype(MXU_DTYPE), w_up_b[l].astype(MXU_DTYPE),
                    w_up_c[l].astype(MXU_DTYPE), w_o[l].astype(MXU_DTYPE), s)

        dest, wgt, starts, plens = _router(x2, mod3, row(norm_ffn[l]), jnp.transpose(router_w[l]),
                                           router_bias[l].reshape(-1, 1), s)
        x2 = _moe_stream(x2, mod3, row(norm_ffn[l]), dest, wgt, starts, plens,
                  exp_w_gate[l].astype(MXU_DTYPE), exp_w_up[l].astype(MXU_DTYPE),
                  exp_w_down[l].astype(MXU_DTYPE), sh_w_gate[l].astype(MXU_DTYPE),
                  sh_w_up[l].astype(MXU_DTYPE), sh_w_down[l].astype(MXU_DTYPE), row(norm_final), s,
                  final=(l == depth - 1))
    return x2.reshape(b, s, d)
```
